```python
import jax
import jax.numpy as jnp
from jax import lax
import numpy as np

D_MODEL = 1024
BATCH = 4
SEQ = 4096
DEPTH = 4
DEC_BATCH = 128
DEC_SEQ = 1
PAST_LEN = 2048
PAGE_SIZE = 128

N_SUB = 3
POOL_WINDOWS = (2, 4, 8, 16)
POOL_GROUPS = 4
POOL_GROUP_DIM = D_MODEL // 16
POOL_WIDTH = POOL_GROUPS * POOL_GROUP_DIM
POOL_STATE = max(POOL_WINDOWS) - 1
HGRN_HEADS = 4
HGRN_K = 128
HGRN_V = D_MODEL // 16
HGRN_F_WIDTH = HGRN_HEADS * HGRN_K
HGRN_I_WIDTH = HGRN_HEADS * HGRN_V
HGRN_CHUNK = 64
F_FLOOR = 1e-30
N_HEADS = 8
N_KV_HEADS = 2
HEAD_DIM = 64
GROUP = N_HEADS // N_KV_HEADS
ATTN_WIDTH = N_HEADS * HEAD_DIM
KV_WIDTH = N_KV_HEADS * HEAD_DIM
IDX_HEADS = 4
IDX_DIM = 64
TOPK_MAX = 256
Q_BLOCK = 128
ROPE_THETA = 10000.0
ATTN_SCALE = HEAD_DIM ** -0.5
IDX_SCALE = (IDX_DIM * IDX_HEADS) ** -0.5
NEG_BIG = -1e30
N_BRANCH = 3
D_FF = 2816
EPS = 1e-6
IN_SPLITS = (POOL_WIDTH, HGRN_F_WIDTH, HGRN_F_WIDTH, HGRN_I_WIDTH, HGRN_I_WIDTH,
             ATTN_WIDTH, KV_WIDTH, KV_WIDTH, IDX_HEADS * IDX_DIM, IDX_DIM, IDX_HEADS,
             N_BRANCH * D_MODEL)
IN_WIDTH = sum(IN_SPLITS)

kernel_name = 'hybrid_pool_hgrn2_dsa_macaron_step'


def _rmsnorm(x, g=None):
    xf = x.astype(jnp.float32)
    y = xf * lax.rsqrt(jnp.mean(xf * xf, axis=-1, keepdims=True) + EPS)
    if g is not None:
        y = y * g.astype(jnp.float32)
    return y.astype(x.dtype)


def _swiglu(h, wg, wu, wd):
    return (jax.nn.silu(h @ wg) * (h @ wu)) @ wd


def _split_in(p):
    parts, off = [], 0
    for n in IN_SPLITS:
        parts.append(p[..., off:off + n])
        off += n
    return parts


def _rope(x, pos):
    half = x.shape[-1] // 2
    inv = ROPE_THETA ** (-jnp.arange(half, dtype=jnp.float32) / half)
    ang = pos.astype(jnp.float32)[:, None] * inv[None, :]
    cos = jnp.cos(ang)[:, None, :]
    sin = jnp.sin(ang)[:, None, :]
    xf = x.astype(jnp.float32)
    x1, x2 = xf[..., :half], xf[..., half:]
    return jnp.concatenate([x1 * cos - x2 * sin, x2 * cos + x1 * sin], axis=-1).astype(x.dtype)


def _gather_pages(pool, page_table):
    g = pool[page_table]
    return g.reshape((g.shape[0], g.shape[1] * g.shape[2]) + g.shape[3:])


def _lower_bounds(p):
    sm = jax.nn.softmax(p.astype(jnp.float32), axis=0)
    return jnp.cumsum(sm, axis=0) - sm[0:1]


def _pool_mix(u, u_prev, p0, w, scale):
    B, T, _ = u.shape
    ue = jnp.concatenate([u_prev.astype(u.dtype), u], axis=1).astype(jnp.float32)
    cs = jnp.pad(jnp.cumsum(ue, axis=1), ((0, 0), (1, 0), (0, 0)))
    hi = cs[:, POOL_STATE + 1:POOL_STATE + 1 + T]
    n_avail = (p0 + 1 + jnp.arange(T)).astype(jnp.float32)
    parts = []
    for gi, wdw in enumerate(POOL_WINDOWS):
        sl = slice(gi * POOL_GROUP_DIM, (gi + 1) * POOL_GROUP_DIM)
        lo = cs[:, POOL_STATE + 1 - wdw:POOL_STATE + 1 - wdw + T, sl]
        cnt = jnp.minimum(n_avail, float(wdw))
        parts.append((hi[..., sl] - lo) / cnt[None, :, None])
    pooled = jnp.concatenate(parts, axis=-1) - ue[:, POOL_STATE:]
    pooled = pooled.reshape(B, T, POOL_GROUPS, POOL_GROUP_DIM).astype(u.dtype)
    y = jnp.einsum('btgc,gcd->btgd', pooled, w).reshape(B, T, POOL_WIDTH) * scale
    return y, ue[:, T:].astype(u.dtype)


def _hgrn_scan(q, g, k, i, s0):
    B, T, H, K = q.shape
    V = i.shape[-1]
    C = min(HGRN_CHUNK, T)
    n = -(-T // C)
    pad = n * C - T

    def chunks(a):
        a = jnp.pad(a, ((0, 0), (0, pad), (0, 0), (0, 0)))
        return a.reshape(B, n, C, H, a.shape[-1]).swapaxes(0, 1)

    causal = jnp.tril(jnp.ones((C, C), dtype=bool))[None, :, :, None, None]

    def step(s, inp):
        qc, gc, kc, ic = inp
        b = jnp.cumsum(gc, axis=1)
        diff = jnp.where(causal, b[:, :, None] - b[:, None, :], 0.0)
        rel = jnp.where(causal, jnp.exp(diff), 0.0)
        att = jnp.einsum('bthk,btshk,bshk->bhts', qc, rel, kc)
        o = jnp.einsum('bhts,bshv->bthv', att, ic) + jnp.einsum('bthk,bhkv->bthv', qc * jnp.exp(b), s)
        b_end = b[:, -1]
        s_new = jnp.exp(b_end)[..., None] * s + jnp.einsum('bshk,bshv->bhkv', kc * jnp.exp(b_end[:, None] - b), ic)
        return s_new, o

    s_fin, o = lax.scan(step, s0, (chunks(q), chunks(g), chunks(k), chunks(i)))
    o = o.swapaxes(0, 1).reshape(B, n * C, H, V)[:, :T]
    return o, s_fin


def _dsa(q, qi, wi, pos, k_all, v_all, ki_all, n_sel):
    B, T = q.shape[:2]
    L = k_all.shape[1]
    qb = Q_BLOCK if T % Q_BLOCK == 0 else T
    nb = T // qb
    kpos = jnp.arange(L, dtype=jnp.int32)
    bidx = jnp.arange(B)[:, None, None]

    def blocks(a):
        return a.reshape((B, nb, qb) + a.shape[2:]).swapaxes(0, 1)

    def one(args):
        q_b, qi_b, wi_b, pos_b = args
        allowed = kpos[None, :] <= pos_b[:, None]
        rel = jax.nn.relu(jnp.einsum('bqhd,bsd->bqhs', qi_b, ki_all).astype(jnp.float32))
        score = jnp.einsum('bqh,bqhs->bqs', wi_b.astype(jnp.float32), rel) * IDX_SCALE
        score = jnp.where(allowed[None], score, NEG_BIG)
        top_val, top_idx = lax.top_k(score, n_sel)
        valid = top_val > 0.5 * NEG_BIG
        kg = k_all[bidx, top_idx]
        vg = v_all[bidx, top_idx]
        qg = q_b.reshape(B, qb, N_KV_HEADS, GROUP, HEAD_DIM)
        logits = jnp.einsum('bqngd,bqknd->bqngk', qg, kg).astype(jnp.float32) * ATTN_SCALE
        logits = jnp.where(valid[:, :, None, None, :], logits, NEG_BIG)
        p = jax.nn.softmax(logits, axis=-1).astype(v_all.dtype)
        o = jnp.einsum('bqngk,bqknd->bqngd', p, vg)
        return o.reshape(B, qb, ATTN_WIDTH)

    out = lax.map(one, (blocks(q), blocks(qi), blocks(wi), pos.reshape(nb, qb)))
    return out.swapaxes(0, 1).reshape(B, T, ATTN_WIDTH)


def _trunk(x, c, pool_prev, hgrn_prev, past, p0, n_sel, P):
    B, T, _ = x.shape
    pos = p0 + jnp.arange(T, dtype=jnp.int32)
    lbs = _lower_bounds(P['hgrn_lb'])
    c_act = jax.nn.silu(c)
    ks, vs, kis, pools, states = [], [], [], [], []
    for l in range(DEPTH):
        mod = (c_act @ P['ada_w'][l] + P['ada_b'][l]).reshape(B, 1, 3 * N_SUB, D_MODEL)

        def pre(y, j):
            return _rmsnorm(y, P['norm_g'][l, j]) * (1 + mod[:, :, 3 * j + 1]) + mod[:, :, 3 * j]

        x = x + 0.5 * mod[:, :, 2] * _swiglu(pre(x, 0), P['ffn_wg'][l, 0], P['ffn_wu'][l, 0], P['ffn_wd'][l, 0])

        h = pre(x, 1)
        (u_pool, hq, hf, hi, hg, aq, ak, av, iq, ik, iw, gl) = _split_in(h @ P['w_in'][l])

        ya, pool_new = _pool_mix(u_pool, pool_prev[l], p0, P['pool_w'][l], P['pool_scale'][l])

        lb = lbs[l]
        f = lb + (1.0 - lb) * jax.nn.sigmoid(hf.astype(jnp.float32))
        f = f.reshape(B, T, HGRN_HEADS, HGRN_K)
        f_log = jnp.log(jnp.maximum(f, F_FLOOR))
        o_b, s_new = _hgrn_scan(hq.astype(jnp.float32).reshape(B, T, HGRN_HEADS, HGRN_K), f_log,
                                1.0 - f, hi.astype(jnp.float32).reshape(B, T, HGRN_HEADS, HGRN_V),
                                hgrn_prev[l].astype(jnp.float32))
        yb = (_rmsnorm(o_b.astype(x.dtype), P['hgrn_onorm'][l])
              * jax.nn.silu(hg.reshape(B, T, HGRN_HEADS, HGRN_V))).reshape(B, T, HGRN_I_WIDTH)

        q = _rope(_rmsnorm(aq.reshape(B, T, N_HEADS, HEAD_DIM), P['q_norm'][l]), pos)
        k = _rope(_rmsnorm(ak.reshape(B, T, N_KV_HEADS, HEAD_DIM), P['k_norm'][l]), pos)
        v = av.reshape(B, T, N_KV_HEADS, HEAD_DIM)
        qi = _rope(iq.reshape(B, T, IDX_HEADS, IDX_DIM), pos)
        ki = _rope(_rmsnorm(ik)[:, :, None, :], pos)[:, :, 0]
        if past is None:
            k_all, v_all, ki_all = k, v, ki
        else:
            cache_k, cache_v, cache_ki, page_table = past
            k_all = jnp.concatenate([_gather_pages(cache_k[l], page_table).astype(k.dtype), k], axis=1)
            v_all = jnp.concatenate([_gather_pages(cache_v[l], page_table).astype(v.dtype), v], axis=1)
            ki_all = jnp.concatenate([_gather_pages(cache_ki[l], page_table).astype(ki.dtype), ki], axis=1)
        yc = _dsa(q, qi, iw, pos, k_all, v_all, ki_all, n_sel)

        gate = jax.nn.sigmoid(gl).reshape(B, T, N_BRANCH, D_MODEL)
        merged = (gate[:, :, 0] * (ya @ P['w_ba'][l]) + gate[:, :, 1] * (yb @ P['w_bb'][l])
                  + gate[:, :, 2] * (yc @ P['w_bc'][l]))
        x = x + mod[:, :, 5] * (merged @ P['w_out'][l])

        x = x + 0.5 * mod[:, :, 8] * _swiglu(pre(x, 2), P['ffn_wg'][l, 1], P['ffn_wu'][l, 1], P['ffn_wd'][l, 1])

        ks.append(k)
        vs.append(v)
        kis.append(ki)
        pools.append(pool_new)
        states.append(s_new.astype(hgrn_prev.dtype))
    return x, jnp.stack(ks), jnp.stack(vs), jnp.stack(kis), jnp.stack(pools), jnp.stack(states)


def setup_inputs(seed: int = 0) -> dict:
    key = jax.random.key(seed)
    kk = jax.random.split(key, 32)
    n_pages = PAST_LEN // PAGE_SIZE
    n_used = DEC_BATCH * n_pages
    n_phys = n_used + n_used // 4

    def nrm(k, shape, scale=1.0):
        return jax.random.normal(k, shape, jnp.float32) * scale

    page_table = jax.random.permutation(kk[7], n_phys)[:n_used].reshape(DEC_BATCH, n_pages).astype(jnp.int32)
    return {
        'x_prompt': nrm(kk[0], (BATCH, SEQ, D_MODEL)),
        'x_sample': nrm(kk[1], (DEC_BATCH, DEC_SEQ, D_MODEL)),
        'c_prompt': nrm(kk[2], (BATCH, D_MODEL)),
        'c_sample': nrm(kk[3], (DEC_BATCH, D_MODEL)),
        'cache_k': nrm(kk[4], (DEPTH, n_phys, PAGE_SIZE, N_KV_HEADS, HEAD_DIM)),
        'cache_v': nrm(kk[5], (DEPTH, n_phys, PAGE_SIZE, N_KV_HEADS, HEAD_DIM)),
        'cache_ki': nrm(kk[6], (DEPTH, n_phys, PAGE_SIZE, IDX_DIM)),
        'page_table': page_table,
        'state_pool': nrm(kk[8], (DEPTH, DEC_BATCH, POOL_STATE, POOL_WIDTH)),
        'state_hgrn': nrm(kk[9], (DEPTH, DEC_BATCH, HGRN_HEADS, HGRN_K, HGRN_V), 0.5),
        'ada_w': nrm(kk[10], (DEPTH, D_MODEL, 3 * N_SUB * D_MODEL), D_MODEL ** -0.5),
        'ada_b': nrm(kk[11], (DEPTH, 3 * N_SUB * D_MODEL), 0.02),
        'norm_g': 1.0 + nrm(kk[12], (DEPTH, N_SUB, D_MODEL), 0.02),
        'ffn_wg': nrm(kk[13], (DEPTH, 2, D_MODEL, D_FF), D_MODEL ** -0.5),
        'ffn_wu': nrm(kk[14], (DEPTH, 2, D_MODEL, D_FF), D_MODEL ** -0.5),
        'ffn_wd': nrm(kk[15], (DEPTH, 2, D_FF, D_MODEL), D_FF ** -0.5),
        'w_in': nrm(kk[16], (DEPTH, D_MODEL, IN_WIDTH), D_MODEL ** -0.5),
        'pool_w': nrm(kk[17], (DEPTH, POOL_GROUPS, POOL_GROUP_DIM, POOL_GROUP_DIM), POOL_GROUP_DIM ** -0.5),
        'pool_scale': 1.0 + nrm(kk[18], (DEPTH, POOL_WIDTH), 0.02),
        'hgrn_lb': nrm(kk[19], (DEPTH, HGRN_F_WIDTH)),
        'hgrn_onorm': 1.0 + nrm(kk[20], (DEPTH, HGRN_V), 0.02),
        'q_norm': 1.0 + nrm(kk[21], (DEPTH, HEAD_DIM), 0.02),
        'k_norm': 1.0 + nrm(kk[22], (DEPTH, HEAD_DIM), 0.02),
        'w_ba': nrm(kk[23], (DEPTH, POOL_WIDTH, D_MODEL), POOL_WIDTH ** -0.5),
        'w_bb': nrm(kk[24], (DEPTH, HGRN_I_WIDTH, D_MODEL), HGRN_I_WIDTH ** -0.5),
        'w_bc': nrm(kk[25], (DEPTH, ATTN_WIDTH, D_MODEL), ATTN_WIDTH ** -0.5),
        'w_out': nrm(kk[26], (DEPTH, D_MODEL, D_MODEL), D_MODEL ** -0.5),
    }


def reference(x_prompt, x_sample, c_prompt, c_sample, cache_k, cache_v, cache_ki, page_table,
              state_pool, state_hgrn, ada_w, ada_b, norm_g, ffn_wg, ffn_wu, ffn_wd, w_in,
              pool_w, pool_scale, hgrn_lb, hgrn_onorm, q_norm, k_norm, w_ba, w_bb, w_bc, w_out):
    P = {'ada_w': ada_w, 'ada_b': ada_b, 'norm_g': norm_g, 'ffn_wg': ffn_wg, 'ffn_wu': ffn_wu,
         'ffn_wd': ffn_wd, 'w_in': w_in, 'pool_w': pool_w, 'pool_scale': pool_scale,
         'hgrn_lb': hgrn_lb, 'hgrn_onorm': hgrn_onorm, 'q_norm': q_norm, 'k_norm': k_norm,
         'w_ba': w_ba, 'w_bb': w_bb, 'w_bc': w_bc, 'w_out': w_out}
    b_p, seq = x_prompt.shape[0], x_prompt.shape[1]
    pool0 = jnp.zeros((DEPTH, b_p, POOL_STATE, POOL_WIDTH), x_prompt.dtype)
    hgrn0 = jnp.zeros((DEPTH, b_p, HGRN_HEADS, HGRN_K, HGRN_V), state_hgrn.dtype)
    y_prompt, k_p, v_p, ki_p, pool_p, hgrn_p = _trunk(
        x_prompt, c_prompt, pool0, hgrn0, None, 0, min(TOPK_MAX, seq // 4), P)
    l_vis = PAST_LEN + x_sample.shape[1]
    y_sample, k_s, v_s, ki_s, pool_s, hgrn_s = _trunk(
        x_sample, c_sample, state_pool, state_hgrn, (cache_k, cache_v, cache_ki, page_table),
        PAST_LEN, min(TOPK_MAX, l_vis // 4), P)
    return (y_prompt, y_sample, k_p, v_p, ki_p, pool_p, hgrn_p, k_s, v_s, ki_s, pool_s, hgrn_s)
```

```python
import functools

import numpy as np
import jax
import jax.numpy as jnp
from jax import lax
from jax.experimental import pallas as pl
from jax.experimental.pallas import tpu as pltpu

F32 = jnp.float32
BF16 = jnp.bfloat16
I32 = jnp.int32

D_MODEL = 1024
DEPTH = 4
PAGE_SIZE = 128
N_SUB = 3
POOL_WINDOWS = (2, 4, 8, 16)
POOL_GROUP_DIM = 64
POOL_WIDTH = 256
POOL_STATE = 15
HGRN_HEADS = 4
HGRN_K = 128
HGRN_V = 64
F_FLOOR = 1e-30
N_HEADS = 8
N_KV_HEADS = 2
HEAD_DIM = 64
GROUP = N_HEADS // N_KV_HEADS
ATTN_WIDTH = 512
KV_WIDTH = 128
IDX_HEADS = 4
IDX_DIM = 64
TOPK_MAX = 256
ROPE_THETA = 10000.0
ATTN_SCALE = HEAD_DIM ** -0.5
IDX_SCALE = (IDX_DIM * IDX_HEADS) ** -0.5
NEG_BIG = -1e30
D_FF = 2816
EPS = 1e-6
INT_MIN = np.int32(-2147483648)

LANES = 128
VMEM_LIMIT = 48 * 1024 * 1024

OFF_GL, OFF_HQ, OFF_HF, OFF_AQ = 0, 3072, 3584, 4096
OFF_UP, OFF_HI, OFF_HG, OFF_IQ = 4608, 4864, 5120, 5376
OFF_AK, OFF_AV, OFF_IKW = 5632, 5760, 5888
IN_PAD = 6144
HGRN_CHUNK = 128
HGRN_LEVELS = (16, 32, 64, HGRN_CHUNK)


def _dot(a, b):
    return jnp.dot(a, b, preferred_element_type=F32)


def _dot_nt(a, b):
    return lax.dot_general(a, b, (((1,), (1,)), ((), ())), preferred_element_type=F32)


def _dot_tn(a, b):
    return lax.dot_general(a, b, (((0,), (0,)), ((), ())), preferred_element_type=F32)


def _split2(x):
    hi = x.astype(BF16)
    lo = (x - hi.astype(F32)).astype(BF16)
    return hi, lo


def _split3(x):
    hi = x.astype(BF16)
    r = x - hi.astype(F32)
    mid = r.astype(BF16)
    lo = (r - mid.astype(F32)).astype(BF16)
    return hi, mid, lo


def _silu(x):
    return x * jax.nn.sigmoid(x)


def _params(sem):
    return pltpu.CompilerParams(dimension_semantics=sem, vmem_limit_bytes=VMEM_LIMIT)


def _mod_spec(mods, l, m, tm, rows_per_seq):
    if mods.ndim == 4:
        return pl.BlockSpec((None, None, 1, D_MODEL), lambda i, *_: (l, (i * tm) // rows_per_seq, 0, m))
    return pl.BlockSpec((None, tm, D_MODEL), lambda i, *_: (l, i, m))


def _row_tile(n, mods, rows_per_seq, cap):
    return min(cap, rows_per_seq if mods.ndim == 4 else n)


def _prenorm(x, g, scale, shift):
    y = x * lax.rsqrt(jnp.mean(x * x, axis=-1, keepdims=True) + EPS) * g
    return y * (1.0 + scale) + shift


def _ada_kernel(c_ref, w_ref, b_ref, o_ref):
    c = c_ref[...]
    o_ref[...] = jnp.dot(_silu(c), w_ref[...], precision=lax.Precision.HIGHEST,
                         preferred_element_type=F32) + b_ref[...]


def _ada(c_all, ada_w, ada_b):
    rows = c_all.shape[0]
    width = ada_w.shape[-1]
    tn = 1024
    return pl.pallas_call(
        _ada_kernel,
        grid=(DEPTH, width // tn),
        in_specs=[pl.BlockSpec((rows, D_MODEL), lambda l, j: (0, 0)),
                  pl.BlockSpec((None, D_MODEL, tn), lambda l, j: (l, 0, j)),
                  pl.BlockSpec((None, 1, tn), lambda l, j: (l, 0, j))],
        out_specs=pl.BlockSpec((None, rows, tn), lambda l, j: (l, 0, j)),
        out_shape=jax.ShapeDtypeStruct((DEPTH, rows, width), F32),
        compiler_params=_params(("parallel", "parallel")),
        name="ada",
    )(c_all, ada_w, ada_b.reshape(DEPTH, 1, width))


def _ffn_kernel(x_ref, sh_ref, sc_ref, gt_ref, g_ref, wg_ref, wu_ref, wd_ref, o_ref, h_scr, acc_scr):
    j = pl.program_id(1)

    @pl.when(j == 0)
    def _():
        h_scr[...] = _prenorm(x_ref[...], g_ref[...], sc_ref[...], sh_ref[...]).astype(BF16)
        acc_scr[...] = jnp.zeros_like(acc_scr)

    h = h_scr[...]
    act = (_silu(_dot(h, wg_ref[...])) * _dot(h, wu_ref[...])).astype(BF16)
    acc_scr[...] += _dot(act, wd_ref[...])

    @pl.when(j == pl.num_programs(1) - 1)
    def _():
        o_ref[...] = x_ref[...] + 0.5 * gt_ref[...] * acc_scr[...]


def _ffn(x, mods, norm_g, wg, wu, wd, l, which, sub, rows_per_seq):
    n = x.shape[0]
    tm = _row_tile(n, mods, rows_per_seq, 1024)
    tf = 256
    return pl.pallas_call(
        _ffn_kernel,
        grid=(n // tm, D_FF // tf),
        in_specs=[pl.BlockSpec((tm, D_MODEL), lambda i, j: (i, 0)),
                  _mod_spec(mods, l, 3 * sub, tm, rows_per_seq),
                  _mod_spec(mods, l, 3 * sub + 1, tm, rows_per_seq),
                  _mod_spec(mods, l, 3 * sub + 2, tm, rows_per_seq),
                  pl.BlockSpec((None, None, 1, D_MODEL), lambda i, j: (l, sub, 0, 0)),
                  pl.BlockSpec((None, None, D_MODEL, tf), lambda i, j: (l, which, 0, j)),
                  pl.BlockSpec((None, None, D_MODEL, tf), lambda i, j: (l, which, 0, j)),
                  pl.BlockSpec((None, None, tf, D_MODEL), lambda i, j: (l, which, j, 0))],
        out_specs=pl.BlockSpec((tm, D_MODEL), lambda i, j: (i, 0)),
        out_shape=jax.ShapeDtypeStruct((n, D_MODEL), F32),
        scratch_shapes=[pltpu.VMEM((tm, D_MODEL), BF16), pltpu.VMEM((tm, D_MODEL), F32)],
        compiler_params=_params(("parallel", "arbitrary")),
        name="ffn",
    )(x, mods, mods, mods, norm_g, wg, wu, wd)


def _inproj_kernel(x_ref, sh_ref, sc_ref, g_ref, w_ref, o_ref, h_scr):
    @pl.when(pl.program_id(1) == 0)
    def _():
        h_scr[...] = _prenorm(x_ref[...], g_ref[...], sc_ref[...], sh_ref[...]).astype(BF16)

    o_ref[...] = _dot(h_scr[...], w_ref[...])


def _inproj(x, mods, norm_g, w_in, l, rows_per_seq):
    n = x.shape[0]
    tm = _row_tile(n, mods, rows_per_seq, 1024)
    tn = 1536
    return pl.pallas_call(
        _inproj_kernel,
        grid=(n // tm, IN_PAD // tn),
        in_specs=[pl.BlockSpec((tm, D_MODEL), lambda i, j: (i, 0)),
                  _mod_spec(mods, l, 3, tm, rows_per_seq),
                  _mod_spec(mods, l, 4, tm, rows_per_seq),
                  pl.BlockSpec((None, None, 1, D_MODEL), lambda i, j: (l, 1, 0, 0)),
                  pl.BlockSpec((None, D_MODEL, tn), lambda i, j: (l, 0, j))],
        out_specs=pl.BlockSpec((tm, tn), lambda i, j: (i, j)),
        out_shape=jax.ShapeDtypeStruct((n, IN_PAD), F32),
        scratch_shapes=[pltpu.VMEM((tm, D_MODEL), BF16)],
        compiler_params=_params(("parallel", "arbitrary")),
        name="inproj",
    )(x, mods, mods, norm_g, w_in)


def _prep_kernel(aq_ref, ak_ref, av_ref, iq_ref, ikw_ref, cos_ref, sin_ref, qg_ref, kg_ref, bd_ref,
                 q_o, k_o, kb_o, vb_o, qi_o, ki_o, kib_o):
    cos = cos_ref[...]
    sin = sin_ref[...]
    bd = bd_ref[...]
    lane = lax.broadcasted_iota(I32, cos.shape, 1)
    first_half = (lane & (HEAD_DIM // 2)) == 0

    def rope(x):
        partner = jnp.where(first_half, pltpu.roll(x, LANES - HEAD_DIM // 2, 1), pltpu.roll(x, HEAD_DIM // 2, 1))
        return x * cos + partner * sin

    def head_norm(x):
        hi, mid, lo = _split3(x * x)
        ms = (_dot(hi, bd) + _dot(mid, bd) + _dot(lo, bd)) * (1.0 / HEAD_DIM)
        return x * lax.rsqrt(ms + EPS)

    qg = qg_ref[...]
    for c in range(ATTN_WIDTH // LANES):
        sl = slice(c * LANES, (c + 1) * LANES)
        q_o[:, sl] = (rope(head_norm(aq_ref[:, sl]) * qg) * ATTN_SCALE).astype(BF16)
    k = rope(head_norm(ak_ref[...]) * kg_ref[...])
    k_o[...] = k
    kb_o[...] = k.astype(BF16)
    vb_o[...] = av_ref[...].astype(BF16)
    for c in range(IDX_HEADS * IDX_DIM // LANES):
        sl = slice(c * LANES, (c + 1) * LANES)
        qi_o[:, sl] = rope(iq_ref[:, sl]).astype(BF16)
    ki = rope(head_norm(ikw_ref[...]))[:, :IDX_DIM]
    ki_o[...] = ki
    kib_o[...] = ki.astype(BF16)


def _prep(p, cos, sin, qg, kg, bd, tm, table_blocks):
    n = p.shape[0]

    def col(off, width):
        return pl.BlockSpec((tm, width), lambda i: (i, off // width))

    tab = pl.BlockSpec((tm, LANES), lambda i: (i % table_blocks, 0))
    vec = pl.BlockSpec((1, LANES), lambda i: (0, 0))

    def row(width):
        return pl.BlockSpec((tm, width), lambda i: (i, 0))

    return pl.pallas_call(
        _prep_kernel,
        grid=(n // tm,),
        in_specs=[col(OFF_AQ, ATTN_WIDTH), col(OFF_AK, KV_WIDTH), col(OFF_AV, KV_WIDTH),
                  col(OFF_IQ, IDX_HEADS * IDX_DIM), col(OFF_IKW, LANES), tab, tab, vec, vec,
                  pl.BlockSpec((LANES, LANES), lambda i: (0, 0))],
        out_specs=[row(ATTN_WIDTH), row(KV_WIDTH), row(KV_WIDTH), row(KV_WIDTH),
                   row(IDX_HEADS * IDX_DIM), row(IDX_DIM), row(IDX_DIM)],
        out_shape=[jax.ShapeDtypeStruct((n, ATTN_WIDTH), BF16),
                   jax.ShapeDtypeStruct((n, KV_WIDTH), F32),
                   jax.ShapeDtypeStruct((n, KV_WIDTH), BF16),
                   jax.ShapeDtypeStruct((n, KV_WIDTH), BF16),
                   jax.ShapeDtypeStruct((n, IDX_HEADS * IDX_DIM), BF16),
                   jax.ShapeDtypeStruct((n, IDX_DIM), F32),
                   jax.ShapeDtypeStruct((n, IDX_DIM), BF16)],
        compiler_params=_params(("parallel",)),
        name="prep",
    )(p, p, p, p, p, cos, sin, qg, kg, bd)


def _rope_tables(pos):
    half = HEAD_DIM // 2
    inv = ROPE_THETA ** (-jnp.arange(half, dtype=F32) / half)
    ang = pos.astype(F32)[:, None] * inv[None, :]
    c, s = jnp.cos(ang), jnp.sin(ang)
    return jnp.concatenate([c, c, c, c], axis=1), jnp.concatenate([-s, s, -s, s], axis=1)


def _pool_select(sums, lane):
    out = sums[-1]
    for gi in range(len(POOL_WINDOWS) - 2, -1, -1):
        out = jnp.where(lane < (gi + 1) * POOL_GROUP_DIM, sums[gi], out)
    return out


def _pool_kernel(u_ref, prev_ref, w_ref, s_ref, o_ref, carry_scr, *, tt, p0):
    t = pl.program_id(1)

    @pl.when(t == 0)
    def _():
        carry_scr[...] = prev_ref[...]

    u = u_ref[...]
    e = jnp.concatenate([carry_scr[...], u], axis=0)
    carry_scr[...] = u[tt - 16:, :]
    s2 = e[1:] + e[:-1]
    s4 = s2[2:] + s2[:-2]
    s8 = s4[4:] + s4[:-4]
    s16 = s8[8:] + s8[:-8]
    sums = (s2[15:15 + tt], s4[13:13 + tt], s8[9:9 + tt], s16[1:1 + tt])
    n_avail = (p0 + 1 + t * tt + lax.broadcasted_iota(I32, (tt, 1), 0)).astype(F32)
    lane = lax.broadcasted_iota(I32, (tt, POOL_WIDTH), 1)
    means = [sums[gi] / jnp.minimum(n_avail, float(w)) for gi, w in enumerate(POOL_WINDOWS)]
    pooled = _pool_select(means, lane) - u
    o_ref[...] = _dot(pooled.astype(BF16), w_ref[...]) * s_ref[...]


def _pool_prompt(p, prev16, wbd, scale, l, b, t_len):
    tt = min(512, t_len)
    nt = t_len // tt
    return pl.pallas_call(
        functools.partial(_pool_kernel, tt=tt, p0=0),
        grid=(b, nt),
        in_specs=[pl.BlockSpec((tt, POOL_WIDTH), lambda bi, ti: (bi * nt + ti, OFF_UP // POOL_WIDTH)),
                  pl.BlockSpec((None, 16, POOL_WIDTH), lambda bi, ti: (bi, 0, 0)),
                  pl.BlockSpec((None, POOL_WIDTH, POOL_WIDTH), lambda bi, ti: (l, 0, 0)),
                  pl.BlockSpec((None, 1, POOL_WIDTH), lambda bi, ti: (l, 0, 0))],
        out_specs=pl.BlockSpec((tt, POOL_WIDTH), lambda bi, ti: (bi * nt + ti, 0)),
        out_shape=jax.ShapeDtypeStruct((b * t_len, POOL_WIDTH), F32),
        scratch_shapes=[pltpu.VMEM((16, POOL_WIDTH), F32)],
        compiler_params=_params(("parallel", "arbitrary")),
        name="pool",
    )(p, prev16, wbd, scale)


def _pool_dec_kernel(u_ref, st_ref, w_ref, s_ref, o_ref, *, p0):
    u = u_ref[...]
    lane = lax.broadcasted_iota(I32, u.shape, 1)
    means = []
    acc = u
    d = 1
    for w in POOL_WINDOWS:
        while d < w:
            acc = acc + st_ref[POOL_STATE - d]
            d += 1
        means.append(acc / float(min(p0 + 1, w)))
    pooled = _pool_select(means, lane) - u
    o_ref[...] = _dot(pooled.astype(BF16), w_ref[...]) * s_ref[...]


def _pool_dec(p, st_t, wbd, scale, l, p0):
    n = p.shape[0]
    return pl.pallas_call(
        functools.partial(_pool_dec_kernel, p0=p0),
        grid=(1,),
        in_specs=[pl.BlockSpec((n, POOL_WIDTH), lambda i: (0, OFF_UP // POOL_WIDTH)),
                  pl.BlockSpec((POOL_STATE, n, POOL_WIDTH), lambda i: (0, 0, 0)),
                  pl.BlockSpec((None, POOL_WIDTH, POOL_WIDTH), lambda i: (l, 0, 0)),
                  pl.BlockSpec((None, 1, POOL_WIDTH), lambda i: (l, 0, 0))],
        out_specs=pl.BlockSpec((n, POOL_WIDTH), lambda i: (0, 0)),
        out_shape=jax.ShapeDtypeStruct((n, POOL_WIDTH), F32),
        compiler_params=_params(("arbitrary",)),
        name="pool_dec",
    )(p, st_t, wbd, scale)


def _hgrn_consts(c):
    r = np.arange(c)[:, None]
    s = np.arange(c)[None, :]
    mats = [(s <= r) & (s // h == r // h) for h in HGRN_LEVELS]
    mats += [(s > r) & (s // h == r // h) for h in HGRN_LEVELS]
    return jnp.asarray(np.concatenate(mats, axis=0).astype(np.float32), BF16)


def _hgrn_kernel(hq_ref, hf_ref, hi_ref, hg_ref, lb_ref, on_ref, ms_ref, yb_ref, st_ref, s_scr, *, c):
    step = pl.program_id(1)
    nl = len(HGRN_LEVELS)

    @pl.when(step == 0)
    def _():
        s_scr[...] = jnp.zeros_like(s_scr)

    ti = lax.broadcasted_iota(I32, (c, c), 0)
    si = lax.broadcasted_iota(I32, (c, c), 1)
    nb = c // 16
    tpos = lax.broadcasted_iota(I32, (1, 16, 1), 1)

    for hd in range(HGRN_HEADS):
        ks = slice(hd * HGRN_K, (hd + 1) * HGRN_K)
        vs = slice(hd * HGRN_V, (hd + 1) * HGRN_V)
        q = hq_ref[:, ks]
        lb = lb_ref[:, ks]
        f = lb + (1.0 - lb) * jax.nn.sigmoid(hf_ref[:, ks])
        g = jnp.log(jnp.maximum(f, F_FLOOR))
        kk = 1.0 - f
        iv = hi_ref[:, vs]
        ivb = iv.astype(BF16)
        g_hi, g_lo = _split2(g)
        r = _dot(ms_ref[...], jnp.concatenate([g_hi, g_lo], axis=1))
        r = r[:, :HGRN_K] + r[:, HGRN_K:]
        low = [r[k * c:(k + 1) * c] for k in range(nl)]
        up = [r[(nl + k) * c:(nl + k + 1) * c] for k in range(nl)]

        att = jnp.zeros((c, c), F32)
        for k, h in enumerate(HGRN_LEVELS[:-1]):
            qe = (q * jnp.exp(low[k])).astype(BF16)
            ke = (kk * jnp.exp(up[k])).astype(BF16)
            sh = h.bit_length() - 1
            tb = jnp.right_shift(ti, sh)
            pair = ((tb & 1) == 1) & (jnp.right_shift(si, sh) == tb - 1)
            att = jnp.where(pair, _dot_nt(qe, ke), att)
        qe = (q * jnp.exp(low[-1])).astype(BF16)
        ke = (kk * jnp.exp(up[-1])).astype(BF16)
        s_t = s_scr[hd]
        o = _dot(att.astype(BF16), ivb) + _dot_nt(qe, s_t.astype(BF16))

        l3 = low[0].reshape(nb, 16, HGRN_K)
        q3 = q.reshape(nb, 16, HGRN_K)
        k3 = kk.reshape(nb, 16, HGRN_K)
        i3 = iv.reshape(nb, 16, HGRN_V)
        o3 = jnp.zeros((nb, 16, HGRN_V), F32)
        for s in range(16):
            e = jnp.exp(jnp.minimum(l3 - l3[:, s:s + 1, :], 0.0))
            a = jnp.sum(q3 * (k3[:, s:s + 1, :] * e), axis=-1, keepdims=True)
            a = jnp.where(tpos >= s, a, 0.0)
            o3 = o3 + a * i3[:, s:s + 1, :]
        o = o + o3.reshape(c, HGRN_V)

        s_scr[hd] = s_t * jnp.exp(low[-1][c - 1:c, :]) + _dot_tn(ivb, ke)

        on = o * lax.rsqrt(jnp.mean(o * o, axis=-1, keepdims=True) + EPS) * on_ref[...]
        yb_ref[:, vs] = on * _silu(hg_ref[:, vs])

    @pl.when(step == pl.num_programs(1) - 1)
    def _():
        st_ref[...] = s_scr[...]


def _hgrn_prompt(p, lb, onorm, mstack, b, t_len):
    c = HGRN_CHUNK
    nc = t_len // c
    kw = HGRN_HEADS * HGRN_K
    vw = HGRN_HEADS * HGRN_V

    def col(off, width):
        return pl.BlockSpec((c, width), lambda bi, ci: (bi * nc + ci, off // width))

    return pl.pallas_call(
        functools.partial(_hgrn_kernel, c=c),
        grid=(b, nc),
        in_specs=[col(OFF_HQ, kw), col(OFF_HF, kw), col(OFF_HI, vw), col(OFF_HG, vw),
                  pl.BlockSpec((1, kw), lambda bi, ci: (0, 0)),
                  pl.BlockSpec((1, HGRN_V), lambda bi, ci: (0, 0)),
                  pl.BlockSpec(mstack.shape, lambda bi, ci: (0, 0))],
        out_specs=[pl.BlockSpec((c, vw), lambda bi, ci: (bi * nc + ci, 0)),
                   pl.BlockSpec((None, HGRN_HEADS, HGRN_V, HGRN_K), lambda bi, ci: (bi, 0, 0, 0))],
        out_shape=[jax.ShapeDtypeStruct((b * t_len, vw), F32),
                   jax.ShapeDtypeStruct((b, HGRN_HEADS, HGRN_V, HGRN_K), F32)],
        scratch_shapes=[pltpu.VMEM((HGRN_HEADS, HGRN_V, HGRN_K), F32)],
        compiler_params=_params(("parallel", "arbitrary")),
        name="hgrn",
    )(p, p, p, p, lb, onorm, mstack)


def _hgrn_dec_kernel(q_ref, z_ref, lb_ref, i_ref, hg_ref, on_ref, s_ref, yb_ref, so_ref):
    lb = lb_ref[...]
    f = lb + (1.0 - lb) * jax.nn.sigmoid(z_ref[...])
    a = jnp.exp(jnp.log(jnp.maximum(f, F_FLOOR)))
    s_new = a * s_ref[...] + (1.0 - f) * i_ref[...]
    so_ref[...] = s_new
    o = jnp.sum(q_ref[...] * s_new, axis=2, keepdims=True)
    on = o * lax.rsqrt(jnp.mean(o * o, axis=-1, keepdims=True) + EPS) * on_ref[...]
    yb_ref[...] = on * _silu(hg_ref[...])


def _hgrn_dec(q_c, z_c, lb_c, i_r, hg_r, onorm, state):
    n = state.shape[0]
    nb = 8
    col = pl.BlockSpec((nb, HGRN_HEADS, HGRN_K, 1), lambda i: (i, 0, 0, 0))
    row = pl.BlockSpec((nb, HGRN_HEADS, 1, HGRN_V), lambda i: (i, 0, 0, 0))
    st = pl.BlockSpec((nb, HGRN_HEADS, HGRN_K, HGRN_V), lambda i: (i, 0, 0, 0))
    return pl.pallas_call(
        _hgrn_dec_kernel,
        grid=(n // nb,),
        in_specs=[col, col, pl.BlockSpec((HGRN_HEADS, HGRN_K, 1), lambda i: (0, 0, 0)), row, row,
                  pl.BlockSpec((1, HGRN_V), lambda i: (0, 0)), st],
        out_specs=[row, st],
        out_shape=[jax.ShapeDtypeStruct((n, HGRN_HEADS, 1, HGRN_V), F32),
                   jax.ShapeDtypeStruct(state.shape, F32)],
        compiler_params=_params(("parallel",)),
        name="hgrn_dec",
    )(q_c, z_c, lb_c, i_r, hg_r, onorm, state)


def _sort_key(s):
    s = jnp.where(s == 0.0, 0.0, s)
    bits = pltpu.bitcast(s, I32)
    return jnp.where(bits < 0, bits ^ np.int32(0x7FFFFFFF), bits)


def _lane_fold(x):
    part = x[:, :LANES]
    for cidx in range(1, x.shape[1] // LANES):
        part = part + x[:, cidx * LANES:(cidx + 1) * LANES]
    return part


def _select_bias(key, thr, idx, j_lim):
    return jnp.where(key > thr, 0.0, jnp.where(key == thr, jnp.where(idx < j_lim, 0.0, NEG_BIG), NEG_BIG))


def _topk_threshold(count, n_rows, n_sel, idx_bits, idx_sentinel):
    k = float(n_sel)

    def bit_body(it, thr):
        cand = thr + lax.shift_left(jnp.int32(1), 31 - it)
        cnt = count(lambda key, idx: jnp.where(key >= cand, 1.0, 0.0))
        return jnp.where(cnt >= k, cand, thr)

    thr = lax.fori_loop(0, 32, bit_body, jnp.full((n_rows, 1), INT_MIN, I32))
    n_ge = count(lambda key, idx: jnp.where(key >= thr, 1.0, 0.0))
    n_gt = count(lambda key, idx: jnp.where(key > thr, 1.0, 0.0))
    need = k - n_gt
    live = thr > INT_MIN
    tie = jnp.where(live, jnp.where(n_ge > k, 1.0, 0.0), 0.0)
    j_default = jnp.where(live, jnp.int32(idx_sentinel), jnp.int32(0))

    def tie_search():
        def idx_body(it, j_lim):
            cand = j_lim + lax.shift_left(jnp.int32(1), idx_bits - 1 - it)
            cnt = count(lambda key, idx: jnp.where(key == thr, jnp.where(idx < cand, 1.0, 0.0), 0.0))
            return jnp.where(cnt <= need, cand, j_lim)

        j_lim = lax.fori_loop(0, idx_bits, idx_body, jnp.zeros((n_rows, 1), I32))
        return jnp.where(tie > 0.0, j_lim, j_default)

    j_lim = lax.cond(jnp.max(tie) > 0.0, tie_search, lambda: j_default)
    return thr, j_lim


def _dsa_kernel(q_ref, qi_ref, ikw_ref, k_ref, v_ref, ki_ref, o_ref, keys_scr, m_scr, l_scr, acc_scr,
                *, tq, tk, n_sel, t_len):
    i = pl.program_id(1)
    nk = ((i + 1) * tq + tk - 1) // tk
    qpos = i * tq + lax.broadcasted_iota(I32, (tq, tk), 0)
    lane = lax.broadcasted_iota(I32, (tq, tk), 1)
    wi = ikw_ref[:, IDX_DIM:IDX_DIM + IDX_HEADS]

    def rows(j):
        return pl.ds(pl.multiple_of(j * tk, tk), tk)

    def score_tile(j, carry):
        kt = ki_ref[rows(j), :]
        s = jnp.zeros((tq, tk), F32)
        for h in range(IDX_HEADS):
            rel = jnp.maximum(_dot_nt(qi_ref[:, h * IDX_DIM:(h + 1) * IDX_DIM], kt), 0.0)
            s = s + wi[:, h:h + 1] * rel
        s = s * IDX_SCALE
        kpos = j * tk + lane
        cand = jnp.where(kpos <= qpos, s, NEG_BIG) > 0.5 * NEG_BIG
        keys_scr[j] = jnp.where(cand, _sort_key(s), INT_MIN)
        return carry

    lax.fori_loop(0, nk, score_tile, 0)

    def count(pred):
        def body(j, acc):
            return acc + _lane_fold(pred(keys_scr[j], j * tk + lane))

        acc = lax.fori_loop(0, nk, body, jnp.zeros((tq, LANES), F32))
        return jnp.sum(acc, axis=1, keepdims=True)

    thr, j_lim = _topk_threshold(count, tq, n_sel, int(t_len).bit_length(), t_len)

    m_scr[...] = jnp.full(m_scr.shape, 0.5 * NEG_BIG, F32)
    l_scr[...] = jnp.zeros_like(l_scr)
    acc_scr[...] = jnp.zeros_like(acc_scr)

    def attend_tile(j, carry):
        bias = _select_bias(keys_scr[j], thr, j * tk + lane, j_lim)
        k_t = k_ref[rows(j), :]
        v_t = v_ref[rows(j), :]
        for hh in range(N_HEADS):
            n = hh // GROUP
            kv = slice(n * HEAD_DIM, (n + 1) * HEAD_DIM)
            lg = _dot_nt(q_ref[:, hh * HEAD_DIM:(hh + 1) * HEAD_DIM], k_t[:, kv]) + bias
            m_old = m_scr[hh]
            m_new = jnp.maximum(m_old, jnp.max(lg, axis=1, keepdims=True))
            p = jnp.exp(lg - m_new)
            alpha = jnp.exp(m_old - m_new)
            l_scr[hh] = alpha * l_scr[hh] + jnp.sum(p, axis=1, keepdims=True)
            acc_scr[hh] = alpha * acc_scr[hh] + _dot(p.astype(BF16), v_t[:, kv])
            m_scr[hh] = m_new
        return carry

    lax.fori_loop(0, nk, attend_tile, 0)
    for hh in range(N_HEADS):
        o_ref[:, hh * HEAD_DIM:(hh + 1) * HEAD_DIM] = acc_scr[hh] / l_scr[hh]


def _dsa_prompt(q_r, qi_r, p, k_b, v_b, ki_b, b, t_len, n_sel):
    tq = min(256, t_len)
    tk = min(512, t_len)
    nq = t_len // tq
    iw = IDX_HEADS * IDX_DIM

    def full(width):
        return pl.BlockSpec((t_len, width), lambda bi, qi: (bi, 0))

    return pl.pallas_call(
        functools.partial(_dsa_kernel, tq=tq, tk=tk, n_sel=n_sel, t_len=t_len),
        grid=(b, nq),
        in_specs=[pl.BlockSpec((tq, ATTN_WIDTH), lambda bi, qi: (bi * nq + qi, 0)),
                  pl.BlockSpec((tq, iw), lambda bi, qi: (bi * nq + qi, 0)),
                  pl.BlockSpec((tq, LANES), lambda bi, qi: (bi * nq + qi, OFF_IKW // LANES)),
                  full(KV_WIDTH), full(KV_WIDTH), full(IDX_DIM)],
        out_specs=pl.BlockSpec((tq, ATTN_WIDTH), lambda bi, qi: (bi * nq + qi, 0)),
        out_shape=jax.ShapeDtypeStruct((b * t_len, ATTN_WIDTH), F32),
        scratch_shapes=[pltpu.VMEM((t_len // tk, tq, tk), I32),
                        pltpu.VMEM((N_HEADS, tq, 1), F32),
                        pltpu.VMEM((N_HEADS, tq, 1), F32),
                        pltpu.VMEM((N_HEADS, tq, HEAD_DIM), F32)],
        compiler_params=_params(("parallel", "arbitrary")),
        name="dsa",
    )(q_r, qi_r, p, k_b, v_b, ki_b)


def _dsa_dec_score_kernel(pt_ref, qi_ref, wi_ref, kin_ref, *rest, n_pages):
    page_refs = rest[:n_pages]
    o_ref = rest[n_pages]
    qi = qi_ref[...]
    wi = wi_ref[...]

    def finish(rel):
        s = jnp.sum(wi * jnp.maximum(rel, 0.0), axis=0, keepdims=True) * IDX_SCALE
        return jnp.where(s > 0.5 * NEG_BIG, _sort_key(s), INT_MIN)

    for pg in range(n_pages):
        o_ref[:, pg * PAGE_SIZE:(pg + 1) * PAGE_SIZE] = finish(_dot_nt(qi, page_refs[pg][...].astype(BF16)))
    kin = kin_ref[...].astype(BF16).astype(F32)
    rel_new = jnp.sum(qi.astype(F32) * kin, axis=1, keepdims=True)
    key_new = finish(rel_new)
    lane = lax.broadcasted_iota(I32, (1, PAGE_SIZE), 1)
    o_ref[:, n_pages * PAGE_SIZE:] = jnp.where(lane == 0, key_new, INT_MIN)


def _dsa_dec_scores(page_table, qi8, wi8, ki_new, cache_ki, l):
    n, n_pages = page_table.shape
    width = (n_pages + 1) * PAGE_SIZE
    page_specs = [pl.BlockSpec((None, None, PAGE_SIZE, IDX_DIM),
                               functools.partial(lambda bi, pt, pg: (l, pt[bi, pg], 0, 0), pg=pg))
                  for pg in range(n_pages)]
    return pl.pallas_call(
        functools.partial(_dsa_dec_score_kernel, n_pages=n_pages),
        grid_spec=pltpu.PrefetchScalarGridSpec(
            num_scalar_prefetch=1,
            grid=(n,),
            in_specs=[pl.BlockSpec((None, 8, IDX_DIM), lambda bi, pt: (bi, 0, 0)),
                      pl.BlockSpec((None, 8, 1), lambda bi, pt: (bi, 0, 0)),
                      pl.BlockSpec((None, 1, IDX_DIM), lambda bi, pt: (bi, 0, 0))] + page_specs,
            out_specs=pl.BlockSpec((None, 1, width), lambda bi, pt: (bi, 0, 0))),
        out_shape=jax.ShapeDtypeStruct((n, 1, width), I32),
        compiler_params=_params(("arbitrary",)),
        name="dsa_dec_scores",
    )(page_table, qi8, wi8, ki_new, *([cache_ki] * n_pages))


def _dsa_dec_thr_kernel(keys_ref, thr_ref, j_ref, *, n_sel):
    n, width = keys_ref.shape
    lane = lax.broadcasted_iota(I32, (n, width), 1)

    def count(pred):
        return jnp.sum(_lane_fold(pred(keys_ref[...], lane)), axis=1, keepdims=True)

    thr, j_lim = _topk_threshold(count, n, n_sel, int(width).bit_length(), width)
    thr_ref[...] = thr
    j_ref[...] = j_lim


def _dsa_dec_thr(keys2d, n_sel):
    n, width = keys2d.shape
    return pl.pallas_call(
        functools.partial(_dsa_dec_thr_kernel, n_sel=n_sel),
        grid=(1,),
        in_specs=[pl.BlockSpec((n, width), lambda i: (0, 0))],
        out_specs=[pl.BlockSpec((n, 1), lambda i: (0, 0)), pl.BlockSpec((n, 1), lambda i: (0, 0))],
        out_shape=[jax.ShapeDtypeStruct((n, 1), I32), jax.ShapeDtypeStruct((n, 1), I32)],
        compiler_params=_params(("arbitrary",)),
        name="dsa_dec_thr",
    )(keys2d)


def _dsa_dec_attn_kernel(pt_ref, thr_ref, j_ref, q_ref, keys_ref, kn_ref, vn_ref, *rest, n_pages):
    k_pages = rest[:n_pages]
    v_pages = rest[n_pages:2 * n_pages]
    o_ref, k_scr, v_scr = rest[2 * n_pages:]
    bi = pl.program_id(0)
    for pg in range(n_pages):
        sl = slice(pg * PAGE_SIZE, (pg + 1) * PAGE_SIZE)
        k_scr[sl, :] = k_pages[pg][...].astype(BF16)
        v_scr[sl, :] = v_pages[pg][...].astype(BF16)
    row = lax.broadcasted_iota(I32, (PAGE_SIZE, KV_WIDTH), 0)
    k_scr[n_pages * PAGE_SIZE:, :] = jnp.where(row == 0, kn_ref[...], 0.0).astype(BF16)
    v_scr[n_pages * PAGE_SIZE:, :] = jnp.where(row == 0, vn_ref[...], 0.0).astype(BF16)
    key = keys_ref[...]
    idx = lax.broadcasted_iota(I32, key.shape, 1)
    bias = _select_bias(key, thr_ref[bi], idx, j_ref[bi])
    for n in range(N_KV_HEADS):
        kv = slice(n * HEAD_DIM, (n + 1) * HEAD_DIM)
        lg = _dot_nt(q_ref[n], k_scr[:, kv]) + bias
        p = jnp.exp(lg - jnp.max(lg, axis=1, keepdims=True))
        o_ref[n] = _dot(p.astype(BF16), v_scr[:, kv]) / jnp.sum(p, axis=1, keepdims=True)


def _dsa_dec_attn(page_table, thr, j_lim, q8, keys, k_new, v_new, cache_k, cache_v, l):
    n, n_pages = page_table.shape
    width = (n_pages + 1) * PAGE_SIZE

    def page_spec(pg):
        return pl.BlockSpec((None, None, PAGE_SIZE, KV_WIDTH),
                            lambda bi, pt, th, jl: (l, pt[bi, pg], 0, 0))

    page_specs = [page_spec(pg) for pg in range(n_pages)]
    qspec = pl.BlockSpec((None, N_KV_HEADS, 8, HEAD_DIM), lambda bi, pt, th, jl: (bi, 0, 0, 0))
    return pl.pallas_call(
        functools.partial(_dsa_dec_attn_kernel, n_pages=n_pages),
        grid_spec=pltpu.PrefetchScalarGridSpec(
            num_scalar_prefetch=3,
            grid=(n,),
            in_specs=[qspec,
                      pl.BlockSpec((None, 1, width), lambda bi, pt, th, jl: (bi, 0, 0)),
                      pl.BlockSpec((None, 1, KV_WIDTH), lambda bi, pt, th, jl: (bi, 0, 0)),
                      pl.BlockSpec((None, 1, KV_WIDTH), lambda bi, pt, th, jl: (bi, 0, 0))]
            + page_specs + page_specs,
            out_specs=qspec,
            scratch_shapes=[pltpu.VMEM((width, KV_WIDTH), BF16), pltpu.VMEM((width, KV_WIDTH), BF16)]),
        out_shape=jax.ShapeDtypeStruct((n, N_KV_HEADS, 8, HEAD_DIM), F32),
        compiler_params=_params(("arbitrary",)),
        name="dsa_dec_attn",
    )(page_table, thr, j_lim, q8, keys, k_new, v_new, *([cache_k] * n_pages), *([cache_v] * n_pages))


def _merge_kernel(x_ref, g0_ref, g1_ref, g2_ref, ya_ref, yb_ref, yc_ref, gt_ref,
                  wa_ref, wb_ref, wc_ref, wo_ref, o_ref):
    merged = (jax.nn.sigmoid(g0_ref[...]) * _dot(ya_ref[...].astype(BF16), wa_ref[...])
              + jax.nn.sigmoid(g1_ref[...]) * _dot(yb_ref[...].astype(BF16), wb_ref[...])
              + jax.nn.sigmoid(g2_ref[...]) * _dot(yc_ref[...].astype(BF16), wc_ref[...]))
    o_ref[...] = x_ref[...] + gt_ref[...] * _dot(merged.astype(BF16), wo_ref[...])


def _merge(x, p, ya, yb, yc, mods, w_ba, w_bb, w_bc, w_out, l, rows_per_seq):
    n = x.shape[0]
    tm = _row_tile(n, mods, rows_per_seq, 512)

    def wspec(rows):
        return pl.BlockSpec((None, rows, D_MODEL), lambda i: (l, 0, 0))

    def row(width):
        return pl.BlockSpec((tm, width), lambda i: (i, 0))

    def gate(bidx):
        return pl.BlockSpec((tm, D_MODEL), lambda i: (i, OFF_GL // D_MODEL + bidx))

    return pl.pallas_call(
        _merge_kernel,
        grid=(n // tm,),
        in_specs=[row(D_MODEL), gate(0), gate(1), gate(2), row(POOL_WIDTH), row(HGRN_HEADS * HGRN_V),
                  row(ATTN_WIDTH), _mod_spec(mods, l, 5, tm, rows_per_seq),
                  wspec(POOL_WIDTH), wspec(HGRN_HEADS * HGRN_V), wspec(ATTN_WIDTH), wspec(D_MODEL)],
        out_specs=row(D_MODEL),
        out_shape=jax.ShapeDtypeStruct((n, D_MODEL), F32),
        compiler_params=_params(("parallel",)),
        name="merge",
    )(x, p, p, p, ya, yb, yc, mods, w_ba, w_bb, w_bc, w_out)


def _permute_w_in(w_in):
    segs = [(2884, 5956), (256, 768), (768, 1280), (1792, 2304), (0, 256), (1280, 1536), (1536, 1792),
            (2560, 2816), (2304, 2432), (2432, 2560), (2816, 2884)]
    parts = [w_in[..., a:b] for a, b in segs]
    used = sum(b - a for a, b in segs)
    parts.append(jnp.zeros(w_in.shape[:-1] + (IN_PAD - used,), w_in.dtype))
    return jnp.concatenate(parts, axis=-1).astype(BF16)


def _lower_bounds(p):
    sm = jax.nn.softmax(p.astype(F32), axis=0)
    return jnp.cumsum(sm, axis=0) - sm[0:1]


def kernel(x_prompt, x_sample, c_prompt, c_sample, cache_k, cache_v, cache_ki, page_table, state_pool, state_hgrn, ada_w, ada_b, norm_g, ffn_wg, ffn_wu, ffn_wd, w_in, pool_w, pool_scale, hgrn_lb, hgrn_onorm, q_norm, k_norm, w_ba, w_bb, w_bc, w_out):
    bp, t_len, _ = x_prompt.shape
    bs = x_sample.shape[0]
    assert x_sample.shape[1] == 1
    n_pages = page_table.shape[1]
    past_len = n_pages * PAGE_SIZE
    n_phys = cache_k.shape[1]
    n_sel_p = min(TOPK_MAX, t_len // 4)
    n_sel_s = min(TOPK_MAX, (past_len + 1) // 4)

    wg, wu, wd = ffn_wg.astype(BF16), ffn_wu.astype(BF16), ffn_wd.astype(BF16)
    w_in_p = _permute_w_in(w_in)
    wa, wb, wc, wo = w_ba.astype(BF16), w_bb.astype(BF16), w_bc.astype(BF16), w_out.astype(BF16)
    eye = jnp.eye(len(POOL_WINDOWS), dtype=F32)
    pool_wbd = jnp.einsum('lgcd,gh->lgchd', pool_w, eye).reshape(DEPTH, POOL_WIDTH, POOL_WIDTH).astype(BF16)
    pool_sc = pool_scale.reshape(DEPTH, 1, POOL_WIDTH)
    norm_g4 = norm_g.reshape(DEPTH, N_SUB, 1, D_MODEL)
    lbs = _lower_bounds(hgrn_lb)
    qg = jnp.tile(q_norm, (1, LANES // HEAD_DIM)).reshape(DEPTH, 1, LANES)
    kg = jnp.tile(k_norm, (1, LANES // HEAD_DIM)).reshape(DEPTH, 1, LANES)
    gid = np.arange(LANES) // HEAD_DIM
    bd = jnp.asarray((gid[:, None] == gid[None, :]).astype(np.float32), BF16)
    mstack = _hgrn_consts(HGRN_CHUNK)
    cos_p, sin_p = _rope_tables(jnp.arange(t_len, dtype=I32))
    cos_s, sin_s = _rope_tables(jnp.full((bs,), past_len, I32))
    ck = cache_k.reshape(DEPTH, n_phys, PAGE_SIZE, KV_WIDTH)
    cv = cache_v.reshape(DEPTH, n_phys, PAGE_SIZE, KV_WIDTH)

    rows_all = bp + bs
    rows_pad = -(-rows_all // 8) * 8
    c_all = jnp.concatenate([c_prompt, c_sample, jnp.zeros((rows_pad - rows_all, D_MODEL), F32)], axis=0)
    mods = _ada(c_all, ada_w, ada_b)
    mods_p = mods[:, :bp].reshape(DEPTH, bp, 1, 3 * N_SUB * D_MODEL)
    mods_s = mods[:, bp:rows_all]

    xp = x_prompt.reshape(bp * t_len, D_MODEL)
    xs = x_sample.reshape(bs, D_MODEL)
    pool_prev_p = jnp.zeros((bp, 16, POOL_WIDTH), F32)
    outs = [[] for _ in range(10)]
    tm_prep = min(512, t_len)

    for l in range(DEPTH):
        xp = _ffn(xp, mods_p, norm_g4, wg, wu, wd, l, 0, 0, t_len)
        p = _inproj(xp, mods_p, norm_g4, w_in_p, l, t_len)
        ya = _pool_prompt(p, pool_prev_p, pool_wbd, pool_sc, l, bp, t_len)
        yb, st_t = _hgrn_prompt(p, lbs[l].reshape(1, -1), hgrn_onorm[l].reshape(1, -1), mstack, bp, t_len)
        q_r, k_r, k_b, v_b, qi_r, ki_r, ki_b = _prep(p, cos_p, sin_p, qg[l], kg[l], bd, tm_prep, t_len // tm_prep)
        yc = _dsa_prompt(q_r, qi_r, p, k_b, v_b, ki_b, bp, t_len, n_sel_p)
        xp = _merge(xp, p, ya, yb, yc, mods_p, wa, wb, wc, wo, l, t_len)
        xp = _ffn(xp, mods_p, norm_g4, wg, wu, wd, l, 1, 2, t_len)
        p3 = p.reshape(bp, t_len, IN_PAD)
        outs[0].append(k_r.reshape(bp, t_len, N_KV_HEADS, HEAD_DIM))
        outs[1].append(p3[:, :, OFF_AV:OFF_AV + KV_WIDTH].reshape(bp, t_len, N_KV_HEADS, HEAD_DIM))
        outs[2].append(ki_r.reshape(bp, t_len, IDX_DIM))
        outs[3].append(p3[:, t_len - POOL_STATE:, OFF_UP:OFF_UP + POOL_WIDTH])
        outs[4].append(jnp.swapaxes(st_t, -1, -2))

        xs = _ffn(xs, mods_s, norm_g4, wg, wu, wd, l, 0, 0, 1)
        ps = _inproj(xs, mods_s, norm_g4, w_in_p, l, 1)
        ya = _pool_dec(ps, jnp.swapaxes(state_pool[l], 0, 1), pool_wbd, pool_sc, l, past_len)
        kw = HGRN_HEADS * HGRN_K
        vw = HGRN_HEADS * HGRN_V
        yb4, st_new = _hgrn_dec(
            ps[:, OFF_HQ:OFF_HQ + kw].reshape(bs, HGRN_HEADS, HGRN_K, 1),
            ps[:, OFF_HF:OFF_HF + kw].reshape(bs, HGRN_HEADS, HGRN_K, 1),
            lbs[l].reshape(HGRN_HEADS, HGRN_K, 1),
            ps[:, OFF_HI:OFF_HI + vw].reshape(bs, HGRN_HEADS, 1, HGRN_V),
            ps[:, OFF_HG:OFF_HG + vw].reshape(bs, HGRN_HEADS, 1, HGRN_V),
            hgrn_onorm[l].reshape(1, -1), state_hgrn[l])
        yb = yb4.reshape(bs, vw)
        q_r, k_r, _, _, qi_r, ki_r, _ = _prep(ps, cos_s, sin_s, qg[l], kg[l], bd, bs, 1)
        v_new = ps[:, OFF_AV:OFF_AV + KV_WIDTH]
        qi8 = jnp.pad(qi_r.reshape(bs, IDX_HEADS, IDX_DIM), ((0, 0), (0, 8 - IDX_HEADS), (0, 0)))
        wi8 = jnp.pad(ps[:, OFF_IKW + IDX_DIM:OFF_IKW + IDX_DIM + IDX_HEADS].reshape(bs, IDX_HEADS, 1),
                      ((0, 0), (0, 8 - IDX_HEADS), (0, 0)))
        keys = _dsa_dec_scores(page_table, qi8, wi8, ki_r.reshape(bs, 1, IDX_DIM), cache_ki, l)
        thr, j_lim = _dsa_dec_thr(keys.reshape(bs, -1), n_sel_s)
        q8 = jnp.pad(q_r.reshape(bs, N_KV_HEADS, GROUP, HEAD_DIM), ((0, 0), (0, 0), (0, 8 - GROUP), (0, 0)))
        yc8 = _dsa_dec_attn(page_table, thr.reshape(bs), j_lim.reshape(bs), q8, keys,
                            k_r.reshape(bs, 1, KV_WIDTH), v_new.reshape(bs, 1, KV_WIDTH), ck, cv, l)
        yc = yc8[:, :, :GROUP, :].reshape(bs, ATTN_WIDTH)
        xs = _merge(xs, ps, ya, yb, yc, mods_s, wa, wb, wc, wo, l, 1)
        xs = _ffn(xs, mods_s, norm_g4, wg, wu, wd, l, 1, 2, 1)
        outs[5].append(k_r.reshape(bs, 1, N_KV_HEADS, HEAD_DIM))
        outs[6].append(v_new.reshape(bs, 1, N_KV_HEADS, HEAD_DIM))
        outs[7].append(ki_r.reshape(bs, 1, IDX_DIM))
        outs[8].append(jnp.concatenate([state_pool[l][:, 1:], ps[:, None, OFF_UP:OFF_UP + POOL_WIDTH]], axis=1))
        outs[9].append(st_new)

    stacked = [jnp.stack(o) for o in outs]
    return (xp.reshape(bp, t_len, D_MODEL), xs.reshape(bs, 1, D_MODEL), *stacked)
```

```python
import functools

import numpy as np
import jax
import jax.numpy as jnp
from jax import lax
from jax.experimental import pallas as pl
from jax.experimental.pallas import tpu as pltpu

F32 = jnp.float32
BF16 = jnp.bfloat16
I32 = jnp.int32

D_MODEL = 1024
DEPTH = 4
PAGE_SIZE = 128
N_SUB = 3
POOL_WINDOWS = (2, 4, 8, 16)
POOL_GROUP_DIM = 64
POOL_WIDTH = 256
POOL_STATE = 15
HGRN_HEADS = 4
HGRN_K = 128
HGRN_V = 64
F_FLOOR = 1e-30
N_HEADS = 8
N_KV_HEADS = 2
HEAD_DIM = 64
GROUP = N_HEADS // N_KV_HEADS
ATTN_WIDTH = 512
KV_WIDTH = 128
IDX_HEADS = 4
IDX_DIM = 64
TOPK_MAX = 256
ROPE_THETA = 10000.0
ATTN_SCALE = HEAD_DIM ** -0.5
IDX_SCALE = (IDX_DIM * IDX_HEADS) ** -0.5
NEG_BIG = -1e30
D_FF = 2816
EPS = 1e-6
INT_MIN = np.int32(-2147483648)

LANES = 128
VMEM_LIMIT = 48 * 1024 * 1024

OFF_GL, OFF_HQ, OFF_HF, OFF_AQ = 0, 3072, 3584, 4096
OFF_UP, OFF_HI, OFF_HG, OFF_IQ = 4608, 4864, 5120, 5376
OFF_AK, OFF_AV, OFF_IKW = 5632, 5760, 5888
IN_PAD = 6144
HGRN_CHUNK = 128
HGRN_LEVELS = (16, 32, 64, HGRN_CHUNK)


def _dot(a, b):
    return jnp.dot(a, b, preferred_element_type=F32)


def _dot_nt(a, b):
    return lax.dot_general(a, b, (((1,), (1,)), ((), ())), preferred_element_type=F32)


def _dot_tn(a, b):
    return lax.dot_general(a, b, (((0,), (0,)), ((), ())), preferred_element_type=F32)


def _split2(x):
    hi = x.astype(BF16)
    lo = (x - hi.astype(F32)).astype(BF16)
    return hi, lo


def _split3(x):
    hi = x.astype(BF16)
    r = x - hi.astype(F32)
    mid = r.astype(BF16)
    lo = (r - mid.astype(F32)).astype(BF16)
    return hi, mid, lo


def _silu(x):
    return x * jax.nn.sigmoid(x)


def _params(sem):
    return pltpu.CompilerParams(dimension_semantics=sem, vmem_limit_bytes=VMEM_LIMIT)


def _mod_spec(mods, l, m, tm, rows_per_seq):
    if mods.ndim == 4:
        return pl.BlockSpec((None, None, 1, D_MODEL), lambda i, *_: (l, (i * tm) // rows_per_seq, 0, m))
    return pl.BlockSpec((None, tm, D_MODEL), lambda i, *_: (l, i, m))


def _row_tile(n, mods, rows_per_seq, cap):
    return min(cap, rows_per_seq if mods.ndim == 4 else n)


def _prenorm(x, g, scale, shift):
    y = x * lax.rsqrt(jnp.mean(x * x, axis=-1, keepdims=True) + EPS) * g
    return y * (1.0 + scale) + shift


def _ada_kernel(c_ref, w_ref, b_ref, o_ref):
    c = c_ref[...]
    o_ref[...] = jnp.dot(_silu(c), w_ref[...], precision=lax.Precision.HIGHEST,
                         preferred_element_type=F32) + b_ref[...]


def _ada(c_all, ada_w, ada_b):
    rows = c_all.shape[0]
    width = ada_w.shape[-1]
    tn = 1024
    return pl.pallas_call(
        _ada_kernel,
        grid=(DEPTH, width // tn),
        in_specs=[pl.BlockSpec((rows, D_MODEL), lambda l, j: (0, 0)),
                  pl.BlockSpec((None, D_MODEL, tn), lambda l, j: (l, 0, j)),
                  pl.BlockSpec((None, 1, tn), lambda l, j: (l, 0, j))],
        out_specs=pl.BlockSpec((None, rows, tn), lambda l, j: (l, 0, j)),
        out_shape=jax.ShapeDtypeStruct((DEPTH, rows, width), F32),
        compiler_params=_params(("parallel", "parallel")),
        name="ada",
    )(c_all, ada_w, ada_b.reshape(DEPTH, 1, width))


def _ffn_kernel(x_ref, sh_ref, sc_ref, gt_ref, g_ref, wg_ref, wu_ref, wd_ref, o_ref, h_scr, acc_scr):
    j = pl.program_id(1)

    @pl.when(j == 0)
    def _():
        h_scr[...] = _prenorm(x_ref[...], g_ref[...], sc_ref[...], sh_ref[...]).astype(BF16)
        acc_scr[...] = jnp.zeros_like(acc_scr)

    h = h_scr[...]
    act = (_silu(_dot(h, wg_ref[...])) * _dot(h, wu_ref[...])).astype(BF16)
    acc_scr[...] += _dot(act, wd_ref[...])

    @pl.when(j == pl.num_programs(1) - 1)
    def _():
        o_ref[...] = x_ref[...] + 0.5 * gt_ref[...] * acc_scr[...]


def _ffn(x, mods, norm_g, wg, wu, wd, l, which, sub, rows_per_seq):
    n = x.shape[0]
    tm = _row_tile(n, mods, rows_per_seq, 1024)
    tf = 256
    return pl.pallas_call(
        _ffn_kernel,
        grid=(n // tm, D_FF // tf),
        in_specs=[pl.BlockSpec((tm, D_MODEL), lambda i, j: (i, 0)),
                  _mod_spec(mods, l, 3 * sub, tm, rows_per_seq),
                  _mod_spec(mods, l, 3 * sub + 1, tm, rows_per_seq),
                  _mod_spec(mods, l, 3 * sub + 2, tm, rows_per_seq),
                  pl.BlockSpec((None, None, 1, D_MODEL), lambda i, j: (l, sub, 0, 0)),
                  pl.BlockSpec((None, None, D_MODEL, tf), lambda i, j: (l, which, 0, j)),
                  pl.BlockSpec((None, None, D_MODEL, tf), lambda i, j: (l, which, 0, j)),
                  pl.BlockSpec((None, None, tf, D_MODEL), lambda i, j: (l, which, j, 0))],
        out_specs=pl.BlockSpec((tm, D_MODEL), lambda i, j: (i, 0)),
        out_shape=jax.ShapeDtypeStruct((n, D_MODEL), F32),
        scratch_shapes=[pltpu.VMEM((tm, D_MODEL), BF16), pltpu.VMEM((tm, D_MODEL), F32)],
        compiler_params=_params(("parallel", "arbitrary")),
        name="ffn",
    )(x, mods, mods, mods, norm_g, wg, wu, wd)


def _inproj_kernel(x_ref, sh_ref, sc_ref, g_ref, w_ref, o_ref, h_scr):
    @pl.when(pl.program_id(1) == 0)
    def _():
        h_scr[...] = _prenorm(x_ref[...], g_ref[...], sc_ref[...], sh_ref[...]).astype(BF16)

    o_ref[...] = _dot(h_scr[...], w_ref[...])


def _inproj(x, mods, norm_g, w_in, l, rows_per_seq):
    n = x.shape[0]
    tm = _row_tile(n, mods, rows_per_seq, 1024)
    tn = 1536
    return pl.pallas_call(
        _inproj_kernel,
        grid=(n // tm, IN_PAD // tn),
        in_specs=[pl.BlockSpec((tm, D_MODEL), lambda i, j: (i, 0)),
                  _mod_spec(mods, l, 3, tm, rows_per_seq),
                  _mod_spec(mods, l, 4, tm, rows_per_seq),
                  pl.BlockSpec((None, None, 1, D_MODEL), lambda i, j: (l, 1, 0, 0)),
                  pl.BlockSpec((None, D_MODEL, tn), lambda i, j: (l, 0, j))],
        out_specs=pl.BlockSpec((tm, tn), lambda i, j: (i, j)),
        out_shape=jax.ShapeDtypeStruct((n, IN_PAD), F32),
        scratch_shapes=[pltpu.VMEM((tm, D_MODEL), BF16)],
        compiler_params=_params(("parallel", "arbitrary")),
        name="inproj",
    )(x, mods, mods, norm_g, w_in)


def _prep_kernel(aq_ref, ak_ref, av_ref, iq_ref, ikw_ref, cos_ref, sin_ref, qg_ref, kg_ref, bd_ref,
                 q_o, k_o, kb_o, vt_o, qi_o, ki_o, kib_o, wit_o):
    cos = cos_ref[...]
    sin = sin_ref[...]
    bd = bd_ref[...]
    lane = lax.broadcasted_iota(I32, cos.shape, 1)
    first_half = (lane & (HEAD_DIM // 2)) == 0

    def rope(x):
        partner = jnp.where(first_half, pltpu.roll(x, LANES - HEAD_DIM // 2, 1), pltpu.roll(x, HEAD_DIM // 2, 1))
        return x * cos + partner * sin

    def head_norm(x):
        hi, mid, lo = _split3(x * x)
        ms = (_dot(hi, bd) + _dot(mid, bd) + _dot(lo, bd)) * (1.0 / HEAD_DIM)
        return x * lax.rsqrt(ms + EPS)

    def put_heads(o_ref, first, y):
        for half in range(LANES // HEAD_DIM):
            o_ref[first + half] = y[:, half * HEAD_DIM:(half + 1) * HEAD_DIM].astype(BF16)

    qg = qg_ref[...]
    for c in range(ATTN_WIDTH // LANES):
        sl = slice(c * LANES, (c + 1) * LANES)
        put_heads(q_o, 2 * c, rope(head_norm(aq_ref[:, sl]) * qg) * ATTN_SCALE)
    k = rope(head_norm(ak_ref[...]) * kg_ref[...])
    k_o[...] = k
    put_heads(kb_o, 0, k)
    vt_o[...] = av_ref[...].T.astype(BF16)
    for c in range(IDX_HEADS * IDX_DIM // LANES):
        sl = slice(c * LANES, (c + 1) * LANES)
        put_heads(qi_o, 2 * c, rope(iq_ref[:, sl]))
    ikw = ikw_ref[...]
    ki = rope(head_norm(ikw))[:, :IDX_DIM]
    ki_o[...] = ki
    kib_o[...] = ki.astype(BF16)
    wit_o[...] = ikw.T[IDX_DIM:IDX_DIM + 8, :]


def _prep(p, cos, sin, qg, kg, bd, tm, table_blocks):
    n = p.shape[0]

    def col(off, width):
        return pl.BlockSpec((tm, width), lambda i: (i, off // width))

    tab = pl.BlockSpec((tm, LANES), lambda i: (i % table_blocks, 0))
    vec = pl.BlockSpec((1, LANES), lambda i: (0, 0))

    def row(width):
        return pl.BlockSpec((tm, width), lambda i: (i, 0))

    def heads(h):
        return pl.BlockSpec((h, tm, HEAD_DIM), lambda i: (0, i, 0))

    return pl.pallas_call(
        _prep_kernel,
        grid=(n // tm,),
        in_specs=[col(OFF_AQ, ATTN_WIDTH), col(OFF_AK, KV_WIDTH), col(OFF_AV, KV_WIDTH),
                  col(OFF_IQ, IDX_HEADS * IDX_DIM), col(OFF_IKW, LANES), tab, tab, vec, vec,
                  pl.BlockSpec((LANES, LANES), lambda i: (0, 0))],
        out_specs=[heads(N_HEADS), row(KV_WIDTH), heads(N_KV_HEADS),
                   pl.BlockSpec((None, KV_WIDTH, tm), lambda i: (i, 0, 0)),
                   heads(IDX_HEADS), row(IDX_DIM), row(IDX_DIM),
                   pl.BlockSpec((8, tm), lambda i: (0, i))],
        out_shape=[jax.ShapeDtypeStruct((N_HEADS, n, HEAD_DIM), BF16),
                   jax.ShapeDtypeStruct((n, KV_WIDTH), F32),
                   jax.ShapeDtypeStruct((N_KV_HEADS, n, HEAD_DIM), BF16),
                   jax.ShapeDtypeStruct((n // tm, KV_WIDTH, tm), BF16),
                   jax.ShapeDtypeStruct((IDX_HEADS, n, IDX_DIM), BF16),
                   jax.ShapeDtypeStruct((n, IDX_DIM), F32),
                   jax.ShapeDtypeStruct((n, IDX_DIM), BF16),
                   jax.ShapeDtypeStruct((8, n), F32)],
        compiler_params=_params(("parallel",)),
        name="prep",
    )(p, p, p, p, p, cos, sin, qg, kg, bd)


def _rope_tables(pos):
    half = HEAD_DIM // 2
    inv = ROPE_THETA ** (-jnp.arange(half, dtype=F32) / half)
    ang = pos.astype(F32)[:, None] * inv[None, :]
    c, s = jnp.cos(ang), jnp.sin(ang)
    return jnp.concatenate([c, c, c, c], axis=1), jnp.concatenate([-s, s, -s, s], axis=1)


def _pool_select(sums, lane):
    out = sums[-1]
    for gi in range(len(POOL_WINDOWS) - 2, -1, -1):
        out = jnp.where(lane < (gi + 1) * POOL_GROUP_DIM, sums[gi], out)
    return out


def _pool_kernel(u_ref, prev_ref, w_ref, s_ref, o_ref, carry_scr, *, tt, p0):
    t = pl.program_id(1)

    @pl.when(t == 0)
    def _():
        carry_scr[...] = prev_ref[...]

    u = u_ref[...]
    e = jnp.concatenate([carry_scr[...], u], axis=0)
    carry_scr[...] = u[tt - 16:, :]
    s2 = e[1:] + e[:-1]
    s4 = s2[2:] + s2[:-2]
    s8 = s4[4:] + s4[:-4]
    s16 = s8[8:] + s8[:-8]
    sums = (s2[15:15 + tt], s4[13:13 + tt], s8[9:9 + tt], s16[1:1 + tt])
    n_avail = (p0 + 1 + t * tt + lax.broadcasted_iota(I32, (tt, 1), 0)).astype(F32)
    lane = lax.broadcasted_iota(I32, (tt, POOL_WIDTH), 1)
    means = [sums[gi] / jnp.minimum(n_avail, float(w)) for gi, w in enumerate(POOL_WINDOWS)]
    pooled = _pool_select(means, lane) - u
    o_ref[...] = _dot(pooled.astype(BF16), w_ref[...]) * s_ref[...]


def _pool_prompt(p, prev16, wbd, scale, l, b, t_len):
    tt = min(512, t_len)
    nt = t_len // tt
    return pl.pallas_call(
        functools.partial(_pool_kernel, tt=tt, p0=0),
        grid=(b, nt),
        in_specs=[pl.BlockSpec((tt, POOL_WIDTH), lambda bi, ti: (bi * nt + ti, OFF_UP // POOL_WIDTH)),
                  pl.BlockSpec((None, 16, POOL_WIDTH), lambda bi, ti: (bi, 0, 0)),
                  pl.BlockSpec((None, POOL_WIDTH, POOL_WIDTH), lambda bi, ti: (l, 0, 0)),
                  pl.BlockSpec((None, 1, POOL_WIDTH), lambda bi, ti: (l, 0, 0))],
        out_specs=pl.BlockSpec((tt, POOL_WIDTH), lambda bi, ti: (bi * nt + ti, 0)),
        out_shape=jax.ShapeDtypeStruct((b * t_len, POOL_WIDTH), F32),
        scratch_shapes=[pltpu.VMEM((16, POOL_WIDTH), F32)],
        compiler_params=_params(("parallel", "arbitrary")),
        name="pool",
    )(p, prev16, wbd, scale)


def _pool_dec_kernel(u_ref, st_ref, w_ref, s_ref, o_ref, *, p0):
    u = u_ref[...]
    lane = lax.broadcasted_iota(I32, u.shape, 1)
    means = []
    acc = u
    d = 1
    for w in POOL_WINDOWS:
        while d < w:
            acc = acc + st_ref[POOL_STATE - d]
            d += 1
        means.append(acc / float(min(p0 + 1, w)))
    pooled = _pool_select(means, lane) - u
    o_ref[...] = _dot(pooled.astype(BF16), w_ref[...]) * s_ref[...]


def _pool_dec(p, st_t, wbd, scale, l, p0):
    n = p.shape[0]
    return pl.pallas_call(
        functools.partial(_pool_dec_kernel, p0=p0),
        grid=(1,),
        in_specs=[pl.BlockSpec((n, POOL_WIDTH), lambda i: (0, OFF_UP // POOL_WIDTH)),
                  pl.BlockSpec((POOL_STATE, n, POOL_WIDTH), lambda i: (0, 0, 0)),
                  pl.BlockSpec((None, POOL_WIDTH, POOL_WIDTH), lambda i: (l, 0, 0)),
                  pl.BlockSpec((None, 1, POOL_WIDTH), lambda i: (l, 0, 0))],
        out_specs=pl.BlockSpec((n, POOL_WIDTH), lambda i: (0, 0)),
        out_shape=jax.ShapeDtypeStruct((n, POOL_WIDTH), F32),
        compiler_params=_params(("arbitrary",)),
        name="pool_dec",
    )(p, st_t, wbd, scale)


def _hgrn_consts(c):
    r = np.arange(c)[:, None]
    s = np.arange(c)[None, :]
    mats = [(s <= r) & (s // h == r // h) for h in HGRN_LEVELS]
    mats += [(s > r) & (s // h == r // h) for h in HGRN_LEVELS]
    return jnp.asarray(np.concatenate(mats, axis=0).astype(np.float32), BF16)


def _hgrn_kernel(hq_ref, hf_ref, hi_ref, hg_ref, lb_ref, on_ref, ms_ref, yb_ref, st_ref, s_scr, *, c):
    step = pl.program_id(1)
    nl = len(HGRN_LEVELS)

    @pl.when(step == 0)
    def _():
        s_scr[...] = jnp.zeros_like(s_scr)

    ti = lax.broadcasted_iota(I32, (c, c), 0)
    si = lax.broadcasted_iota(I32, (c, c), 1)
    nb = c // 16
    tpos = lax.broadcasted_iota(I32, (1, 16, 1), 1)

    for hd in range(HGRN_HEADS):
        ks = slice(hd * HGRN_K, (hd + 1) * HGRN_K)
        vs = slice(hd * HGRN_V, (hd + 1) * HGRN_V)
        q = hq_ref[:, ks]
        lb = lb_ref[:, ks]
        f = lb + (1.0 - lb) * jax.nn.sigmoid(hf_ref[:, ks])
        g = jnp.log(jnp.maximum(f, F_FLOOR))
        kk = 1.0 - f
        iv = hi_ref[:, vs]
        ivb = iv.astype(BF16)
        g_hi, g_lo = _split2(g)
        r = _dot(ms_ref[...], jnp.concatenate([g_hi, g_lo], axis=1))
        r = r[:, :HGRN_K] + r[:, HGRN_K:]
        low = [r[k * c:(k + 1) * c] for k in range(nl)]
        up = [r[(nl + k) * c:(nl + k + 1) * c] for k in range(nl)]

        att = jnp.zeros((c, c), F32)
        for k, h in enumerate(HGRN_LEVELS[:-1]):
            qe = (q * jnp.exp(low[k])).astype(BF16)
            ke = (kk * jnp.exp(up[k])).astype(BF16)
            sh = h.bit_length() - 1
            tb = jnp.right_shift(ti, sh)
            pair = ((tb & 1) == 1) & (jnp.right_shift(si, sh) == tb - 1)
            att = jnp.where(pair, _dot_nt(qe, ke), att)
        qe = (q * jnp.exp(low[-1])).astype(BF16)
        ke = (kk * jnp.exp(up[-1])).astype(BF16)
        s_t = s_scr[hd]
        o = _dot(att.astype(BF16), ivb) + _dot_nt(qe, s_t.astype(BF16))

        l3 = low[0].reshape(nb, 16, HGRN_K)
        q3 = q.reshape(nb, 16, HGRN_K)
        k3 = kk.reshape(nb, 16, HGRN_K)
        i3 = iv.reshape(nb, 16, HGRN_V)
        o3 = jnp.zeros((nb, 16, HGRN_V), F32)
        for s in range(16):
            e = jnp.exp(jnp.minimum(l3 - l3[:, s:s + 1, :], 0.0))
            a = jnp.sum(q3 * (k3[:, s:s + 1, :] * e), axis=-1, keepdims=True)
            a = jnp.where(tpos >= s, a, 0.0)
            o3 = o3 + a * i3[:, s:s + 1, :]
        o = o + o3.reshape(c, HGRN_V)

        s_scr[hd] = s_t * jnp.exp(low[-1][c - 1:c, :]) + _dot_tn(ivb, ke)

        on = o * lax.rsqrt(jnp.mean(o * o, axis=-1, keepdims=True) + EPS) * on_ref[...]
        yb_ref[:, vs] = on * _silu(hg_ref[:, vs])

    @pl.when(step == pl.num_programs(1) - 1)
    def _():
        st_ref[...] = s_scr[...]


def _hgrn_prompt(p, lb, onorm, mstack, b, t_len):
    c = HGRN_CHUNK
    nc = t_len // c
    kw = HGRN_HEADS * HGRN_K
    vw = HGRN_HEADS * HGRN_V

    def col(off, width):
        return pl.BlockSpec((c, width), lambda bi, ci: (bi * nc + ci, off // width))

    return pl.pallas_call(
        functools.partial(_hgrn_kernel, c=c),
        grid=(b, nc),
        in_specs=[col(OFF_HQ, kw), col(OFF_HF, kw), col(OFF_HI, vw), col(OFF_HG, vw),
                  pl.BlockSpec((1, kw), lambda bi, ci: (0, 0)),
                  pl.BlockSpec((1, HGRN_V), lambda bi, ci: (0, 0)),
                  pl.BlockSpec(mstack.shape, lambda bi, ci: (0, 0))],
        out_specs=[pl.BlockSpec((c, vw), lambda bi, ci: (bi * nc + ci, 0)),
                   pl.BlockSpec((None, HGRN_HEADS, HGRN_V, HGRN_K), lambda bi, ci: (bi, 0, 0, 0))],
        out_shape=[jax.ShapeDtypeStruct((b * t_len, vw), F32),
                   jax.ShapeDtypeStruct((b, HGRN_HEADS, HGRN_V, HGRN_K), F32)],
        scratch_shapes=[pltpu.VMEM((HGRN_HEADS, HGRN_V, HGRN_K), F32)],
        compiler_params=_params(("parallel", "arbitrary")),
        name="hgrn",
    )(p, p, p, p, lb, onorm, mstack)


def _hgrn_dec_kernel(q_ref, z_ref, lb_ref, i_ref, hg_ref, on_ref, s_ref, yb_ref, so_ref):
    lb = lb_ref[...]
    f = lb + (1.0 - lb) * jax.nn.sigmoid(z_ref[...])
    a = jnp.exp(jnp.log(jnp.maximum(f, F_FLOOR)))
    s_new = a * s_ref[...] + (1.0 - f) * i_ref[...]
    so_ref[...] = s_new
    o = jnp.sum(q_ref[...] * s_new, axis=2, keepdims=True)
    on = o * lax.rsqrt(jnp.mean(o * o, axis=-1, keepdims=True) + EPS) * on_ref[...]
    yb_ref[...] = on * _silu(hg_ref[...])


def _hgrn_dec(q_c, z_c, lb_c, i_r, hg_r, onorm, state):
    n = state.shape[0]
    nb = 8
    col = pl.BlockSpec((nb, HGRN_HEADS, HGRN_K, 1), lambda i: (i, 0, 0, 0))
    row = pl.BlockSpec((nb, HGRN_HEADS, 1, HGRN_V), lambda i: (i, 0, 0, 0))
    st = pl.BlockSpec((nb, HGRN_HEADS, HGRN_K, HGRN_V), lambda i: (i, 0, 0, 0))
    return pl.pallas_call(
        _hgrn_dec_kernel,
        grid=(n // nb,),
        in_specs=[col, col, pl.BlockSpec((HGRN_HEADS, HGRN_K, 1), lambda i: (0, 0, 0)), row, row,
                  pl.BlockSpec((1, HGRN_V), lambda i: (0, 0)), st],
        out_specs=[row, st],
        out_shape=[jax.ShapeDtypeStruct((n, HGRN_HEADS, 1, HGRN_V), F32),
                   jax.ShapeDtypeStruct(state.shape, F32)],
        compiler_params=_params(("parallel",)),
        name="hgrn_dec",
    )(q_c, z_c, lb_c, i_r, hg_r, onorm, state)


def _sort_key(s):
    bits = pltpu.bitcast(s, I32)
    return jnp.where(bits < 0, INT_MIN - bits, bits)


def _lane_fold(x):
    part = x[:, :LANES]
    for cidx in range(1, x.shape[1] // LANES):
        part = part + x[:, cidx * LANES:(cidx + 1) * LANES]
    return part


def _select_bias(key, thr, idx, j_lim):
    return jnp.where(key > thr, 0.0, jnp.where(key == thr, jnp.where(idx < j_lim, 0.0, NEG_BIG), NEG_BIG))


def _topk_threshold(count, stat_shape, n_sel, idx_bits, idx_sentinel):
    k = float(n_sel)

    def bit_body(it, thr):
        cand = thr + lax.shift_left(jnp.int32(1), 31 - it)
        cnt = count(lambda key, idx: jnp.where(key >= cand, 1.0, 0.0))
        return jnp.where(cnt >= k, cand, thr)

    thr = lax.fori_loop(0, 32, bit_body, jnp.full(stat_shape, INT_MIN, I32))
    n_ge = count(lambda key, idx: jnp.where(key >= thr, 1.0, 0.0))
    n_gt = count(lambda key, idx: jnp.where(key > thr, 1.0, 0.0))
    need = k - n_gt
    live = thr > INT_MIN
    tie = jnp.where(live, jnp.where(n_ge > k, 1.0, 0.0), 0.0)
    j_default = jnp.where(live, jnp.int32(idx_sentinel), jnp.int32(0))

    def tie_search():
        def idx_body(it, j_lim):
            cand = j_lim + lax.shift_left(jnp.int32(1), idx_bits - 1 - it)
            cnt = count(lambda key, idx: jnp.where(key == thr, jnp.where(idx < cand, 1.0, 0.0), 0.0))
            return jnp.where(cnt <= need, cand, j_lim)

        j_lim = lax.fori_loop(0, idx_bits, idx_body, jnp.zeros(stat_shape, I32))
        return jnp.where(tie > 0.0, j_lim, j_default)

    j_lim = lax.cond(jnp.max(tie) > 0.0, tie_search, lambda: j_default)
    return thr, j_lim


FOLD_ROWS = 64


def _fold(x, op):
    return op(x.reshape(x.shape[0] // FOLD_ROWS, FOLD_ROWS, x.shape[1]), axis=0)


def _dsa_kernel(q_ref, qi_ref, wit_ref, k_ref, vt_ref, ki_ref, o_ref, keys_scr, m_scr, l_scr, acc_scr,
                *, tq, tk, n_sel, t_len):
    i = pl.program_id(1)
    nk = ((i + 1) * tq + tk - 1) // tk
    krow = lax.broadcasted_iota(I32, (tk, tq), 0)
    qpos = i * tq + lax.broadcasted_iota(I32, (tk, tq), 1)
    wit = wit_ref[...] * IDX_SCALE
    qi_all = qi_ref[...].reshape(IDX_HEADS * tq, IDX_DIM)

    def rows(j):
        return pl.ds(pl.multiple_of(j * tk, tk), tk)

    def score_tile(j, causal):
        rel = _dot_nt(ki_ref[rows(j), :], qi_all)
        s = wit[0:1, :] * jnp.maximum(rel[:, :tq], 0.0)
        for h in range(1, IDX_HEADS):
            s = s + wit[h:h + 1, :] * jnp.maximum(rel[:, h * tq:(h + 1) * tq], 0.0)
        key = jnp.where(s > 0.5 * NEG_BIG, _sort_key(s), INT_MIN)
        if causal:
            key = jnp.where(j * tk + krow <= qpos, key, INT_MIN)
        keys_scr[j] = key

    def score_body(j, carry):
        score_tile(j, False)
        return carry

    lax.fori_loop(0, nk - 1, score_body, 0)
    score_tile(nk - 1, True)

    def count(pred):
        def body(j, acc):
            return acc + _fold(pred(keys_scr[j], j * tk + krow), jnp.sum)

        acc = lax.fori_loop(0, nk, body, jnp.zeros((FOLD_ROWS, tq), F32))
        return jnp.sum(acc, axis=0, keepdims=True)

    thr, j_lim = _topk_threshold(count, (1, tq), n_sel, int(t_len).bit_length(), t_len)

    m_scr[...] = jnp.full(m_scr.shape, 0.5 * NEG_BIG, F32)
    l_scr[...] = jnp.zeros_like(l_scr)
    acc_scr[...] = jnp.zeros_like(acc_scr)

    def attend_tile(j, carry):
        bias = _select_bias(keys_scr[j], thr, j * tk + krow, j_lim)
        bias = jnp.concatenate([bias] * GROUP, axis=1)
        v_t = vt_ref[j]
        for n in range(N_KV_HEADS):
            qg = q_ref[n * GROUP:(n + 1) * GROUP].reshape(GROUP * tq, HEAD_DIM)
            lg = _dot_nt(k_ref[n, rows(j), :], qg) + bias
            m_old = m_scr[n]
            m_new = jnp.maximum(m_old, jnp.max(_fold(lg, jnp.max), axis=0, keepdims=True))
            p = jnp.exp(lg - m_new)
            alpha = jnp.exp(m_old - m_new)
            l_scr[n] = alpha * l_scr[n] + jnp.sum(_fold(p, jnp.sum), axis=0, keepdims=True)
            acc_scr[n] = alpha * acc_scr[n] + _dot(v_t[n * HEAD_DIM:(n + 1) * HEAD_DIM, :], p.astype(BF16))
            m_scr[n] = m_new
        return carry

    lax.fori_loop(0, nk, attend_tile, 0)
    outs = []
    for n in range(N_KV_HEADS):
        o_n = acc_scr[n] / l_scr[n]
        outs += [o_n[:, g * tq:(g + 1) * tq] for g in range(GROUP)]
    o_ref[...] = jnp.concatenate(outs, axis=0).T


def _dsa_prompt(q_h, qi_h, wit, k_h, v_t, ki_b, b, t_len, n_sel, tk):
    tq = min(256, t_len)
    nq = t_len // tq
    nt = t_len // tk

    def heads(h, rows, imap):
        return pl.BlockSpec((h, rows, HEAD_DIM), imap)

    return pl.pallas_call(
        functools.partial(_dsa_kernel, tq=tq, tk=tk, n_sel=n_sel, t_len=t_len),
        grid=(b, nq),
        in_specs=[heads(N_HEADS, tq, lambda bi, qi: (0, bi * nq + qi, 0)),
                  heads(IDX_HEADS, tq, lambda bi, qi: (0, bi * nq + qi, 0)),
                  pl.BlockSpec((8, tq), lambda bi, qi: (0, bi * nq + qi)),
                  heads(N_KV_HEADS, t_len, lambda bi, qi: (0, bi, 0)),
                  pl.BlockSpec((nt, KV_WIDTH, tk), lambda bi, qi: (bi, 0, 0)),
                  pl.BlockSpec((t_len, IDX_DIM), lambda bi, qi: (bi, 0))],
        out_specs=pl.BlockSpec((tq, ATTN_WIDTH), lambda bi, qi: (bi * nq + qi, 0)),
        out_shape=jax.ShapeDtypeStruct((b * t_len, ATTN_WIDTH), F32),
        scratch_shapes=[pltpu.VMEM((nt, tk, tq), I32),
                        pltpu.VMEM((N_KV_HEADS, 1, GROUP * tq), F32),
                        pltpu.VMEM((N_KV_HEADS, 1, GROUP * tq), F32),
                        pltpu.VMEM((N_KV_HEADS, HEAD_DIM, GROUP * tq), F32)],
        compiler_params=_params(("parallel", "arbitrary")),
        name="dsa",
    )(q_h, qi_h, wit, k_h, v_t, ki_b)


def _dsa_dec_score_kernel(pt_ref, qi_ref, wi_ref, kin_ref, *rest, n_pages):
    page_refs = rest[:n_pages]
    o_ref = rest[n_pages]
    qi = qi_ref[...]
    wi = wi_ref[...]

    def finish(rel):
        s = jnp.sum(wi * jnp.maximum(rel, 0.0), axis=0, keepdims=True) * IDX_SCALE
        return jnp.where(s > 0.5 * NEG_BIG, _sort_key(s), INT_MIN)

    for pg in range(n_pages):
        o_ref[:, pg * PAGE_SIZE:(pg + 1) * PAGE_SIZE] = finish(_dot(qi, page_refs[pg][...].astype(BF16)))
    kin = kin_ref[...].astype(BF16).astype(F32)
    rel_new = jnp.sum(qi.astype(F32) * kin, axis=1, keepdims=True)
    key_new = finish(rel_new)
    lane = lax.broadcasted_iota(I32, (1, PAGE_SIZE), 1)
    o_ref[:, n_pages * PAGE_SIZE:] = jnp.where(lane == 0, key_new, INT_MIN)


def _dsa_dec_scores(page_table, qi8, wi8, ki_new, cache_ki_t, l):
    n, n_pages = page_table.shape
    width = (n_pages + 1) * PAGE_SIZE
    page_specs = [pl.BlockSpec((None, None, IDX_DIM, PAGE_SIZE),
                               functools.partial(lambda bi, pt, pg: (l, pt[bi, pg], 0, 0), pg=pg))
                  for pg in range(n_pages)]
    return pl.pallas_call(
        functools.partial(_dsa_dec_score_kernel, n_pages=n_pages),
        grid_spec=pltpu.PrefetchScalarGridSpec(
            num_scalar_prefetch=1,
            grid=(n,),
            in_specs=[pl.BlockSpec((None, 8, IDX_DIM), lambda bi, pt: (bi, 0, 0)),
                      pl.BlockSpec((None, 8, 1), lambda bi, pt: (bi, 0, 0)),
                      pl.BlockSpec((None, 1, IDX_DIM), lambda bi, pt: (bi, 0, 0))] + page_specs,
            out_specs=pl.BlockSpec((None, 1, width), lambda bi, pt: (bi, 0, 0))),
        out_shape=jax.ShapeDtypeStruct((n, 1, width), I32),
        compiler_params=_params(("arbitrary",)),
        name="dsa_dec_scores",
    )(page_table, qi8, wi8, ki_new, *([cache_ki_t] * n_pages))


def _dsa_dec_thr_kernel(keys_ref, thr_ref, j_ref, *, n_sel):
    n, width = keys_ref.shape
    lane = lax.broadcasted_iota(I32, (n, width), 1)

    def count(pred):
        return jnp.sum(_lane_fold(pred(keys_ref[...], lane)), axis=1, keepdims=True)

    thr, j_lim = _topk_threshold(count, (n, 1), n_sel, int(width).bit_length(), width)
    thr_ref[...] = thr
    j_ref[...] = j_lim


def _dsa_dec_thr(keys2d, n_sel):
    n, width = keys2d.shape
    return pl.pallas_call(
        functools.partial(_dsa_dec_thr_kernel, n_sel=n_sel),
        grid=(1,),
        in_specs=[pl.BlockSpec((n, width), lambda i: (0, 0))],
        out_specs=[pl.BlockSpec((n, 1), lambda i: (0, 0)), pl.BlockSpec((n, 1), lambda i: (0, 0))],
        out_shape=[jax.ShapeDtypeStruct((n, 1), I32), jax.ShapeDtypeStruct((n, 1), I32)],
        compiler_params=_params(("arbitrary",)),
        name="dsa_dec_thr",
    )(keys2d)


def _dsa_dec_attn_kernel(pt_ref, thr_ref, j_ref, q_ref, keys_ref, kn_ref, vn_ref, *rest, n_pages):
    k_pages = rest[:n_pages]
    v_pages = rest[n_pages:2 * n_pages]
    o_ref, k_scr, v_scr = rest[2 * n_pages:]
    bi = pl.program_id(0)
    n_past = n_pages * PAGE_SIZE
    for pg in range(n_pages):
        sl = slice(pg * PAGE_SIZE, (pg + 1) * PAGE_SIZE)
        k_scr[:, sl] = k_pages[pg][...].astype(BF16)
        v_scr[:, sl] = v_pages[pg][...].astype(BF16)
    key = keys_ref[...]
    idx = lax.broadcasted_iota(I32, key.shape, 1)
    bias_all = _select_bias(key, thr_ref[bi], idx, j_ref[bi])
    bias = bias_all[:, :n_past]
    bias_new = bias_all[:, n_past:n_past + 1]
    kn = kn_ref[...].astype(BF16).astype(F32)
    vn = vn_ref[...].astype(BF16).astype(F32)
    for n in range(N_KV_HEADS):
        fs = slice(n * HEAD_DIM, (n + 1) * HEAD_DIM)
        q = q_ref[n]
        lg = _dot(q, k_scr[fs, :]) + bias
        lg_new = jnp.sum(q.astype(F32) * kn[:, fs], axis=1, keepdims=True) + bias_new
        m = jnp.maximum(jnp.max(lg, axis=1, keepdims=True), lg_new)
        p = jnp.exp(lg - m)
        p_new = jnp.exp(lg_new - m)
        den = jnp.sum(p, axis=1, keepdims=True) + p_new
        o_ref[n] = (_dot_nt(p.astype(BF16), v_scr[fs, :]) + p_new * vn[:, fs]) / den


def _dsa_dec_attn(page_table, thr, j_lim, q8, keys, k_new, v_new, cache_k_t, cache_v_t, l):
    n, n_pages = page_table.shape
    width = (n_pages + 1) * PAGE_SIZE

    def page_spec(pg):
        return pl.BlockSpec((None, None, KV_WIDTH, PAGE_SIZE),
                            lambda bi, pt, th, jl: (l, pt[bi, pg], 0, 0))

    page_specs = [page_spec(pg) for pg in range(n_pages)]
    qspec = pl.BlockSpec((None, N_KV_HEADS, 8, HEAD_DIM), lambda bi, pt, th, jl: (bi, 0, 0, 0))
    return pl.pallas_call(
        functools.partial(_dsa_dec_attn_kernel, n_pages=n_pages),
        grid_spec=pltpu.PrefetchScalarGridSpec(
            num_scalar_prefetch=3,
            grid=(n,),
            in_specs=[qspec,
                      pl.BlockSpec((None, 1, width), lambda bi, pt, th, jl: (bi, 0, 0)),
                      pl.BlockSpec((None, 1, KV_WIDTH), lambda bi, pt, th, jl: (bi, 0, 0)),
                      pl.BlockSpec((None, 1, KV_WIDTH), lambda bi, pt, th, jl: (bi, 0, 0))]
            + page_specs + page_specs,
            out_specs=qspec,
            scratch_shapes=[pltpu.VMEM((KV_WIDTH, n_pages * PAGE_SIZE), BF16),
                            pltpu.VMEM((KV_WIDTH, n_pages * PAGE_SIZE), BF16)]),
        out_shape=jax.ShapeDtypeStruct((n, N_KV_HEADS, 8, HEAD_DIM), F32),
        compiler_params=_params(("arbitrary",)),
        name="dsa_dec_attn",
    )(page_table, thr, j_lim, q8, keys, k_new, v_new, *([cache_k_t] * n_pages), *([cache_v_t] * n_pages))


def _merge_kernel(x_ref, g0_ref, g1_ref, g2_ref, ya_ref, yb_ref, yc_ref, gt_ref,
                  wa_ref, wb_ref, wc_ref, wo_ref, o_ref):
    merged = (jax.nn.sigmoid(g0_ref[...]) * _dot(ya_ref[...].astype(BF16), wa_ref[...])
              + jax.nn.sigmoid(g1_ref[...]) * _dot(yb_ref[...].astype(BF16), wb_ref[...])
              + jax.nn.sigmoid(g2_ref[...]) * _dot(yc_ref[...].astype(BF16), wc_ref[...]))
    o_ref[...] = x_ref[...] + gt_ref[...] * _dot(merged.astype(BF16), wo_ref[...])


def _merge(x, p, ya, yb, yc, mods, w_ba, w_bb, w_bc, w_out, l, rows_per_seq):
    n = x.shape[0]
    tm = _row_tile(n, mods, rows_per_seq, 512)

    def wspec(rows):
        return pl.BlockSpec((None, rows, D_MODEL), lambda i: (l, 0, 0))

    def row(width):
        return pl.BlockSpec((tm, width), lambda i: (i, 0))

    def gate(bidx):
        return pl.BlockSpec((tm, D_MODEL), lambda i: (i, OFF_GL // D_MODEL + bidx))

    return pl.pallas_call(
        _merge_kernel,
        grid=(n // tm,),
        in_specs=[row(D_MODEL), gate(0), gate(1), gate(2), row(POOL_WIDTH), row(HGRN_HEADS * HGRN_V),
                  row(ATTN_WIDTH), _mod_spec(mods, l, 5, tm, rows_per_seq),
                  wspec(POOL_WIDTH), wspec(HGRN_HEADS * HGRN_V), wspec(ATTN_WIDTH), wspec(D_MODEL)],
        out_specs=row(D_MODEL),
        out_shape=jax.ShapeDtypeStruct((n, D_MODEL), F32),
        compiler_params=_params(("parallel",)),
        name="merge",
    )(x, p, p, p, ya, yb, yc, mods, w_ba, w_bb, w_bc, w_out)


def _permute_w_in(w_in):
    segs = [(2884, 5956), (256, 768), (768, 1280), (1792, 2304), (0, 256), (1280, 1536), (1536, 1792),
            (2560, 2816), (2304, 2432), (2432, 2560), (2816, 2884)]
    parts = [w_in[..., a:b] for a, b in segs]
    used = sum(b - a for a, b in segs)
    parts.append(jnp.zeros(w_in.shape[:-1] + (IN_PAD - used,), w_in.dtype))
    return jnp.concatenate(parts, axis=-1).astype(BF16)


def _lower_bounds(p):
    sm = jax.nn.softmax(p.astype(F32), axis=0)
    return jnp.cumsum(sm, axis=0) - sm[0:1]


def kernel(x_prompt, x_sample, c_prompt, c_sample, cache_k, cache_v, cache_ki, page_table, state_pool, state_hgrn, ada_w, ada_b, norm_g, ffn_wg, ffn_wu, ffn_wd, w_in, pool_w, pool_scale, hgrn_lb, hgrn_onorm, q_norm, k_norm, w_ba, w_bb, w_bc, w_out):
    bp, t_len, _ = x_prompt.shape
    bs = x_sample.shape[0]
    assert x_sample.shape[1] == 1
    n_pages = page_table.shape[1]
    past_len = n_pages * PAGE_SIZE
    n_phys = cache_k.shape[1]
    n_sel_p = min(TOPK_MAX, t_len // 4)
    n_sel_s = min(TOPK_MAX, (past_len + 1) // 4)

    wg, wu, wd = ffn_wg.astype(BF16), ffn_wu.astype(BF16), ffn_wd.astype(BF16)
    w_in_p = _permute_w_in(w_in)
    wa, wb, wc, wo = w_ba.astype(BF16), w_bb.astype(BF16), w_bc.astype(BF16), w_out.astype(BF16)
    eye = jnp.eye(len(POOL_WINDOWS), dtype=F32)
    pool_wbd = jnp.einsum('lgcd,gh->lgchd', pool_w, eye).reshape(DEPTH, POOL_WIDTH, POOL_WIDTH).astype(BF16)
    pool_sc = pool_scale.reshape(DEPTH, 1, POOL_WIDTH)
    norm_g4 = norm_g.reshape(DEPTH, N_SUB, 1, D_MODEL)
    lbs = _lower_bounds(hgrn_lb)
    qg = jnp.tile(q_norm, (1, LANES // HEAD_DIM)).reshape(DEPTH, 1, LANES)
    kg = jnp.tile(k_norm, (1, LANES // HEAD_DIM)).reshape(DEPTH, 1, LANES)
    gid = np.arange(LANES) // HEAD_DIM
    bd = jnp.asarray((gid[:, None] == gid[None, :]).astype(np.float32), BF16)
    mstack = _hgrn_consts(HGRN_CHUNK)
    cos_p, sin_p = _rope_tables(jnp.arange(t_len, dtype=I32))
    cos_s, sin_s = _rope_tables(jnp.full((bs,), past_len, I32))
    ck = jnp.transpose(cache_k, (0, 1, 3, 4, 2)).reshape(DEPTH, n_phys, KV_WIDTH, PAGE_SIZE)
    cv = jnp.transpose(cache_v, (0, 1, 3, 4, 2)).reshape(DEPTH, n_phys, KV_WIDTH, PAGE_SIZE)
    cki = jnp.transpose(cache_ki, (0, 1, 3, 2))

    rows_all = bp + bs
    rows_pad = -(-rows_all // 8) * 8
    c_all = jnp.concatenate([c_prompt, c_sample, jnp.zeros((rows_pad - rows_all, D_MODEL), F32)], axis=0)
    mods = _ada(c_all, ada_w, ada_b)
    mods_p = mods[:, :bp].reshape(DEPTH, bp, 1, 3 * N_SUB * D_MODEL)
    mods_s = mods[:, bp:rows_all]

    xp = x_prompt.reshape(bp * t_len, D_MODEL)
    xs = x_sample.reshape(bs, D_MODEL)
    pool_prev_p = jnp.zeros((bp, 16, POOL_WIDTH), F32)
    outs = [[] for _ in range(10)]
    tm_prep = min(512, t_len)

    for l in range(DEPTH):
        xp = _ffn(xp, mods_p, norm_g4, wg, wu, wd, l, 0, 0, t_len)
        p = _inproj(xp, mods_p, norm_g4, w_in_p, l, t_len)
        ya = _pool_prompt(p, pool_prev_p, pool_wbd, pool_sc, l, bp, t_len)
        yb, st_t = _hgrn_prompt(p, lbs[l].reshape(1, -1), hgrn_onorm[l].reshape(1, -1), mstack, bp, t_len)
        q_h, k_r, k_h, v_t, qi_h, ki_r, ki_b, wit = _prep(p, cos_p, sin_p, qg[l], kg[l], bd, tm_prep,
                                                          t_len // tm_prep)
        yc = _dsa_prompt(q_h, qi_h, wit, k_h, v_t, ki_b, bp, t_len, n_sel_p, tm_prep)
        xp = _merge(xp, p, ya, yb, yc, mods_p, wa, wb, wc, wo, l, t_len)
        xp = _ffn(xp, mods_p, norm_g4, wg, wu, wd, l, 1, 2, t_len)
        p3 = p.reshape(bp, t_len, IN_PAD)
        outs[0].append(k_r.reshape(bp, t_len, N_KV_HEADS, HEAD_DIM))
        outs[1].append(p3[:, :, OFF_AV:OFF_AV + KV_WIDTH].reshape(bp, t_len, N_KV_HEADS, HEAD_DIM))
        outs[2].append(ki_r.reshape(bp, t_len, IDX_DIM))
        outs[3].append(p3[:, t_len - POOL_STATE:, OFF_UP:OFF_UP + POOL_WIDTH])
        outs[4].append(jnp.swapaxes(st_t, -1, -2))

        xs = _ffn(xs, mods_s, norm_g4, wg, wu, wd, l, 0, 0, 1)
        ps = _inproj(xs, mods_s, norm_g4, w_in_p, l, 1)
        ya = _pool_dec(ps, jnp.swapaxes(state_pool[l], 0, 1), pool_wbd, pool_sc, l, past_len)
        kw = HGRN_HEADS * HGRN_K
        vw = HGRN_HEADS * HGRN_V
        yb4, st_new = _hgrn_dec(
            ps[:, OFF_HQ:OFF_HQ + kw].reshape(bs, HGRN_HEADS, HGRN_K, 1),
            ps[:, OFF_HF:OFF_HF + kw].reshape(bs, HGRN_HEADS, HGRN_K, 1),
            lbs[l].reshape(HGRN_HEADS, HGRN_K, 1),
            ps[:, OFF_HI:OFF_HI + vw].reshape(bs, HGRN_HEADS, 1, HGRN_V),
            ps[:, OFF_HG:OFF_HG + vw].reshape(bs, HGRN_HEADS, 1, HGRN_V),
            hgrn_onorm[l].reshape(1, -1), state_hgrn[l])
        yb = yb4.reshape(bs, vw)
        q_h, k_r, _, _, qi_h, ki_r, _, _ = _prep(ps, cos_s, sin_s, qg[l], kg[l], bd, bs, 1)
        v_new = ps[:, OFF_AV:OFF_AV + KV_WIDTH]
        qi8 = jnp.pad(jnp.swapaxes(qi_h, 0, 1), ((0, 0), (0, 8 - IDX_HEADS), (0, 0)))
        wi8 = jnp.pad(ps[:, OFF_IKW + IDX_DIM:OFF_IKW + IDX_DIM + IDX_HEADS].reshape(bs, IDX_HEADS, 1),
                      ((0, 0), (0, 8 - IDX_HEADS), (0, 0)))
        keys = _dsa_dec_scores(page_table, qi8, wi8, ki_r.reshape(bs, 1, IDX_DIM), cki, l)
        thr, j_lim = _dsa_dec_thr(keys.reshape(bs, -1), n_sel_s)
        q8 = jnp.pad(jnp.swapaxes(q_h, 0, 1).reshape(bs, N_KV_HEADS, GROUP, HEAD_DIM),
                     ((0, 0), (0, 0), (0, 8 - GROUP), (0, 0)))
        yc8 = _dsa_dec_attn(page_table, thr.reshape(bs), j_lim.reshape(bs), q8, keys,
                            k_r.reshape(bs, 1, KV_WIDTH), v_new.reshape(bs, 1, KV_WIDTH), ck, cv, l)
        yc = yc8[:, :, :GROUP, :].reshape(bs, ATTN_WIDTH)
        xs = _merge(xs, ps, ya, yb, yc, mods_s, wa, wb, wc, wo, l, 1)
        xs = _ffn(xs, mods_s, norm_g4, wg, wu, wd, l, 1, 2, 1)
        outs[5].append(k_r.reshape(bs, 1, N_KV_HEADS, HEAD_DIM))
        outs[6].append(v_new.reshape(bs, 1, N_KV_HEADS, HEAD_DIM))
        outs[7].append(ki_r.reshape(bs, 1, IDX_DIM))
        outs[8].append(jnp.concatenate([state_pool[l][:, 1:], ps[:, None, OFF_UP:OFF_UP + POOL_WIDTH]], axis=1))
        outs[9].append(st_new)

    stacked = [jnp.stack(o) for o in outs]
    return (xp.reshape(bp, t_len, D_MODEL), xs.reshape(bs, 1, D_MODEL), *stacked)
```

```python
import functools

import numpy as np
import jax
import jax.numpy as jnp
from jax import lax
from jax.experimental import pallas as pl
from jax.experimental.pallas import tpu as pltpu

F32 = jnp.float32
BF16 = jnp.bfloat16
I32 = jnp.int32
I16 = jnp.int16

D_MODEL = 1024
DEPTH = 4
PAGE_SIZE = 128
N_SUB = 3
POOL_WINDOWS = (2, 4, 8, 16)
POOL_GROUP_DIM = 64
POOL_WIDTH = 256
POOL_STATE = 15
HGRN_HEADS = 4
HGRN_K = 128
HGRN_V = 64
F_FLOOR = 1e-30
N_HEADS = 8
N_KV_HEADS = 2
HEAD_DIM = 64
GROUP = N_HEADS // N_KV_HEADS
ATTN_WIDTH = 512
KV_WIDTH = 128
IDX_HEADS = 4
IDX_DIM = 64
TOPK_MAX = 256
ROPE_THETA = 10000.0
LOG2_E = 1.4426950408889634
ATTN_SCALE = HEAD_DIM ** -0.5
IDX_SCALE = (IDX_DIM * IDX_HEADS) ** -0.5
NEG_BIG = -1e30
D_FF = 2816
EPS = 1e-6
INT_MIN = np.int32(-2147483648)

LANES = 128
VMEM_LIMIT = 48 * 1024 * 1024

OFF_GL, OFF_HQ, OFF_HF, OFF_AQ = 0, 3072, 3584, 4096
OFF_UP, OFF_HI, OFF_HG, OFF_IQ = 4608, 4864, 5120, 5376
OFF_AK, OFF_AV, OFF_IKW = 5632, 5760, 5888
IN_PAD = 6144
HGRN_CHUNK = 128
HGRN_LEVELS = (16, 32, 64, HGRN_CHUNK)


def _dot(a, b):
    return jnp.dot(a, b, preferred_element_type=F32)


def _dot_nt(a, b):
    return lax.dot_general(a, b, (((1,), (1,)), ((), ())), preferred_element_type=F32)


def _dot_tn(a, b):
    return lax.dot_general(a, b, (((0,), (0,)), ((), ())), preferred_element_type=F32)


def _split2(x):
    hi = x.astype(BF16)
    lo = (x - hi.astype(F32)).astype(BF16)
    return hi, lo


def _split3(x):
    hi = x.astype(BF16)
    r = x - hi.astype(F32)
    mid = r.astype(BF16)
    lo = (r - mid.astype(F32)).astype(BF16)
    return hi, mid, lo


def _silu(x):
    return x * jax.nn.sigmoid(x)


def _params(sem):
    return pltpu.CompilerParams(dimension_semantics=sem, vmem_limit_bytes=VMEM_LIMIT)


def _mod_spec(mods, l, m, tm, rows_per_seq):
    if mods.ndim == 4:
        return pl.BlockSpec((None, None, 1, D_MODEL), lambda i, *_: (l, (i * tm) // rows_per_seq, 0, m))
    return pl.BlockSpec((None, tm, D_MODEL), lambda i, *_: (l, i, m))


def _row_tile(n, mods, rows_per_seq, cap):
    return min(cap, rows_per_seq if mods.ndim == 4 else n)


def _prenorm(x, g, scale, shift):
    y = x * lax.rsqrt(jnp.mean(x * x, axis=-1, keepdims=True) + EPS) * g
    return y * (1.0 + scale) + shift


def _ada_kernel(c_ref, w_ref, b_ref, o_ref):
    c = c_ref[...]
    o_ref[...] = jnp.dot(_silu(c), w_ref[...], precision=lax.Precision.HIGHEST,
                         preferred_element_type=F32) + b_ref[...]


def _ada(c_all, ada_w, ada_b):
    rows = c_all.shape[0]
    width = ada_w.shape[-1]
    tn = 1024
    return pl.pallas_call(
        _ada_kernel,
        grid=(DEPTH, width // tn),
        in_specs=[pl.BlockSpec((rows, D_MODEL), lambda l, j: (0, 0)),
                  pl.BlockSpec((None, D_MODEL, tn), lambda l, j: (l, 0, j)),
                  pl.BlockSpec((None, 1, tn), lambda l, j: (l, 0, j))],
        out_specs=pl.BlockSpec((None, rows, tn), lambda l, j: (l, 0, j)),
        out_shape=jax.ShapeDtypeStruct((DEPTH, rows, width), F32),
        compiler_params=_params(("parallel", "parallel")),
        name="ada",
    )(c_all, ada_w, ada_b.reshape(DEPTH, 1, width))


def _ffn_kernel(x_ref, sh_ref, sc_ref, gt_ref, g_ref, wg_ref, wu_ref, wd_ref, o_ref, h_scr, acc_scr):
    j = pl.program_id(1)

    @pl.when(j == 0)
    def _():
        h_scr[...] = _prenorm(x_ref[...], g_ref[...], sc_ref[...], sh_ref[...]).astype(BF16)
        acc_scr[...] = jnp.zeros_like(acc_scr)

    h = h_scr[...]
    act = (_silu(_dot(h, wg_ref[...])) * _dot(h, wu_ref[...])).astype(BF16)
    acc_scr[...] += _dot(act, wd_ref[...])

    @pl.when(j == pl.num_programs(1) - 1)
    def _():
        o_ref[...] = x_ref[...] + 0.5 * gt_ref[...] * acc_scr[...]


def _ffn(x, mods, norm_g, wg, wu, wd, l, which, sub, rows_per_seq):
    n = x.shape[0]
    tm = _row_tile(n, mods, rows_per_seq, 1024)
    tf = 256
    return pl.pallas_call(
        _ffn_kernel,
        grid=(n // tm, D_FF // tf),
        in_specs=[pl.BlockSpec((tm, D_MODEL), lambda i, j: (i, 0)),
                  _mod_spec(mods, l, 3 * sub, tm, rows_per_seq),
                  _mod_spec(mods, l, 3 * sub + 1, tm, rows_per_seq),
                  _mod_spec(mods, l, 3 * sub + 2, tm, rows_per_seq),
                  pl.BlockSpec((None, None, 1, D_MODEL), lambda i, j: (l, sub, 0, 0)),
                  pl.BlockSpec((None, None, D_MODEL, tf), lambda i, j: (l, which, 0, j)),
                  pl.BlockSpec((None, None, D_MODEL, tf), lambda i, j: (l, which, 0, j)),
                  pl.BlockSpec((None, None, tf, D_MODEL), lambda i, j: (l, which, j, 0))],
        out_specs=pl.BlockSpec((tm, D_MODEL), lambda i, j: (i, 0)),
        out_shape=jax.ShapeDtypeStruct((n, D_MODEL), F32),
        scratch_shapes=[pltpu.VMEM((tm, D_MODEL), BF16), pltpu.VMEM((tm, D_MODEL), F32)],
        compiler_params=_params(("parallel", "arbitrary")),
        name="ffn",
    )(x, mods, mods, mods, norm_g, wg, wu, wd)


def _inproj_kernel(x_ref, sh_ref, sc_ref, g_ref, w_ref, o_ref, h_scr):
    @pl.when(pl.program_id(1) == 0)
    def _():
        h_scr[...] = _prenorm(x_ref[...], g_ref[...], sc_ref[...], sh_ref[...]).astype(BF16)

    o_ref[...] = _dot(h_scr[...], w_ref[...])


def _inproj(x, mods, norm_g, w_in, l, rows_per_seq):
    n = x.shape[0]
    tm = _row_tile(n, mods, rows_per_seq, 1024)
    tn = 1536
    return pl.pallas_call(
        _inproj_kernel,
        grid=(n // tm, IN_PAD // tn),
        in_specs=[pl.BlockSpec((tm, D_MODEL), lambda i, j: (i, 0)),
                  _mod_spec(mods, l, 3, tm, rows_per_seq),
                  _mod_spec(mods, l, 4, tm, rows_per_seq),
                  pl.BlockSpec((None, None, 1, D_MODEL), lambda i, j: (l, 1, 0, 0)),
                  pl.BlockSpec((None, D_MODEL, tn), lambda i, j: (l, 0, j))],
        out_specs=pl.BlockSpec((tm, tn), lambda i, j: (i, j)),
        out_shape=jax.ShapeDtypeStruct((n, IN_PAD), F32),
        scratch_shapes=[pltpu.VMEM((tm, D_MODEL), BF16)],
        compiler_params=_params(("parallel", "arbitrary")),
        name="inproj",
    )(x, mods, mods, norm_g, w_in)


def _prep_kernel(aq_ref, ak_ref, av_ref, iq_ref, ikw_ref, cos_ref, sin_ref, qg_ref, kg_ref, bd_ref,
                 q_o, k_o, kb_o, vt_o, qi_o, ki_o, kib_o, wit_o):
    cos = cos_ref[...]
    sin = sin_ref[...]
    bd = bd_ref[...]
    lane = lax.broadcasted_iota(I32, cos.shape, 1)
    first_half = (lane & (HEAD_DIM // 2)) == 0

    def rope(x):
        partner = jnp.where(first_half, pltpu.roll(x, LANES - HEAD_DIM // 2, 1), pltpu.roll(x, HEAD_DIM // 2, 1))
        return x * cos + partner * sin

    def head_norm(x):
        hi, mid, lo = _split3(x * x)
        ms = (_dot(hi, bd) + _dot(mid, bd) + _dot(lo, bd)) * (1.0 / HEAD_DIM)
        return x * lax.rsqrt(ms + EPS)

    def put_heads(o_ref, first, y):
        for half in range(LANES // HEAD_DIM):
            o_ref[first + half] = y[:, half * HEAD_DIM:(half + 1) * HEAD_DIM].astype(BF16)

    qg = qg_ref[...]
    for c in range(ATTN_WIDTH // LANES):
        sl = slice(c * LANES, (c + 1) * LANES)
        put_heads(q_o, 2 * c, rope(head_norm(aq_ref[:, sl]) * qg) * (ATTN_SCALE * LOG2_E))
    k = rope(head_norm(ak_ref[...]) * kg_ref[...])
    k_o[...] = k
    put_heads(kb_o, 0, k)
    v_t = av_ref[...].T.astype(BF16)
    ones = jnp.ones((V_ROWS - HEAD_DIM, v_t.shape[1]), BF16)
    vt_o[...] = jnp.concatenate([v_t[:HEAD_DIM], ones, v_t[HEAD_DIM:], ones], axis=0)
    for c in range(IDX_HEADS * IDX_DIM // LANES):
        sl = slice(c * LANES, (c + 1) * LANES)
        put_heads(qi_o, 2 * c, rope(iq_ref[:, sl]))
    ikw = ikw_ref[...]
    ki = rope(head_norm(ikw))[:, :IDX_DIM]
    ki_o[...] = ki
    kib_o[...] = ki.astype(BF16)
    wit_o[...] = ikw.T[IDX_DIM:IDX_DIM + 8, :]


def _prep(p, cos, sin, qg, kg, bd, tm, table_blocks):
    n = p.shape[0]

    def col(off, width):
        return pl.BlockSpec((tm, width), lambda i: (i, off // width))

    tab = pl.BlockSpec((tm, LANES), lambda i: (i % table_blocks, 0))
    vec = pl.BlockSpec((1, LANES), lambda i: (0, 0))

    def row(width):
        return pl.BlockSpec((tm, width), lambda i: (i, 0))

    def heads(h):
        return pl.BlockSpec((h, tm, HEAD_DIM), lambda i: (0, i, 0))

    return pl.pallas_call(
        _prep_kernel,
        grid=(n // tm,),
        in_specs=[col(OFF_AQ, ATTN_WIDTH), col(OFF_AK, KV_WIDTH), col(OFF_AV, KV_WIDTH),
                  col(OFF_IQ, IDX_HEADS * IDX_DIM), col(OFF_IKW, LANES), tab, tab, vec, vec,
                  pl.BlockSpec((LANES, LANES), lambda i: (0, 0))],
        out_specs=[heads(N_HEADS), row(KV_WIDTH), heads(N_KV_HEADS),
                   pl.BlockSpec((None, N_KV_HEADS * V_ROWS, tm), lambda i: (i, 0, 0)),
                   heads(IDX_HEADS), row(IDX_DIM), row(IDX_DIM),
                   pl.BlockSpec((8, tm), lambda i: (0, i))],
        out_shape=[jax.ShapeDtypeStruct((N_HEADS, n, HEAD_DIM), BF16),
                   jax.ShapeDtypeStruct((n, KV_WIDTH), F32),
                   jax.ShapeDtypeStruct((N_KV_HEADS, n, HEAD_DIM), BF16),
                   jax.ShapeDtypeStruct((n // tm, N_KV_HEADS * V_ROWS, tm), BF16),
                   jax.ShapeDtypeStruct((IDX_HEADS, n, IDX_DIM), BF16),
                   jax.ShapeDtypeStruct((n, IDX_DIM), F32),
                   jax.ShapeDtypeStruct((n, IDX_DIM), BF16),
                   jax.ShapeDtypeStruct((8, n), F32)],
        compiler_params=_params(("parallel",)),
        name="prep",
    )(p, p, p, p, p, cos, sin, qg, kg, bd)


def _rope_tables(pos):
    half = HEAD_DIM // 2
    inv = ROPE_THETA ** (-jnp.arange(half, dtype=F32) / half)
    ang = pos.astype(F32)[:, None] * inv[None, :]
    c, s = jnp.cos(ang), jnp.sin(ang)
    return jnp.concatenate([c, c, c, c], axis=1), jnp.concatenate([-s, s, -s, s], axis=1)


def _pool_select(sums, lane):
    out = sums[-1]
    for gi in range(len(POOL_WINDOWS) - 2, -1, -1):
        out = jnp.where(lane < (gi + 1) * POOL_GROUP_DIM, sums[gi], out)
    return out


def _pool_kernel(u_ref, prev_ref, w_ref, s_ref, o_ref, carry_scr, *, tt, p0):
    t = pl.program_id(1)

    @pl.when(t == 0)
    def _():
        carry_scr[...] = prev_ref[...]

    u = u_ref[...]
    e = jnp.concatenate([carry_scr[...], u], axis=0)
    carry_scr[...] = u[tt - 16:, :]
    s2 = e[1:] + e[:-1]
    s4 = s2[2:] + s2[:-2]
    s8 = s4[4:] + s4[:-4]
    s16 = s8[8:] + s8[:-8]
    sums = (s2[15:15 + tt], s4[13:13 + tt], s8[9:9 + tt], s16[1:1 + tt])
    n_avail = (p0 + 1 + t * tt + lax.broadcasted_iota(I32, (tt, 1), 0)).astype(F32)
    lane = lax.broadcasted_iota(I32, (tt, POOL_WIDTH), 1)
    means = [sums[gi] / jnp.minimum(n_avail, float(w)) for gi, w in enumerate(POOL_WINDOWS)]
    pooled = _pool_select(means, lane) - u
    o_ref[...] = _dot(pooled.astype(BF16), w_ref[...]) * s_ref[...]


def _pool_prompt(p, prev16, wbd, scale, l, b, t_len):
    tt = min(512, t_len)
    nt = t_len // tt
    return pl.pallas_call(
        functools.partial(_pool_kernel, tt=tt, p0=0),
        grid=(b, nt),
        in_specs=[pl.BlockSpec((tt, POOL_WIDTH), lambda bi, ti: (bi * nt + ti, OFF_UP // POOL_WIDTH)),
                  pl.BlockSpec((None, 16, POOL_WIDTH), lambda bi, ti: (bi, 0, 0)),
                  pl.BlockSpec((None, POOL_WIDTH, POOL_WIDTH), lambda bi, ti: (l, 0, 0)),
                  pl.BlockSpec((None, 1, POOL_WIDTH), lambda bi, ti: (l, 0, 0))],
        out_specs=pl.BlockSpec((tt, POOL_WIDTH), lambda bi, ti: (bi * nt + ti, 0)),
        out_shape=jax.ShapeDtypeStruct((b * t_len, POOL_WIDTH), F32),
        scratch_shapes=[pltpu.VMEM((16, POOL_WIDTH), F32)],
        compiler_params=_params(("parallel", "arbitrary")),
        name="pool",
    )(p, prev16, wbd, scale)


def _pool_dec_kernel(u_ref, st_ref, w_ref, s_ref, o_ref, *, p0):
    u = u_ref[...]
    lane = lax.broadcasted_iota(I32, u.shape, 1)
    means = []
    acc = u
    d = 1
    for w in POOL_WINDOWS:
        while d < w:
            acc = acc + st_ref[POOL_STATE - d]
            d += 1
        means.append(acc / float(min(p0 + 1, w)))
    pooled = _pool_select(means, lane) - u
    o_ref[...] = _dot(pooled.astype(BF16), w_ref[...]) * s_ref[...]


def _pool_dec(p, st_t, wbd, scale, l, p0):
    n = p.shape[0]
    return pl.pallas_call(
        functools.partial(_pool_dec_kernel, p0=p0),
        grid=(1,),
        in_specs=[pl.BlockSpec((n, POOL_WIDTH), lambda i: (0, OFF_UP // POOL_WIDTH)),
                  pl.BlockSpec((POOL_STATE, n, POOL_WIDTH), lambda i: (0, 0, 0)),
                  pl.BlockSpec((None, POOL_WIDTH, POOL_WIDTH), lambda i: (l, 0, 0)),
                  pl.BlockSpec((None, 1, POOL_WIDTH), lambda i: (l, 0, 0))],
        out_specs=pl.BlockSpec((n, POOL_WIDTH), lambda i: (0, 0)),
        out_shape=jax.ShapeDtypeStruct((n, POOL_WIDTH), F32),
        compiler_params=_params(("arbitrary",)),
        name="pool_dec",
    )(p, st_t, wbd, scale)


def _hgrn_consts(c):
    r = np.arange(c)[:, None]
    s = np.arange(c)[None, :]
    mats = [(s <= r) & (s // h == r // h) for h in HGRN_LEVELS]
    mats += [(s > r) & (s // h == r // h) for h in HGRN_LEVELS]
    return jnp.asarray(np.concatenate(mats, axis=0).astype(np.float32), BF16)


def _hgrn_kernel(hq_ref, hf_ref, hi_ref, hg_ref, lb_ref, on_ref, ms_ref, yb_ref, st_ref, s_scr, *, c):
    step = pl.program_id(1)
    nl = len(HGRN_LEVELS)

    @pl.when(step == 0)
    def _():
        s_scr[...] = jnp.zeros_like(s_scr)

    ti = lax.broadcasted_iota(I32, (c, c), 0)
    si = lax.broadcasted_iota(I32, (c, c), 1)
    nb = c // 16
    tpos = lax.broadcasted_iota(I32, (1, 16, 1), 1)

    for hd in range(HGRN_HEADS):
        ks = slice(hd * HGRN_K, (hd + 1) * HGRN_K)
        vs = slice(hd * HGRN_V, (hd + 1) * HGRN_V)
        q = hq_ref[:, ks]
        lb = lb_ref[:, ks]
        f = lb + (1.0 - lb) * jax.nn.sigmoid(hf_ref[:, ks])
        g = jnp.log(jnp.maximum(f, F_FLOOR))
        kk = 1.0 - f
        iv = hi_ref[:, vs]
        ivb = iv.astype(BF16)
        g_hi, g_lo = _split2(g)
        r = _dot(ms_ref[...], jnp.concatenate([g_hi, g_lo], axis=1))
        r = r[:, :HGRN_K] + r[:, HGRN_K:]
        low = [r[k * c:(k + 1) * c] for k in range(nl)]
        up = [r[(nl + k) * c:(nl + k + 1) * c] for k in range(nl)]

        att = jnp.zeros((c, c), F32)
        for k, h in enumerate(HGRN_LEVELS[:-1]):
            qe = (q * jnp.exp(low[k])).astype(BF16)
            ke = (kk * jnp.exp(up[k])).astype(BF16)
            sh = h.bit_length() - 1
            tb = jnp.right_shift(ti, sh)
            pair = ((tb & 1) == 1) & (jnp.right_shift(si, sh) == tb - 1)
            att = jnp.where(pair, _dot_nt(qe, ke), att)
        qe = (q * jnp.exp(low[-1])).astype(BF16)
        ke = (kk * jnp.exp(up[-1])).astype(BF16)
        s_t = s_scr[hd]
        o = _dot(att.astype(BF16), ivb) + _dot_nt(qe, s_t.astype(BF16))

        l3 = low[0].reshape(nb, 16, HGRN_K)
        q3 = q.reshape(nb, 16, HGRN_K)
        k3 = kk.reshape(nb, 16, HGRN_K)
        i3 = iv.reshape(nb, 16, HGRN_V)
        o3 = jnp.zeros((nb, 16, HGRN_V), F32)
        for s in range(16):
            e = jnp.exp(jnp.minimum(l3 - l3[:, s:s + 1, :], 0.0))
            a = jnp.sum(q3 * (k3[:, s:s + 1, :] * e), axis=-1, keepdims=True)
            a = jnp.where(tpos >= s, a, 0.0)
            o3 = o3 + a * i3[:, s:s + 1, :]
        o = o + o3.reshape(c, HGRN_V)

        s_scr[hd] = s_t * jnp.exp(low[-1][c - 1:c, :]) + _dot_tn(ivb, ke)

        on = o * lax.rsqrt(jnp.mean(o * o, axis=-1, keepdims=True) + EPS) * on_ref[...]
        yb_ref[:, vs] = on * _silu(hg_ref[:, vs])

    @pl.when(step == pl.num_programs(1) - 1)
    def _():
        st_ref[...] = s_scr[...]


def _hgrn_prompt(p, lb, onorm, mstack, b, t_len):
    c = HGRN_CHUNK
    nc = t_len // c
    kw = HGRN_HEADS * HGRN_K
    vw = HGRN_HEADS * HGRN_V

    def col(off, width):
        return pl.BlockSpec((c, width), lambda bi, ci: (bi * nc + ci, off // width))

    return pl.pallas_call(
        functools.partial(_hgrn_kernel, c=c),
        grid=(b, nc),
        in_specs=[col(OFF_HQ, kw), col(OFF_HF, kw), col(OFF_HI, vw), col(OFF_HG, vw),
                  pl.BlockSpec((1, kw), lambda bi, ci: (0, 0)),
                  pl.BlockSpec((1, HGRN_V), lambda bi, ci: (0, 0)),
                  pl.BlockSpec(mstack.shape, lambda bi, ci: (0, 0))],
        out_specs=[pl.BlockSpec((c, vw), lambda bi, ci: (bi * nc + ci, 0)),
                   pl.BlockSpec((None, HGRN_HEADS, HGRN_V, HGRN_K), lambda bi, ci: (bi, 0, 0, 0))],
        out_shape=[jax.ShapeDtypeStruct((b * t_len, vw), F32),
                   jax.ShapeDtypeStruct((b, HGRN_HEADS, HGRN_V, HGRN_K), F32)],
        scratch_shapes=[pltpu.VMEM((HGRN_HEADS, HGRN_V, HGRN_K), F32)],
        compiler_params=_params(("parallel", "arbitrary")),
        name="hgrn",
    )(p, p, p, p, lb, onorm, mstack)


def _hgrn_dec_kernel(q_ref, z_ref, lb_ref, i_ref, hg_ref, on_ref, s_ref, yb_ref, so_ref):
    lb = lb_ref[...]
    f = lb + (1.0 - lb) * jax.nn.sigmoid(z_ref[...])
    a = jnp.exp(jnp.log(jnp.maximum(f, F_FLOOR)))
    s_new = a * s_ref[...] + (1.0 - f) * i_ref[...]
    so_ref[...] = s_new
    o = jnp.sum(q_ref[...] * s_new, axis=2, keepdims=True)
    on = o * lax.rsqrt(jnp.mean(o * o, axis=-1, keepdims=True) + EPS) * on_ref[...]
    yb_ref[...] = on * _silu(hg_ref[...])


def _hgrn_dec(q_c, z_c, lb_c, i_r, hg_r, onorm, state):
    n = state.shape[0]
    nb = 8
    col = pl.BlockSpec((nb, HGRN_HEADS, HGRN_K, 1), lambda i: (i, 0, 0, 0))
    row = pl.BlockSpec((nb, HGRN_HEADS, 1, HGRN_V), lambda i: (i, 0, 0, 0))
    st = pl.BlockSpec((nb, HGRN_HEADS, HGRN_K, HGRN_V), lambda i: (i, 0, 0, 0))
    return pl.pallas_call(
        _hgrn_dec_kernel,
        grid=(n // nb,),
        in_specs=[col, col, pl.BlockSpec((HGRN_HEADS, HGRN_K, 1), lambda i: (0, 0, 0)), row, row,
                  pl.BlockSpec((1, HGRN_V), lambda i: (0, 0)), st],
        out_specs=[row, st],
        out_shape=[jax.ShapeDtypeStruct((n, HGRN_HEADS, 1, HGRN_V), F32),
                   jax.ShapeDtypeStruct(state.shape, F32)],
        compiler_params=_params(("parallel",)),
        name="hgrn_dec",
    )(q_c, z_c, lb_c, i_r, hg_r, onorm, state)


def _sort_key(s):
    bits = pltpu.bitcast(s, I32)
    return jnp.where(bits < 0, INT_MIN - bits, bits)


def _lane_fold(x):
    part = x[:, :LANES]
    for cidx in range(1, x.shape[1] // LANES):
        part = part + x[:, cidx * LANES:(cidx + 1) * LANES]
    return part


def _select_bias(key, thr, idx, j_lim):
    return jnp.where(key > thr, 0.0, jnp.where(key == thr, jnp.where(idx < j_lim, 0.0, NEG_BIG), NEG_BIG))


def _bisect_bits(count_ge, thr, first_bit, n_bits, k):
    def bit_body(it, thr):
        cand = thr + lax.shift_left(jnp.int32(1), first_bit - it)
        return jnp.where(count_ge(cand) >= k, cand, thr)

    return lax.fori_loop(0, n_bits, bit_body, thr)


def _topk_threshold(count, stat_shape, n_sel, idx_bits, idx_sentinel, kth_largest=None):
    k = float(n_sel)
    if kth_largest is None:
        thr = _bisect_bits(lambda cand: count(lambda key, idx: jnp.where(key >= cand, 1.0, 0.0)),
                           jnp.full(stat_shape, INT_MIN, I32), 31, 32, k)
    else:
        thr = kth_largest(k)
    n_ge = count(lambda key, idx: jnp.where(key >= thr, 1.0, 0.0))
    n_gt = count(lambda key, idx: jnp.where(key > thr, 1.0, 0.0))
    need = k - n_gt
    live = thr > INT_MIN
    tie = jnp.where(live, jnp.where(n_ge > k, 1.0, 0.0), 0.0)
    j_default = jnp.where(live, jnp.int32(idx_sentinel), jnp.int32(0))

    def tie_search():
        def idx_body(it, j_lim):
            cand = j_lim + lax.shift_left(jnp.int32(1), idx_bits - 1 - it)
            cnt = count(lambda key, idx: jnp.where(key == thr, jnp.where(idx < cand, 1.0, 0.0), 0.0))
            return jnp.where(cnt <= need, cand, j_lim)

        j_lim = lax.fori_loop(0, idx_bits, idx_body, jnp.zeros(stat_shape, I32))
        return jnp.where(tie > 0.0, j_lim, j_default)

    j_lim = lax.cond(jnp.max(tie) > 0.0, tie_search, lambda: j_default)
    return thr, j_lim


FOLD_ROWS = 64
V_ROWS = HEAD_DIM + 16


def _fold(x, op):
    parts = [x[r * FOLD_ROWS:(r + 1) * FOLD_ROWS] for r in range(x.shape[0] // FOLD_ROWS)]
    while len(parts) > 1:
        parts = [op(parts[i], parts[i + 1]) if i + 1 < len(parts) else parts[i]
                 for i in range(0, len(parts), 2)]
    return parts[0]


def _dsa_kernel(q_ref, qi_ref, wit_ref, k_ref, vt_ref, ki_ref, o_ref, keys_scr, k16_scr, m_scr, acc_scr,
                *, tq, tk, n_sel, t_len):
    i = pl.program_id(1)
    nk = ((i + 1) * tq + tk - 1) // tk
    krow = lax.broadcasted_iota(I32, (tk, tq), 0)
    qpos = i * tq + lax.broadcasted_iota(I32, (tk, tq), 1)
    wit = wit_ref[...] * IDX_SCALE
    qi_all = qi_ref[...].reshape(IDX_HEADS * tq, IDX_DIM)

    def rows(j):
        return pl.ds(pl.multiple_of(j * tk, tk), tk)

    def score_tile(j, causal):
        rel = _dot_nt(ki_ref[rows(j), :], qi_all)
        s = wit[0:1, :] * jnp.maximum(rel[:, :tq], 0.0)
        for h in range(1, IDX_HEADS):
            s = s + wit[h:h + 1, :] * jnp.maximum(rel[:, h * tq:(h + 1) * tq], 0.0)
        key = jnp.where(s > 0.5 * NEG_BIG, _sort_key(s), INT_MIN)
        if causal:
            key = jnp.where(j * tk + krow <= qpos, key, INT_MIN)
        keys_scr[j] = key
        k16_scr[j] = jnp.right_shift(key, 16).astype(I16)

    def score_body(j, carry):
        score_tile(j, False)
        return carry

    lax.fori_loop(0, nk - 1, score_body, 0)
    score_tile(nk - 1, True)

    def count(pred):
        def body(j, acc):
            return acc + _fold(pred(keys_scr[j], j * tk + krow), jnp.add)

        acc = lax.fori_loop(0, nk, body, jnp.zeros((FOLD_ROWS, tq), F32))
        return jnp.sum(acc, axis=0, keepdims=True)

    one16, zero16 = jnp.int16(1), jnp.int16(0)

    def count16(pred):
        def body(j, acc):
            return acc + _fold(pred(k16_scr[j]), jnp.add)

        acc = lax.fori_loop(0, nk, body, jnp.zeros((FOLD_ROWS, tq), I16))
        return jnp.sum(acc.astype(F32), axis=0, keepdims=True)

    def high16(x):
        return jnp.right_shift(x, 16).astype(I16)

    def low16(x):
        return ((x & 0xFFFF) - 32768).astype(I16)

    def kth_largest(k):
        thr = jnp.full((1, tq), INT_MIN, I32)
        thr = _bisect_bits(lambda cand: count16(lambda t: jnp.where(t >= high16(cand), one16, zero16)),
                           thr, 31, 16, k)
        thr_hi = high16(thr)
        n_above = count16(lambda t: jnp.where(t > thr_hi, one16, zero16))

        def low_body(j, carry):
            key = keys_scr[j]
            k16_scr[j] = jnp.where(high16(key) == thr_hi, low16(key), jnp.int16(-32768))
            return carry

        lax.fori_loop(0, nk, low_body, 0)
        return _bisect_bits(
            lambda cand: n_above + count16(lambda t: jnp.where(t >= low16(cand), one16, zero16)),
            thr, 15, 16, k)

    thr, j_lim = _topk_threshold(count, (1, tq), n_sel, int(t_len).bit_length(), t_len, kth_largest)

    m_scr[...] = jnp.full(m_scr.shape, 0.5 * NEG_BIG, F32)
    acc_scr[...] = jnp.zeros_like(acc_scr)

    def attend_tile(j, carry):
        bias = _select_bias(keys_scr[j], thr, j * tk + krow, j_lim)
        bias = jnp.concatenate([bias] * GROUP, axis=1)
        v_t = vt_ref[j]
        for n in range(N_KV_HEADS):
            qg = q_ref[n * GROUP:(n + 1) * GROUP].reshape(GROUP * tq, HEAD_DIM)
            lg = _dot_nt(k_ref[n, rows(j), :], qg) + bias
            m_old = m_scr[n]
            m_new = jnp.maximum(m_old, jnp.max(_fold(lg, jnp.maximum), axis=0, keepdims=True))
            p = jnp.exp2(lg - m_new).astype(BF16)
            acc_scr[n] = jnp.exp2(m_old - m_new) * acc_scr[n] + _dot(v_t[n * V_ROWS:(n + 1) * V_ROWS, :], p)
            m_scr[n] = m_new
        return carry

    lax.fori_loop(0, nk, attend_tile, 0)
    outs = []
    for n in range(N_KV_HEADS):
        acc = acc_scr[n]
        o_n = acc[:HEAD_DIM] / acc[HEAD_DIM:HEAD_DIM + 1]
        outs += [o_n[:, g * tq:(g + 1) * tq] for g in range(GROUP)]
    o_ref[...] = jnp.concatenate(outs, axis=0).T


def _dsa_prompt(q_h, qi_h, wit, k_h, v_t, ki_b, b, t_len, n_sel, tk):
    tq = min(256, t_len)
    nq = t_len // tq
    nt = t_len // tk

    def heads(h, rows, imap):
        return pl.BlockSpec((h, rows, HEAD_DIM), imap)

    return pl.pallas_call(
        functools.partial(_dsa_kernel, tq=tq, tk=tk, n_sel=n_sel, t_len=t_len),
        grid=(b, nq),
        in_specs=[heads(N_HEADS, tq, lambda bi, qi: (0, bi * nq + qi, 0)),
                  heads(IDX_HEADS, tq, lambda bi, qi: (0, bi * nq + qi, 0)),
                  pl.BlockSpec((8, tq), lambda bi, qi: (0, bi * nq + qi)),
                  heads(N_KV_HEADS, t_len, lambda bi, qi: (0, bi, 0)),
                  pl.BlockSpec((nt, N_KV_HEADS * V_ROWS, tk), lambda bi, qi: (bi, 0, 0)),
                  pl.BlockSpec((t_len, IDX_DIM), lambda bi, qi: (bi, 0))],
        out_specs=pl.BlockSpec((tq, ATTN_WIDTH), lambda bi, qi: (bi * nq + qi, 0)),
        out_shape=jax.ShapeDtypeStruct((b * t_len, ATTN_WIDTH), F32),
        scratch_shapes=[pltpu.VMEM((nt, tk, tq), I32),
                        pltpu.VMEM((nt, tk, tq), I16),
                        pltpu.VMEM((N_KV_HEADS, 1, GROUP * tq), F32),
                        pltpu.VMEM((N_KV_HEADS, V_ROWS, GROUP * tq), F32)],
        compiler_params=_params(("parallel", "arbitrary")),
        name="dsa",
    )(q_h, qi_h, wit, k_h, v_t, ki_b)


def _dsa_dec_score_kernel(pt_ref, qi_ref, wi_ref, kin_ref, *rest, n_pages):
    page_refs = rest[:n_pages]
    o_ref = rest[n_pages]
    qi = qi_ref[...]
    wi = wi_ref[...]

    def finish(rel):
        s = jnp.sum(wi * jnp.maximum(rel, 0.0), axis=0, keepdims=True) * IDX_SCALE
        return jnp.where(s > 0.5 * NEG_BIG, _sort_key(s), INT_MIN)

    for pg in range(n_pages):
        o_ref[:, pg * PAGE_SIZE:(pg + 1) * PAGE_SIZE] = finish(_dot(qi, page_refs[pg][...].astype(BF16)))
    kin = kin_ref[...].astype(BF16).astype(F32)
    rel_new = jnp.sum(qi.astype(F32) * kin, axis=1, keepdims=True)
    key_new = finish(rel_new)
    lane = lax.broadcasted_iota(I32, (1, PAGE_SIZE), 1)
    o_ref[:, n_pages * PAGE_SIZE:] = jnp.where(lane == 0, key_new, INT_MIN)


def _dsa_dec_scores(page_table, qi8, wi8, ki_new, cache_ki_t, l):
    n, n_pages = page_table.shape
    width = (n_pages + 1) * PAGE_SIZE
    page_specs = [pl.BlockSpec((None, None, IDX_DIM, PAGE_SIZE),
                               functools.partial(lambda bi, pt, pg: (l, pt[bi, pg], 0, 0), pg=pg))
                  for pg in range(n_pages)]
    return pl.pallas_call(
        functools.partial(_dsa_dec_score_kernel, n_pages=n_pages),
        grid_spec=pltpu.PrefetchScalarGridSpec(
            num_scalar_prefetch=1,
            grid=(n,),
            in_specs=[pl.BlockSpec((None, 8, IDX_DIM), lambda bi, pt: (bi, 0, 0)),
                      pl.BlockSpec((None, 8, 1), lambda bi, pt: (bi, 0, 0)),
                      pl.BlockSpec((None, 1, IDX_DIM), lambda bi, pt: (bi, 0, 0))] + page_specs,
            out_specs=pl.BlockSpec((None, 1, width), lambda bi, pt: (bi, 0, 0))),
        out_shape=jax.ShapeDtypeStruct((n, 1, width), I32),
        compiler_params=_params(("arbitrary",)),
        name="dsa_dec_scores",
    )(page_table, qi8, wi8, ki_new, *([cache_ki_t] * n_pages))


def _dsa_dec_thr_kernel(keys_ref, thr_ref, j_ref, *, n_sel):
    n, width = keys_ref.shape
    lane = lax.broadcasted_iota(I32, (n, width), 1)

    def count(pred):
        return jnp.sum(_lane_fold(pred(keys_ref[...], lane)), axis=1, keepdims=True)

    thr, j_lim = _topk_threshold(count, (n, 1), n_sel, int(width).bit_length(), width)
    thr_ref[...] = thr
    j_ref[...] = j_lim


def _dsa_dec_thr(keys2d, n_sel):
    n, width = keys2d.shape
    return pl.pallas_call(
        functools.partial(_dsa_dec_thr_kernel, n_sel=n_sel),
        grid=(1,),
        in_specs=[pl.BlockSpec((n, width), lambda i: (0, 0))],
        out_specs=[pl.BlockSpec((n, 1), lambda i: (0, 0)), pl.BlockSpec((n, 1), lambda i: (0, 0))],
        out_shape=[jax.ShapeDtypeStruct((n, 1), I32), jax.ShapeDtypeStruct((n, 1), I32)],
        compiler_params=_params(("arbitrary",)),
        name="dsa_dec_thr",
    )(keys2d)


def _dsa_dec_attn_kernel(pt_ref, thr_ref, j_ref, q_ref, keys_ref, kn_ref, vn_ref, *rest, n_pages):
    k_pages = rest[:n_pages]
    v_pages = rest[n_pages:2 * n_pages]
    o_ref, k_scr, v_scr = rest[2 * n_pages:]
    bi = pl.program_id(0)
    n_past = n_pages * PAGE_SIZE
    for pg in range(n_pages):
        sl = slice(pg * PAGE_SIZE, (pg + 1) * PAGE_SIZE)
        k_scr[:, sl] = k_pages[pg][...].astype(BF16)
        v_scr[:, sl] = v_pages[pg][...].astype(BF16)
    key = keys_ref[...]
    idx = lax.broadcasted_iota(I32, key.shape, 1)
    bias_all = _select_bias(key, thr_ref[bi], idx, j_ref[bi])
    bias = bias_all[:, :n_past]
    bias_new = bias_all[:, n_past:n_past + 1]
    kn = kn_ref[...].astype(BF16).astype(F32)
    vn = vn_ref[...].astype(BF16).astype(F32)
    for n in range(N_KV_HEADS):
        fs = slice(n * HEAD_DIM, (n + 1) * HEAD_DIM)
        q = q_ref[n]
        lg = _dot(q, k_scr[fs, :]) + bias
        lg_new = jnp.sum(q.astype(F32) * kn[:, fs], axis=1, keepdims=True) + bias_new
        m = jnp.maximum(jnp.max(lg, axis=1, keepdims=True), lg_new)
        p = jnp.exp2(lg - m)
        p_new = jnp.exp2(lg_new - m)
        den = jnp.sum(p, axis=1, keepdims=True) + p_new
        o_ref[n] = (_dot_nt(p.astype(BF16), v_scr[fs, :]) + p_new * vn[:, fs]) / den


def _dsa_dec_attn(page_table, thr, j_lim, q8, keys, k_new, v_new, cache_k_t, cache_v_t, l):
    n, n_pages = page_table.shape
    width = (n_pages + 1) * PAGE_SIZE

    def page_spec(pg):
        return pl.BlockSpec((None, None, KV_WIDTH, PAGE_SIZE),
                            lambda bi, pt, th, jl: (l, pt[bi, pg], 0, 0))

    page_specs = [page_spec(pg) for pg in range(n_pages)]
    qspec = pl.BlockSpec((None, N_KV_HEADS, 8, HEAD_DIM), lambda bi, pt, th, jl: (bi, 0, 0, 0))
    return pl.pallas_call(
        functools.partial(_dsa_dec_attn_kernel, n_pages=n_pages),
        grid_spec=pltpu.PrefetchScalarGridSpec(
            num_scalar_prefetch=3,
            grid=(n,),
            in_specs=[qspec,
                      pl.BlockSpec((None, 1, width), lambda bi, pt, th, jl: (bi, 0, 0)),
                      pl.BlockSpec((None, 1, KV_WIDTH), lambda bi, pt, th, jl: (bi, 0, 0)),
                      pl.BlockSpec((None, 1, KV_WIDTH), lambda bi, pt, th, jl: (bi, 0, 0))]
            + page_specs + page_specs,
            out_specs=qspec,
            scratch_shapes=[pltpu.VMEM((KV_WIDTH, n_pages * PAGE_SIZE), BF16),
                            pltpu.VMEM((KV_WIDTH, n_pages * PAGE_SIZE), BF16)]),
        out_shape=jax.ShapeDtypeStruct((n, N_KV_HEADS, 8, HEAD_DIM), F32),
        compiler_params=_params(("arbitrary",)),
        name="dsa_dec_attn",
    )(page_table, thr, j_lim, q8, keys, k_new, v_new, *([cache_k_t] * n_pages), *([cache_v_t] * n_pages))


def _merge_kernel(x_ref, g0_ref, g1_ref, g2_ref, ya_ref, yb_ref, yc_ref, gt_ref,
                  wa_ref, wb_ref, wc_ref, wo_ref, o_ref):
    merged = (jax.nn.sigmoid(g0_ref[...]) * _dot(ya_ref[...].astype(BF16), wa_ref[...])
              + jax.nn.sigmoid(g1_ref[...]) * _dot(yb_ref[...].astype(BF16), wb_ref[...])
              + jax.nn.sigmoid(g2_ref[...]) * _dot(yc_ref[...].astype(BF16), wc_ref[...]))
    o_ref[...] = x_ref[...] + gt_ref[...] * _dot(merged.astype(BF16), wo_ref[...])


def _merge(x, p, ya, yb, yc, mods, w_ba, w_bb, w_bc, w_out, l, rows_per_seq):
    n = x.shape[0]
    tm = _row_tile(n, mods, rows_per_seq, 512)

    def wspec(rows):
        return pl.BlockSpec((None, rows, D_MODEL), lambda i: (l, 0, 0))

    def row(width):
        return pl.BlockSpec((tm, width), lambda i: (i, 0))

    def gate(bidx):
        return pl.BlockSpec((tm, D_MODEL), lambda i: (i, OFF_GL // D_MODEL + bidx))

    return pl.pallas_call(
        _merge_kernel,
        grid=(n // tm,),
        in_specs=[row(D_MODEL), gate(0), gate(1), gate(2), row(POOL_WIDTH), row(HGRN_HEADS * HGRN_V),
                  row(ATTN_WIDTH), _mod_spec(mods, l, 5, tm, rows_per_seq),
                  wspec(POOL_WIDTH), wspec(HGRN_HEADS * HGRN_V), wspec(ATTN_WIDTH), wspec(D_MODEL)],
        out_specs=row(D_MODEL),
        out_shape=jax.ShapeDtypeStruct((n, D_MODEL), F32),
        compiler_params=_params(("parallel",)),
        name="merge",
    )(x, p, p, p, ya, yb, yc, mods, w_ba, w_bb, w_bc, w_out)


def _permute_w_in(w_in):
    segs = [(2884, 5956), (256, 768), (768, 1280), (1792, 2304), (0, 256), (1280, 1536), (1536, 1792),
            (2560, 2816), (2304, 2432), (2432, 2560), (2816, 2884)]
    parts = [w_in[..., a:b] for a, b in segs]
    used = sum(b - a for a, b in segs)
    parts.append(jnp.zeros(w_in.shape[:-1] + (IN_PAD - used,), w_in.dtype))
    return jnp.concatenate(parts, axis=-1).astype(BF16)


def _lower_bounds(p):
    sm = jax.nn.softmax(p.astype(F32), axis=0)
    return jnp.cumsum(sm, axis=0) - sm[0:1]


def kernel(x_prompt, x_sample, c_prompt, c_sample, cache_k, cache_v, cache_ki, page_table, state_pool, state_hgrn, ada_w, ada_b, norm_g, ffn_wg, ffn_wu, ffn_wd, w_in, pool_w, pool_scale, hgrn_lb, hgrn_onorm, q_norm, k_norm, w_ba, w_bb, w_bc, w_out):
    bp, t_len, _ = x_prompt.shape
    bs = x_sample.shape[0]
    assert x_sample.shape[1] == 1
    n_pages = page_table.shape[1]
    past_len = n_pages * PAGE_SIZE
    n_phys = cache_k.shape[1]
    n_sel_p = min(TOPK_MAX, t_len // 4)
    n_sel_s = min(TOPK_MAX, (past_len + 1) // 4)

    wg, wu, wd = ffn_wg.astype(BF16), ffn_wu.astype(BF16), ffn_wd.astype(BF16)
    w_in_p = _permute_w_in(w_in)
    wa, wb, wc, wo = w_ba.astype(BF16), w_bb.astype(BF16), w_bc.astype(BF16), w_out.astype(BF16)
    eye = jnp.eye(len(POOL_WINDOWS), dtype=F32)
    pool_wbd = jnp.einsum('lgcd,gh->lgchd', pool_w, eye).reshape(DEPTH, POOL_WIDTH, POOL_WIDTH).astype(BF16)
    pool_sc = pool_scale.reshape(DEPTH, 1, POOL_WIDTH)
    norm_g4 = norm_g.reshape(DEPTH, N_SUB, 1, D_MODEL)
    lbs = _lower_bounds(hgrn_lb)
    qg = jnp.tile(q_norm, (1, LANES // HEAD_DIM)).reshape(DEPTH, 1, LANES)
    kg = jnp.tile(k_norm, (1, LANES // HEAD_DIM)).reshape(DEPTH, 1, LANES)
    gid = np.arange(LANES) // HEAD_DIM
    bd = jnp.asarray((gid[:, None] == gid[None, :]).astype(np.float32), BF16)
    mstack = _hgrn_consts(HGRN_CHUNK)
    cos_p, sin_p = _rope_tables(jnp.arange(t_len, dtype=I32))
    cos_s, sin_s = _rope_tables(jnp.full((bs,), past_len, I32))
    ck = jnp.transpose(cache_k, (0, 1, 3, 4, 2)).reshape(DEPTH, n_phys, KV_WIDTH, PAGE_SIZE)
    cv = jnp.transpose(cache_v, (0, 1, 3, 4, 2)).reshape(DEPTH, n_phys, KV_WIDTH, PAGE_SIZE)
    cki = jnp.transpose(cache_ki, (0, 1, 3, 2))

    rows_all = bp + bs
    rows_pad = -(-rows_all // 8) * 8
    c_all = jnp.concatenate([c_prompt, c_sample, jnp.zeros((rows_pad - rows_all, D_MODEL), F32)], axis=0)
    mods = _ada(c_all, ada_w, ada_b)
    mods_p = mods[:, :bp].reshape(DEPTH, bp, 1, 3 * N_SUB * D_MODEL)
    mods_s = mods[:, bp:rows_all]

    xp = x_prompt.reshape(bp * t_len, D_MODEL)
    xs = x_sample.reshape(bs, D_MODEL)
    pool_prev_p = jnp.zeros((bp, 16, POOL_WIDTH), F32)
    outs = [[] for _ in range(10)]
    tm_prep = min(512, t_len)

    for l in range(DEPTH):
        xp = _ffn(xp, mods_p, norm_g4, wg, wu, wd, l, 0, 0, t_len)
        p = _inproj(xp, mods_p, norm_g4, w_in_p, l, t_len)
        ya = _pool_prompt(p, pool_prev_p, pool_wbd, pool_sc, l, bp, t_len)
        yb, st_t = _hgrn_prompt(p, lbs[l].reshape(1, -1), hgrn_onorm[l].reshape(1, -1), mstack, bp, t_len)
        q_h, k_r, k_h, v_t, qi_h, ki_r, ki_b, wit = _prep(p, cos_p, sin_p, qg[l], kg[l], bd, tm_prep,
                                                          t_len // tm_prep)
        yc = _dsa_prompt(q_h, qi_h, wit, k_h, v_t, ki_b, bp, t_len, n_sel_p, tm_prep)
        xp = _merge(xp, p, ya, yb, yc, mods_p, wa, wb, wc, wo, l, t_len)
        xp = _ffn(xp, mods_p, norm_g4, wg, wu, wd, l, 1, 2, t_len)
        p3 = p.reshape(bp, t_len, IN_PAD)
        outs[0].append(k_r.reshape(bp, t_len, N_KV_HEADS, HEAD_DIM))
        outs[1].append(p3[:, :, OFF_AV:OFF_AV + KV_WIDTH].reshape(bp, t_len, N_KV_HEADS, HEAD_DIM))
        outs[2].append(ki_r.reshape(bp, t_len, IDX_DIM))
        outs[3].append(p3[:, t_len - POOL_STATE:, OFF_UP:OFF_UP + POOL_WIDTH])
        outs[4].append(jnp.swapaxes(st_t, -1, -2))

        xs = _ffn(xs, mods_s, norm_g4, wg, wu, wd, l, 0, 0, 1)
        ps = _inproj(xs, mods_s, norm_g4, w_in_p, l, 1)
        ya = _pool_dec(ps, jnp.swapaxes(state_pool[l], 0, 1), pool_wbd, pool_sc, l, past_len)
        kw = HGRN_HEADS * HGRN_K
        vw = HGRN_HEADS * HGRN_V
        yb4, st_new = _hgrn_dec(
            ps[:, OFF_HQ:OFF_HQ + kw].reshape(bs, HGRN_HEADS, HGRN_K, 1),
            ps[:, OFF_HF:OFF_HF + kw].reshape(bs, HGRN_HEADS, HGRN_K, 1),
            lbs[l].reshape(HGRN_HEADS, HGRN_K, 1),
            ps[:, OFF_HI:OFF_HI + vw].reshape(bs, HGRN_HEADS, 1, HGRN_V),
            ps[:, OFF_HG:OFF_HG + vw].reshape(bs, HGRN_HEADS, 1, HGRN_V),
            hgrn_onorm[l].reshape(1, -1), state_hgrn[l])
        yb = yb4.reshape(bs, vw)
        q_h, k_r, _, _, qi_h, ki_r, _, _ = _prep(ps, cos_s, sin_s, qg[l], kg[l], bd, bs, 1)
        v_new = ps[:, OFF_AV:OFF_AV + KV_WIDTH]
        qi8 = jnp.pad(jnp.swapaxes(qi_h, 0, 1), ((0, 0), (0, 8 - IDX_HEADS), (0, 0)))
        wi8 = jnp.pad(ps[:, OFF_IKW + IDX_DIM:OFF_IKW + IDX_DIM + IDX_HEADS].reshape(bs, IDX_HEADS, 1),
                      ((0, 0), (0, 8 - IDX_HEADS), (0, 0)))
        keys = _dsa_dec_scores(page_table, qi8, wi8, ki_r.reshape(bs, 1, IDX_DIM), cki, l)
        thr, j_lim = _dsa_dec_thr(keys.reshape(bs, -1), n_sel_s)
        q8 = jnp.pad(jnp.swapaxes(q_h, 0, 1).reshape(bs, N_KV_HEADS, GROUP, HEAD_DIM),
                     ((0, 0), (0, 0), (0, 8 - GROUP), (0, 0)))
        yc8 = _dsa_dec_attn(page_table, thr.reshape(bs), j_lim.reshape(bs), q8, keys,
                            k_r.reshape(bs, 1, KV_WIDTH), v_new.reshape(bs, 1, KV_WIDTH), ck, cv, l)
        yc = yc8[:, :, :GROUP, :].reshape(bs, ATTN_WIDTH)
        xs = _merge(xs, ps, ya, yb, yc, mods_s, wa, wb, wc, wo, l, 1)
        xs = _ffn(xs, mods_s, norm_g4, wg, wu, wd, l, 1, 2, 1)
        outs[5].append(k_r.reshape(bs, 1, N_KV_HEADS, HEAD_DIM))
        outs[6].append(v_new.reshape(bs, 1, N_KV_HEADS, HEAD_DIM))
        outs[7].append(ki_r.reshape(bs, 1, IDX_DIM))
        outs[8].append(jnp.concatenate([state_pool[l][:, 1:], ps[:, None, OFF_UP:OFF_UP + POOL_WIDTH]], axis=1))
        outs[9].append(st_new)

    stacked = [jnp.stack(o) for o in outs]
    return (xp.reshape(bp, t_len, D_MODEL), xs.reshape(bs, 1, D_MODEL), *stacked)
```

```python
import functools

import numpy as np
import jax
import jax.numpy as jnp
from jax import lax
from jax.experimental import pallas as pl
from jax.experimental.pallas import tpu as pltpu

F32 = jnp.float32
BF16 = jnp.bfloat16
I32 = jnp.int32
I16 = jnp.int16

D_MODEL = 1024
DEPTH = 4
PAGE_SIZE = 128
N_SUB = 3
POOL_WINDOWS = (2, 4, 8, 16)
POOL_GROUP_DIM = 64
POOL_WIDTH = 256
POOL_STATE = 15
HGRN_HEADS = 4
HGRN_K = 128
HGRN_V = 64
F_FLOOR = 1e-30
N_HEADS = 8
N_KV_HEADS = 2
HEAD_DIM = 64
GROUP = N_HEADS // N_KV_HEADS
ATTN_WIDTH = 512
KV_WIDTH = 128
IDX_HEADS = 4
IDX_DIM = 64
TOPK_MAX = 256
ROPE_THETA = 10000.0
LOG2_E = 1.4426950408889634
ATTN_SCALE = HEAD_DIM ** -0.5
IDX_SCALE = (IDX_DIM * IDX_HEADS) ** -0.5
NEG_BIG = -1e30
D_FF = 2816
EPS = 1e-6
INT_MIN = np.int32(-2147483648)

LANES = 128
VMEM_LIMIT = 48 * 1024 * 1024

OFF_GL, OFF_HQ, OFF_HF, OFF_AQ = 0, 3072, 3584, 4096
OFF_UP, OFF_HI, OFF_HG, OFF_IQ = 4608, 4864, 5120, 5376
OFF_AK, OFF_AV, OFF_IKW = 5632, 5760, 5888
IN_PAD = 6144
HGRN_CHUNK = 128
HGRN_BASE = 16
HGRN_LEVELS = (HGRN_BASE, 16, 32, 64, HGRN_CHUNK)


def _dot(a, b):
    return jnp.dot(a, b, preferred_element_type=F32)


def _dot_nt(a, b):
    return lax.dot_general(a, b, (((1,), (1,)), ((), ())), preferred_element_type=F32)


def _dot_tn(a, b):
    return lax.dot_general(a, b, (((0,), (0,)), ((), ())), preferred_element_type=F32)


def _split2(x):
    hi = x.astype(BF16)
    lo = (x - hi.astype(F32)).astype(BF16)
    return hi, lo


def _split3(x):
    hi = x.astype(BF16)
    r = x - hi.astype(F32)
    mid = r.astype(BF16)
    lo = (r - mid.astype(F32)).astype(BF16)
    return hi, mid, lo


def _silu(x):
    return x * jax.nn.sigmoid(x)


def _params(sem):
    return pltpu.CompilerParams(dimension_semantics=sem, vmem_limit_bytes=VMEM_LIMIT)


def _mod_spec(mods, l, m, tm, rows_per_seq):
    if mods.ndim == 4:
        return pl.BlockSpec((None, None, 1, D_MODEL), lambda i, *_: (l, (i * tm) // rows_per_seq, 0, m))
    return pl.BlockSpec((None, tm, D_MODEL), lambda i, *_: (l, i, m))


def _row_tile(n, mods, rows_per_seq, cap):
    return min(cap, rows_per_seq if mods.ndim == 4 else n)


def _prenorm(x, g, scale, shift):
    y = x * lax.rsqrt(jnp.mean(x * x, axis=-1, keepdims=True) + EPS) * g
    return y * (1.0 + scale) + shift


def _ada_kernel(c_ref, w_ref, b_ref, o_ref):
    c = c_ref[...]
    o_ref[...] = jnp.dot(_silu(c), w_ref[...], precision=lax.Precision.HIGHEST,
                         preferred_element_type=F32) + b_ref[...]


def _ada(c_all, ada_w, ada_b):
    rows = c_all.shape[0]
    width = ada_w.shape[-1]
    tn = 1024
    return pl.pallas_call(
        _ada_kernel,
        grid=(DEPTH, width // tn),
        in_specs=[pl.BlockSpec((rows, D_MODEL), lambda l, j: (0, 0)),
                  pl.BlockSpec((None, D_MODEL, tn), lambda l, j: (l, 0, j)),
                  pl.BlockSpec((None, 1, tn), lambda l, j: (l, 0, j))],
        out_specs=pl.BlockSpec((None, rows, tn), lambda l, j: (l, 0, j)),
        out_shape=jax.ShapeDtypeStruct((DEPTH, rows, width), F32),
        compiler_params=_params(("parallel", "parallel")),
        name="ada",
    )(c_all, ada_w, ada_b.reshape(DEPTH, 1, width))


def _ffn_kernel(x_ref, sh_ref, sc_ref, gt_ref, g_ref, wg_ref, wu_ref, wd_ref, o_ref, h_scr, acc_scr):
    j = pl.program_id(1)

    @pl.when(j == 0)
    def _():
        h_scr[...] = _prenorm(x_ref[...], g_ref[...], sc_ref[...], sh_ref[...]).astype(BF16)
        acc_scr[...] = jnp.zeros_like(acc_scr)

    h = h_scr[...]
    act = (_silu(_dot(h, wg_ref[...])) * _dot(h, wu_ref[...])).astype(BF16)
    acc_scr[...] += _dot(act, wd_ref[...])

    @pl.when(j == pl.num_programs(1) - 1)
    def _():
        o_ref[...] = x_ref[...] + 0.5 * gt_ref[...] * acc_scr[...]


def _ffn(x, mods, norm_g, wg, wu, wd, l, which, sub, rows_per_seq):
    n = x.shape[0]
    tm = _row_tile(n, mods, rows_per_seq, 1024)
    tf = 256
    return pl.pallas_call(
        _ffn_kernel,
        grid=(n // tm, D_FF // tf),
        in_specs=[pl.BlockSpec((tm, D_MODEL), lambda i, j: (i, 0)),
                  _mod_spec(mods, l, 3 * sub, tm, rows_per_seq),
                  _mod_spec(mods, l, 3 * sub + 1, tm, rows_per_seq),
                  _mod_spec(mods, l, 3 * sub + 2, tm, rows_per_seq),
                  pl.BlockSpec((None, None, 1, D_MODEL), lambda i, j: (l, sub, 0, 0)),
                  pl.BlockSpec((None, None, D_MODEL, tf), lambda i, j: (l, which, 0, j)),
                  pl.BlockSpec((None, None, D_MODEL, tf), lambda i, j: (l, which, 0, j)),
                  pl.BlockSpec((None, None, tf, D_MODEL), lambda i, j: (l, which, j, 0))],
        out_specs=pl.BlockSpec((tm, D_MODEL), lambda i, j: (i, 0)),
        out_shape=jax.ShapeDtypeStruct((n, D_MODEL), F32),
        scratch_shapes=[pltpu.VMEM((tm, D_MODEL), BF16), pltpu.VMEM((tm, D_MODEL), F32)],
        compiler_params=_params(("parallel", "arbitrary")),
        name="ffn",
    )(x, mods, mods, mods, norm_g, wg, wu, wd)


def _inproj_kernel(x_ref, sh_ref, sc_ref, g_ref, w_ref, o_ref, h_scr):
    @pl.when(pl.program_id(1) == 0)
    def _():
        h_scr[...] = _prenorm(x_ref[...], g_ref[...], sc_ref[...], sh_ref[...]).astype(BF16)

    o_ref[...] = _dot(h_scr[...], w_ref[...])


def _inproj(x, mods, norm_g, w_in, l, rows_per_seq):
    n = x.shape[0]
    tm = _row_tile(n, mods, rows_per_seq, 1024)
    tn = 1536
    return pl.pallas_call(
        _inproj_kernel,
        grid=(n // tm, IN_PAD // tn),
        in_specs=[pl.BlockSpec((tm, D_MODEL), lambda i, j: (i, 0)),
                  _mod_spec(mods, l, 3, tm, rows_per_seq),
                  _mod_spec(mods, l, 4, tm, rows_per_seq),
                  pl.BlockSpec((None, None, 1, D_MODEL), lambda i, j: (l, 1, 0, 0)),
                  pl.BlockSpec((None, D_MODEL, tn), lambda i, j: (l, 0, j))],
        out_specs=pl.BlockSpec((tm, tn), lambda i, j: (i, j)),
        out_shape=jax.ShapeDtypeStruct((n, IN_PAD), F32),
        scratch_shapes=[pltpu.VMEM((tm, D_MODEL), BF16)],
        compiler_params=_params(("parallel", "arbitrary")),
        name="inproj",
    )(x, mods, mods, norm_g, w_in)


def _prep_kernel(aq_ref, ak_ref, av_ref, iq_ref, ikw_ref, cos_ref, sin_ref, qg_ref, kg_ref, bd_ref,
                 q_o, k_o, kb_o, vt_o, qi_o, ki_o, kib_o, wit_o, ktf_o, vtf_o, kitf_o):
    cos = cos_ref[...]
    sin = sin_ref[...]
    bd = bd_ref[...]
    lane = lax.broadcasted_iota(I32, cos.shape, 1)
    first_half = (lane & (HEAD_DIM // 2)) == 0

    def rope(x):
        partner = jnp.where(first_half, pltpu.roll(x, LANES - HEAD_DIM // 2, 1), pltpu.roll(x, HEAD_DIM // 2, 1))
        return x * cos + partner * sin

    def head_norm(x):
        hi, mid, lo = _split3(x * x)
        ms = (_dot(hi, bd) + _dot(mid, bd) + _dot(lo, bd)) * (1.0 / HEAD_DIM)
        return x * lax.rsqrt(ms + EPS)

    def put_heads(o_ref, first, y):
        for half in range(LANES // HEAD_DIM):
            o_ref[first + half] = y[:, half * HEAD_DIM:(half + 1) * HEAD_DIM].astype(BF16)

    qg = qg_ref[...]
    for c in range(ATTN_WIDTH // LANES):
        sl = slice(c * LANES, (c + 1) * LANES)
        put_heads(q_o, 2 * c, rope(head_norm(aq_ref[:, sl]) * qg) * (ATTN_SCALE * LOG2_E))
    k = rope(head_norm(ak_ref[...]) * kg_ref[...])
    k_o[...] = k
    ktf_o[...] = k.T
    put_heads(kb_o, 0, k)
    v_tf = av_ref[...].T
    vtf_o[...] = v_tf
    v_t = v_tf.astype(BF16)
    ones = jnp.ones((V_ROWS - HEAD_DIM, v_t.shape[1]), BF16)
    vt_o[...] = jnp.concatenate([v_t[:HEAD_DIM], ones, v_t[HEAD_DIM:], ones], axis=0)
    for c in range(IDX_HEADS * IDX_DIM // LANES):
        sl = slice(c * LANES, (c + 1) * LANES)
        put_heads(qi_o, 2 * c, rope(iq_ref[:, sl]))
    ikw = ikw_ref[...]
    ki_wide = rope(head_norm(ikw))
    ki = ki_wide[:, :IDX_DIM]
    ki_o[...] = ki
    kitf_o[...] = ki_wide.T[:IDX_DIM, :]
    kib_o[...] = ki.astype(BF16)
    wit_o[...] = ikw.T[IDX_DIM:IDX_DIM + 8, :]


def _prep(p, cos, sin, qg, kg, bd, tm, table_blocks):
    n = p.shape[0]

    def col(off, width):
        return pl.BlockSpec((tm, width), lambda i: (i, off // width))

    tab = pl.BlockSpec((tm, LANES), lambda i: (i % table_blocks, 0))
    vec = pl.BlockSpec((1, LANES), lambda i: (0, 0))

    def row(width):
        return pl.BlockSpec((tm, width), lambda i: (i, 0))

    def heads(h):
        return pl.BlockSpec((h, tm, HEAD_DIM), lambda i: (0, i, 0))

    n_seq = n // (tm * table_blocks)
    seq_len = tm * table_blocks

    def feature_major(width):
        return pl.BlockSpec((None, width, tm), lambda i: (i // table_blocks, 0, i % table_blocks))

    return pl.pallas_call(
        _prep_kernel,
        grid=(n // tm,),
        in_specs=[col(OFF_AQ, ATTN_WIDTH), col(OFF_AK, KV_WIDTH), col(OFF_AV, KV_WIDTH),
                  col(OFF_IQ, IDX_HEADS * IDX_DIM), col(OFF_IKW, LANES), tab, tab, vec, vec,
                  pl.BlockSpec((LANES, LANES), lambda i: (0, 0))],
        out_specs=[heads(N_HEADS), row(KV_WIDTH), heads(N_KV_HEADS),
                   pl.BlockSpec((None, N_KV_HEADS * V_ROWS, tm), lambda i: (i, 0, 0)),
                   heads(IDX_HEADS), row(IDX_DIM), row(IDX_DIM),
                   pl.BlockSpec((8, tm), lambda i: (0, i)),
                   feature_major(KV_WIDTH), feature_major(KV_WIDTH), feature_major(IDX_DIM)],
        out_shape=[jax.ShapeDtypeStruct((N_HEADS, n, HEAD_DIM), BF16),
                   jax.ShapeDtypeStruct((n, KV_WIDTH), F32),
                   jax.ShapeDtypeStruct((N_KV_HEADS, n, HEAD_DIM), BF16),
                   jax.ShapeDtypeStruct((n // tm, N_KV_HEADS * V_ROWS, tm), BF16),
                   jax.ShapeDtypeStruct((IDX_HEADS, n, IDX_DIM), BF16),
                   jax.ShapeDtypeStruct((n, IDX_DIM), F32),
                   jax.ShapeDtypeStruct((n, IDX_DIM), BF16),
                   jax.ShapeDtypeStruct((8, n), F32),
                   jax.ShapeDtypeStruct((n_seq, KV_WIDTH, seq_len), F32),
                   jax.ShapeDtypeStruct((n_seq, KV_WIDTH, seq_len), F32),
                   jax.ShapeDtypeStruct((n_seq, IDX_DIM, seq_len), F32)],
        compiler_params=_params(("parallel",)),
        name="prep",
    )(p, p, p, p, p, cos, sin, qg, kg, bd)


def _rope_tables(pos):
    half = HEAD_DIM // 2
    inv = ROPE_THETA ** (-jnp.arange(half, dtype=F32) / half)
    ang = pos.astype(F32)[:, None] * inv[None, :]
    c, s = jnp.cos(ang), jnp.sin(ang)
    return jnp.concatenate([c, c, c, c], axis=1), jnp.concatenate([-s, s, -s, s], axis=1)


def _pool_select(sums, lane):
    out = sums[-1]
    for gi in range(len(POOL_WINDOWS) - 2, -1, -1):
        out = jnp.where(lane < (gi + 1) * POOL_GROUP_DIM, sums[gi], out)
    return out


def _pool_kernel(u_ref, prev_ref, w_ref, s_ref, o_ref, carry_scr, *, tt, p0):
    t = pl.program_id(1)

    @pl.when(t == 0)
    def _():
        carry_scr[...] = prev_ref[...]

    u = u_ref[...]
    e = jnp.concatenate([carry_scr[...], u], axis=0)
    carry_scr[...] = u[tt - 16:, :]
    s2 = e[1:] + e[:-1]
    s4 = s2[2:] + s2[:-2]
    s8 = s4[4:] + s4[:-4]
    s16 = s8[8:] + s8[:-8]
    sums = (s2[15:15 + tt], s4[13:13 + tt], s8[9:9 + tt], s16[1:1 + tt])
    n_avail = (p0 + 1 + t * tt + lax.broadcasted_iota(I32, (tt, 1), 0)).astype(F32)
    lane = lax.broadcasted_iota(I32, (tt, POOL_WIDTH), 1)
    means = [sums[gi] / jnp.minimum(n_avail, float(w)) for gi, w in enumerate(POOL_WINDOWS)]
    pooled = _pool_select(means, lane) - u
    o_ref[...] = _dot(pooled.astype(BF16), w_ref[...]) * s_ref[...]


def _pool_prompt(p, prev16, wbd, scale, l, b, t_len):
    tt = min(512, t_len)
    nt = t_len // tt
    return pl.pallas_call(
        functools.partial(_pool_kernel, tt=tt, p0=0),
        grid=(b, nt),
        in_specs=[pl.BlockSpec((tt, POOL_WIDTH), lambda bi, ti: (bi * nt + ti, OFF_UP // POOL_WIDTH)),
                  pl.BlockSpec((None, 16, POOL_WIDTH), lambda bi, ti: (bi, 0, 0)),
                  pl.BlockSpec((None, POOL_WIDTH, POOL_WIDTH), lambda bi, ti: (l, 0, 0)),
                  pl.BlockSpec((None, 1, POOL_WIDTH), lambda bi, ti: (l, 0, 0))],
        out_specs=pl.BlockSpec((tt, POOL_WIDTH), lambda bi, ti: (bi * nt + ti, 0)),
        out_shape=jax.ShapeDtypeStruct((b * t_len, POOL_WIDTH), F32),
        scratch_shapes=[pltpu.VMEM((16, POOL_WIDTH), F32)],
        compiler_params=_params(("parallel", "arbitrary")),
        name="pool",
    )(p, prev16, wbd, scale)


def _pool_dec_kernel(u_ref, st_ref, w_ref, s_ref, o_ref, *, p0):
    u = u_ref[...]
    lane = lax.broadcasted_iota(I32, u.shape, 1)
    means = []
    acc = u
    d = 1
    for w in POOL_WINDOWS:
        while d < w:
            acc = acc + st_ref[POOL_STATE - d]
            d += 1
        means.append(acc / float(min(p0 + 1, w)))
    pooled = _pool_select(means, lane) - u
    o_ref[...] = _dot(pooled.astype(BF16), w_ref[...]) * s_ref[...]


def _pool_dec(p, st_t, wbd, scale, l, p0):
    n = p.shape[0]
    return pl.pallas_call(
        functools.partial(_pool_dec_kernel, p0=p0),
        grid=(1,),
        in_specs=[pl.BlockSpec((n, POOL_WIDTH), lambda i: (0, OFF_UP // POOL_WIDTH)),
                  pl.BlockSpec((POOL_STATE, n, POOL_WIDTH), lambda i: (0, 0, 0)),
                  pl.BlockSpec((None, POOL_WIDTH, POOL_WIDTH), lambda i: (l, 0, 0)),
                  pl.BlockSpec((None, 1, POOL_WIDTH), lambda i: (l, 0, 0))],
        out_specs=pl.BlockSpec((n, POOL_WIDTH), lambda i: (0, 0)),
        out_shape=jax.ShapeDtypeStruct((n, POOL_WIDTH), F32),
        compiler_params=_params(("arbitrary",)),
        name="pool_dec",
    )(p, st_t, wbd, scale)


def _hgrn_consts(c):
    r = np.arange(c)[:, None]
    s = np.arange(c)[None, :]
    mats = [(s <= r) & (s // h == r // h) for h in HGRN_LEVELS]
    mats += [(s > r) & (s // h == r // h) for h in HGRN_LEVELS]
    return jnp.asarray(np.concatenate(mats, axis=0).astype(np.float32), BF16)


def _hgrn_kernel(hq_ref, hf_ref, hi_ref, hg_ref, lb_ref, on_ref, ms_ref, yb_ref, st_ref, s_scr, *, c):
    step = pl.program_id(1)
    nl = len(HGRN_LEVELS)

    @pl.when(step == 0)
    def _():
        s_scr[...] = jnp.zeros_like(s_scr)

    ti = lax.broadcasted_iota(I32, (c, c), 0)
    si = lax.broadcasted_iota(I32, (c, c), 1)
    nb = c // HGRN_BASE
    tpos = lax.broadcasted_iota(I32, (1, HGRN_BASE, 1), 1)

    for hd in range(HGRN_HEADS):
        ks = slice(hd * HGRN_K, (hd + 1) * HGRN_K)
        vs = slice(hd * HGRN_V, (hd + 1) * HGRN_V)
        q = hq_ref[:, ks]
        lb = lb_ref[:, ks]
        f = lb + (1.0 - lb) * jax.nn.sigmoid(hf_ref[:, ks])
        g = jnp.log(jnp.maximum(f, F_FLOOR))
        kk = 1.0 - f
        iv = hi_ref[:, vs]
        ivb = iv.astype(BF16)
        g_hi, g_lo = _split2(g)
        r = _dot(ms_ref[...], jnp.concatenate([g_hi, g_lo], axis=1))
        r = r[:, :HGRN_K] + r[:, HGRN_K:]
        low = [r[k * c:(k + 1) * c] for k in range(nl)]
        up = [r[(nl + k) * c:(nl + k + 1) * c] for k in range(nl)]

        att = jnp.zeros((c, c), F32)
        for k, h in enumerate(HGRN_LEVELS[:-1]):
            qe = (q * jnp.exp(low[k])).astype(BF16)
            ke = (kk * jnp.exp(up[k])).astype(BF16)
            sh = h.bit_length() - 1
            tb = jnp.right_shift(ti, sh)
            pair = ((tb & 1) == 1) & (jnp.right_shift(si, sh) == tb - 1)
            att = jnp.where(pair, _dot_nt(qe, ke), att)
        qe = (q * jnp.exp(low[-1])).astype(BF16)
        ke = (kk * jnp.exp(up[-1])).astype(BF16)
        s_t = s_scr[hd]
        o = _dot(att.astype(BF16), ivb) + _dot_nt(qe, s_t.astype(BF16))

        l3 = low[0].reshape(nb, HGRN_BASE, HGRN_K)
        q3 = q.reshape(nb, HGRN_BASE, HGRN_K)
        k3 = kk.reshape(nb, HGRN_BASE, HGRN_K)
        i3 = iv.reshape(nb, HGRN_BASE, HGRN_V)
        o3 = jnp.zeros((nb, HGRN_BASE, HGRN_V), F32)
        for s in range(HGRN_BASE):
            e = jnp.exp(jnp.minimum(l3 - l3[:, s:s + 1, :], 0.0))
            a = jnp.sum(q3 * (k3[:, s:s + 1, :] * e), axis=-1, keepdims=True)
            a = jnp.where(tpos >= s, a, 0.0)
            o3 = o3 + a * i3[:, s:s + 1, :]
        o = o + o3.reshape(c, HGRN_V)

        s_scr[hd] = s_t * jnp.exp(low[-1][c - 1:c, :]) + _dot_tn(ivb, ke)

        on = o * lax.rsqrt(jnp.mean(o * o, axis=-1, keepdims=True) + EPS) * on_ref[...]
        yb_ref[:, vs] = on * _silu(hg_ref[:, vs])

    @pl.when(step == pl.num_programs(1) - 1)
    def _():
        st_ref[...] = s_scr[...]


def _hgrn_prompt(p, lb, onorm, mstack, b, t_len):
    c = HGRN_CHUNK
    nc = t_len // c
    kw = HGRN_HEADS * HGRN_K
    vw = HGRN_HEADS * HGRN_V

    def col(off, width):
        return pl.BlockSpec((c, width), lambda bi, ci: (bi * nc + ci, off // width))

    return pl.pallas_call(
        functools.partial(_hgrn_kernel, c=c),
        grid=(b, nc),
        in_specs=[col(OFF_HQ, kw), col(OFF_HF, kw), col(OFF_HI, vw), col(OFF_HG, vw),
                  pl.BlockSpec((1, kw), lambda bi, ci: (0, 0)),
                  pl.BlockSpec((1, HGRN_V), lambda bi, ci: (0, 0)),
                  pl.BlockSpec(mstack.shape, lambda bi, ci: (0, 0))],
        out_specs=[pl.BlockSpec((c, vw), lambda bi, ci: (bi * nc + ci, 0)),
                   pl.BlockSpec((None, HGRN_HEADS, HGRN_V, HGRN_K), lambda bi, ci: (bi, 0, 0, 0))],
        out_shape=[jax.ShapeDtypeStruct((b * t_len, vw), F32),
                   jax.ShapeDtypeStruct((b, HGRN_HEADS, HGRN_V, HGRN_K), F32)],
        scratch_shapes=[pltpu.VMEM((HGRN_HEADS, HGRN_V, HGRN_K), F32)],
        compiler_params=_params(("parallel", "arbitrary")),
        name="hgrn",
    )(p, p, p, p, lb, onorm, mstack)


def _hgrn_dec_kernel(q_ref, z_ref, lb_ref, i_ref, hg_ref, on_ref, s_ref, yb_ref, so_ref):
    lb = lb_ref[...]
    f = lb + (1.0 - lb) * jax.nn.sigmoid(z_ref[...])
    a = jnp.exp(jnp.log(jnp.maximum(f, F_FLOOR)))
    s_new = a * s_ref[...] + i_ref[...] * (1.0 - f)
    so_ref[...] = s_new
    o = jnp.sum(q_ref[...] * s_new, axis=-1, keepdims=True)
    on = o * lax.rsqrt(jnp.mean(o * o, axis=-2, keepdims=True) + EPS) * on_ref[...]
    yb_ref[...] = on * _silu(hg_ref[...])


def _hgrn_dec(q_r, z_r, lb_r, i_c, hg_c, onorm_c, state_t, l):
    n = state_t.shape[1]
    nb = 8
    row = pl.BlockSpec((nb, HGRN_HEADS, 1, HGRN_K), lambda i: (i, 0, 0, 0))
    col = pl.BlockSpec((nb, HGRN_HEADS, HGRN_V, 1), lambda i: (i, 0, 0, 0))
    return pl.pallas_call(
        _hgrn_dec_kernel,
        grid=(n // nb,),
        in_specs=[row, row, pl.BlockSpec((HGRN_HEADS, 1, HGRN_K), lambda i: (0, 0, 0)), col, col,
                  pl.BlockSpec((HGRN_V, 1), lambda i: (0, 0)),
                  pl.BlockSpec((None, nb, HGRN_HEADS, HGRN_V, HGRN_K), lambda i: (l, i, 0, 0, 0))],
        out_specs=[col, pl.BlockSpec((nb, HGRN_HEADS, HGRN_V, HGRN_K), lambda i: (i, 0, 0, 0))],
        out_shape=[jax.ShapeDtypeStruct((n, HGRN_HEADS, HGRN_V, 1), F32),
                   jax.ShapeDtypeStruct(state_t.shape[1:], F32)],
        compiler_params=_params(("parallel",)),
        name="hgrn_dec",
    )(q_r, z_r, lb_r, i_c, hg_c, onorm_c, state_t)


def _sort_key(s):
    bits = pltpu.bitcast(s, I32)
    return jnp.where(bits < 0, INT_MIN - bits, bits)


def _lane_fold(x):
    part = x[:, :LANES]
    for cidx in range(1, x.shape[1] // LANES):
        part = part + x[:, cidx * LANES:(cidx + 1) * LANES]
    return part


def _select_bias(key, thr, idx, j_lim):
    return jnp.where(key > thr, 0.0, jnp.where(key == thr, jnp.where(idx < j_lim, 0.0, NEG_BIG), NEG_BIG))


def _bisect_bits(count_ge, thr, first_bit, n_bits, k):
    def bit_body(it, thr):
        cand = thr + lax.shift_left(jnp.int32(1), first_bit - it)
        return jnp.where(count_ge(cand) >= k, cand, thr)

    return lax.fori_loop(0, n_bits, bit_body, thr)


def _topk_threshold(count, stat_shape, n_sel, idx_bits, idx_sentinel, kth_largest=None):
    k = float(n_sel)
    if kth_largest is None:
        thr = _bisect_bits(lambda cand: count(lambda key, idx: jnp.where(key >= cand, 1.0, 0.0)),
                           jnp.full(stat_shape, INT_MIN, I32), 31, 32, k)
    else:
        thr = kth_largest(k)
    n_ge = count(lambda key, idx: jnp.where(key >= thr, 1.0, 0.0))
    n_gt = count(lambda key, idx: jnp.where(key > thr, 1.0, 0.0))
    need = k - n_gt
    live = thr > INT_MIN
    tie = jnp.where(live, jnp.where(n_ge > k, 1.0, 0.0), 0.0)
    j_default = jnp.where(live, jnp.int32(idx_sentinel), jnp.int32(0))

    def tie_search():
        def idx_body(it, j_lim):
            cand = j_lim + lax.shift_left(jnp.int32(1), idx_bits - 1 - it)
            cnt = count(lambda key, idx: jnp.where(key == thr, jnp.where(idx < cand, 1.0, 0.0), 0.0))
            return jnp.where(cnt <= need, cand, j_lim)

        j_lim = lax.fori_loop(0, idx_bits, idx_body, jnp.zeros(stat_shape, I32))
        return jnp.where(tie > 0.0, j_lim, j_default)

    j_lim = lax.cond(jnp.max(tie) > 0.0, tie_search, lambda: j_default)
    return thr, j_lim


FOLD_ROWS = 64
V_ROWS = HEAD_DIM + 16


def _fold(x, op):
    parts = [x[r * FOLD_ROWS:(r + 1) * FOLD_ROWS] for r in range(x.shape[0] // FOLD_ROWS)]
    while len(parts) > 1:
        parts = [op(parts[i], parts[i + 1]) if i + 1 < len(parts) else parts[i]
                 for i in range(0, len(parts), 2)]
    return parts[0]


def _dsa_kernel(q_ref, qi_ref, wit_ref, k_ref, vt_ref, ki_ref, o_ref, keys_scr, k16_scr, lg_scr, p_scr, m_scr, acc_scr,
                *, tq, tk, n_sel, t_len):
    i = pl.program_id(1)
    nk = ((i + 1) * tq + tk - 1) // tk
    krow = lax.broadcasted_iota(I32, (tk, tq), 0)
    qpos = i * tq + lax.broadcasted_iota(I32, (tk, tq), 1)
    wit = wit_ref[...] * IDX_SCALE
    qi_all = qi_ref[...].reshape(IDX_HEADS * tq, IDX_DIM)

    def rows(j):
        return pl.ds(pl.multiple_of(j * tk, tk), tk)

    def score_tile(j, causal):
        rel = _dot_nt(ki_ref[rows(j), :], qi_all)
        s = wit[0:1, :] * jnp.maximum(rel[:, :tq], 0.0)
        for h in range(1, IDX_HEADS):
            s = s + wit[h:h + 1, :] * jnp.maximum(rel[:, h * tq:(h + 1) * tq], 0.0)
        key = jnp.where(s > 0.5 * NEG_BIG, _sort_key(s), INT_MIN)
        if causal:
            key = jnp.where(j * tk + krow <= qpos, key, INT_MIN)
        keys_scr[j] = key
        k16_scr[j] = jnp.right_shift(key, 16).astype(I16)

    def score_body(j, carry):
        score_tile(j, False)
        return carry

    lax.fori_loop(0, nk - 1, score_body, 0)
    score_tile(nk - 1, True)

    def count(pred):
        def body(j, acc):
            return acc + _fold(pred(keys_scr[j], j * tk + krow), jnp.add)

        acc = lax.fori_loop(0, nk, body, jnp.zeros((FOLD_ROWS, tq), F32))
        return jnp.sum(acc, axis=0, keepdims=True)

    one16, zero16 = jnp.int16(1), jnp.int16(0)

    def count16(pred):
        def body(j, acc):
            return acc + _fold(pred(k16_scr[j]), jnp.add)

        acc = lax.fori_loop(0, nk, body, jnp.zeros((FOLD_ROWS, tq), I16))
        return jnp.sum(acc.astype(F32), axis=0, keepdims=True)

    def high16(x):
        return jnp.right_shift(x, 16).astype(I16)

    def low16(x):
        return ((x & 0xFFFF) - 32768).astype(I16)

    def kth_largest(k):
        thr = jnp.full((1, tq), INT_MIN, I32)
        thr = _bisect_bits(lambda cand: count16(lambda t: jnp.where(t >= high16(cand), one16, zero16)),
                           thr, 31, 16, k)
        thr_hi = high16(thr)
        n_above = count16(lambda t: jnp.where(t > thr_hi, one16, zero16))

        def low_body(j, carry):
            key = keys_scr[j]
            k16_scr[j] = jnp.where(high16(key) == thr_hi, low16(key), jnp.int16(-32768))
            return carry

        lax.fori_loop(0, nk, low_body, 0)
        return _bisect_bits(
            lambda cand: n_above + count16(lambda t: jnp.where(t >= low16(cand), one16, zero16)),
            thr, 15, 16, k)

    thr, j_lim = _topk_threshold(count, (1, tq), n_sel, int(t_len).bit_length(), t_len, kth_largest)

    m_scr[...] = jnp.full(m_scr.shape, 0.5 * NEG_BIG, F32)
    acc_scr[...] = jnp.zeros_like(acc_scr)

    def attend_tile(j, carry):
        bias = _select_bias(keys_scr[j], thr, j * tk + krow, j_lim)
        bias = jnp.concatenate([bias] * GROUP, axis=1)
        v_t = vt_ref[j]
        heads = range(N_KV_HEADS)
        for n in heads:
            qg = q_ref[n * GROUP:(n + 1) * GROUP].reshape(GROUP * tq, HEAD_DIM)
            lg_scr[n] = _dot_nt(k_ref[n, rows(j), :], qg) + bias
        m_old = [m_scr[n] for n in heads]
        m_new = [jnp.maximum(m_old[n], jnp.max(_fold(lg_scr[n], jnp.maximum), axis=0, keepdims=True))
                 for n in heads]
        for n in heads:
            p_scr[n] = jnp.exp2(lg_scr[n] - m_new[n]).astype(BF16)
        for n in heads:
            acc_scr[n] = (jnp.exp2(m_old[n] - m_new[n]) * acc_scr[n]
                          + _dot(v_t[n * V_ROWS:(n + 1) * V_ROWS, :], p_scr[n]))
            m_scr[n] = m_new[n]
        return carry

    lax.fori_loop(0, nk, attend_tile, 0)
    outs = []
    for n in range(N_KV_HEADS):
        acc = acc_scr[n]
        o_n = acc[:HEAD_DIM] / acc[HEAD_DIM:HEAD_DIM + 1]
        outs += [o_n[:, g * tq:(g + 1) * tq] for g in range(GROUP)]
    o_ref[...] = jnp.concatenate(outs, axis=0).T


def _dsa_prompt(q_h, qi_h, wit, k_h, v_t, ki_b, b, t_len, n_sel, tk):
    tq = min(256, t_len)
    nq = t_len // tq
    nt = t_len // tk

    def heads(h, rows, imap):
        return pl.BlockSpec((h, rows, HEAD_DIM), imap)

    return pl.pallas_call(
        functools.partial(_dsa_kernel, tq=tq, tk=tk, n_sel=n_sel, t_len=t_len),
        grid=(b, nq),
        in_specs=[heads(N_HEADS, tq, lambda bi, qi: (0, bi * nq + qi, 0)),
                  heads(IDX_HEADS, tq, lambda bi, qi: (0, bi * nq + qi, 0)),
                  pl.BlockSpec((8, tq), lambda bi, qi: (0, bi * nq + qi)),
                  heads(N_KV_HEADS, t_len, lambda bi, qi: (0, bi, 0)),
                  pl.BlockSpec((nt, N_KV_HEADS * V_ROWS, tk), lambda bi, qi: (bi, 0, 0)),
                  pl.BlockSpec((t_len, IDX_DIM), lambda bi, qi: (bi, 0))],
        out_specs=pl.BlockSpec((tq, ATTN_WIDTH), lambda bi, qi: (bi * nq + qi, 0)),
        out_shape=jax.ShapeDtypeStruct((b * t_len, ATTN_WIDTH), F32),
        scratch_shapes=[pltpu.VMEM((nt, tk, tq), I32),
                        pltpu.VMEM((nt, tk, tq), I16),
                        pltpu.VMEM((N_KV_HEADS, tk, GROUP * tq), F32),
                        pltpu.VMEM((N_KV_HEADS, tk, GROUP * tq), BF16),
                        pltpu.VMEM((N_KV_HEADS, 1, GROUP * tq), F32),
                        pltpu.VMEM((N_KV_HEADS, V_ROWS, GROUP * tq), F32)],
        compiler_params=_params(("parallel", "arbitrary")),
        name="dsa",
    )(q_h, qi_h, wit, k_h, v_t, ki_b)


def _dsa_dec_score_kernel(pt_ref, qi_ref, wi_ref, kin_ref, *rest, n_pages):
    page_refs = rest[:n_pages]
    o_ref = rest[n_pages]
    qi = qi_ref[...]
    wi = wi_ref[...]

    def finish(rel):
        s = jnp.sum(wi * jnp.maximum(rel, 0.0), axis=0, keepdims=True) * IDX_SCALE
        return jnp.where(s > 0.5 * NEG_BIG, _sort_key(s), INT_MIN)

    for pg in range(n_pages):
        o_ref[:, pg * PAGE_SIZE:(pg + 1) * PAGE_SIZE] = finish(_dot(qi, page_refs[pg][...].astype(BF16)))
    kin = kin_ref[...].astype(BF16).astype(F32)
    rel_new = jnp.sum(qi.astype(F32) * kin, axis=1, keepdims=True)
    key_new = finish(rel_new)
    lane = lax.broadcasted_iota(I32, (1, PAGE_SIZE), 1)
    o_ref[:, n_pages * PAGE_SIZE:] = jnp.where(lane == 0, key_new, INT_MIN)


def _dsa_dec_scores(page_table, qi8, wi8, ki_new, cache_ki_t, l):
    n, n_pages = page_table.shape
    width = (n_pages + 1) * PAGE_SIZE
    page_specs = [pl.BlockSpec((None, None, IDX_DIM, PAGE_SIZE),
                               functools.partial(lambda bi, pt, pg: (l, pt[bi, pg], 0, 0), pg=pg))
                  for pg in range(n_pages)]
    return pl.pallas_call(
        functools.partial(_dsa_dec_score_kernel, n_pages=n_pages),
        grid_spec=pltpu.PrefetchScalarGridSpec(
            num_scalar_prefetch=1,
            grid=(n,),
            in_specs=[pl.BlockSpec((None, 8, IDX_DIM), lambda bi, pt: (bi, 0, 0)),
                      pl.BlockSpec((None, 8, 1), lambda bi, pt: (bi, 0, 0)),
                      pl.BlockSpec((None, 1, IDX_DIM), lambda bi, pt: (bi, 0, 0))] + page_specs,
            out_specs=pl.BlockSpec((None, 1, width), lambda bi, pt: (bi, 0, 0))),
        out_shape=jax.ShapeDtypeStruct((n, 1, width), I32),
        compiler_params=_params(("arbitrary",)),
        name="dsa_dec_scores",
    )(page_table, qi8, wi8, ki_new, *([cache_ki_t] * n_pages))


def _dsa_dec_thr_kernel(keys_ref, thr_ref, j_ref, *, n_sel):
    n, width = keys_ref.shape
    lane = lax.broadcasted_iota(I32, (n, width), 1)

    def count(pred):
        return jnp.sum(_lane_fold(pred(keys_ref[...], lane)), axis=1, keepdims=True)

    thr, j_lim = _topk_threshold(count, (n, 1), n_sel, int(width).bit_length(), width)
    thr_ref[...] = thr
    j_ref[...] = j_lim


def _dsa_dec_thr(keys2d, n_sel):
    n, width = keys2d.shape
    return pl.pallas_call(
        functools.partial(_dsa_dec_thr_kernel, n_sel=n_sel),
        grid=(1,),
        in_specs=[pl.BlockSpec((n, width), lambda i: (0, 0))],
        out_specs=[pl.BlockSpec((n, 1), lambda i: (0, 0)), pl.BlockSpec((n, 1), lambda i: (0, 0))],
        out_shape=[jax.ShapeDtypeStruct((n, 1), I32), jax.ShapeDtypeStruct((n, 1), I32)],
        compiler_params=_params(("arbitrary",)),
        name="dsa_dec_thr",
    )(keys2d)


def _dsa_dec_attn_kernel(pt_ref, thr_ref, j_ref, q_ref, keys_ref, kn_ref, vn_ref, *rest, n_pages):
    k_pages = rest[:n_pages]
    v_pages = rest[n_pages:2 * n_pages]
    o_ref, k_scr, v_scr = rest[2 * n_pages:]
    bi = pl.program_id(0)
    n_past = n_pages * PAGE_SIZE
    for pg in range(n_pages):
        sl = slice(pg * PAGE_SIZE, (pg + 1) * PAGE_SIZE)
        k_scr[:, sl] = k_pages[pg][...].astype(BF16)
        v_scr[:, sl] = v_pages[pg][...].astype(BF16)
    key = keys_ref[...]
    idx = lax.broadcasted_iota(I32, key.shape, 1)
    bias_all = _select_bias(key, thr_ref[bi], idx, j_ref[bi])
    bias = bias_all[:, :n_past]
    bias_new = bias_all[:, n_past:n_past + 1]
    kn = kn_ref[...].astype(BF16).astype(F32)
    vn = vn_ref[...].astype(BF16).astype(F32)
    for n in range(N_KV_HEADS):
        fs = slice(n * HEAD_DIM, (n + 1) * HEAD_DIM)
        q = q_ref[n]
        lg = _dot(q, k_scr[fs, :]) + bias
        lg_new = jnp.sum(q.astype(F32) * kn[:, fs], axis=1, keepdims=True) + bias_new
        m = jnp.maximum(jnp.max(lg, axis=1, keepdims=True), lg_new)
        p = jnp.exp2(lg - m)
        p_new = jnp.exp2(lg_new - m)
        den = jnp.sum(p, axis=1, keepdims=True) + p_new
        o_ref[n] = (_dot_nt(p.astype(BF16), v_scr[fs, :]) + p_new * vn[:, fs]) / den


def _dsa_dec_attn(page_table, thr, j_lim, q8, keys, k_new, v_new, cache_k_t, cache_v_t, l):
    n, n_pages = page_table.shape
    width = (n_pages + 1) * PAGE_SIZE

    def page_spec(pg):
        return pl.BlockSpec((None, None, KV_WIDTH, PAGE_SIZE),
                            lambda bi, pt, th, jl: (l, pt[bi, pg], 0, 0))

    page_specs = [page_spec(pg) for pg in range(n_pages)]
    qspec = pl.BlockSpec((None, N_KV_HEADS, 8, HEAD_DIM), lambda bi, pt, th, jl: (bi, 0, 0, 0))
    return pl.pallas_call(
        functools.partial(_dsa_dec_attn_kernel, n_pages=n_pages),
        grid_spec=pltpu.PrefetchScalarGridSpec(
            num_scalar_prefetch=3,
            grid=(n,),
            in_specs=[qspec,
                      pl.BlockSpec((None, 1, width), lambda bi, pt, th, jl: (bi, 0, 0)),
                      pl.BlockSpec((None, 1, KV_WIDTH), lambda bi, pt, th, jl: (bi, 0, 0)),
                      pl.BlockSpec((None, 1, KV_WIDTH), lambda bi, pt, th, jl: (bi, 0, 0))]
            + page_specs + page_specs,
            out_specs=qspec,
            scratch_shapes=[pltpu.VMEM((KV_WIDTH, n_pages * PAGE_SIZE), BF16),
                            pltpu.VMEM((KV_WIDTH, n_pages * PAGE_SIZE), BF16)]),
        out_shape=jax.ShapeDtypeStruct((n, N_KV_HEADS, 8, HEAD_DIM), F32),
        compiler_params=_params(("arbitrary",)),
        name="dsa_dec_attn",
    )(page_table, thr, j_lim, q8, keys, k_new, v_new, *([cache_k_t] * n_pages), *([cache_v_t] * n_pages))


def _merge_kernel(x_ref, g0_ref, g1_ref, g2_ref, ya_ref, yb_ref, yc_ref, gt_ref,
                  wa_ref, wb_ref, wc_ref, wo_ref, o_ref):
    merged = (jax.nn.sigmoid(g0_ref[...]) * _dot(ya_ref[...].astype(BF16), wa_ref[...])
              + jax.nn.sigmoid(g1_ref[...]) * _dot(yb_ref[...].astype(BF16), wb_ref[...])
              + jax.nn.sigmoid(g2_ref[...]) * _dot(yc_ref[...].astype(BF16), wc_ref[...]))
    o_ref[...] = x_ref[...] + gt_ref[...] * _dot(merged.astype(BF16), wo_ref[...])


def _merge(x, p, ya, yb, yc, mods, w_ba, w_bb, w_bc, w_out, l, rows_per_seq):
    n = x.shape[0]
    tm = _row_tile(n, mods, rows_per_seq, 512)

    def wspec(rows):
        return pl.BlockSpec((None, rows, D_MODEL), lambda i: (l, 0, 0))

    def row(width):
        return pl.BlockSpec((tm, width), lambda i: (i, 0))

    def gate(bidx):
        return pl.BlockSpec((tm, D_MODEL), lambda i: (i, OFF_GL // D_MODEL + bidx))

    return pl.pallas_call(
        _merge_kernel,
        grid=(n // tm,),
        in_specs=[row(D_MODEL), gate(0), gate(1), gate(2), row(POOL_WIDTH), row(HGRN_HEADS * HGRN_V),
                  row(ATTN_WIDTH), _mod_spec(mods, l, 5, tm, rows_per_seq),
                  wspec(POOL_WIDTH), wspec(HGRN_HEADS * HGRN_V), wspec(ATTN_WIDTH), wspec(D_MODEL)],
        out_specs=row(D_MODEL),
        out_shape=jax.ShapeDtypeStruct((n, D_MODEL), F32),
        compiler_params=_params(("parallel",)),
        name="merge",
    )(x, p, p, p, ya, yb, yc, mods, w_ba, w_bb, w_bc, w_out)


def _permute_w_in(w_in):
    segs = [(2884, 5956), (256, 768), (768, 1280), (1792, 2304), (0, 256), (1280, 1536), (1536, 1792),
            (2560, 2816), (2304, 2432), (2432, 2560), (2816, 2884)]
    parts = [w_in[..., a:b] for a, b in segs]
    used = sum(b - a for a, b in segs)
    parts.append(jnp.zeros(w_in.shape[:-1] + (IN_PAD - used,), w_in.dtype))
    return jnp.concatenate(parts, axis=-1).astype(BF16)


def _lower_bounds(p):
    sm = jax.nn.softmax(p.astype(F32), axis=0)
    return jnp.cumsum(sm, axis=0) - sm[0:1]


def kernel(x_prompt, x_sample, c_prompt, c_sample, cache_k, cache_v, cache_ki, page_table, state_pool, state_hgrn, ada_w, ada_b, norm_g, ffn_wg, ffn_wu, ffn_wd, w_in, pool_w, pool_scale, hgrn_lb, hgrn_onorm, q_norm, k_norm, w_ba, w_bb, w_bc, w_out):
    bp, t_len, _ = x_prompt.shape
    bs = x_sample.shape[0]
    assert x_sample.shape[1] == 1
    n_pages = page_table.shape[1]
    past_len = n_pages * PAGE_SIZE
    n_phys = cache_k.shape[1]
    n_sel_p = min(TOPK_MAX, t_len // 4)
    n_sel_s = min(TOPK_MAX, (past_len + 1) // 4)

    wg, wu, wd = ffn_wg.astype(BF16), ffn_wu.astype(BF16), ffn_wd.astype(BF16)
    w_in_p = _permute_w_in(w_in)
    wa, wb, wc, wo = w_ba.astype(BF16), w_bb.astype(BF16), w_bc.astype(BF16), w_out.astype(BF16)
    eye = jnp.eye(len(POOL_WINDOWS), dtype=F32)
    pool_wbd = jnp.einsum('lgcd,gh->lgchd', pool_w, eye).reshape(DEPTH, POOL_WIDTH, POOL_WIDTH).astype(BF16)
    pool_sc = pool_scale.reshape(DEPTH, 1, POOL_WIDTH)
    norm_g4 = norm_g.reshape(DEPTH, N_SUB, 1, D_MODEL)
    lbs = _lower_bounds(hgrn_lb)
    qg = jnp.tile(q_norm, (1, LANES // HEAD_DIM)).reshape(DEPTH, 1, LANES)
    kg = jnp.tile(k_norm, (1, LANES // HEAD_DIM)).reshape(DEPTH, 1, LANES)
    gid = np.arange(LANES) // HEAD_DIM
    bd = jnp.asarray((gid[:, None] == gid[None, :]).astype(np.float32), BF16)
    mstack = _hgrn_consts(HGRN_CHUNK)
    cos_p, sin_p = _rope_tables(jnp.arange(t_len, dtype=I32))
    cos_s, sin_s = _rope_tables(jnp.full((bs,), past_len, I32))
    ck = jnp.transpose(cache_k, (0, 1, 3, 4, 2)).reshape(DEPTH, n_phys, KV_WIDTH, PAGE_SIZE)
    cv = jnp.transpose(cache_v, (0, 1, 3, 4, 2)).reshape(DEPTH, n_phys, KV_WIDTH, PAGE_SIZE)
    cki = jnp.transpose(cache_ki, (0, 1, 3, 2))
    state_hgrn_t = jnp.swapaxes(state_hgrn, -1, -2)

    rows_all = bp + bs
    rows_pad = -(-rows_all // 8) * 8
    c_all = jnp.concatenate([c_prompt, c_sample, jnp.zeros((rows_pad - rows_all, D_MODEL), F32)], axis=0)
    mods = _ada(c_all, ada_w, ada_b)
    mods_p = mods[:, :bp].reshape(DEPTH, bp, 1, 3 * N_SUB * D_MODEL)
    mods_s = mods[:, bp:rows_all]

    xp = x_prompt.reshape(bp * t_len, D_MODEL)
    xs = x_sample.reshape(bs, D_MODEL)
    pool_prev_p = jnp.zeros((bp, 16, POOL_WIDTH), F32)
    outs = [[] for _ in range(10)]
    tm_prep = min(512, t_len)

    for l in range(DEPTH):
        xp = _ffn(xp, mods_p, norm_g4, wg, wu, wd, l, 0, 0, t_len)
        p = _inproj(xp, mods_p, norm_g4, w_in_p, l, t_len)
        ya = _pool_prompt(p, pool_prev_p, pool_wbd, pool_sc, l, bp, t_len)
        yb, st_t = _hgrn_prompt(p, lbs[l].reshape(1, -1), hgrn_onorm[l].reshape(1, -1), mstack, bp, t_len)
        q_h, _, k_h, v_t, qi_h, _, ki_b, wit, k_tf, v_tf, ki_tf = _prep(
            p, cos_p, sin_p, qg[l], kg[l], bd, tm_prep, t_len // tm_prep)
        yc = _dsa_prompt(q_h, qi_h, wit, k_h, v_t, ki_b, bp, t_len, n_sel_p, tm_prep)
        xp = _merge(xp, p, ya, yb, yc, mods_p, wa, wb, wc, wo, l, t_len)
        xp = _ffn(xp, mods_p, norm_g4, wg, wu, wd, l, 1, 2, t_len)
        p3 = p.reshape(bp, t_len, IN_PAD)
        outs[0].append(jnp.transpose(k_tf.reshape(bp, N_KV_HEADS, HEAD_DIM, t_len), (0, 3, 1, 2)))
        outs[1].append(jnp.transpose(v_tf.reshape(bp, N_KV_HEADS, HEAD_DIM, t_len), (0, 3, 1, 2)))
        outs[2].append(jnp.swapaxes(ki_tf, 1, 2))
        outs[3].append(p3[:, t_len - POOL_STATE:, OFF_UP:OFF_UP + POOL_WIDTH])
        outs[4].append(jnp.swapaxes(st_t, -1, -2))

        xs = _ffn(xs, mods_s, norm_g4, wg, wu, wd, l, 0, 0, 1)
        ps = _inproj(xs, mods_s, norm_g4, w_in_p, l, 1)
        ya = _pool_dec(ps, jnp.swapaxes(state_pool[l], 0, 1), pool_wbd, pool_sc, l, past_len)
        kw = HGRN_HEADS * HGRN_K
        vw = HGRN_HEADS * HGRN_V
        yb4, st_new = _hgrn_dec(
            ps[:, OFF_HQ:OFF_HQ + kw].reshape(bs, HGRN_HEADS, 1, HGRN_K),
            ps[:, OFF_HF:OFF_HF + kw].reshape(bs, HGRN_HEADS, 1, HGRN_K),
            lbs[l].reshape(HGRN_HEADS, 1, HGRN_K),
            ps[:, OFF_HI:OFF_HI + vw].reshape(bs, HGRN_HEADS, HGRN_V, 1),
            ps[:, OFF_HG:OFF_HG + vw].reshape(bs, HGRN_HEADS, HGRN_V, 1),
            hgrn_onorm[l].reshape(-1, 1), state_hgrn_t, l)
        yb = yb4.reshape(bs, vw)
        q_h, k_r, _, _, qi_h, ki_r, _, _, _, _, _ = _prep(ps, cos_s, sin_s, qg[l], kg[l], bd, bs, 1)
        v_new = ps[:, OFF_AV:OFF_AV + KV_WIDTH]
        qi8 = jnp.pad(jnp.swapaxes(qi_h, 0, 1), ((0, 0), (0, 8 - IDX_HEADS), (0, 0)))
        wi8 = jnp.pad(ps[:, OFF_IKW + IDX_DIM:OFF_IKW + IDX_DIM + IDX_HEADS].reshape(bs, IDX_HEADS, 1),
                      ((0, 0), (0, 8 - IDX_HEADS), (0, 0)))
        keys = _dsa_dec_scores(page_table, qi8, wi8, ki_r.reshape(bs, 1, IDX_DIM), cki, l)
        thr, j_lim = _dsa_dec_thr(keys.reshape(bs, -1), n_sel_s)
        q8 = jnp.pad(jnp.swapaxes(q_h, 0, 1).reshape(bs, N_KV_HEADS, GROUP, HEAD_DIM),
                     ((0, 0), (0, 0), (0, 8 - GROUP), (0, 0)))
        yc8 = _dsa_dec_attn(page_table, thr.reshape(bs), j_lim.reshape(bs), q8, keys,
                            k_r.reshape(bs, 1, KV_WIDTH), v_new.reshape(bs, 1, KV_WIDTH), ck, cv, l)
        yc = yc8[:, :, :GROUP, :].reshape(bs, ATTN_WIDTH)
        xs = _merge(xs, ps, ya, yb, yc, mods_s, wa, wb, wc, wo, l, 1)
        xs = _ffn(xs, mods_s, norm_g4, wg, wu, wd, l, 1, 2, 1)
        outs[5].append(k_r.reshape(bs, 1, N_KV_HEADS, HEAD_DIM))
        outs[6].append(v_new.reshape(bs, 1, N_KV_HEADS, HEAD_DIM))
        outs[7].append(ki_r.reshape(bs, 1, IDX_DIM))
        outs[8].append(jnp.concatenate([state_pool[l][:, 1:], ps[:, None, OFF_UP:OFF_UP + POOL_WIDTH]], axis=1))
        outs[9].append(jnp.swapaxes(st_new, -1, -2))

    stacked = [jnp.stack(o) for o in outs]
    return (xp.reshape(bp, t_len, D_MODEL), xs.reshape(bs, 1, D_MODEL), *stacked)
```

```python
import functools

import numpy as np
import jax
import jax.numpy as jnp
from jax import lax
from jax.experimental import pallas as pl
from jax.experimental.pallas import tpu as pltpu

F32 = jnp.float32
BF16 = jnp.bfloat16
I32 = jnp.int32
I16 = jnp.int16

D_MODEL = 1024
DEPTH = 4
PAGE_SIZE = 128
N_SUB = 3
POOL_WINDOWS = (2, 4, 8, 16)
POOL_GROUP_DIM = 64
POOL_WIDTH = 256
POOL_STATE = 15
HGRN_HEADS = 4
HGRN_K = 128
HGRN_V = 64
F_FLOOR = 1e-30
N_HEADS = 8
N_KV_HEADS = 2
HEAD_DIM = 64
GROUP = N_HEADS // N_KV_HEADS
ATTN_WIDTH = 512
KV_WIDTH = 128
IDX_HEADS = 4
IDX_DIM = 64
TOPK_MAX = 256
ROPE_THETA = 10000.0
LOG2_E = 1.4426950408889634
ATTN_SCALE = HEAD_DIM ** -0.5
IDX_SCALE = (IDX_DIM * IDX_HEADS) ** -0.5
NEG_BIG = -1e30
D_FF = 2816
EPS = 1e-6
INT_MIN = np.int32(-2147483648)

LANES = 128
VMEM_LIMIT = 48 * 1024 * 1024

OFF_GL, OFF_HQ, OFF_HF, OFF_AQ = 0, 3072, 3584, 4096
OFF_UP, OFF_HI, OFF_HG, OFF_IQ = 4608, 4864, 5120, 5376
OFF_AK, OFF_AV, OFF_IKW = 5632, 5760, 5888
IN_PAD = 6144
HGRN_CHUNK = 128
HGRN_BASE = 16
HGRN_LEVELS = tuple(HGRN_BASE << i for i in range((HGRN_CHUNK // HGRN_BASE).bit_length()))


def _dot(a, b):
    return jnp.dot(a, b, preferred_element_type=F32)


def _dot_nt(a, b):
    return lax.dot_general(a, b, (((1,), (1,)), ((), ())), preferred_element_type=F32)


def _dot_tn(a, b):
    return lax.dot_general(a, b, (((0,), (0,)), ((), ())), preferred_element_type=F32)


def _split2(x):
    hi = x.astype(BF16)
    lo = (x - hi.astype(F32)).astype(BF16)
    return hi, lo


def _split3(x):
    hi = x.astype(BF16)
    r = x - hi.astype(F32)
    mid = r.astype(BF16)
    lo = (r - mid.astype(F32)).astype(BF16)
    return hi, mid, lo


def _silu(x):
    return x * jax.nn.sigmoid(x)


def _params(sem):
    return pltpu.CompilerParams(dimension_semantics=sem, vmem_limit_bytes=VMEM_LIMIT)


def _mod_spec(mods, l, m, tm, rows_per_seq):
    if mods.ndim == 4:
        return pl.BlockSpec((None, None, 1, D_MODEL), lambda i, *_: (l, (i * tm) // rows_per_seq, 0, m))
    return pl.BlockSpec((None, tm, D_MODEL), lambda i, *_: (l, i, m))


def _row_tile(n, mods, rows_per_seq, cap):
    return min(cap, rows_per_seq if mods.ndim == 4 else n)


def _prenorm(x, g, scale, shift):
    y = x * lax.rsqrt(jnp.mean(x * x, axis=-1, keepdims=True) + EPS) * g
    return y * (1.0 + scale) + shift


def _ada_kernel(c_ref, w_ref, b_ref, o_ref):
    c = c_ref[...]
    o_ref[...] = jnp.dot(_silu(c), w_ref[...], precision=lax.Precision.HIGHEST,
                         preferred_element_type=F32) + b_ref[...]


def _ada(c_all, ada_w, ada_b):
    rows = c_all.shape[0]
    width = ada_w.shape[-1]
    tn = 1024
    return pl.pallas_call(
        _ada_kernel,
        grid=(DEPTH, width // tn),
        in_specs=[pl.BlockSpec((rows, D_MODEL), lambda l, j: (0, 0)),
                  pl.BlockSpec((None, D_MODEL, tn), lambda l, j: (l, 0, j)),
                  pl.BlockSpec((None, 1, tn), lambda l, j: (l, 0, j))],
        out_specs=pl.BlockSpec((None, rows, tn), lambda l, j: (l, 0, j)),
        out_shape=jax.ShapeDtypeStruct((DEPTH, rows, width), F32),
        compiler_params=_params(("parallel", "parallel")),
        name="ada",
    )(c_all, ada_w, ada_b.reshape(DEPTH, 1, width))


def _ffn_kernel(x_ref, sh_ref, sc_ref, gt_ref, g_ref, wg_ref, wu_ref, wd_ref, o_ref, h_scr, acc_scr):
    j = pl.program_id(1)

    @pl.when(j == 0)
    def _():
        h_scr[...] = _prenorm(x_ref[...], g_ref[...], sc_ref[...], sh_ref[...]).astype(BF16)
        acc_scr[...] = jnp.zeros_like(acc_scr)

    h = h_scr[...]
    act = (_silu(_dot(h, wg_ref[...])) * _dot(h, wu_ref[...])).astype(BF16)
    acc_scr[...] += _dot(act, wd_ref[...])

    @pl.when(j == pl.num_programs(1) - 1)
    def _():
        o_ref[...] = x_ref[...] + 0.5 * gt_ref[...] * acc_scr[...]


def _ffn(x, mods, norm_g, wg, wu, wd, l, which, sub, rows_per_seq):
    n = x.shape[0]
    tm = _row_tile(n, mods, rows_per_seq, 1024)
    tf = 1408
    return pl.pallas_call(
        _ffn_kernel,
        grid=(n // tm, D_FF // tf),
        in_specs=[pl.BlockSpec((tm, D_MODEL), lambda i, j: (i, 0)),
                  _mod_spec(mods, l, 3 * sub, tm, rows_per_seq),
                  _mod_spec(mods, l, 3 * sub + 1, tm, rows_per_seq),
                  _mod_spec(mods, l, 3 * sub + 2, tm, rows_per_seq),
                  pl.BlockSpec((None, None, 1, D_MODEL), lambda i, j: (l, sub, 0, 0)),
                  pl.BlockSpec((None, None, D_MODEL, tf), lambda i, j: (l, which, 0, j)),
                  pl.BlockSpec((None, None, D_MODEL, tf), lambda i, j: (l, which, 0, j)),
                  pl.BlockSpec((None, None, tf, D_MODEL), lambda i, j: (l, which, j, 0))],
        out_specs=pl.BlockSpec((tm, D_MODEL), lambda i, j: (i, 0)),
        out_shape=jax.ShapeDtypeStruct((n, D_MODEL), F32),
        scratch_shapes=[pltpu.VMEM((tm, D_MODEL), BF16), pltpu.VMEM((tm, D_MODEL), F32)],
        compiler_params=_params(("parallel", "arbitrary")),
        name="ffn",
    )(x, mods, mods, mods, norm_g, wg, wu, wd)


def _inproj_kernel(x_ref, sh_ref, sc_ref, g_ref, w_ref, o_ref, h_scr):
    @pl.when(pl.program_id(1) == 0)
    def _():
        h_scr[...] = _prenorm(x_ref[...], g_ref[...], sc_ref[...], sh_ref[...]).astype(BF16)

    o_ref[...] = _dot(h_scr[...], w_ref[...])


def _inproj(x, mods, norm_g, w_in, l, rows_per_seq):
    n = x.shape[0]
    tm = _row_tile(n, mods, rows_per_seq, 1024)
    tn = 1536
    return pl.pallas_call(
        _inproj_kernel,
        grid=(n // tm, IN_PAD // tn),
        in_specs=[pl.BlockSpec((tm, D_MODEL), lambda i, j: (i, 0)),
                  _mod_spec(mods, l, 3, tm, rows_per_seq),
                  _mod_spec(mods, l, 4, tm, rows_per_seq),
                  pl.BlockSpec((None, None, 1, D_MODEL), lambda i, j: (l, 1, 0, 0)),
                  pl.BlockSpec((None, D_MODEL, tn), lambda i, j: (l, 0, j))],
        out_specs=pl.BlockSpec((tm, tn), lambda i, j: (i, j)),
        out_shape=jax.ShapeDtypeStruct((n, IN_PAD), F32),
        scratch_shapes=[pltpu.VMEM((tm, D_MODEL), BF16)],
        compiler_params=_params(("parallel", "arbitrary")),
        name="inproj",
    )(x, mods, mods, norm_g, w_in)


def _prep_kernel(aq_ref, ak_ref, av_ref, iq_ref, ikw_ref, cos_ref, sin_ref, qg_ref, kg_ref, bd_ref,
                 q_o, k_o, kb_o, vt_o, qi_o, ki_o, kib_o, wit_o, ktf_o, vtf_o, kitf_o):
    cos = cos_ref[...]
    sin = sin_ref[...]
    bd = bd_ref[...]
    lane = lax.broadcasted_iota(I32, cos.shape, 1)
    first_half = (lane & (HEAD_DIM // 2)) == 0

    def rope(x):
        partner = jnp.where(first_half, pltpu.roll(x, LANES - HEAD_DIM // 2, 1), pltpu.roll(x, HEAD_DIM // 2, 1))
        return x * cos + partner * sin

    def head_norm(x):
        hi, mid, lo = _split3(x * x)
        ms = (_dot(hi, bd) + _dot(mid, bd) + _dot(lo, bd)) * (1.0 / HEAD_DIM)
        return x * lax.rsqrt(ms + EPS)

    def put_heads(o_ref, first, y):
        for half in range(LANES // HEAD_DIM):
            o_ref[first + half] = y[:, half * HEAD_DIM:(half + 1) * HEAD_DIM].astype(BF16)

    qg = qg_ref[...]
    for c in range(ATTN_WIDTH // LANES):
        sl = slice(c * LANES, (c + 1) * LANES)
        put_heads(q_o, 2 * c, rope(head_norm(aq_ref[:, sl]) * qg) * (ATTN_SCALE * LOG2_E))
    k = rope(head_norm(ak_ref[...]) * kg_ref[...])
    k_o[...] = k
    ktf_o[...] = k.T
    put_heads(kb_o, 0, k)
    v_tf = av_ref[...].T
    vtf_o[...] = v_tf
    v_t = v_tf.astype(BF16)
    ones = jnp.ones((V_ROWS - HEAD_DIM, v_t.shape[1]), BF16)
    vt_o[...] = jnp.concatenate([v_t[:HEAD_DIM], ones, v_t[HEAD_DIM:], ones], axis=0)
    for c in range(IDX_HEADS * IDX_DIM // LANES):
        sl = slice(c * LANES, (c + 1) * LANES)
        put_heads(qi_o, 2 * c, rope(iq_ref[:, sl]))
    ikw = ikw_ref[...]
    ki_wide = rope(head_norm(ikw))
    ki = ki_wide[:, :IDX_DIM]
    ki_o[...] = ki
    kitf_o[...] = ki_wide.T[:IDX_DIM, :]
    kib_o[...] = ki.astype(BF16)
    wit_o[...] = ikw.T[IDX_DIM:IDX_DIM + 8, :]


def _prep(p, cos, sin, qg, kg, bd, tm, table_blocks):
    n = p.shape[0]

    def col(off, width):
        return pl.BlockSpec((tm, width), lambda i: (i, off // width))

    tab = pl.BlockSpec((tm, LANES), lambda i: (i % table_blocks, 0))
    vec = pl.BlockSpec((1, LANES), lambda i: (0, 0))

    def row(width):
        return pl.BlockSpec((tm, width), lambda i: (i, 0))

    def heads(h):
        return pl.BlockSpec((h, tm, HEAD_DIM), lambda i: (0, i, 0))

    n_seq = n // (tm * table_blocks)
    seq_len = tm * table_blocks

    def feature_major(width):
        return pl.BlockSpec((None, width, tm), lambda i: (i // table_blocks, 0, i % table_blocks))

    return pl.pallas_call(
        _prep_kernel,
        grid=(n // tm,),
        in_specs=[col(OFF_AQ, ATTN_WIDTH), col(OFF_AK, KV_WIDTH), col(OFF_AV, KV_WIDTH),
                  col(OFF_IQ, IDX_HEADS * IDX_DIM), col(OFF_IKW, LANES), tab, tab, vec, vec,
                  pl.BlockSpec((LANES, LANES), lambda i: (0, 0))],
        out_specs=[heads(N_HEADS), row(KV_WIDTH), heads(N_KV_HEADS),
                   pl.BlockSpec((None, N_KV_HEADS * V_ROWS, tm), lambda i: (i, 0, 0)),
                   heads(IDX_HEADS), row(IDX_DIM), row(IDX_DIM),
                   pl.BlockSpec((8, tm), lambda i: (0, i)),
                   feature_major(KV_WIDTH), feature_major(KV_WIDTH), feature_major(IDX_DIM)],
        out_shape=[jax.ShapeDtypeStruct((N_HEADS, n, HEAD_DIM), BF16),
                   jax.ShapeDtypeStruct((n, KV_WIDTH), F32),
                   jax.ShapeDtypeStruct((N_KV_HEADS, n, HEAD_DIM), BF16),
                   jax.ShapeDtypeStruct((n // tm, N_KV_HEADS * V_ROWS, tm), BF16),
                   jax.ShapeDtypeStruct((IDX_HEADS, n, IDX_DIM), BF16),
                   jax.ShapeDtypeStruct((n, IDX_DIM), F32),
                   jax.ShapeDtypeStruct((n, IDX_DIM), BF16),
                   jax.ShapeDtypeStruct((8, n), F32),
                   jax.ShapeDtypeStruct((n_seq, KV_WIDTH, seq_len), F32),
                   jax.ShapeDtypeStruct((n_seq, KV_WIDTH, seq_len), F32),
                   jax.ShapeDtypeStruct((n_seq, IDX_DIM, seq_len), F32)],
        compiler_params=_params(("parallel",)),
        name="prep",
    )(p, p, p, p, p, cos, sin, qg, kg, bd)


def _rope_tables(pos):
    half = HEAD_DIM // 2
    inv = ROPE_THETA ** (-jnp.arange(half, dtype=F32) / half)
    ang = pos.astype(F32)[:, None] * inv[None, :]
    c, s = jnp.cos(ang), jnp.sin(ang)
    return jnp.concatenate([c, c, c, c], axis=1), jnp.concatenate([-s, s, -s, s], axis=1)


def _pool_select(sums, lane):
    out = sums[-1]
    for gi in range(len(POOL_WINDOWS) - 2, -1, -1):
        out = jnp.where(lane < (gi + 1) * POOL_GROUP_DIM, sums[gi], out)
    return out


def _pool_kernel(u_ref, prev_ref, w_ref, s_ref, o_ref, carry_scr, *, tt, p0):
    t = pl.program_id(1)

    @pl.when(t == 0)
    def _():
        carry_scr[...] = prev_ref[...]

    u = u_ref[...]
    e = jnp.concatenate([carry_scr[...], u], axis=0)
    carry_scr[...] = u[tt - 16:, :]
    s2 = e[1:] + e[:-1]
    s4 = s2[2:] + s2[:-2]
    s8 = s4[4:] + s4[:-4]
    s16 = s8[8:] + s8[:-8]
    sums = (s2[15:15 + tt], s4[13:13 + tt], s8[9:9 + tt], s16[1:1 + tt])
    n_avail = (p0 + 1 + t * tt + lax.broadcasted_iota(I32, (tt, 1), 0)).astype(F32)
    lane = lax.broadcasted_iota(I32, (tt, POOL_WIDTH), 1)
    means = [sums[gi] / jnp.minimum(n_avail, float(w)) for gi, w in enumerate(POOL_WINDOWS)]
    pooled = _pool_select(means, lane) - u
    o_ref[...] = _dot(pooled.astype(BF16), w_ref[...]) * s_ref[...]


def _pool_prompt(p, prev16, wbd, scale, l, b, t_len):
    tt = min(512, t_len)
    nt = t_len // tt
    return pl.pallas_call(
        functools.partial(_pool_kernel, tt=tt, p0=0),
        grid=(b, nt),
        in_specs=[pl.BlockSpec((tt, POOL_WIDTH), lambda bi, ti: (bi * nt + ti, OFF_UP // POOL_WIDTH)),
                  pl.BlockSpec((None, 16, POOL_WIDTH), lambda bi, ti: (bi, 0, 0)),
                  pl.BlockSpec((None, POOL_WIDTH, POOL_WIDTH), lambda bi, ti: (l, 0, 0)),
                  pl.BlockSpec((None, 1, POOL_WIDTH), lambda bi, ti: (l, 0, 0))],
        out_specs=pl.BlockSpec((tt, POOL_WIDTH), lambda bi, ti: (bi * nt + ti, 0)),
        out_shape=jax.ShapeDtypeStruct((b * t_len, POOL_WIDTH), F32),
        scratch_shapes=[pltpu.VMEM((16, POOL_WIDTH), F32)],
        compiler_params=_params(("parallel", "arbitrary")),
        name="pool",
    )(p, prev16, wbd, scale)


def _pool_dec_kernel(u_ref, st_ref, w_ref, s_ref, o_ref, *, p0):
    u = u_ref[...]
    lane = lax.broadcasted_iota(I32, u.shape, 1)
    means = []
    acc = u
    d = 1
    for w in POOL_WINDOWS:
        while d < w:
            acc = acc + st_ref[POOL_STATE - d]
            d += 1
        means.append(acc / float(min(p0 + 1, w)))
    pooled = _pool_select(means, lane) - u
    o_ref[...] = _dot(pooled.astype(BF16), w_ref[...]) * s_ref[...]


def _pool_dec(p, st_t, wbd, scale, l, p0):
    n = p.shape[0]
    return pl.pallas_call(
        functools.partial(_pool_dec_kernel, p0=p0),
        grid=(1,),
        in_specs=[pl.BlockSpec((n, POOL_WIDTH), lambda i: (0, OFF_UP // POOL_WIDTH)),
                  pl.BlockSpec((POOL_STATE, n, POOL_WIDTH), lambda i: (0, 0, 0)),
                  pl.BlockSpec((None, POOL_WIDTH, POOL_WIDTH), lambda i: (l, 0, 0)),
                  pl.BlockSpec((None, 1, POOL_WIDTH), lambda i: (l, 0, 0))],
        out_specs=pl.BlockSpec((n, POOL_WIDTH), lambda i: (0, 0)),
        out_shape=jax.ShapeDtypeStruct((n, POOL_WIDTH), F32),
        compiler_params=_params(("arbitrary",)),
        name="pool_dec",
    )(p, st_t, wbd, scale)


def _hgrn_consts(c):
    r = np.arange(c)[:, None]
    s = np.arange(c)[None, :]
    mats = [(s <= r) & (s // h == r // h) for h in HGRN_LEVELS]
    mats += [(s > r) & (s // h == r // h) for h in HGRN_LEVELS]
    return jnp.asarray(np.concatenate(mats, axis=0).astype(np.float32), BF16)


def _hgrn_kernel(hq_ref, hf_ref, hi_ref, hg_ref, lb_ref, on_ref, ms_ref, yb_ref, st_ref, s_scr, *, c):
    step = pl.program_id(1)
    nl = len(HGRN_LEVELS)

    @pl.when(step == 0)
    def _():
        s_scr[...] = jnp.zeros_like(s_scr)

    ti = lax.broadcasted_iota(I32, (c, c), 0)
    si = lax.broadcasted_iota(I32, (c, c), 1)
    nb = c // HGRN_BASE
    tpos = lax.broadcasted_iota(I32, (1, HGRN_BASE, 1), 1)

    for hd in range(HGRN_HEADS):
        ks = slice(hd * HGRN_K, (hd + 1) * HGRN_K)
        vs = slice(hd * HGRN_V, (hd + 1) * HGRN_V)
        q = hq_ref[:, ks]
        lb = lb_ref[:, ks]
        f = lb + (1.0 - lb) * jax.nn.sigmoid(hf_ref[:, ks])
        g = jnp.log(jnp.maximum(f, F_FLOOR))
        kk = 1.0 - f
        iv = hi_ref[:, vs]
        ivb = iv.astype(BF16)
        g_hi, g_lo = _split2(g)
        r = _dot(ms_ref[...], jnp.concatenate([g_hi, g_lo], axis=1))
        r = r[:, :HGRN_K] + r[:, HGRN_K:]
        low = [r[k * c:(k + 1) * c] for k in range(nl)]
        up = [r[(nl + k) * c:(nl + k + 1) * c] for k in range(nl)]

        att = jnp.zeros((c, c), F32)
        for k, h in enumerate(HGRN_LEVELS[:-1]):
            qe = (q * jnp.exp(low[k])).astype(BF16)
            ke = (kk * jnp.exp(up[k])).astype(BF16)
            sh = h.bit_length() - 1
            tb = jnp.right_shift(ti, sh)
            pair = ((tb & 1) == 1) & (jnp.right_shift(si, sh) == tb - 1)
            att = jnp.where(pair, _dot_nt(qe, ke), att)
        qe = (q * jnp.exp(low[-1])).astype(BF16)
        ke = (kk * jnp.exp(up[-1])).astype(BF16)
        s_t = s_scr[hd]
        o = _dot(att.astype(BF16), ivb) + _dot_nt(qe, s_t.astype(BF16))

        l3 = low[0].reshape(nb, HGRN_BASE, HGRN_K)
        q3 = q.reshape(nb, HGRN_BASE, HGRN_K)
        k3 = kk.reshape(nb, HGRN_BASE, HGRN_K)
        i3 = iv.reshape(nb, HGRN_BASE, HGRN_V)
        o3 = jnp.zeros((nb, HGRN_BASE, HGRN_V), F32)
        for s in range(HGRN_BASE):
            e = jnp.exp(jnp.minimum(l3 - l3[:, s:s + 1, :], 0.0))
            a = jnp.sum(q3 * (k3[:, s:s + 1, :] * e), axis=-1, keepdims=True)
            a = jnp.where(tpos >= s, a, 0.0)
            o3 = o3 + a * i3[:, s:s + 1, :]
        o = o + o3.reshape(c, HGRN_V)

        s_scr[hd] = s_t * jnp.exp(low[-1][c - 1:c, :]) + _dot_tn(ivb, ke)

        on = o * lax.rsqrt(jnp.mean(o * o, axis=-1, keepdims=True) + EPS) * on_ref[...]
        yb_ref[:, vs] = on * _silu(hg_ref[:, vs])

    @pl.when(step == pl.num_programs(1) - 1)
    def _():
        st_ref[...] = s_scr[...]


def _hgrn_prompt(p, lb, onorm, mstack, b, t_len):
    c = HGRN_CHUNK
    nc = t_len // c
    kw = HGRN_HEADS * HGRN_K
    vw = HGRN_HEADS * HGRN_V

    def col(off, width):
        return pl.BlockSpec((c, width), lambda bi, ci: (bi * nc + ci, off // width))

    return pl.pallas_call(
        functools.partial(_hgrn_kernel, c=c),
        grid=(b, nc),
        in_specs=[col(OFF_HQ, kw), col(OFF_HF, kw), col(OFF_HI, vw), col(OFF_HG, vw),
                  pl.BlockSpec((1, kw), lambda bi, ci: (0, 0)),
                  pl.BlockSpec((1, HGRN_V), lambda bi, ci: (0, 0)),
                  pl.BlockSpec(mstack.shape, lambda bi, ci: (0, 0))],
        out_specs=[pl.BlockSpec((c, vw), lambda bi, ci: (bi * nc + ci, 0)),
                   pl.BlockSpec((None, HGRN_HEADS, HGRN_V, HGRN_K), lambda bi, ci: (bi, 0, 0, 0))],
        out_shape=[jax.ShapeDtypeStruct((b * t_len, vw), F32),
                   jax.ShapeDtypeStruct((b, HGRN_HEADS, HGRN_V, HGRN_K), F32)],
        scratch_shapes=[pltpu.VMEM((HGRN_HEADS, HGRN_V, HGRN_K), F32)],
        compiler_params=_params(("parallel", "arbitrary")),
        name="hgrn",
    )(p, p, p, p, lb, onorm, mstack)


def _hgrn_dec_kernel(q_ref, z_ref, lb_ref, i_ref, hg_ref, on_ref, s_ref, yb_ref, so_ref):
    lb = lb_ref[...]
    f = lb + (1.0 - lb) * jax.nn.sigmoid(z_ref[...])
    a = jnp.exp(jnp.log(jnp.maximum(f, F_FLOOR)))
    s_new = a * s_ref[...] + i_ref[...] * (1.0 - f)
    so_ref[...] = s_new
    o = jnp.sum(q_ref[...] * s_new, axis=-1, keepdims=True)
    on = o * lax.rsqrt(jnp.mean(o * o, axis=-2, keepdims=True) + EPS) * on_ref[...]
    yb_ref[...] = on * _silu(hg_ref[...])


def _hgrn_dec(q_r, z_r, lb_r, i_c, hg_c, onorm_c, state_t, l):
    n = state_t.shape[1]
    nb = 8
    row = pl.BlockSpec((nb, HGRN_HEADS, 1, HGRN_K), lambda i: (i, 0, 0, 0))
    col = pl.BlockSpec((nb, HGRN_HEADS, HGRN_V, 1), lambda i: (i, 0, 0, 0))
    return pl.pallas_call(
        _hgrn_dec_kernel,
        grid=(n // nb,),
        in_specs=[row, row, pl.BlockSpec((HGRN_HEADS, 1, HGRN_K), lambda i: (0, 0, 0)), col, col,
                  pl.BlockSpec((HGRN_V, 1), lambda i: (0, 0)),
                  pl.BlockSpec((None, nb, HGRN_HEADS, HGRN_V, HGRN_K), lambda i: (l, i, 0, 0, 0))],
        out_specs=[col, pl.BlockSpec((nb, HGRN_HEADS, HGRN_V, HGRN_K), lambda i: (i, 0, 0, 0))],
        out_shape=[jax.ShapeDtypeStruct((n, HGRN_HEADS, HGRN_V, 1), F32),
                   jax.ShapeDtypeStruct(state_t.shape[1:], F32)],
        compiler_params=_params(("parallel",)),
        name="hgrn_dec",
    )(q_r, z_r, lb_r, i_c, hg_c, onorm_c, state_t)


def _sort_key(s):
    bits = pltpu.bitcast(s, I32)
    return jnp.where(bits < 0, INT_MIN - bits, bits)


def _lane_fold(x):
    part = x[:, :LANES]
    for cidx in range(1, x.shape[1] // LANES):
        part = part + x[:, cidx * LANES:(cidx + 1) * LANES]
    return part


def _select_bias(key, thr, idx, j_lim):
    return jnp.where(key > thr, 0.0, jnp.where(key == thr, jnp.where(idx < j_lim, 0.0, NEG_BIG), NEG_BIG))


def _bisect_bits(count_ge, thr, first_bit, n_bits, k):
    def bit_body(it, thr):
        cand = thr + lax.shift_left(jnp.int32(1), first_bit - it)
        return jnp.where(count_ge(cand) >= k, cand, thr)

    return lax.fori_loop(0, n_bits, bit_body, thr)


def _topk_threshold(count, stat_shape, n_sel, idx_bits, idx_sentinel, kth_largest=None):
    k = float(n_sel)
    if kth_largest is None:
        thr = _bisect_bits(lambda cand: count(lambda key, idx: jnp.where(key >= cand, 1.0, 0.0)),
                           jnp.full(stat_shape, INT_MIN, I32), 31, 32, k)
    else:
        thr = kth_largest(k)
    n_ge = count(lambda key, idx: jnp.where(key >= thr, 1.0, 0.0))
    n_gt = count(lambda key, idx: jnp.where(key > thr, 1.0, 0.0))
    need = k - n_gt
    live = thr > INT_MIN
    tie = jnp.where(live, jnp.where(n_ge > k, 1.0, 0.0), 0.0)
    j_default = jnp.where(live, jnp.int32(idx_sentinel), jnp.int32(0))

    def tie_search():
        def idx_body(it, j_lim):
            cand = j_lim + lax.shift_left(jnp.int32(1), idx_bits - 1 - it)
            cnt = count(lambda key, idx: jnp.where(key == thr, jnp.where(idx < cand, 1.0, 0.0), 0.0))
            return jnp.where(cnt <= need, cand, j_lim)

        j_lim = lax.fori_loop(0, idx_bits, idx_body, jnp.zeros(stat_shape, I32))
        return jnp.where(tie > 0.0, j_lim, j_default)

    j_lim = lax.cond(jnp.max(tie) > 0.0, tie_search, lambda: j_default)
    return thr, j_lim


FOLD_ROWS = 64
V_ROWS = HEAD_DIM + 16


def _fold(x, op):
    parts = [x[r * FOLD_ROWS:(r + 1) * FOLD_ROWS] for r in range(x.shape[0] // FOLD_ROWS)]
    while len(parts) > 1:
        parts = [op(parts[i], parts[i + 1]) if i + 1 < len(parts) else parts[i]
                 for i in range(0, len(parts), 2)]
    return parts[0]


def _dsa_kernel(q_ref, qi_ref, wit_ref, k_ref, vt_ref, ki_ref, o_ref, keys_scr, k16_scr, lg_scr, p_scr, m_scr, acc_scr,
                *, tq, tk, n_sel, t_len):
    i = pl.program_id(1)
    nk = ((i + 1) * tq + tk - 1) // tk
    krow = lax.broadcasted_iota(I32, (tk, tq), 0)
    qpos = i * tq + lax.broadcasted_iota(I32, (tk, tq), 1)
    wit = wit_ref[...] * IDX_SCALE
    qi_all = qi_ref[...].reshape(IDX_HEADS * tq, IDX_DIM)

    def rows(j):
        return pl.ds(pl.multiple_of(j * tk, tk), tk)

    def score_tile(j, causal):
        rel = _dot_nt(ki_ref[rows(j), :], qi_all)
        s = wit[0:1, :] * jnp.maximum(rel[:, :tq], 0.0)
        for h in range(1, IDX_HEADS):
            s = s + wit[h:h + 1, :] * jnp.maximum(rel[:, h * tq:(h + 1) * tq], 0.0)
        key = jnp.where(s > 0.5 * NEG_BIG, _sort_key(s), INT_MIN)
        if causal:
            key = jnp.where(j * tk + krow <= qpos, key, INT_MIN)
        keys_scr[j] = key
        k16_scr[j] = jnp.right_shift(key, 16).astype(I16)

    def score_body(j, carry):
        score_tile(j, False)
        return carry

    lax.fori_loop(0, nk - 1, score_body, 0)
    score_tile(nk - 1, True)

    def count(pred):
        def body(j, acc):
            return acc + _fold(pred(keys_scr[j], j * tk + krow), jnp.add)

        acc = lax.fori_loop(0, nk, body, jnp.zeros((FOLD_ROWS, tq), F32))
        return jnp.sum(acc, axis=0, keepdims=True)

    one16, zero16 = jnp.int16(1), jnp.int16(0)

    def count16(pred):
        def body(j, acc):
            return acc + _fold(pred(k16_scr[j]), jnp.add)

        acc = lax.fori_loop(0, nk, body, jnp.zeros((FOLD_ROWS, tq), I16))
        return jnp.sum(acc.astype(F32), axis=0, keepdims=True)

    def high16(x):
        return jnp.right_shift(x, 16).astype(I16)

    def low16(x):
        return ((x & 0xFFFF) - 32768).astype(I16)

    def kth_largest(k):
        thr = jnp.full((1, tq), INT_MIN, I32)
        thr = _bisect_bits(lambda cand: count16(lambda t: jnp.where(t >= high16(cand), one16, zero16)),
                           thr, 31, 16, k)
        thr_hi = high16(thr)
        n_above = count16(lambda t: jnp.where(t > thr_hi, one16, zero16))

        def low_body(j, carry):
            key = keys_scr[j]
            k16_scr[j] = jnp.where(high16(key) == thr_hi, low16(key), jnp.int16(-32768))
            return carry

        lax.fori_loop(0, nk, low_body, 0)
        return _bisect_bits(
            lambda cand: n_above + count16(lambda t: jnp.where(t >= low16(cand), one16, zero16)),
            thr, 15, 16, k)

    thr, j_lim = _topk_threshold(count, (1, tq), n_sel, int(t_len).bit_length(), t_len, kth_largest)

    m_scr[...] = jnp.full(m_scr.shape, 0.5 * NEG_BIG, F32)
    acc_scr[...] = jnp.zeros_like(acc_scr)

    def attend_tile(j, carry):
        bias = _select_bias(keys_scr[j], thr, j * tk + krow, j_lim)
        bias = jnp.concatenate([bias] * GROUP, axis=1)
        v_t = vt_ref[j]
        heads = range(N_KV_HEADS)
        for n in heads:
            qg = q_ref[n * GROUP:(n + 1) * GROUP].reshape(GROUP * tq, HEAD_DIM)
            lg_scr[n] = _dot_nt(k_ref[n, rows(j), :], qg) + bias
        m_old = [m_scr[n] for n in heads]
        m_new = [jnp.maximum(m_old[n], jnp.max(_fold(lg_scr[n], jnp.maximum), axis=0, keepdims=True))
                 for n in heads]
        for n in heads:
            p_scr[n] = jnp.exp2(lg_scr[n] - m_new[n]).astype(BF16)
        for n in heads:
            acc_scr[n] = (jnp.exp2(m_old[n] - m_new[n]) * acc_scr[n]
                          + _dot(v_t[n * V_ROWS:(n + 1) * V_ROWS, :], p_scr[n]))
            m_scr[n] = m_new[n]
        return carry

    lax.fori_loop(0, nk, attend_tile, 0)
    outs = []
    for n in range(N_KV_HEADS):
        acc = acc_scr[n]
        o_n = acc[:HEAD_DIM] / acc[HEAD_DIM:HEAD_DIM + 1]
        outs += [o_n[:, g * tq:(g + 1) * tq] for g in range(GROUP)]
    o_ref[...] = jnp.concatenate(outs, axis=0).T


def _dsa_prompt(q_h, qi_h, wit, k_h, v_t, ki_b, b, t_len, n_sel, tk):
    tq = min(256, t_len)
    nq = t_len // tq
    nt = t_len // tk

    def heads(h, rows, imap):
        return pl.BlockSpec((h, rows, HEAD_DIM), imap)

    return pl.pallas_call(
        functools.partial(_dsa_kernel, tq=tq, tk=tk, n_sel=n_sel, t_len=t_len),
        grid=(b, nq),
        in_specs=[heads(N_HEADS, tq, lambda bi, qi: (0, bi * nq + qi, 0)),
                  heads(IDX_HEADS, tq, lambda bi, qi: (0, bi * nq + qi, 0)),
                  pl.BlockSpec((8, tq), lambda bi, qi: (0, bi * nq + qi)),
                  heads(N_KV_HEADS, t_len, lambda bi, qi: (0, bi, 0)),
                  pl.BlockSpec((nt, N_KV_HEADS * V_ROWS, tk), lambda bi, qi: (bi, 0, 0)),
                  pl.BlockSpec((t_len, IDX_DIM), lambda bi, qi: (bi, 0))],
        out_specs=pl.BlockSpec((tq, ATTN_WIDTH), lambda bi, qi: (bi * nq + qi, 0)),
        out_shape=jax.ShapeDtypeStruct((b * t_len, ATTN_WIDTH), F32),
        scratch_shapes=[pltpu.VMEM((nt, tk, tq), I32),
                        pltpu.VMEM((nt, tk, tq), I16),
                        pltpu.VMEM((N_KV_HEADS, tk, GROUP * tq), F32),
                        pltpu.VMEM((N_KV_HEADS, tk, GROUP * tq), BF16),
                        pltpu.VMEM((N_KV_HEADS, 1, GROUP * tq), F32),
                        pltpu.VMEM((N_KV_HEADS, V_ROWS, GROUP * tq), F32)],
        compiler_params=_params(("parallel", "arbitrary")),
        name="dsa",
    )(q_h, qi_h, wit, k_h, v_t, ki_b)


DEC_SEQS_PER_STEP = 4


def _dsa_dec_score_kernel(pt_ref, qi_ref, wi_ref, kin_ref, *rest, n_pages, n_seq):
    page_refs = rest[:n_seq * n_pages]
    o_ref = rest[n_seq * n_pages]
    lane = lax.broadcasted_iota(I32, (1, PAGE_SIZE), 1)
    for s in range(n_seq):
        qi = qi_ref[s]
        wi = wi_ref[s]

        def finish(rel):
            sc = jnp.sum(wi * jnp.maximum(rel, 0.0), axis=0, keepdims=True) * IDX_SCALE
            return jnp.where(sc > 0.5 * NEG_BIG, _sort_key(sc), INT_MIN)

        for pg in range(n_pages):
            page = page_refs[s * n_pages + pg][...].astype(BF16)
            o_ref[s, :, pg * PAGE_SIZE:(pg + 1) * PAGE_SIZE] = finish(_dot(qi, page))
        kin = kin_ref[s].astype(BF16).astype(F32)
        key_new = finish(jnp.sum(qi.astype(F32) * kin, axis=1, keepdims=True))
        o_ref[s, :, n_pages * PAGE_SIZE:] = jnp.where(lane == 0, key_new, INT_MIN)


def _page_specs(block, l, n_seq, n_pages):
    def spec(s, pg):
        return pl.BlockSpec(block, lambda bi, pt, *_: (l, pt[bi * n_seq + s, pg], 0, 0))

    return [spec(s, pg) for s in range(n_seq) for pg in range(n_pages)]


def _dsa_dec_scores(page_table, qi8, wi8, ki_new, cache_ki_t, l):
    n, n_pages = page_table.shape
    ns = min(DEC_SEQS_PER_STEP, n)
    width = (n_pages + 1) * PAGE_SIZE
    page_specs = _page_specs((None, None, IDX_DIM, PAGE_SIZE), l, ns, n_pages)
    return pl.pallas_call(
        functools.partial(_dsa_dec_score_kernel, n_pages=n_pages, n_seq=ns),
        grid_spec=pltpu.PrefetchScalarGridSpec(
            num_scalar_prefetch=1,
            grid=(n // ns,),
            in_specs=[pl.BlockSpec((ns, 8, IDX_DIM), lambda bi, pt: (bi, 0, 0)),
                      pl.BlockSpec((ns, 8, 1), lambda bi, pt: (bi, 0, 0)),
                      pl.BlockSpec((ns, 1, IDX_DIM), lambda bi, pt: (bi, 0, 0))] + page_specs,
            out_specs=pl.BlockSpec((ns, 1, width), lambda bi, pt: (bi, 0, 0))),
        out_shape=jax.ShapeDtypeStruct((n, 1, width), I32),
        compiler_params=_params(("arbitrary",)),
        name="dsa_dec_scores",
    )(page_table, qi8, wi8, ki_new, *([cache_ki_t] * (ns * n_pages)))


def _dsa_dec_thr_kernel(keys_ref, thr_ref, j_ref, *, n_sel):
    n, width = keys_ref.shape
    lane = lax.broadcasted_iota(I32, (n, width), 1)

    def count(pred):
        return jnp.sum(_lane_fold(pred(keys_ref[...], lane)), axis=1, keepdims=True)

    thr, j_lim = _topk_threshold(count, (n, 1), n_sel, int(width).bit_length(), width)
    thr_ref[...] = thr
    j_ref[...] = j_lim


def _dsa_dec_thr(keys2d, n_sel):
    n, width = keys2d.shape
    return pl.pallas_call(
        functools.partial(_dsa_dec_thr_kernel, n_sel=n_sel),
        grid=(1,),
        in_specs=[pl.BlockSpec((n, width), lambda i: (0, 0))],
        out_specs=[pl.BlockSpec((n, 1), lambda i: (0, 0)), pl.BlockSpec((n, 1), lambda i: (0, 0))],
        out_shape=[jax.ShapeDtypeStruct((n, 1), I32), jax.ShapeDtypeStruct((n, 1), I32)],
        compiler_params=_params(("arbitrary",)),
        name="dsa_dec_thr",
    )(keys2d)


def _dsa_dec_attn_kernel(pt_ref, thr_ref, j_ref, q_ref, keys_ref, kn_ref, vn_ref, *rest, n_pages, n_seq):
    k_pages = rest[:n_seq * n_pages]
    v_pages = rest[n_seq * n_pages:2 * n_seq * n_pages]
    o_ref, k_scr, v_scr = rest[2 * n_seq * n_pages:]
    n_past = n_pages * PAGE_SIZE
    idx = lax.broadcasted_iota(I32, (1, keys_ref.shape[-1]), 1)
    for s in range(n_seq):
        b = pl.program_id(0) * n_seq + s
        for pg in range(n_pages):
            sl = slice(pg * PAGE_SIZE, (pg + 1) * PAGE_SIZE)
            k_scr[s, :, sl] = k_pages[s * n_pages + pg][...].astype(BF16)
            v_scr[s, :, sl] = v_pages[s * n_pages + pg][...].astype(BF16)
        bias_all = _select_bias(keys_ref[s], thr_ref[b], idx, j_ref[b])
        bias = bias_all[:, :n_past]
        bias_new = bias_all[:, n_past:n_past + 1]
        kn = kn_ref[s].astype(BF16).astype(F32)
        vn = vn_ref[s].astype(BF16).astype(F32)
        for n in range(N_KV_HEADS):
            fs = slice(n * HEAD_DIM, (n + 1) * HEAD_DIM)
            q = q_ref[s, n]
            lg = _dot(q, k_scr[s, fs, :]) + bias
            lg_new = jnp.sum(q.astype(F32) * kn[:, fs], axis=1, keepdims=True) + bias_new
            m = jnp.maximum(jnp.max(lg, axis=1, keepdims=True), lg_new)
            p = jnp.exp2(lg - m)
            p_new = jnp.exp2(lg_new - m)
            den = jnp.sum(p, axis=1, keepdims=True) + p_new
            o_ref[s, n] = (_dot_nt(p.astype(BF16), v_scr[s, fs, :]) + p_new * vn[:, fs]) / den


def _dsa_dec_attn(page_table, thr, j_lim, q8, keys, k_new, v_new, cache_k_t, cache_v_t, l):
    n, n_pages = page_table.shape
    ns = min(DEC_SEQS_PER_STEP, n)
    width = (n_pages + 1) * PAGE_SIZE
    page_specs = _page_specs((None, None, KV_WIDTH, PAGE_SIZE), l, ns, n_pages)
    qspec = pl.BlockSpec((ns, N_KV_HEADS, 8, HEAD_DIM), lambda bi, pt, th, jl: (bi, 0, 0, 0))

    def per_seq(width_):
        return pl.BlockSpec((ns, 1, width_), lambda bi, pt, th, jl: (bi, 0, 0))

    return pl.pallas_call(
        functools.partial(_dsa_dec_attn_kernel, n_pages=n_pages, n_seq=ns),
        grid_spec=pltpu.PrefetchScalarGridSpec(
            num_scalar_prefetch=3,
            grid=(n // ns,),
            in_specs=[qspec, per_seq(width), per_seq(KV_WIDTH), per_seq(KV_WIDTH)] + page_specs + page_specs,
            out_specs=qspec,
            scratch_shapes=[pltpu.VMEM((ns, KV_WIDTH, n_pages * PAGE_SIZE), BF16),
                            pltpu.VMEM((ns, KV_WIDTH, n_pages * PAGE_SIZE), BF16)]),
        out_shape=jax.ShapeDtypeStruct((n, N_KV_HEADS, 8, HEAD_DIM), F32),
        compiler_params=_params(("arbitrary",)),
        name="dsa_dec_attn",
    )(page_table, thr, j_lim, q8, keys, k_new, v_new,
      *([cache_k_t] * (ns * n_pages)), *([cache_v_t] * (ns * n_pages)))


def _merge_kernel(x_ref, g0_ref, g1_ref, g2_ref, ya_ref, yb_ref, yc_ref, gt_ref,
                  wa_ref, wb_ref, wc_ref, wo_ref, o_ref):
    merged = (jax.nn.sigmoid(g0_ref[...]) * _dot(ya_ref[...].astype(BF16), wa_ref[...])
              + jax.nn.sigmoid(g1_ref[...]) * _dot(yb_ref[...].astype(BF16), wb_ref[...])
              + jax.nn.sigmoid(g2_ref[...]) * _dot(yc_ref[...].astype(BF16), wc_ref[...]))
    o_ref[...] = x_ref[...] + gt_ref[...] * _dot(merged.astype(BF16), wo_ref[...])


def _merge(x, p, ya, yb, yc, mods, w_ba, w_bb, w_bc, w_out, l, rows_per_seq):
    n = x.shape[0]
    tm = _row_tile(n, mods, rows_per_seq, 512)

    def wspec(rows):
        return pl.BlockSpec((None, rows, D_MODEL), lambda i: (l, 0, 0))

    def row(width):
        return pl.BlockSpec((tm, width), lambda i: (i, 0))

    def gate(bidx):
        return pl.BlockSpec((tm, D_MODEL), lambda i: (i, OFF_GL // D_MODEL + bidx))

    return pl.pallas_call(
        _merge_kernel,
        grid=(n // tm,),
        in_specs=[row(D_MODEL), gate(0), gate(1), gate(2), row(POOL_WIDTH), row(HGRN_HEADS * HGRN_V),
                  row(ATTN_WIDTH), _mod_spec(mods, l, 5, tm, rows_per_seq),
                  wspec(POOL_WIDTH), wspec(HGRN_HEADS * HGRN_V), wspec(ATTN_WIDTH), wspec(D_MODEL)],
        out_specs=row(D_MODEL),
        out_shape=jax.ShapeDtypeStruct((n, D_MODEL), F32),
        compiler_params=_params(("parallel",)),
        name="merge",
    )(x, p, p, p, ya, yb, yc, mods, w_ba, w_bb, w_bc, w_out)


def _permute_w_in(w_in):
    segs = [(2884, 5956), (256, 768), (768, 1280), (1792, 2304), (0, 256), (1280, 1536), (1536, 1792),
            (2560, 2816), (2304, 2432), (2432, 2560), (2816, 2884)]
    parts = [w_in[..., a:b] for a, b in segs]
    used = sum(b - a for a, b in segs)
    parts.append(jnp.zeros(w_in.shape[:-1] + (IN_PAD - used,), w_in.dtype))
    return jnp.concatenate(parts, axis=-1).astype(BF16)


def _lower_bounds(p):
    sm = jax.nn.softmax(p.astype(F32), axis=0)
    return jnp.cumsum(sm, axis=0) - sm[0:1]


def kernel(x_prompt, x_sample, c_prompt, c_sample, cache_k, cache_v, cache_ki, page_table, state_pool, state_hgrn, ada_w, ada_b, norm_g, ffn_wg, ffn_wu, ffn_wd, w_in, pool_w, pool_scale, hgrn_lb, hgrn_onorm, q_norm, k_norm, w_ba, w_bb, w_bc, w_out):
    bp, t_len, _ = x_prompt.shape
    bs = x_sample.shape[0]
    assert x_sample.shape[1] == 1
    n_pages = page_table.shape[1]
    past_len = n_pages * PAGE_SIZE
    n_phys = cache_k.shape[1]
    n_sel_p = min(TOPK_MAX, t_len // 4)
    n_sel_s = min(TOPK_MAX, (past_len + 1) // 4)

    wg, wu, wd = ffn_wg.astype(BF16), ffn_wu.astype(BF16), ffn_wd.astype(BF16)
    w_in_p = _permute_w_in(w_in)
    wa, wb, wc, wo = w_ba.astype(BF16), w_bb.astype(BF16), w_bc.astype(BF16), w_out.astype(BF16)
    eye = jnp.eye(len(POOL_WINDOWS), dtype=F32)
    pool_wbd = jnp.einsum('lgcd,gh->lgchd', pool_w, eye).reshape(DEPTH, POOL_WIDTH, POOL_WIDTH).astype(BF16)
    pool_sc = pool_scale.reshape(DEPTH, 1, POOL_WIDTH)
    norm_g4 = norm_g.reshape(DEPTH, N_SUB, 1, D_MODEL)
    lbs = _lower_bounds(hgrn_lb)
    qg = jnp.tile(q_norm, (1, LANES // HEAD_DIM)).reshape(DEPTH, 1, LANES)
    kg = jnp.tile(k_norm, (1, LANES // HEAD_DIM)).reshape(DEPTH, 1, LANES)
    gid = np.arange(LANES) // HEAD_DIM
    bd = jnp.asarray((gid[:, None] == gid[None, :]).astype(np.float32), BF16)
    mstack = _hgrn_consts(HGRN_CHUNK)
    cos_p, sin_p = _rope_tables(jnp.arange(t_len, dtype=I32))
    cos_s, sin_s = _rope_tables(jnp.full((bs,), past_len, I32))
    ck = jnp.transpose(cache_k, (0, 1, 3, 4, 2)).reshape(DEPTH, n_phys, KV_WIDTH, PAGE_SIZE)
    cv = jnp.transpose(cache_v, (0, 1, 3, 4, 2)).reshape(DEPTH, n_phys, KV_WIDTH, PAGE_SIZE)
    cki = jnp.transpose(cache_ki, (0, 1, 3, 2))
    state_hgrn_t = jnp.swapaxes(state_hgrn, -1, -2)

    rows_all = bp + bs
    rows_pad = -(-rows_all // 8) * 8
    c_all = jnp.concatenate([c_prompt, c_sample, jnp.zeros((rows_pad - rows_all, D_MODEL), F32)], axis=0)
    mods = _ada(c_all, ada_w, ada_b)
    mods_p = mods[:, :bp].reshape(DEPTH, bp, 1, 3 * N_SUB * D_MODEL)
    mods_s = mods[:, bp:rows_all]

    xp = x_prompt.reshape(bp * t_len, D_MODEL)
    xs = x_sample.reshape(bs, D_MODEL)
    pool_prev_p = jnp.zeros((bp, 16, POOL_WIDTH), F32)
    outs = [[] for _ in range(10)]
    tm_prep = min(512, t_len)

    for l in range(DEPTH):
        xp = _ffn(xp, mods_p, norm_g4, wg, wu, wd, l, 0, 0, t_len)
        p = _inproj(xp, mods_p, norm_g4, w_in_p, l, t_len)
        ya = _pool_prompt(p, pool_prev_p, pool_wbd, pool_sc, l, bp, t_len)
        yb, st_t = _hgrn_prompt(p, lbs[l].reshape(1, -1), hgrn_onorm[l].reshape(1, -1), mstack, bp, t_len)
        q_h, _, k_h, v_t, qi_h, _, ki_b, wit, k_tf, v_tf, ki_tf = _prep(
            p, cos_p, sin_p, qg[l], kg[l], bd, tm_prep, t_len // tm_prep)
        yc = _dsa_prompt(q_h, qi_h, wit, k_h, v_t, ki_b, bp, t_len, n_sel_p, tm_prep)
        xp = _merge(xp, p, ya, yb, yc, mods_p, wa, wb, wc, wo, l, t_len)
        xp = _ffn(xp, mods_p, norm_g4, wg, wu, wd, l, 1, 2, t_len)
        p3 = p.reshape(bp, t_len, IN_PAD)
        outs[0].append(jnp.transpose(k_tf.reshape(bp, N_KV_HEADS, HEAD_DIM, t_len), (0, 3, 1, 2)))
        outs[1].append(jnp.transpose(v_tf.reshape(bp, N_KV_HEADS, HEAD_DIM, t_len), (0, 3, 1, 2)))
        outs[2].append(jnp.swapaxes(ki_tf, 1, 2))
        outs[3].append(p3[:, t_len - POOL_STATE:, OFF_UP:OFF_UP + POOL_WIDTH])
        outs[4].append(jnp.swapaxes(st_t, -1, -2))

        xs = _ffn(xs, mods_s, norm_g4, wg, wu, wd, l, 0, 0, 1)
        ps = _inproj(xs, mods_s, norm_g4, w_in_p, l, 1)
        ya = _pool_dec(ps, jnp.swapaxes(state_pool[l], 0, 1), pool_wbd, pool_sc, l, past_len)
        kw = HGRN_HEADS * HGRN_K
        vw = HGRN_HEADS * HGRN_V
        yb4, st_new = _hgrn_dec(
            ps[:, OFF_HQ:OFF_HQ + kw].reshape(bs, HGRN_HEADS, 1, HGRN_K),
            ps[:, OFF_HF:OFF_HF + kw].reshape(bs, HGRN_HEADS, 1, HGRN_K),
            lbs[l].reshape(HGRN_HEADS, 1, HGRN_K),
            ps[:, OFF_HI:OFF_HI + vw].reshape(bs, HGRN_HEADS, HGRN_V, 1),
            ps[:, OFF_HG:OFF_HG + vw].reshape(bs, HGRN_HEADS, HGRN_V, 1),
            hgrn_onorm[l].reshape(-1, 1), state_hgrn_t, l)
        yb = yb4.reshape(bs, vw)
        q_h, k_r, _, _, qi_h, ki_r, _, _, _, _, _ = _prep(ps, cos_s, sin_s, qg[l], kg[l], bd, bs, 1)
        v_new = ps[:, OFF_AV:OFF_AV + KV_WIDTH]
        qi8 = jnp.pad(jnp.swapaxes(qi_h, 0, 1), ((0, 0), (0, 8 - IDX_HEADS), (0, 0)))
        wi8 = jnp.pad(ps[:, OFF_IKW + IDX_DIM:OFF_IKW + IDX_DIM + IDX_HEADS].reshape(bs, IDX_HEADS, 1),
                      ((0, 0), (0, 8 - IDX_HEADS), (0, 0)))
        keys = _dsa_dec_scores(page_table, qi8, wi8, ki_r.reshape(bs, 1, IDX_DIM), cki, l)
        thr, j_lim = _dsa_dec_thr(keys.reshape(bs, -1), n_sel_s)
        q8 = jnp.pad(jnp.swapaxes(q_h, 0, 1).reshape(bs, N_KV_HEADS, GROUP, HEAD_DIM),
                     ((0, 0), (0, 0), (0, 8 - GROUP), (0, 0)))
        yc8 = _dsa_dec_attn(page_table, thr.reshape(bs), j_lim.reshape(bs), q8, keys,
                            k_r.reshape(bs, 1, KV_WIDTH), v_new.reshape(bs, 1, KV_WIDTH), ck, cv, l)
        yc = yc8[:, :, :GROUP, :].reshape(bs, ATTN_WIDTH)
        xs = _merge(xs, ps, ya, yb, yc, mods_s, wa, wb, wc, wo, l, 1)
        xs = _ffn(xs, mods_s, norm_g4, wg, wu, wd, l, 1, 2, 1)
        outs[5].append(k_r.reshape(bs, 1, N_KV_HEADS, HEAD_DIM))
        outs[6].append(v_new.reshape(bs, 1, N_KV_HEADS, HEAD_DIM))
        outs[7].append(ki_r.reshape(bs, 1, IDX_DIM))
        outs[8].append(jnp.concatenate([state_pool[l][:, 1:], ps[:, None, OFF_UP:OFF_UP + POOL_WIDTH]], axis=1))
        outs[9].append(jnp.swapaxes(st_new, -1, -2))

    stacked = [jnp.stack(o) for o in outs]
    return (xp.reshape(bp, t_len, D_MODEL), xs.reshape(bs, 1, D_MODEL), *stacked)
```

```python
import functools

import numpy as np
import jax
import jax.numpy as jnp
from jax import lax
from jax.experimental import pallas as pl
from jax.experimental.pallas import tpu as pltpu

F32 = jnp.float32
BF16 = jnp.bfloat16
I32 = jnp.int32
I16 = jnp.int16

D_MODEL = 1024
DEPTH = 4
PAGE_SIZE = 128
N_SUB = 3
POOL_WINDOWS = (2, 4, 8, 16)
POOL_GROUP_DIM = 64
POOL_WIDTH = 256
POOL_STATE = 15
HGRN_HEADS = 4
HGRN_K = 128
HGRN_V = 64
F_FLOOR = 1e-30
N_HEADS = 8
N_KV_HEADS = 2
HEAD_DIM = 64
GROUP = N_HEADS // N_KV_HEADS
ATTN_WIDTH = 512
KV_WIDTH = 128
IDX_HEADS = 4
IDX_DIM = 64
TOPK_MAX = 256
ROPE_THETA = 10000.0
LOG2_E = 1.4426950408889634
ATTN_SCALE = HEAD_DIM ** -0.5
IDX_SCALE = (IDX_DIM * IDX_HEADS) ** -0.5
NEG_BIG = -1e30
D_FF = 2816
EPS = 1e-6
INT_MIN = np.int32(-2147483648)

LANES = 128
VMEM_LIMIT = 48 * 1024 * 1024

OFF_GL, OFF_HQ, OFF_HF, OFF_AQ = 0, 3072, 3584, 4096
OFF_UP, OFF_HI, OFF_HG, OFF_IQ = 4608, 4864, 5120, 5376
OFF_AK, OFF_AV, OFF_IKW = 5632, 5760, 5888
IN_PAD = 6144
HGRN_CHUNK = 128
HGRN_LEVELS = tuple(1 << i for i in range(HGRN_CHUNK.bit_length()))
HGRN_SMALL = tuple(h for h in HGRN_LEVELS if 1 < h < 8)


def _dot(a, b):
    return jnp.dot(a, b, preferred_element_type=F32)


def _dot_nt(a, b):
    return lax.dot_general(a, b, (((1,), (1,)), ((), ())), preferred_element_type=F32)


def _dot_tn(a, b):
    return lax.dot_general(a, b, (((0,), (0,)), ((), ())), preferred_element_type=F32)


def _split2(x):
    hi = x.astype(BF16)
    lo = (x - hi.astype(F32)).astype(BF16)
    return hi, lo


def _split3(x):
    hi = x.astype(BF16)
    r = x - hi.astype(F32)
    mid = r.astype(BF16)
    lo = (r - mid.astype(F32)).astype(BF16)
    return hi, mid, lo


def _silu(x):
    return x * jax.nn.sigmoid(x)


def _params(sem):
    return pltpu.CompilerParams(dimension_semantics=sem, vmem_limit_bytes=VMEM_LIMIT)


def _mod_spec(mods, l, m, tm, rows_per_seq):
    if mods.ndim == 4:
        return pl.BlockSpec((None, None, 1, D_MODEL), lambda i, *_: (l, (i * tm) // rows_per_seq, 0, m))
    return pl.BlockSpec((None, tm, D_MODEL), lambda i, *_: (l, i, m))


def _row_tile(n, mods, rows_per_seq, cap):
    return min(cap, rows_per_seq if mods.ndim == 4 else n)


def _prenorm(x, g, scale, shift):
    y = x * lax.rsqrt(jnp.mean(x * x, axis=-1, keepdims=True) + EPS) * g
    return y * (1.0 + scale) + shift


def _ada_kernel(c_ref, w_ref, b_ref, o_ref):
    c = c_ref[...]
    o_ref[...] = jnp.dot(_silu(c), w_ref[...], precision=lax.Precision.HIGHEST,
                         preferred_element_type=F32) + b_ref[...]


def _ada(c_all, ada_w, ada_b):
    rows = c_all.shape[0]
    width = ada_w.shape[-1]
    tn = 1024
    return pl.pallas_call(
        _ada_kernel,
        grid=(DEPTH, width // tn),
        in_specs=[pl.BlockSpec((rows, D_MODEL), lambda l, j: (0, 0)),
                  pl.BlockSpec((None, D_MODEL, tn), lambda l, j: (l, 0, j)),
                  pl.BlockSpec((None, 1, tn), lambda l, j: (l, 0, j))],
        out_specs=pl.BlockSpec((None, rows, tn), lambda l, j: (l, 0, j)),
        out_shape=jax.ShapeDtypeStruct((DEPTH, rows, width), F32),
        compiler_params=_params(("parallel", "parallel")),
        name="ada",
    )(c_all, ada_w, ada_b.reshape(DEPTH, 1, width))


def _ffn_kernel(x_ref, sh_ref, sc_ref, gt_ref, g_ref, wg_ref, wu_ref, wd_ref, o_ref, h_scr, acc_scr):
    j = pl.program_id(1)

    @pl.when(j == 0)
    def _():
        h_scr[...] = _prenorm(x_ref[...], g_ref[...], sc_ref[...], sh_ref[...]).astype(BF16)
        acc_scr[...] = jnp.zeros_like(acc_scr)

    h = h_scr[...]
    act = (_silu(_dot(h, wg_ref[...])) * _dot(h, wu_ref[...])).astype(BF16)
    acc_scr[...] += _dot(act, wd_ref[...])

    @pl.when(j == pl.num_programs(1) - 1)
    def _():
        o_ref[...] = x_ref[...] + 0.5 * gt_ref[...] * acc_scr[...]


def _ffn(x, mods, norm_g, wg, wu, wd, l, which, sub, rows_per_seq):
    n = x.shape[0]
    tm = _row_tile(n, mods, rows_per_seq, 1024)
    tf = 1408
    return pl.pallas_call(
        _ffn_kernel,
        grid=(n // tm, D_FF // tf),
        in_specs=[pl.BlockSpec((tm, D_MODEL), lambda i, j: (i, 0)),
                  _mod_spec(mods, l, 3 * sub, tm, rows_per_seq),
                  _mod_spec(mods, l, 3 * sub + 1, tm, rows_per_seq),
                  _mod_spec(mods, l, 3 * sub + 2, tm, rows_per_seq),
                  pl.BlockSpec((None, None, 1, D_MODEL), lambda i, j: (l, sub, 0, 0)),
                  pl.BlockSpec((None, None, D_MODEL, tf), lambda i, j: (l, which, 0, j)),
                  pl.BlockSpec((None, None, D_MODEL, tf), lambda i, j: (l, which, 0, j)),
                  pl.BlockSpec((None, None, tf, D_MODEL), lambda i, j: (l, which, j, 0))],
        out_specs=pl.BlockSpec((tm, D_MODEL), lambda i, j: (i, 0)),
        out_shape=jax.ShapeDtypeStruct((n, D_MODEL), F32),
        scratch_shapes=[pltpu.VMEM((tm, D_MODEL), BF16), pltpu.VMEM((tm, D_MODEL), F32)],
        compiler_params=_params(("parallel", "arbitrary")),
        name="ffn",
    )(x, mods, mods, mods, norm_g, wg, wu, wd)


def _inproj_kernel(x_ref, sh_ref, sc_ref, g_ref, w_ref, o_ref, h_scr):
    @pl.when(pl.program_id(1) == 0)
    def _():
        h_scr[...] = _prenorm(x_ref[...], g_ref[...], sc_ref[...], sh_ref[...]).astype(BF16)

    o_ref[...] = _dot(h_scr[...], w_ref[...])


def _inproj(x, mods, norm_g, w_in, l, rows_per_seq):
    n = x.shape[0]
    tm = _row_tile(n, mods, rows_per_seq, 1024)
    tn = 1536
    return pl.pallas_call(
        _inproj_kernel,
        grid=(n // tm, IN_PAD // tn),
        in_specs=[pl.BlockSpec((tm, D_MODEL), lambda i, j: (i, 0)),
                  _mod_spec(mods, l, 3, tm, rows_per_seq),
                  _mod_spec(mods, l, 4, tm, rows_per_seq),
                  pl.BlockSpec((None, None, 1, D_MODEL), lambda i, j: (l, 1, 0, 0)),
                  pl.BlockSpec((None, D_MODEL, tn), lambda i, j: (l, 0, j))],
        out_specs=pl.BlockSpec((tm, tn), lambda i, j: (i, j)),
        out_shape=jax.ShapeDtypeStruct((n, IN_PAD), F32),
        scratch_shapes=[pltpu.VMEM((tm, D_MODEL), BF16)],
        compiler_params=_params(("parallel", "arbitrary")),
        name="inproj",
    )(x, mods, mods, norm_g, w_in)


def _prep_kernel(aq_ref, ak_ref, av_ref, iq_ref, ikw_ref, cos_ref, sin_ref, qg_ref, kg_ref, bd_ref,
                 q_o, k_o, kb_o, vt_o, qi_o, ki_o, kib_o, wit_o, ktf_o, vtf_o, kitf_o):
    cos = cos_ref[...]
    sin = sin_ref[...]
    bd = bd_ref[...]
    lane = lax.broadcasted_iota(I32, cos.shape, 1)
    first_half = (lane & (HEAD_DIM // 2)) == 0

    def rope(x):
        partner = jnp.where(first_half, pltpu.roll(x, LANES - HEAD_DIM // 2, 1), pltpu.roll(x, HEAD_DIM // 2, 1))
        return x * cos + partner * sin

    def head_norm(x):
        hi, mid, lo = _split3(x * x)
        ms = (_dot(hi, bd) + _dot(mid, bd) + _dot(lo, bd)) * (1.0 / HEAD_DIM)
        return x * lax.rsqrt(ms + EPS)

    def put_heads(o_ref, first, y):
        for half in range(LANES // HEAD_DIM):
            o_ref[first + half] = y[:, half * HEAD_DIM:(half + 1) * HEAD_DIM].astype(BF16)

    qg = qg_ref[...]
    for c in range(ATTN_WIDTH // LANES):
        sl = slice(c * LANES, (c + 1) * LANES)
        put_heads(q_o, 2 * c, rope(head_norm(aq_ref[:, sl]) * qg) * (ATTN_SCALE * LOG2_E))
    k = rope(head_norm(ak_ref[...]) * kg_ref[...])
    k_o[...] = k
    ktf_o[...] = k.T
    put_heads(kb_o, 0, k)
    v_tf = av_ref[...].T
    vtf_o[...] = v_tf
    v_t = v_tf.astype(BF16)
    ones = jnp.ones((V_ROWS - HEAD_DIM, v_t.shape[1]), BF16)
    vt_o[...] = jnp.concatenate([v_t[:HEAD_DIM], ones, v_t[HEAD_DIM:], ones], axis=0)
    for c in range(IDX_HEADS * IDX_DIM // LANES):
        sl = slice(c * LANES, (c + 1) * LANES)
        put_heads(qi_o, 2 * c, rope(iq_ref[:, sl]))
    ikw = ikw_ref[...]
    ki_wide = rope(head_norm(ikw))
    ki = ki_wide[:, :IDX_DIM]
    ki_o[...] = ki
    kitf_o[...] = ki_wide.T[:IDX_DIM, :]
    kib_o[...] = ki.astype(BF16)
    wit_o[...] = ikw.T[IDX_DIM:IDX_DIM + 8, :]


def _prep(p, cos, sin, qg, kg, bd, tm, table_blocks):
    n = p.shape[0]

    def col(off, width):
        return pl.BlockSpec((tm, width), lambda i: (i, off // width))

    tab = pl.BlockSpec((tm, LANES), lambda i: (i % table_blocks, 0))
    vec = pl.BlockSpec((1, LANES), lambda i: (0, 0))

    def row(width):
        return pl.BlockSpec((tm, width), lambda i: (i, 0))

    def heads(h):
        return pl.BlockSpec((h, tm, HEAD_DIM), lambda i: (0, i, 0))

    n_seq = n // (tm * table_blocks)
    seq_len = tm * table_blocks

    def feature_major(width):
        return pl.BlockSpec((None, width, tm), lambda i: (i // table_blocks, 0, i % table_blocks))

    return pl.pallas_call(
        _prep_kernel,
        grid=(n // tm,),
        in_specs=[col(OFF_AQ, ATTN_WIDTH), col(OFF_AK, KV_WIDTH), col(OFF_AV, KV_WIDTH),
                  col(OFF_IQ, IDX_HEADS * IDX_DIM), col(OFF_IKW, LANES), tab, tab, vec, vec,
                  pl.BlockSpec((LANES, LANES), lambda i: (0, 0))],
        out_specs=[heads(N_HEADS), row(KV_WIDTH), heads(N_KV_HEADS),
                   pl.BlockSpec((None, N_KV_HEADS * V_ROWS, tm), lambda i: (i, 0, 0)),
                   heads(IDX_HEADS), row(IDX_DIM), row(IDX_DIM),
                   pl.BlockSpec((8, tm), lambda i: (0, i)),
                   feature_major(KV_WIDTH), feature_major(KV_WIDTH), feature_major(IDX_DIM)],
        out_shape=[jax.ShapeDtypeStruct((N_HEADS, n, HEAD_DIM), BF16),
                   jax.ShapeDtypeStruct((n, KV_WIDTH), F32),
                   jax.ShapeDtypeStruct((N_KV_HEADS, n, HEAD_DIM), BF16),
                   jax.ShapeDtypeStruct((n // tm, N_KV_HEADS * V_ROWS, tm), BF16),
                   jax.ShapeDtypeStruct((IDX_HEADS, n, IDX_DIM), BF16),
                   jax.ShapeDtypeStruct((n, IDX_DIM), F32),
                   jax.ShapeDtypeStruct((n, IDX_DIM), BF16),
                   jax.ShapeDtypeStruct((8, n), F32),
                   jax.ShapeDtypeStruct((n_seq, KV_WIDTH, seq_len), F32),
                   jax.ShapeDtypeStruct((n_seq, KV_WIDTH, seq_len), F32),
                   jax.ShapeDtypeStruct((n_seq, IDX_DIM, seq_len), F32)],
        compiler_params=_params(("parallel",)),
        name="prep",
    )(p, p, p, p, p, cos, sin, qg, kg, bd)


def _rope_tables(pos):
    half = HEAD_DIM // 2
    inv = ROPE_THETA ** (-jnp.arange(half, dtype=F32) / half)
    ang = pos.astype(F32)[:, None] * inv[None, :]
    c, s = jnp.cos(ang), jnp.sin(ang)
    return jnp.concatenate([c, c, c, c], axis=1), jnp.concatenate([-s, s, -s, s], axis=1)


def _pool_select(sums, lane):
    out = sums[-1]
    for gi in range(len(POOL_WINDOWS) - 2, -1, -1):
        out = jnp.where(lane < (gi + 1) * POOL_GROUP_DIM, sums[gi], out)
    return out


def _pool_kernel(u_ref, prev_ref, w_ref, s_ref, o_ref, carry_scr, *, tt, p0):
    t = pl.program_id(1)

    @pl.when(t == 0)
    def _():
        carry_scr[...] = prev_ref[...]

    u = u_ref[...]
    e = jnp.concatenate([carry_scr[...], u], axis=0)
    carry_scr[...] = u[tt - 16:, :]
    s2 = e[1:] + e[:-1]
    s4 = s2[2:] + s2[:-2]
    s8 = s4[4:] + s4[:-4]
    s16 = s8[8:] + s8[:-8]
    sums = (s2[15:15 + tt], s4[13:13 + tt], s8[9:9 + tt], s16[1:1 + tt])
    n_avail = (p0 + 1 + t * tt + lax.broadcasted_iota(I32, (tt, 1), 0)).astype(F32)
    lane = lax.broadcasted_iota(I32, (tt, POOL_WIDTH), 1)
    means = [sums[gi] / jnp.minimum(n_avail, float(w)) for gi, w in enumerate(POOL_WINDOWS)]
    pooled = _pool_select(means, lane) - u
    o_ref[...] = _dot(pooled.astype(BF16), w_ref[...]) * s_ref[...]


def _pool_prompt(p, prev16, wbd, scale, l, b, t_len):
    tt = min(512, t_len)
    nt = t_len // tt
    return pl.pallas_call(
        functools.partial(_pool_kernel, tt=tt, p0=0),
        grid=(b, nt),
        in_specs=[pl.BlockSpec((tt, POOL_WIDTH), lambda bi, ti: (bi * nt + ti, OFF_UP // POOL_WIDTH)),
                  pl.BlockSpec((None, 16, POOL_WIDTH), lambda bi, ti: (bi, 0, 0)),
                  pl.BlockSpec((None, POOL_WIDTH, POOL_WIDTH), lambda bi, ti: (l, 0, 0)),
                  pl.BlockSpec((None, 1, POOL_WIDTH), lambda bi, ti: (l, 0, 0))],
        out_specs=pl.BlockSpec((tt, POOL_WIDTH), lambda bi, ti: (bi * nt + ti, 0)),
        out_shape=jax.ShapeDtypeStruct((b * t_len, POOL_WIDTH), F32),
        scratch_shapes=[pltpu.VMEM((16, POOL_WIDTH), F32)],
        compiler_params=_params(("parallel", "arbitrary")),
        name="pool",
    )(p, prev16, wbd, scale)


def _pool_dec_kernel(u_ref, st_ref, w_ref, s_ref, o_ref, *, p0):
    u = u_ref[...]
    lane = lax.broadcasted_iota(I32, u.shape, 1)
    means = []
    acc = u
    d = 1
    for w in POOL_WINDOWS:
        while d < w:
            acc = acc + st_ref[POOL_STATE - d]
            d += 1
        means.append(acc / float(min(p0 + 1, w)))
    pooled = _pool_select(means, lane) - u
    o_ref[...] = _dot(pooled.astype(BF16), w_ref[...]) * s_ref[...]


def _pool_dec(p, st_t, wbd, scale, l, p0):
    n = p.shape[0]
    return pl.pallas_call(
        functools.partial(_pool_dec_kernel, p0=p0),
        grid=(1,),
        in_specs=[pl.BlockSpec((n, POOL_WIDTH), lambda i: (0, OFF_UP // POOL_WIDTH)),
                  pl.BlockSpec((POOL_STATE, n, POOL_WIDTH), lambda i: (0, 0, 0)),
                  pl.BlockSpec((None, POOL_WIDTH, POOL_WIDTH), lambda i: (l, 0, 0)),
                  pl.BlockSpec((None, 1, POOL_WIDTH), lambda i: (l, 0, 0))],
        out_specs=pl.BlockSpec((n, POOL_WIDTH), lambda i: (0, 0)),
        out_shape=jax.ShapeDtypeStruct((n, POOL_WIDTH), F32),
        compiler_params=_params(("arbitrary",)),
        name="pool_dec",
    )(p, st_t, wbd, scale)


def _hgrn_consts(c):
    r = np.arange(c)[:, None]
    s = np.arange(c)[None, :]
    mats = [(s <= r) & (s // h == r // h) for h in HGRN_SMALL]
    mats += [(s > r) & (s // h == r // h) for h in HGRN_SMALL]
    mats.append(s <= r)
    return jnp.asarray(np.concatenate(mats, axis=0).astype(np.float32), BF16)


def _hgrn_kernel(hq_ref, hf_ref, hi_ref, hg_ref, lb_ref, on_ref, ms_ref, yb_ref, st_ref, s_scr, *, c):
    step = pl.program_id(1)
    nl = len(HGRN_LEVELS)

    @pl.when(step == 0)
    def _():
        s_scr[...] = jnp.zeros_like(s_scr)

    ti = lax.broadcasted_iota(I32, (c, c), 0)
    si = lax.broadcasted_iota(I32, (c, c), 1)
    pairs = []
    for h in HGRN_LEVELS[:-1]:
        sh = h.bit_length() - 1
        tb = jnp.right_shift(ti, sh)
        pairs.append(((tb & 1) == 1) & (jnp.right_shift(si, sh) == tb - 1))

    for hd in range(HGRN_HEADS):
        ks = slice(hd * HGRN_K, (hd + 1) * HGRN_K)
        vs = slice(hd * HGRN_V, (hd + 1) * HGRN_V)
        q = hq_ref[:, ks]
        lb = lb_ref[:, ks]
        f = lb + (1.0 - lb) * jax.nn.sigmoid(hf_ref[:, ks])
        g = jnp.log(jnp.maximum(f, F_FLOOR))
        kk = 1.0 - f
        kkb = kk.astype(BF16)
        ivb = hi_ref[:, vs].astype(BF16)
        g_hi, g_lo = _split2(g)
        r = _dot(ms_ref[...], jnp.concatenate([g_hi, g_lo], axis=1))
        r = r[:, :HGRN_K] + r[:, HGRN_K:]
        n_small = len(HGRN_SMALL)
        low = [g] + [r[k * c:(k + 1) * c] for k in range(n_small)]
        up = [None] + [r[(n_small + k) * c:(n_small + k + 1) * c] for k in range(n_small)]
        cum = r[2 * n_small * c:]
        for h in HGRN_LEVELS[1 + n_small:]:
            cum3 = cum.reshape(c // h, h, HGRN_K)
            ends = cum3[:, h - 1:h, :]
            if h == c:
                low.append(cum)
            else:
                starts = jnp.concatenate([jnp.zeros((1, 1, HGRN_K), F32), ends[:-1]], axis=0)
                low.append((cum3 - starts).reshape(c, HGRN_K))
            up.append((ends - cum3).reshape(c, HGRN_K))

        att = jnp.where(ti == si, _dot_nt(q.astype(BF16), kkb), 0.0)
        for k in range(nl - 1):
            qe = (q * jnp.exp(low[k])).astype(BF16)
            ke = kkb if k == 0 else (kk * jnp.exp(up[k])).astype(BF16)
            att = jnp.where(pairs[k], _dot_nt(qe, ke), att)
        qe = (q * jnp.exp(low[-1])).astype(BF16)
        ke = (kk * jnp.exp(up[-1])).astype(BF16)
        s_t = s_scr[hd]
        o = _dot(att.astype(BF16), ivb) + _dot_nt(qe, s_t.astype(BF16))
        s_scr[hd] = s_t * jnp.exp(low[-1][c - 1:c, :]) + _dot_tn(ivb, ke)

        on = o * lax.rsqrt(jnp.mean(o * o, axis=-1, keepdims=True) + EPS) * on_ref[...]
        yb_ref[:, vs] = on * _silu(hg_ref[:, vs])

    @pl.when(step == pl.num_programs(1) - 1)
    def _():
        st_ref[...] = s_scr[...]


def _hgrn_prompt(p, lb, onorm, mstack, b, t_len):
    c = HGRN_CHUNK
    nc = t_len // c
    kw = HGRN_HEADS * HGRN_K
    vw = HGRN_HEADS * HGRN_V

    def col(off, width):
        return pl.BlockSpec((c, width), lambda bi, ci: (bi * nc + ci, off // width))

    return pl.pallas_call(
        functools.partial(_hgrn_kernel, c=c),
        grid=(b, nc),
        in_specs=[col(OFF_HQ, kw), col(OFF_HF, kw), col(OFF_HI, vw), col(OFF_HG, vw),
                  pl.BlockSpec((1, kw), lambda bi, ci: (0, 0)),
                  pl.BlockSpec((1, HGRN_V), lambda bi, ci: (0, 0)),
                  pl.BlockSpec(mstack.shape, lambda bi, ci: (0, 0))],
        out_specs=[pl.BlockSpec((c, vw), lambda bi, ci: (bi * nc + ci, 0)),
                   pl.BlockSpec((None, HGRN_HEADS, HGRN_V, HGRN_K), lambda bi, ci: (bi, 0, 0, 0))],
        out_shape=[jax.ShapeDtypeStruct((b * t_len, vw), F32),
                   jax.ShapeDtypeStruct((b, HGRN_HEADS, HGRN_V, HGRN_K), F32)],
        scratch_shapes=[pltpu.VMEM((HGRN_HEADS, HGRN_V, HGRN_K), F32)],
        compiler_params=_params(("parallel", "arbitrary")),
        name="hgrn",
    )(p, p, p, p, lb, onorm, mstack)


def _hgrn_dec_kernel(q_ref, z_ref, lb_ref, i_ref, hg_ref, on_ref, s_ref, yb_ref, so_ref):
    lb = lb_ref[...]
    f = lb + (1.0 - lb) * jax.nn.sigmoid(z_ref[...])
    a = jnp.exp(jnp.log(jnp.maximum(f, F_FLOOR)))
    s_new = a * s_ref[...] + i_ref[...] * (1.0 - f)
    so_ref[...] = s_new
    o = jnp.sum(q_ref[...] * s_new, axis=-1, keepdims=True)
    on = o * lax.rsqrt(jnp.mean(o * o, axis=-2, keepdims=True) + EPS) * on_ref[...]
    yb_ref[...] = on * _silu(hg_ref[...])


def _hgrn_dec(q_r, z_r, lb_r, i_c, hg_c, onorm_c, state_t, l):
    n = state_t.shape[1]
    nb = 8
    row = pl.BlockSpec((nb, HGRN_HEADS, 1, HGRN_K), lambda i: (i, 0, 0, 0))
    col = pl.BlockSpec((nb, HGRN_HEADS, HGRN_V, 1), lambda i: (i, 0, 0, 0))
    return pl.pallas_call(
        _hgrn_dec_kernel,
        grid=(n // nb,),
        in_specs=[row, row, pl.BlockSpec((HGRN_HEADS, 1, HGRN_K), lambda i: (0, 0, 0)), col, col,
                  pl.BlockSpec((HGRN_V, 1), lambda i: (0, 0)),
                  pl.BlockSpec((None, nb, HGRN_HEADS, HGRN_V, HGRN_K), lambda i: (l, i, 0, 0, 0))],
        out_specs=[col, pl.BlockSpec((nb, HGRN_HEADS, HGRN_V, HGRN_K), lambda i: (i, 0, 0, 0))],
        out_shape=[jax.ShapeDtypeStruct((n, HGRN_HEADS, HGRN_V, 1), F32),
                   jax.ShapeDtypeStruct(state_t.shape[1:], F32)],
        compiler_params=_params(("parallel",)),
        name="hgrn_dec",
    )(q_r, z_r, lb_r, i_c, hg_c, onorm_c, state_t)


def _sort_key(s):
    bits = pltpu.bitcast(s, I32)
    return jnp.where(bits < 0, INT_MIN - bits, bits)


def _lane_fold(x):
    part = x[:, :LANES]
    for cidx in range(1, x.shape[1] // LANES):
        part = part + x[:, cidx * LANES:(cidx + 1) * LANES]
    return part


def _select_bias(key, thr, idx, j_lim):
    return jnp.where(key > thr, 0.0, jnp.where(key == thr, jnp.where(idx < j_lim, 0.0, NEG_BIG), NEG_BIG))


def _bisect_bits(count_ge, thr, first_bit, n_bits, k):
    def bit_body(it, thr):
        cand = thr + lax.shift_left(jnp.int32(1), first_bit - it)
        return jnp.where(count_ge(cand) >= k, cand, thr)

    return lax.fori_loop(0, n_bits, bit_body, thr)


def _topk_threshold(count, stat_shape, n_sel, idx_bits, idx_sentinel, kth_largest=None):
    k = float(n_sel)
    if kth_largest is None:
        thr = _bisect_bits(lambda cand: count(lambda key, idx: jnp.where(key >= cand, 1.0, 0.0)),
                           jnp.full(stat_shape, INT_MIN, I32), 31, 32, k)
    else:
        thr = kth_largest(k)
    n_ge = count(lambda key, idx: jnp.where(key >= thr, 1.0, 0.0))
    n_gt = count(lambda key, idx: jnp.where(key > thr, 1.0, 0.0))
    need = k - n_gt
    live = thr > INT_MIN
    tie = jnp.where(live, jnp.where(n_ge > k, 1.0, 0.0), 0.0)
    j_default = jnp.where(live, jnp.int32(idx_sentinel), jnp.int32(0))

    def tie_search():
        def idx_body(it, j_lim):
            cand = j_lim + lax.shift_left(jnp.int32(1), idx_bits - 1 - it)
            cnt = count(lambda key, idx: jnp.where(key == thr, jnp.where(idx < cand, 1.0, 0.0), 0.0))
            return jnp.where(cnt <= need, cand, j_lim)

        j_lim = lax.fori_loop(0, idx_bits, idx_body, jnp.zeros(stat_shape, I32))
        return jnp.where(tie > 0.0, j_lim, j_default)

    j_lim = lax.cond(jnp.max(tie) > 0.0, tie_search, lambda: j_default)
    return thr, j_lim


FOLD_ROWS = 64
V_ROWS = HEAD_DIM + 16


def _fold(x, op):
    parts = [x[r * FOLD_ROWS:(r + 1) * FOLD_ROWS] for r in range(x.shape[0] // FOLD_ROWS)]
    while len(parts) > 1:
        parts = [op(parts[i], parts[i + 1]) if i + 1 < len(parts) else parts[i]
                 for i in range(0, len(parts), 2)]
    return parts[0]


def _dsa_kernel(q_ref, qi_ref, wit_ref, k_ref, vt_ref, ki_ref, o_ref, keys_scr, k16_scr, lg_scr, p_scr, m_scr, acc_scr,
                *, tq, tk, n_sel, t_len):
    i = pl.program_id(1)
    nk = ((i + 1) * tq + tk - 1) // tk
    krow = lax.broadcasted_iota(I32, (tk, tq), 0)
    qpos = i * tq + lax.broadcasted_iota(I32, (tk, tq), 1)
    wit = wit_ref[...] * IDX_SCALE
    qi_all = qi_ref[...].reshape(IDX_HEADS * tq, IDX_DIM)

    def rows(j):
        return pl.ds(pl.multiple_of(j * tk, tk), tk)

    def score_tile(j, causal):
        rel = _dot_nt(ki_ref[rows(j), :], qi_all)
        s = wit[0:1, :] * jnp.maximum(rel[:, :tq], 0.0)
        for h in range(1, IDX_HEADS):
            s = s + wit[h:h + 1, :] * jnp.maximum(rel[:, h * tq:(h + 1) * tq], 0.0)
        key = jnp.where(s > 0.5 * NEG_BIG, _sort_key(s), INT_MIN)
        if causal:
            key = jnp.where(j * tk + krow <= qpos, key, INT_MIN)
        keys_scr[j] = key
        k16_scr[j] = jnp.right_shift(key, 16).astype(I16)

    def score_body(j, carry):
        score_tile(j, False)
        return carry

    lax.fori_loop(0, nk - 1, score_body, 0)
    score_tile(nk - 1, True)

    def count(pred):
        def body(j, acc):
            return acc + _fold(pred(keys_scr[j], j * tk + krow), jnp.add)

        acc = lax.fori_loop(0, nk, body, jnp.zeros((FOLD_ROWS, tq), F32))
        return jnp.sum(acc, axis=0, keepdims=True)

    one16, zero16 = jnp.int16(1), jnp.int16(0)

    def count16(pred):
        def body(j, acc):
            return acc + _fold(pred(k16_scr[j]), jnp.add)

        acc = lax.fori_loop(0, nk, body, jnp.zeros((FOLD_ROWS, tq), I16))
        return jnp.sum(acc.astype(F32), axis=0, keepdims=True)

    def high16(x):
        return jnp.right_shift(x, 16).astype(I16)

    def low16(x):
        return ((x & 0xFFFF) - 32768).astype(I16)

    def kth_largest(k):
        thr = jnp.full((1, tq), INT_MIN, I32)
        thr = _bisect_bits(lambda cand: count16(lambda t: jnp.where(t >= high16(cand), one16, zero16)),
                           thr, 31, 16, k)
        thr_hi = high16(thr)
        n_above = count16(lambda t: jnp.where(t > thr_hi, one16, zero16))

        def low_body(j, carry):
            key = keys_scr[j]
            k16_scr[j] = jnp.where(high16(key) == thr_hi, low16(key), jnp.int16(-32768))
            return carry

        lax.fori_loop(0, nk, low_body, 0)
        return _bisect_bits(
            lambda cand: n_above + count16(lambda t: jnp.where(t >= low16(cand), one16, zero16)),
            thr, 15, 16, k)

    thr, j_lim = _topk_threshold(count, (1, tq), n_sel, int(t_len).bit_length(), t_len, kth_largest)

    m_scr[...] = jnp.full(m_scr.shape, 0.5 * NEG_BIG, F32)
    acc_scr[...] = jnp.zeros_like(acc_scr)

    def attend_tile(j, carry):
        bias = _select_bias(keys_scr[j], thr, j * tk + krow, j_lim)
        bias = jnp.concatenate([bias] * GROUP, axis=1)
        v_t = vt_ref[j]
        heads = range(N_KV_HEADS)
        for n in heads:
            qg = q_ref[n * GROUP:(n + 1) * GROUP].reshape(GROUP * tq, HEAD_DIM)
            lg_scr[n] = _dot_nt(k_ref[n, rows(j), :], qg) + bias
        m_old = [m_scr[n] for n in heads]
        m_new = [jnp.maximum(m_old[n], jnp.max(_fold(lg_scr[n], jnp.maximum), axis=0, keepdims=True))
                 for n in heads]
        for n in heads:
            p_scr[n] = jnp.exp2(lg_scr[n] - m_new[n]).astype(BF16)
        for n in heads:
            acc_scr[n] = (jnp.exp2(m_old[n] - m_new[n]) * acc_scr[n]
                          + _dot(v_t[n * V_ROWS:(n + 1) * V_ROWS, :], p_scr[n]))
            m_scr[n] = m_new[n]
        return carry

    lax.fori_loop(0, nk, attend_tile, 0)
    outs = []
    for n in range(N_KV_HEADS):
        acc = acc_scr[n]
        o_n = acc[:HEAD_DIM] / acc[HEAD_DIM:HEAD_DIM + 1]
        outs += [o_n[:, g * tq:(g + 1) * tq] for g in range(GROUP)]
    o_ref[...] = jnp.concatenate(outs, axis=0).T


def _dsa_prompt(q_h, qi_h, wit, k_h, v_t, ki_b, b, t_len, n_sel, tk):
    tq = min(256, t_len)
    nq = t_len // tq
    nt = t_len // tk

    def heads(h, rows, imap):
        return pl.BlockSpec((h, rows, HEAD_DIM), imap)

    return pl.pallas_call(
        functools.partial(_dsa_kernel, tq=tq, tk=tk, n_sel=n_sel, t_len=t_len),
        grid=(b, nq),
        in_specs=[heads(N_HEADS, tq, lambda bi, qi: (0, bi * nq + qi, 0)),
                  heads(IDX_HEADS, tq, lambda bi, qi: (0, bi * nq + qi, 0)),
                  pl.BlockSpec((8, tq), lambda bi, qi: (0, bi * nq + qi)),
                  heads(N_KV_HEADS, t_len, lambda bi, qi: (0, bi, 0)),
                  pl.BlockSpec((nt, N_KV_HEADS * V_ROWS, tk), lambda bi, qi: (bi, 0, 0)),
                  pl.BlockSpec((t_len, IDX_DIM), lambda bi, qi: (bi, 0))],
        out_specs=pl.BlockSpec((tq, ATTN_WIDTH), lambda bi, qi: (bi * nq + qi, 0)),
        out_shape=jax.ShapeDtypeStruct((b * t_len, ATTN_WIDTH), F32),
        scratch_shapes=[pltpu.VMEM((nt, tk, tq), I32),
                        pltpu.VMEM((nt, tk, tq), I16),
                        pltpu.VMEM((N_KV_HEADS, tk, GROUP * tq), F32),
                        pltpu.VMEM((N_KV_HEADS, tk, GROUP * tq), BF16),
                        pltpu.VMEM((N_KV_HEADS, 1, GROUP * tq), F32),
                        pltpu.VMEM((N_KV_HEADS, V_ROWS, GROUP * tq), F32)],
        compiler_params=_params(("parallel", "arbitrary")),
        name="dsa",
    )(q_h, qi_h, wit, k_h, v_t, ki_b)


DEC_SEQS_PER_STEP = 4


def _dsa_dec_score_kernel(pt_ref, qi_ref, wi_ref, kin_ref, *rest, n_pages, n_seq):
    page_refs = rest[:n_seq * n_pages]
    o_ref = rest[n_seq * n_pages]
    lane = lax.broadcasted_iota(I32, (1, PAGE_SIZE), 1)
    for s in range(n_seq):
        qi = qi_ref[s]
        wi = wi_ref[s]

        def finish(rel):
            sc = jnp.sum(wi * jnp.maximum(rel, 0.0), axis=0, keepdims=True) * IDX_SCALE
            return jnp.where(sc > 0.5 * NEG_BIG, _sort_key(sc), INT_MIN)

        for pg in range(n_pages):
            page = page_refs[s * n_pages + pg][...].astype(BF16)
            o_ref[s, :, pg * PAGE_SIZE:(pg + 1) * PAGE_SIZE] = finish(_dot(qi, page))
        kin = kin_ref[s].astype(BF16).astype(F32)
        key_new = finish(jnp.sum(qi.astype(F32) * kin, axis=1, keepdims=True))
        o_ref[s, :, n_pages * PAGE_SIZE:] = jnp.where(lane == 0, key_new, INT_MIN)


def _page_specs(block, l, n_seq, n_pages):
    def spec(s, pg):
        return pl.BlockSpec(block, lambda bi, pt, *_: (l, pt[bi * n_seq + s, pg], 0, 0))

    return [spec(s, pg) for s in range(n_seq) for pg in range(n_pages)]


def _dsa_dec_scores(page_table, qi8, wi8, ki_new, cache_ki_t, l):
    n, n_pages = page_table.shape
    ns = min(DEC_SEQS_PER_STEP, n)
    width = (n_pages + 1) * PAGE_SIZE
    page_specs = _page_specs((None, None, IDX_DIM, PAGE_SIZE), l, ns, n_pages)
    return pl.pallas_call(
        functools.partial(_dsa_dec_score_kernel, n_pages=n_pages, n_seq=ns),
        grid_spec=pltpu.PrefetchScalarGridSpec(
            num_scalar_prefetch=1,
            grid=(n // ns,),
            in_specs=[pl.BlockSpec((ns, 8, IDX_DIM), lambda bi, pt: (bi, 0, 0)),
                      pl.BlockSpec((ns, 8, 1), lambda bi, pt: (bi, 0, 0)),
                      pl.BlockSpec((ns, 1, IDX_DIM), lambda bi, pt: (bi, 0, 0))] + page_specs,
            out_specs=pl.BlockSpec((ns, 1, width), lambda bi, pt: (bi, 0, 0))),
        out_shape=jax.ShapeDtypeStruct((n, 1, width), I32),
        compiler_params=_params(("arbitrary",)),
        name="dsa_dec_scores",
    )(page_table, qi8, wi8, ki_new, *([cache_ki_t] * (ns * n_pages)))


def _dsa_dec_thr_kernel(keys_ref, thr_ref, j_ref, *, n_sel):
    n, width = keys_ref.shape
    lane = lax.broadcasted_iota(I32, (n, width), 1)

    def count(pred):
        return jnp.sum(_lane_fold(pred(keys_ref[...], lane)), axis=1, keepdims=True)

    thr, j_lim = _topk_threshold(count, (n, 1), n_sel, int(width).bit_length(), width)
    thr_ref[...] = thr
    j_ref[...] = j_lim


def _dsa_dec_thr(keys2d, n_sel):
    n, width = keys2d.shape
    return pl.pallas_call(
        functools.partial(_dsa_dec_thr_kernel, n_sel=n_sel),
        grid=(1,),
        in_specs=[pl.BlockSpec((n, width), lambda i: (0, 0))],
        out_specs=[pl.BlockSpec((n, 1), lambda i: (0, 0)), pl.BlockSpec((n, 1), lambda i: (0, 0))],
        out_shape=[jax.ShapeDtypeStruct((n, 1), I32), jax.ShapeDtypeStruct((n, 1), I32)],
        compiler_params=_params(("arbitrary",)),
        name="dsa_dec_thr",
    )(keys2d)


def _dsa_dec_attn_kernel(pt_ref, thr_ref, j_ref, q_ref, keys_ref, kn_ref, vn_ref, *rest, n_pages, n_seq):
    k_pages = rest[:n_seq * n_pages]
    v_pages = rest[n_seq * n_pages:2 * n_seq * n_pages]
    o_ref, k_scr, v_scr = rest[2 * n_seq * n_pages:]
    n_past = n_pages * PAGE_SIZE
    idx = lax.broadcasted_iota(I32, (1, keys_ref.shape[-1]), 1)
    for s in range(n_seq):
        b = pl.program_id(0) * n_seq + s
        for pg in range(n_pages):
            sl = slice(pg * PAGE_SIZE, (pg + 1) * PAGE_SIZE)
            k_scr[s, :, sl] = k_pages[s * n_pages + pg][...].astype(BF16)
            v_scr[s, :, sl] = v_pages[s * n_pages + pg][...].astype(BF16)
        bias_all = _select_bias(keys_ref[s], thr_ref[b], idx, j_ref[b])
        bias = bias_all[:, :n_past]
        bias_new = bias_all[:, n_past:n_past + 1]
        kn = kn_ref[s].astype(BF16).astype(F32)
        vn = vn_ref[s].astype(BF16).astype(F32)
        for n in range(N_KV_HEADS):
            fs = slice(n * HEAD_DIM, (n + 1) * HEAD_DIM)
            q = q_ref[s, n]
            lg = _dot(q, k_scr[s, fs, :]) + bias
            lg_new = jnp.sum(q.astype(F32) * kn[:, fs], axis=1, keepdims=True) + bias_new
            m = jnp.maximum(jnp.max(lg, axis=1, keepdims=True), lg_new)
            p = jnp.exp2(lg - m)
            p_new = jnp.exp2(lg_new - m)
            den = jnp.sum(p, axis=1, keepdims=True) + p_new
            o_ref[s, n] = (_dot_nt(p.astype(BF16), v_scr[s, fs, :]) + p_new * vn[:, fs]) / den


def _dsa_dec_attn(page_table, thr, j_lim, q8, keys, k_new, v_new, cache_k_t, cache_v_t, l):
    n, n_pages = page_table.shape
    ns = min(DEC_SEQS_PER_STEP, n)
    width = (n_pages + 1) * PAGE_SIZE
    page_specs = _page_specs((None, None, KV_WIDTH, PAGE_SIZE), l, ns, n_pages)
    qspec = pl.BlockSpec((ns, N_KV_HEADS, 8, HEAD_DIM), lambda bi, pt, th, jl: (bi, 0, 0, 0))

    def per_seq(width_):
        return pl.BlockSpec((ns, 1, width_), lambda bi, pt, th, jl: (bi, 0, 0))

    return pl.pallas_call(
        functools.partial(_dsa_dec_attn_kernel, n_pages=n_pages, n_seq=ns),
        grid_spec=pltpu.PrefetchScalarGridSpec(
            num_scalar_prefetch=3,
            grid=(n // ns,),
            in_specs=[qspec, per_seq(width), per_seq(KV_WIDTH), per_seq(KV_WIDTH)] + page_specs + page_specs,
            out_specs=qspec,
            scratch_shapes=[pltpu.VMEM((ns, KV_WIDTH, n_pages * PAGE_SIZE), BF16),
                            pltpu.VMEM((ns, KV_WIDTH, n_pages * PAGE_SIZE), BF16)]),
        out_shape=jax.ShapeDtypeStruct((n, N_KV_HEADS, 8, HEAD_DIM), F32),
        compiler_params=_params(("arbitrary",)),
        name="dsa_dec_attn",
    )(page_table, thr, j_lim, q8, keys, k_new, v_new,
      *([cache_k_t] * (ns * n_pages)), *([cache_v_t] * (ns * n_pages)))


def _merge_kernel(x_ref, g0_ref, g1_ref, g2_ref, ya_ref, yb_ref, yc_ref, gt_ref,
                  wa_ref, wb_ref, wc_ref, wo_ref, o_ref):
    merged = (jax.nn.sigmoid(g0_ref[...]) * _dot(ya_ref[...].astype(BF16), wa_ref[...])
              + jax.nn.sigmoid(g1_ref[...]) * _dot(yb_ref[...].astype(BF16), wb_ref[...])
              + jax.nn.sigmoid(g2_ref[...]) * _dot(yc_ref[...].astype(BF16), wc_ref[...]))
    o_ref[...] = x_ref[...] + gt_ref[...] * _dot(merged.astype(BF16), wo_ref[...])


def _merge(x, p, ya, yb, yc, mods, w_ba, w_bb, w_bc, w_out, l, rows_per_seq):
    n = x.shape[0]
    tm = _row_tile(n, mods, rows_per_seq, 512)

    def wspec(rows):
        return pl.BlockSpec((None, rows, D_MODEL), lambda i: (l, 0, 0))

    def row(width):
        return pl.BlockSpec((tm, width), lambda i: (i, 0))

    def gate(bidx):
        return pl.BlockSpec((tm, D_MODEL), lambda i: (i, OFF_GL // D_MODEL + bidx))

    return pl.pallas_call(
        _merge_kernel,
        grid=(n // tm,),
        in_specs=[row(D_MODEL), gate(0), gate(1), gate(2), row(POOL_WIDTH), row(HGRN_HEADS * HGRN_V),
                  row(ATTN_WIDTH), _mod_spec(mods, l, 5, tm, rows_per_seq),
                  wspec(POOL_WIDTH), wspec(HGRN_HEADS * HGRN_V), wspec(ATTN_WIDTH), wspec(D_MODEL)],
        out_specs=row(D_MODEL),
        out_shape=jax.ShapeDtypeStruct((n, D_MODEL), F32),
        compiler_params=_params(("parallel",)),
        name="merge",
    )(x, p, p, p, ya, yb, yc, mods, w_ba, w_bb, w_bc, w_out)


def _permute_w_in(w_in):
    segs = [(2884, 5956), (256, 768), (768, 1280), (1792, 2304), (0, 256), (1280, 1536), (1536, 1792),
            (2560, 2816), (2304, 2432), (2432, 2560), (2816, 2884)]
    parts = [w_in[..., a:b] for a, b in segs]
    used = sum(b - a for a, b in segs)
    parts.append(jnp.zeros(w_in.shape[:-1] + (IN_PAD - used,), w_in.dtype))
    return jnp.concatenate(parts, axis=-1).astype(BF16)


def _lower_bounds(p):
    sm = jax.nn.softmax(p.astype(F32), axis=0)
    return jnp.cumsum(sm, axis=0) - sm[0:1]


def kernel(x_prompt, x_sample, c_prompt, c_sample, cache_k, cache_v, cache_ki, page_table, state_pool, state_hgrn, ada_w, ada_b, norm_g, ffn_wg, ffn_wu, ffn_wd, w_in, pool_w, pool_scale, hgrn_lb, hgrn_onorm, q_norm, k_norm, w_ba, w_bb, w_bc, w_out):
    bp, t_len, _ = x_prompt.shape
    bs = x_sample.shape[0]
    assert x_sample.shape[1] == 1
    n_pages = page_table.shape[1]
    past_len = n_pages * PAGE_SIZE
    n_phys = cache_k.shape[1]
    n_sel_p = min(TOPK_MAX, t_len // 4)
    n_sel_s = min(TOPK_MAX, (past_len + 1) // 4)

    wg, wu, wd = ffn_wg.astype(BF16), ffn_wu.astype(BF16), ffn_wd.astype(BF16)
    w_in_p = _permute_w_in(w_in)
    wa, wb, wc, wo = w_ba.astype(BF16), w_bb.astype(BF16), w_bc.astype(BF16), w_out.astype(BF16)
    eye = jnp.eye(len(POOL_WINDOWS), dtype=F32)
    pool_wbd = jnp.einsum('lgcd,gh->lgchd', pool_w, eye).reshape(DEPTH, POOL_WIDTH, POOL_WIDTH).astype(BF16)
    pool_sc = pool_scale.reshape(DEPTH, 1, POOL_WIDTH)
    norm_g4 = norm_g.reshape(DEPTH, N_SUB, 1, D_MODEL)
    lbs = _lower_bounds(hgrn_lb)
    qg = jnp.tile(q_norm, (1, LANES // HEAD_DIM)).reshape(DEPTH, 1, LANES)
    kg = jnp.tile(k_norm, (1, LANES // HEAD_DIM)).reshape(DEPTH, 1, LANES)
    gid = np.arange(LANES) // HEAD_DIM
    bd = jnp.asarray((gid[:, None] == gid[None, :]).astype(np.float32), BF16)
    mstack = _hgrn_consts(HGRN_CHUNK)
    cos_p, sin_p = _rope_tables(jnp.arange(t_len, dtype=I32))
    cos_s, sin_s = _rope_tables(jnp.full((bs,), past_len, I32))
    ck = jnp.transpose(cache_k, (0, 1, 3, 4, 2)).reshape(DEPTH, n_phys, KV_WIDTH, PAGE_SIZE)
    cv = jnp.transpose(cache_v, (0, 1, 3, 4, 2)).reshape(DEPTH, n_phys, KV_WIDTH, PAGE_SIZE)
    cki = jnp.transpose(cache_ki, (0, 1, 3, 2))
    state_hgrn_t = jnp.swapaxes(state_hgrn, -1, -2)

    rows_all = bp + bs
    rows_pad = -(-rows_all // 8) * 8
    c_all = jnp.concatenate([c_prompt, c_sample, jnp.zeros((rows_pad - rows_all, D_MODEL), F32)], axis=0)
    mods = _ada(c_all, ada_w, ada_b)
    mods_p = mods[:, :bp].reshape(DEPTH, bp, 1, 3 * N_SUB * D_MODEL)
    mods_s = mods[:, bp:rows_all]

    xp = x_prompt.reshape(bp * t_len, D_MODEL)
    xs = x_sample.reshape(bs, D_MODEL)
    pool_prev_p = jnp.zeros((bp, 16, POOL_WIDTH), F32)
    outs = [[] for _ in range(10)]
    tm_prep = min(512, t_len)

    for l in range(DEPTH):
        xp = _ffn(xp, mods_p, norm_g4, wg, wu, wd, l, 0, 0, t_len)
        p = _inproj(xp, mods_p, norm_g4, w_in_p, l, t_len)
        ya = _pool_prompt(p, pool_prev_p, pool_wbd, pool_sc, l, bp, t_len)
        yb, st_t = _hgrn_prompt(p, lbs[l].reshape(1, -1), hgrn_onorm[l].reshape(1, -1), mstack, bp, t_len)
        q_h, _, k_h, v_t, qi_h, _, ki_b, wit, k_tf, v_tf, ki_tf = _prep(
            p, cos_p, sin_p, qg[l], kg[l], bd, tm_prep, t_len // tm_prep)
        yc = _dsa_prompt(q_h, qi_h, wit, k_h, v_t, ki_b, bp, t_len, n_sel_p, tm_prep)
        xp = _merge(xp, p, ya, yb, yc, mods_p, wa, wb, wc, wo, l, t_len)
        xp = _ffn(xp, mods_p, norm_g4, wg, wu, wd, l, 1, 2, t_len)
        p3 = p.reshape(bp, t_len, IN_PAD)
        outs[0].append(jnp.transpose(k_tf.reshape(bp, N_KV_HEADS, HEAD_DIM, t_len), (0, 3, 1, 2)))
        outs[1].append(jnp.transpose(v_tf.reshape(bp, N_KV_HEADS, HEAD_DIM, t_len), (0, 3, 1, 2)))
        outs[2].append(jnp.swapaxes(ki_tf, 1, 2))
        outs[3].append(p3[:, t_len - POOL_STATE:, OFF_UP:OFF_UP + POOL_WIDTH])
        outs[4].append(jnp.swapaxes(st_t, -1, -2))

        xs = _ffn(xs, mods_s, norm_g4, wg, wu, wd, l, 0, 0, 1)
        ps = _inproj(xs, mods_s, norm_g4, w_in_p, l, 1)
        ya = _pool_dec(ps, jnp.swapaxes(state_pool[l], 0, 1), pool_wbd, pool_sc, l, past_len)
        kw = HGRN_HEADS * HGRN_K
        vw = HGRN_HEADS * HGRN_V
        yb4, st_new = _hgrn_dec(
            ps[:, OFF_HQ:OFF_HQ + kw].reshape(bs, HGRN_HEADS, 1, HGRN_K),
            ps[:, OFF_HF:OFF_HF + kw].reshape(bs, HGRN_HEADS, 1, HGRN_K),
            lbs[l].reshape(HGRN_HEADS, 1, HGRN_K),
            ps[:, OFF_HI:OFF_HI + vw].reshape(bs, HGRN_HEADS, HGRN_V, 1),
            ps[:, OFF_HG:OFF_HG + vw].reshape(bs, HGRN_HEADS, HGRN_V, 1),
            hgrn_onorm[l].reshape(-1, 1), state_hgrn_t, l)
        yb = yb4.reshape(bs, vw)
        q_h, k_r, _, _, qi_h, ki_r, _, _, _, _, _ = _prep(ps, cos_s, sin_s, qg[l], kg[l], bd, bs, 1)
        v_new = ps[:, OFF_AV:OFF_AV + KV_WIDTH]
        qi8 = jnp.pad(jnp.swapaxes(qi_h, 0, 1), ((0, 0), (0, 8 - IDX_HEADS), (0, 0)))
        wi8 = jnp.pad(ps[:, OFF_IKW + IDX_DIM:OFF_IKW + IDX_DIM + IDX_HEADS].reshape(bs, IDX_HEADS, 1),
                      ((0, 0), (0, 8 - IDX_HEADS), (0, 0)))
        keys = _dsa_dec_scores(page_table, qi8, wi8, ki_r.reshape(bs, 1, IDX_DIM), cki, l)
        thr, j_lim = _dsa_dec_thr(keys.reshape(bs, -1), n_sel_s)
        q8 = jnp.pad(jnp.swapaxes(q_h, 0, 1).reshape(bs, N_KV_HEADS, GROUP, HEAD_DIM),
                     ((0, 0), (0, 0), (0, 8 - GROUP), (0, 0)))
        yc8 = _dsa_dec_attn(page_table, thr.reshape(bs), j_lim.reshape(bs), q8, keys,
                            k_r.reshape(bs, 1, KV_WIDTH), v_new.reshape(bs, 1, KV_WIDTH), ck, cv, l)
        yc = yc8[:, :, :GROUP, :].reshape(bs, ATTN_WIDTH)
        xs = _merge(xs, ps, ya, yb, yc, mods_s, wa, wb, wc, wo, l, 1)
        xs = _ffn(xs, mods_s, norm_g4, wg, wu, wd, l, 1, 2, 1)
        outs[5].append(k_r.reshape(bs, 1, N_KV_HEADS, HEAD_DIM))
        outs[6].append(v_new.reshape(bs, 1, N_KV_HEADS, HEAD_DIM))
        outs[7].append(ki_r.reshape(bs, 1, IDX_DIM))
        outs[8].append(jnp.concatenate([state_pool[l][:, 1:], ps[:, None, OFF_UP:OFF_UP + POOL_WIDTH]], axis=1))
        outs[9].append(jnp.swapaxes(st_new, -1, -2))

    stacked = [jnp.stack(o) for o in outs]
    return (xp.reshape(bp, t_len, D_MODEL), xs.reshape(bs, 1, D_MODEL), *stacked)
```

```python
import functools

import numpy as np
import jax
import jax.numpy as jnp
from jax import lax
from jax.experimental import pallas as pl
from jax.experimental.pallas import tpu as pltpu

F32 = jnp.float32
BF16 = jnp.bfloat16
I32 = jnp.int32
I16 = jnp.int16

D_MODEL = 1024
DEPTH = 4
PAGE_SIZE = 128
N_SUB = 3
POOL_WINDOWS = (2, 4, 8, 16)
POOL_GROUP_DIM = 64
POOL_WIDTH = 256
POOL_STATE = 15
HGRN_HEADS = 4
HGRN_K = 128
HGRN_V = 64
F_FLOOR = 1e-30
N_HEADS = 8
N_KV_HEADS = 2
HEAD_DIM = 64
GROUP = N_HEADS // N_KV_HEADS
ATTN_WIDTH = 512
KV_WIDTH = 128
IDX_HEADS = 4
IDX_DIM = 64
TOPK_MAX = 256
ROPE_THETA = 10000.0
LOG2_E = 1.4426950408889634
ATTN_SCALE = HEAD_DIM ** -0.5
IDX_SCALE = (IDX_DIM * IDX_HEADS) ** -0.5
NEG_BIG = -1e30
D_FF = 2816
EPS = 1e-6
INT_MIN = np.int32(-2147483648)

LANES = 128
VMEM_LIMIT = 48 * 1024 * 1024

OFF_GL, OFF_HQ, OFF_HF, OFF_AQ = 0, 3072, 3584, 4096
OFF_UP, OFF_HI, OFF_HG, OFF_IQ = 4608, 4864, 5120, 5376
OFF_AK, OFF_AV, OFF_IKW = 5632, 5760, 5888
IN_PAD = 6144
HGRN_CHUNK = 128
HGRN_LEVELS = tuple(1 << i for i in range(HGRN_CHUNK.bit_length()))
HGRN_SMALL = tuple(h for h in HGRN_LEVELS if 1 < h < 8)


def _dot(a, b):
    return jnp.dot(a, b, preferred_element_type=F32)


def _dot_nt(a, b):
    return lax.dot_general(a, b, (((1,), (1,)), ((), ())), preferred_element_type=F32)


def _dot_tn(a, b):
    return lax.dot_general(a, b, (((0,), (0,)), ((), ())), preferred_element_type=F32)


def _split2(x):
    hi = x.astype(BF16)
    lo = (x - hi.astype(F32)).astype(BF16)
    return hi, lo


def _split3(x):
    hi = x.astype(BF16)
    r = x - hi.astype(F32)
    mid = r.astype(BF16)
    lo = (r - mid.astype(F32)).astype(BF16)
    return hi, mid, lo


def _silu(x):
    return x * jax.nn.sigmoid(x)


def _params(sem):
    return pltpu.CompilerParams(dimension_semantics=sem, vmem_limit_bytes=VMEM_LIMIT)


def _mod_spec(mods, l, m, tm, rows_per_seq):
    if mods.ndim == 4:
        return pl.BlockSpec((None, None, 1, D_MODEL), lambda i, *_: (l, (i * tm) // rows_per_seq, 0, m))
    return pl.BlockSpec((None, tm, D_MODEL), lambda i, *_: (l, i, m))


def _row_tile(n, mods, rows_per_seq, cap):
    return min(cap, rows_per_seq if mods.ndim == 4 else n)


def _prenorm(x, g, scale, shift):
    y = x * lax.rsqrt(jnp.mean(x * x, axis=-1, keepdims=True) + EPS) * g
    return y * (1.0 + scale) + shift


def _ada_kernel(c_ref, w_ref, b_ref, o_ref):
    c = c_ref[...]
    o_ref[...] = jnp.dot(_silu(c), w_ref[...], precision=lax.Precision.HIGHEST,
                         preferred_element_type=F32) + b_ref[...]


def _ada(c_all, ada_w, ada_b):
    rows = c_all.shape[0]
    width = ada_w.shape[-1]
    tn = 1024
    return pl.pallas_call(
        _ada_kernel,
        grid=(DEPTH, width // tn),
        in_specs=[pl.BlockSpec((rows, D_MODEL), lambda l, j: (0, 0)),
                  pl.BlockSpec((None, D_MODEL, tn), lambda l, j: (l, 0, j)),
                  pl.BlockSpec((None, 1, tn), lambda l, j: (l, 0, j))],
        out_specs=pl.BlockSpec((None, rows, tn), lambda l, j: (l, 0, j)),
        out_shape=jax.ShapeDtypeStruct((DEPTH, rows, width), F32),
        compiler_params=_params(("parallel", "parallel")),
        name="ada",
    )(c_all, ada_w, ada_b.reshape(DEPTH, 1, width))


def _ffn_kernel(x_ref, sh_ref, sc_ref, gt_ref, g_ref, wg_ref, wu_ref, wd_ref, o_ref, h_scr, acc_scr):
    j = pl.program_id(1)

    @pl.when(j == 0)
    def _():
        h_scr[...] = _prenorm(x_ref[...], g_ref[...], sc_ref[...], sh_ref[...]).astype(BF16)
        acc_scr[...] = jnp.zeros_like(acc_scr)

    h = h_scr[...]
    act = (_silu(_dot(h, wg_ref[...])) * _dot(h, wu_ref[...])).astype(BF16)
    acc_scr[...] += _dot(act, wd_ref[...])

    @pl.when(j == pl.num_programs(1) - 1)
    def _():
        o_ref[...] = x_ref[...] + 0.5 * gt_ref[...] * acc_scr[...]


def _ffn(x, mods, norm_g, wg, wu, wd, l, which, sub, rows_per_seq):
    n = x.shape[0]
    tm = _row_tile(n, mods, rows_per_seq, 1024)
    tf = 1408
    return pl.pallas_call(
        _ffn_kernel,
        grid=(n // tm, D_FF // tf),
        in_specs=[pl.BlockSpec((tm, D_MODEL), lambda i, j: (i, 0)),
                  _mod_spec(mods, l, 3 * sub, tm, rows_per_seq),
                  _mod_spec(mods, l, 3 * sub + 1, tm, rows_per_seq),
                  _mod_spec(mods, l, 3 * sub + 2, tm, rows_per_seq),
                  pl.BlockSpec((None, None, 1, D_MODEL), lambda i, j: (l, sub, 0, 0)),
                  pl.BlockSpec((None, None, D_MODEL, tf), lambda i, j: (l, which, 0, j)),
                  pl.BlockSpec((None, None, D_MODEL, tf), lambda i, j: (l, which, 0, j)),
                  pl.BlockSpec((None, None, tf, D_MODEL), lambda i, j: (l, which, j, 0))],
        out_specs=pl.BlockSpec((tm, D_MODEL), lambda i, j: (i, 0)),
        out_shape=jax.ShapeDtypeStruct((n, D_MODEL), F32),
        scratch_shapes=[pltpu.VMEM((tm, D_MODEL), BF16), pltpu.VMEM((tm, D_MODEL), F32)],
        compiler_params=_params(("parallel", "arbitrary")),
        name="ffn",
    )(x, mods, mods, mods, norm_g, wg, wu, wd)


def _inproj_kernel(x_ref, sh_ref, sc_ref, g_ref, w_ref, o_ref, h_scr):
    @pl.when(pl.program_id(1) == 0)
    def _():
        h_scr[...] = _prenorm(x_ref[...], g_ref[...], sc_ref[...], sh_ref[...]).astype(BF16)

    o_ref[...] = _dot(h_scr[...], w_ref[...])


def _inproj(x, mods, norm_g, w_in, l, rows_per_seq):
    n = x.shape[0]
    tm = _row_tile(n, mods, rows_per_seq, 1024)
    tn = 1536
    return pl.pallas_call(
        _inproj_kernel,
        grid=(n // tm, IN_PAD // tn),
        in_specs=[pl.BlockSpec((tm, D_MODEL), lambda i, j: (i, 0)),
                  _mod_spec(mods, l, 3, tm, rows_per_seq),
                  _mod_spec(mods, l, 4, tm, rows_per_seq),
                  pl.BlockSpec((None, None, 1, D_MODEL), lambda i, j: (l, 1, 0, 0)),
                  pl.BlockSpec((None, D_MODEL, tn), lambda i, j: (l, 0, j))],
        out_specs=pl.BlockSpec((tm, tn), lambda i, j: (i, j)),
        out_shape=jax.ShapeDtypeStruct((n, IN_PAD), F32),
        scratch_shapes=[pltpu.VMEM((tm, D_MODEL), BF16)],
        compiler_params=_params(("parallel", "arbitrary")),
        name="inproj",
    )(x, mods, mods, norm_g, w_in)


def _prep_kernel(aq_ref, ak_ref, av_ref, iq_ref, ikw_ref, cos_ref, sin_ref, qg_ref, kg_ref, bd_ref,
                 q_o, k_o, kb_o, vt_o, qi_o, ki_o, kib_o, wit_o, ktf_o, vtf_o, kitf_o):
    cos = cos_ref[...]
    sin = sin_ref[...]
    bd = bd_ref[...]
    lane = lax.broadcasted_iota(I32, cos.shape, 1)
    first_half = (lane & (HEAD_DIM // 2)) == 0

    def rope(x):
        partner = jnp.where(first_half, pltpu.roll(x, LANES - HEAD_DIM // 2, 1), pltpu.roll(x, HEAD_DIM // 2, 1))
        return x * cos + partner * sin

    def head_norm(x):
        hi, mid, lo = _split3(x * x)
        ms = (_dot(hi, bd) + _dot(mid, bd) + _dot(lo, bd)) * (1.0 / HEAD_DIM)
        return x * lax.rsqrt(ms + EPS)

    def put_heads(o_ref, first, y):
        for half in range(LANES // HEAD_DIM):
            o_ref[first + half] = y[:, half * HEAD_DIM:(half + 1) * HEAD_DIM].astype(BF16)

    qg = qg_ref[...]
    for c in range(ATTN_WIDTH // LANES):
        sl = slice(c * LANES, (c + 1) * LANES)
        put_heads(q_o, 2 * c, rope(head_norm(aq_ref[:, sl]) * qg) * (ATTN_SCALE * LOG2_E))
    k = rope(head_norm(ak_ref[...]) * kg_ref[...])
    k_o[...] = k
    ktf_o[...] = k.T
    put_heads(kb_o, 0, k)
    v_tf = av_ref[...].T
    vtf_o[...] = v_tf
    v_t = v_tf.astype(BF16)
    ones = jnp.ones((V_ROWS - HEAD_DIM, v_t.shape[1]), BF16)
    vt_o[...] = jnp.concatenate([v_t[:HEAD_DIM], ones, v_t[HEAD_DIM:], ones], axis=0)
    for c in range(IDX_HEADS * IDX_DIM // LANES):
        sl = slice(c * LANES, (c + 1) * LANES)
        put_heads(qi_o, 2 * c, rope(iq_ref[:, sl]))
    ikw = ikw_ref[...]
    ki_wide = rope(head_norm(ikw))
    ki = ki_wide[:, :IDX_DIM]
    ki_o[...] = ki
    kitf_o[...] = ki_wide.T[:IDX_DIM, :]
    kib_o[...] = ki.astype(BF16)
    wit_o[...] = ikw.T[IDX_DIM:IDX_DIM + 8, :]


def _prep(p, cos, sin, qg, kg, bd, tm, table_blocks):
    n = p.shape[0]

    def col(off, width):
        return pl.BlockSpec((tm, width), lambda i: (i, off // width))

    tab = pl.BlockSpec((tm, LANES), lambda i: (i % table_blocks, 0))
    vec = pl.BlockSpec((1, LANES), lambda i: (0, 0))

    def row(width):
        return pl.BlockSpec((tm, width), lambda i: (i, 0))

    def heads(h):
        return pl.BlockSpec((h, tm, HEAD_DIM), lambda i: (0, i, 0))

    n_seq = n // (tm * table_blocks)
    seq_len = tm * table_blocks

    def feature_major(width):
        return pl.BlockSpec((None, width, tm), lambda i: (i // table_blocks, 0, i % table_blocks))

    return pl.pallas_call(
        _prep_kernel,
        grid=(n // tm,),
        in_specs=[col(OFF_AQ, ATTN_WIDTH), col(OFF_AK, KV_WIDTH), col(OFF_AV, KV_WIDTH),
                  col(OFF_IQ, IDX_HEADS * IDX_DIM), col(OFF_IKW, LANES), tab, tab, vec, vec,
                  pl.BlockSpec((LANES, LANES), lambda i: (0, 0))],
        out_specs=[heads(N_HEADS), row(KV_WIDTH), heads(N_KV_HEADS),
                   pl.BlockSpec((None, N_KV_HEADS * V_ROWS, tm), lambda i: (i, 0, 0)),
                   heads(IDX_HEADS), row(IDX_DIM), row(IDX_DIM),
                   pl.BlockSpec((8, tm), lambda i: (0, i)),
                   feature_major(KV_WIDTH), feature_major(KV_WIDTH), feature_major(IDX_DIM)],
        out_shape=[jax.ShapeDtypeStruct((N_HEADS, n, HEAD_DIM), BF16),
                   jax.ShapeDtypeStruct((n, KV_WIDTH), F32),
                   jax.ShapeDtypeStruct((N_KV_HEADS, n, HEAD_DIM), BF16),
                   jax.ShapeDtypeStruct((n // tm, N_KV_HEADS * V_ROWS, tm), BF16),
                   jax.ShapeDtypeStruct((IDX_HEADS, n, IDX_DIM), BF16),
                   jax.ShapeDtypeStruct((n, IDX_DIM), F32),
                   jax.ShapeDtypeStruct((n, IDX_DIM), BF16),
                   jax.ShapeDtypeStruct((8, n), F32),
                   jax.ShapeDtypeStruct((n_seq, KV_WIDTH, seq_len), F32),
                   jax.ShapeDtypeStruct((n_seq, KV_WIDTH, seq_len), F32),
                   jax.ShapeDtypeStruct((n_seq, IDX_DIM, seq_len), F32)],
        compiler_params=_params(("parallel",)),
        name="prep",
    )(p, p, p, p, p, cos, sin, qg, kg, bd)


def _rope_tables(pos):
    half = HEAD_DIM // 2
    inv = ROPE_THETA ** (-jnp.arange(half, dtype=F32) / half)
    ang = pos.astype(F32)[:, None] * inv[None, :]
    c, s = jnp.cos(ang), jnp.sin(ang)
    return jnp.concatenate([c, c, c, c], axis=1), jnp.concatenate([-s, s, -s, s], axis=1)


def _pool_select(sums, lane):
    out = sums[-1]
    for gi in range(len(POOL_WINDOWS) - 2, -1, -1):
        out = jnp.where(lane < (gi + 1) * POOL_GROUP_DIM, sums[gi], out)
    return out


def _pool_kernel(u_ref, prev_ref, w_ref, s_ref, o_ref, carry_scr, *, tt, p0):
    t = pl.program_id(1)

    @pl.when(t == 0)
    def _():
        carry_scr[...] = prev_ref[...]

    u = u_ref[...]
    e = jnp.concatenate([carry_scr[...], u], axis=0)
    carry_scr[...] = u[tt - 16:, :]
    s2 = e[1:] + e[:-1]
    s4 = s2[2:] + s2[:-2]
    s8 = s4[4:] + s4[:-4]
    s16 = s8[8:] + s8[:-8]
    sums = (s2[15:15 + tt], s4[13:13 + tt], s8[9:9 + tt], s16[1:1 + tt])
    n_avail = (p0 + 1 + t * tt + lax.broadcasted_iota(I32, (tt, 1), 0)).astype(F32)
    lane = lax.broadcasted_iota(I32, (tt, POOL_WIDTH), 1)
    means = [sums[gi] / jnp.minimum(n_avail, float(w)) for gi, w in enumerate(POOL_WINDOWS)]
    pooled = _pool_select(means, lane) - u
    o_ref[...] = _dot(pooled.astype(BF16), w_ref[...]) * s_ref[...]


def _pool_prompt(p, prev16, wbd, scale, l, b, t_len):
    tt = min(512, t_len)
    nt = t_len // tt
    return pl.pallas_call(
        functools.partial(_pool_kernel, tt=tt, p0=0),
        grid=(b, nt),
        in_specs=[pl.BlockSpec((tt, POOL_WIDTH), lambda bi, ti: (bi * nt + ti, OFF_UP // POOL_WIDTH)),
                  pl.BlockSpec((None, 16, POOL_WIDTH), lambda bi, ti: (bi, 0, 0)),
                  pl.BlockSpec((None, POOL_WIDTH, POOL_WIDTH), lambda bi, ti: (l, 0, 0)),
                  pl.BlockSpec((None, 1, POOL_WIDTH), lambda bi, ti: (l, 0, 0))],
        out_specs=pl.BlockSpec((tt, POOL_WIDTH), lambda bi, ti: (bi * nt + ti, 0)),
        out_shape=jax.ShapeDtypeStruct((b * t_len, POOL_WIDTH), F32),
        scratch_shapes=[pltpu.VMEM((16, POOL_WIDTH), F32)],
        compiler_params=_params(("parallel", "arbitrary")),
        name="pool",
    )(p, prev16, wbd, scale)


def _pool_dec_kernel(u_ref, st_ref, w_ref, s_ref, o_ref, *, p0):
    u = u_ref[...]
    lane = lax.broadcasted_iota(I32, u.shape, 1)
    means = []
    acc = u
    d = 1
    for w in POOL_WINDOWS:
        while d < w:
            acc = acc + st_ref[POOL_STATE - d]
            d += 1
        means.append(acc / float(min(p0 + 1, w)))
    pooled = _pool_select(means, lane) - u
    o_ref[...] = _dot(pooled.astype(BF16), w_ref[...]) * s_ref[...]


def _pool_dec(p, st_t, wbd, scale, l, p0):
    n = p.shape[0]
    return pl.pallas_call(
        functools.partial(_pool_dec_kernel, p0=p0),
        grid=(1,),
        in_specs=[pl.BlockSpec((n, POOL_WIDTH), lambda i: (0, OFF_UP // POOL_WIDTH)),
                  pl.BlockSpec((POOL_STATE, n, POOL_WIDTH), lambda i: (0, 0, 0)),
                  pl.BlockSpec((None, POOL_WIDTH, POOL_WIDTH), lambda i: (l, 0, 0)),
                  pl.BlockSpec((None, 1, POOL_WIDTH), lambda i: (l, 0, 0))],
        out_specs=pl.BlockSpec((n, POOL_WIDTH), lambda i: (0, 0)),
        out_shape=jax.ShapeDtypeStruct((n, POOL_WIDTH), F32),
        compiler_params=_params(("arbitrary",)),
        name="pool_dec",
    )(p, st_t, wbd, scale)


def _hgrn_consts(c):
    r = np.arange(c)[:, None]
    s = np.arange(c)[None, :]
    mats = [(s <= r) & (s // h == r // h) for h in HGRN_SMALL]
    mats += [(s > r) & (s // h == r // h) for h in HGRN_SMALL]
    mats.append(s <= r)
    return jnp.asarray(np.concatenate(mats, axis=0).astype(np.float32), BF16)


def _hgrn_kernel(hq_ref, hf_ref, hi_ref, hg_ref, lb_ref, on_ref, ms_ref, yb_ref, st_ref, s_scr, *, c):
    step = pl.program_id(1)
    nl = len(HGRN_LEVELS)

    @pl.when(step == 0)
    def _():
        s_scr[...] = jnp.zeros_like(s_scr)

    ti = lax.broadcasted_iota(I32, (c, c), 0)
    si = lax.broadcasted_iota(I32, (c, c), 1)
    pairs = []
    for h in HGRN_LEVELS[:-1]:
        sh = h.bit_length() - 1
        tb = jnp.right_shift(ti, sh)
        pairs.append(((tb & 1) == 1) & (jnp.right_shift(si, sh) == tb - 1))

    for hd in range(HGRN_HEADS):
        ks = slice(hd * HGRN_K, (hd + 1) * HGRN_K)
        vs = slice(hd * HGRN_V, (hd + 1) * HGRN_V)
        q = hq_ref[:, ks]
        lb = lb_ref[:, ks]
        f = lb + (1.0 - lb) * jax.nn.sigmoid(hf_ref[:, ks])
        g = jnp.log(jnp.maximum(f, F_FLOOR))
        kk = 1.0 - f
        kkb = kk.astype(BF16)
        ivb = hi_ref[:, vs].astype(BF16)
        g_hi, g_lo = _split2(g)
        r = _dot(ms_ref[...], jnp.concatenate([g_hi, g_lo], axis=1))
        r = r[:, :HGRN_K] + r[:, HGRN_K:]
        n_small = len(HGRN_SMALL)
        low = [g] + [r[k * c:(k + 1) * c] for k in range(n_small)]
        up = [None] + [r[(n_small + k) * c:(n_small + k + 1) * c] for k in range(n_small)]
        cum = r[2 * n_small * c:]
        for h in HGRN_LEVELS[1 + n_small:]:
            cum3 = cum.reshape(c // h, h, HGRN_K)
            ends = cum3[:, h - 1:h, :]
            if h == c:
                low.append(cum)
            else:
                starts = jnp.concatenate([jnp.zeros((1, 1, HGRN_K), F32), ends[:-1]], axis=0)
                low.append((cum3 - starts).reshape(c, HGRN_K))
            up.append((ends - cum3).reshape(c, HGRN_K))

        att = jnp.where(ti == si, _dot_nt(q.astype(BF16), kkb), 0.0)
        for k in range(nl - 1):
            qe = (q * jnp.exp(low[k])).astype(BF16)
            ke = kkb if k == 0 else (kk * jnp.exp(up[k])).astype(BF16)
            att = jnp.where(pairs[k], _dot_nt(qe, ke), att)
        qe = (q * jnp.exp(low[-1])).astype(BF16)
        ke = (kk * jnp.exp(up[-1])).astype(BF16)
        s_t = s_scr[hd]
        o = _dot(att.astype(BF16), ivb) + _dot_nt(qe, s_t.astype(BF16))
        s_scr[hd] = s_t * jnp.exp(low[-1][c - 1:c, :]) + _dot_tn(ivb, ke)

        on = o * lax.rsqrt(jnp.mean(o * o, axis=-1, keepdims=True) + EPS) * on_ref[...]
        yb_ref[:, vs] = on * _silu(hg_ref[:, vs])

    @pl.when(step == pl.num_programs(1) - 1)
    def _():
        st_ref[...] = s_scr[...]


def _hgrn_prompt(p, lb, onorm, mstack, b, t_len):
    c = HGRN_CHUNK
    nc = t_len // c
    kw = HGRN_HEADS * HGRN_K
    vw = HGRN_HEADS * HGRN_V

    def col(off, width):
        return pl.BlockSpec((c, width), lambda bi, ci: (bi * nc + ci, off // width))

    return pl.pallas_call(
        functools.partial(_hgrn_kernel, c=c),
        grid=(b, nc),
        in_specs=[col(OFF_HQ, kw), col(OFF_HF, kw), col(OFF_HI, vw), col(OFF_HG, vw),
                  pl.BlockSpec((1, kw), lambda bi, ci: (0, 0)),
                  pl.BlockSpec((1, HGRN_V), lambda bi, ci: (0, 0)),
                  pl.BlockSpec(mstack.shape, lambda bi, ci: (0, 0))],
        out_specs=[pl.BlockSpec((c, vw), lambda bi, ci: (bi * nc + ci, 0)),
                   pl.BlockSpec((None, HGRN_HEADS, HGRN_V, HGRN_K), lambda bi, ci: (bi, 0, 0, 0))],
        out_shape=[jax.ShapeDtypeStruct((b * t_len, vw), F32),
                   jax.ShapeDtypeStruct((b, HGRN_HEADS, HGRN_V, HGRN_K), F32)],
        scratch_shapes=[pltpu.VMEM((HGRN_HEADS, HGRN_V, HGRN_K), F32)],
        compiler_params=_params(("parallel", "arbitrary")),
        name="hgrn",
    )(p, p, p, p, lb, onorm, mstack)


def _hgrn_dec_kernel(q_ref, z_ref, lb_ref, i_ref, hg_ref, on_ref, s_ref, yb_ref, so_ref):
    lb = lb_ref[...]
    f = lb + (1.0 - lb) * jax.nn.sigmoid(z_ref[...])
    a = jnp.exp(jnp.log(jnp.maximum(f, F_FLOOR)))
    s_new = a * s_ref[...] + i_ref[...] * (1.0 - f)
    so_ref[...] = s_new
    o = jnp.sum(q_ref[...] * s_new, axis=-1, keepdims=True)
    on = o * lax.rsqrt(jnp.mean(o * o, axis=-2, keepdims=True) + EPS) * on_ref[...]
    yb_ref[...] = on * _silu(hg_ref[...])


def _hgrn_dec(q_r, z_r, lb_r, i_c, hg_c, onorm_c, state_t, l):
    n = state_t.shape[1]
    nb = 8
    row = pl.BlockSpec((nb, HGRN_HEADS, 1, HGRN_K), lambda i: (i, 0, 0, 0))
    col = pl.BlockSpec((nb, HGRN_HEADS, HGRN_V, 1), lambda i: (i, 0, 0, 0))
    return pl.pallas_call(
        _hgrn_dec_kernel,
        grid=(n // nb,),
        in_specs=[row, row, pl.BlockSpec((HGRN_HEADS, 1, HGRN_K), lambda i: (0, 0, 0)), col, col,
                  pl.BlockSpec((HGRN_V, 1), lambda i: (0, 0)),
                  pl.BlockSpec((None, nb, HGRN_HEADS, HGRN_V, HGRN_K), lambda i: (l, i, 0, 0, 0))],
        out_specs=[col, pl.BlockSpec((nb, HGRN_HEADS, HGRN_V, HGRN_K), lambda i: (i, 0, 0, 0))],
        out_shape=[jax.ShapeDtypeStruct((n, HGRN_HEADS, HGRN_V, 1), F32),
                   jax.ShapeDtypeStruct(state_t.shape[1:], F32)],
        compiler_params=_params(("parallel",)),
        name="hgrn_dec",
    )(q_r, z_r, lb_r, i_c, hg_c, onorm_c, state_t)


def _sort_key(s):
    bits = pltpu.bitcast(s, I32)
    return jnp.where(bits < 0, INT_MIN - bits, bits)


def _lane_fold(x):
    part = x[:, :LANES]
    for cidx in range(1, x.shape[1] // LANES):
        part = part + x[:, cidx * LANES:(cidx + 1) * LANES]
    return part


def _select_bias(key, thr, idx, j_lim):
    return jnp.where(key > thr, 0.0, jnp.where(key == thr, jnp.where(idx < j_lim, 0.0, NEG_BIG), NEG_BIG))


def _bisect_bits(count_ge, thr, first_bit, n_bits, k):
    def bit_body(it, thr):
        cand = thr + lax.shift_left(jnp.int32(1), first_bit - it)
        return jnp.where(count_ge(cand) >= k, cand, thr)

    return lax.fori_loop(0, n_bits, bit_body, thr)


def _topk_threshold(count, stat_shape, n_sel, idx_bits, idx_sentinel, kth_largest=None):
    k = float(n_sel)
    if kth_largest is None:
        thr = _bisect_bits(lambda cand: count(lambda key, idx: jnp.where(key >= cand, 1.0, 0.0)),
                           jnp.full(stat_shape, INT_MIN, I32), 31, 32, k)
    else:
        thr = kth_largest(k)
    n_ge = count(lambda key, idx: jnp.where(key >= thr, 1.0, 0.0))
    n_gt = count(lambda key, idx: jnp.where(key > thr, 1.0, 0.0))
    need = k - n_gt
    live = thr > INT_MIN
    tie = jnp.where(live, jnp.where(n_ge > k, 1.0, 0.0), 0.0)
    j_default = jnp.where(live, jnp.int32(idx_sentinel), jnp.int32(0))

    def tie_search():
        def idx_body(it, j_lim):
            cand = j_lim + lax.shift_left(jnp.int32(1), idx_bits - 1 - it)
            cnt = count(lambda key, idx: jnp.where(key == thr, jnp.where(idx < cand, 1.0, 0.0), 0.0))
            return jnp.where(cnt <= need, cand, j_lim)

        j_lim = lax.fori_loop(0, idx_bits, idx_body, jnp.zeros(stat_shape, I32))
        return jnp.where(tie > 0.0, j_lim, j_default)

    j_lim = lax.cond(jnp.max(tie) > 0.0, tie_search, lambda: j_default)
    return thr, j_lim


FOLD_ROWS = 64
V_ROWS = HEAD_DIM + 16


def _fold(x, op):
    parts = [x[r * FOLD_ROWS:(r + 1) * FOLD_ROWS] for r in range(x.shape[0] // FOLD_ROWS)]
    while len(parts) > 1:
        parts = [op(parts[i], parts[i + 1]) if i + 1 < len(parts) else parts[i]
                 for i in range(0, len(parts), 2)]
    return parts[0]


def _dsa_kernel(q_ref, qi_ref, wit_ref, k_ref, vt_ref, ki_ref, o_ref, keys_scr, k16_scr, lg_scr, p_scr, m_scr, acc_scr,
                *, tq, tk, n_sel, t_len):
    i = pl.program_id(1)
    nk = ((i + 1) * tq + tk - 1) // tk
    krow = lax.broadcasted_iota(I32, (tk, tq), 0)
    qpos = i * tq + lax.broadcasted_iota(I32, (tk, tq), 1)
    wit = wit_ref[...] * IDX_SCALE
    qi_all = qi_ref[...].reshape(IDX_HEADS * tq, IDX_DIM)

    def rows(j):
        return pl.ds(pl.multiple_of(j * tk, tk), tk)

    def score_tile(j, causal):
        rel = _dot_nt(ki_ref[rows(j), :], qi_all)
        s = wit[0:1, :] * jnp.maximum(rel[:, :tq], 0.0)
        for h in range(1, IDX_HEADS):
            s = s + wit[h:h + 1, :] * jnp.maximum(rel[:, h * tq:(h + 1) * tq], 0.0)
        key = jnp.where(s > 0.5 * NEG_BIG, _sort_key(s), INT_MIN)
        if causal:
            key = jnp.where(j * tk + krow <= qpos, key, INT_MIN)
        keys_scr[j] = key
        k16_scr[j] = jnp.right_shift(key, 16).astype(I16)

    def score_body(j, carry):
        score_tile(j, False)
        return carry

    lax.fori_loop(0, nk - 1, score_body, 0)
    score_tile(nk - 1, True)

    def count(pred):
        def body(j, acc):
            return acc + _fold(pred(keys_scr[j], j * tk + krow), jnp.add)

        acc = lax.fori_loop(0, nk, body, jnp.zeros((FOLD_ROWS, tq), F32))
        return jnp.sum(acc, axis=0, keepdims=True)

    one16, zero16 = jnp.int16(1), jnp.int16(0)

    def count16(pred):
        def body(j, acc):
            return acc + _fold(pred(k16_scr[j]), jnp.add)

        acc = lax.fori_loop(0, nk, body, jnp.zeros((FOLD_ROWS, tq), I16))
        return jnp.sum(acc.astype(F32), axis=0, keepdims=True)

    def high16(x):
        return jnp.right_shift(x, 16).astype(I16)

    def low16(x):
        return ((x & 0xFFFF) - 32768).astype(I16)

    def kth_largest(k):
        def gmax_body(j, g):
            t = k16_scr[j]
            for r0 in range(0, tk, n_sel):
                part = t[r0:r0 + n_sel]
                g = jnp.where(part > g, part, g)
            return g

        g = lax.fori_loop(0, nk, gmax_body, jnp.full((n_sel, tq), -32768, I16)).astype(F32)
        lo = jnp.min(g, axis=0, keepdims=True).astype(I32)
        hi = jnp.max(g, axis=0, keepdims=True).astype(I32)
        shared = lax.clz((lo ^ hi) & 0xFFFF) - 16
        skip = jnp.min(shared.astype(F32)).astype(I32)
        keep = ~lax.shift_right_logical(jnp.full((1, tq), -1, I32), jnp.full((1, tq), skip, I32))
        thr = ((jnp.left_shift(hi, 16) ^ INT_MIN) & keep) ^ INT_MIN
        thr = _bisect_bits(lambda cand: count16(lambda t: jnp.where(t >= high16(cand), one16, zero16)),
                           thr, 31 - skip, 16 - skip, k)
        thr_hi = high16(thr)
        n_above = count16(lambda t: jnp.where(t > thr_hi, one16, zero16))

        def low_body(j, carry):
            key = keys_scr[j]
            k16_scr[j] = jnp.where(high16(key) == thr_hi, low16(key), jnp.int16(-32768))
            return carry

        lax.fori_loop(0, nk, low_body, 0)
        return _bisect_bits(
            lambda cand: n_above + count16(lambda t: jnp.where(t >= low16(cand), one16, zero16)),
            thr, 15, 16, k)

    thr, j_lim = _topk_threshold(count, (1, tq), n_sel, int(t_len).bit_length(), t_len, kth_largest)

    m_scr[...] = jnp.full(m_scr.shape, 0.5 * NEG_BIG, F32)
    acc_scr[...] = jnp.zeros_like(acc_scr)

    def attend_tile(j, carry):
        bias = _select_bias(keys_scr[j], thr, j * tk + krow, j_lim)
        bias = jnp.concatenate([bias] * GROUP, axis=1)
        v_t = vt_ref[j]
        heads = range(N_KV_HEADS)
        for n in heads:
            qg = q_ref[n * GROUP:(n + 1) * GROUP].reshape(GROUP * tq, HEAD_DIM)
            lg_scr[n] = _dot_nt(k_ref[n, rows(j), :], qg) + bias
        m_old = [m_scr[n] for n in heads]
        m_new = [jnp.maximum(m_old[n], jnp.max(_fold(lg_scr[n], jnp.maximum), axis=0, keepdims=True))
                 for n in heads]
        for n in heads:
            p_scr[n] = jnp.exp2(lg_scr[n] - m_new[n]).astype(BF16)
        for n in heads:
            acc_scr[n] = (jnp.exp2(m_old[n] - m_new[n]) * acc_scr[n]
                          + _dot(v_t[n * V_ROWS:(n + 1) * V_ROWS, :], p_scr[n]))
            m_scr[n] = m_new[n]
        return carry

    lax.fori_loop(0, nk, attend_tile, 0)
    outs = []
    for n in range(N_KV_HEADS):
        acc = acc_scr[n]
        o_n = acc[:HEAD_DIM] / acc[HEAD_DIM:HEAD_DIM + 1]
        outs += [o_n[:, g * tq:(g + 1) * tq] for g in range(GROUP)]
    o_ref[...] = jnp.concatenate(outs, axis=0).T


def _dsa_prompt(q_h, qi_h, wit, k_h, v_t, ki_b, b, t_len, n_sel, tk):
    tq = min(256, t_len)
    nq = t_len // tq
    nt = t_len // tk
    assert tk % n_sel == 0 and tk % tq == 0, (tk, n_sel, tq)

    def heads(h, rows, imap):
        return pl.BlockSpec((h, rows, HEAD_DIM), imap)

    return pl.pallas_call(
        functools.partial(_dsa_kernel, tq=tq, tk=tk, n_sel=n_sel, t_len=t_len),
        grid=(b, nq),
        in_specs=[heads(N_HEADS, tq, lambda bi, qi: (0, bi * nq + qi, 0)),
                  heads(IDX_HEADS, tq, lambda bi, qi: (0, bi * nq + qi, 0)),
                  pl.BlockSpec((8, tq), lambda bi, qi: (0, bi * nq + qi)),
                  heads(N_KV_HEADS, t_len, lambda bi, qi: (0, bi, 0)),
                  pl.BlockSpec((nt, N_KV_HEADS * V_ROWS, tk), lambda bi, qi: (bi, 0, 0)),
                  pl.BlockSpec((t_len, IDX_DIM), lambda bi, qi: (bi, 0))],
        out_specs=pl.BlockSpec((tq, ATTN_WIDTH), lambda bi, qi: (bi * nq + qi, 0)),
        out_shape=jax.ShapeDtypeStruct((b * t_len, ATTN_WIDTH), F32),
        scratch_shapes=[pltpu.VMEM((nt, tk, tq), I32),
                        pltpu.VMEM((nt, tk, tq), I16),
                        pltpu.VMEM((N_KV_HEADS, tk, GROUP * tq), F32),
                        pltpu.VMEM((N_KV_HEADS, tk, GROUP * tq), BF16),
                        pltpu.VMEM((N_KV_HEADS, 1, GROUP * tq), F32),
                        pltpu.VMEM((N_KV_HEADS, V_ROWS, GROUP * tq), F32)],
        compiler_params=_params(("parallel", "arbitrary")),
        name="dsa",
    )(q_h, qi_h, wit, k_h, v_t, ki_b)


DEC_SEQS_PER_STEP = 4


def _dsa_dec_score_kernel(pt_ref, qi_ref, wi_ref, kin_ref, *rest, n_pages, n_seq):
    page_refs = rest[:n_seq * n_pages]
    o_ref = rest[n_seq * n_pages]
    lane = lax.broadcasted_iota(I32, (1, PAGE_SIZE), 1)
    for s in range(n_seq):
        qi = qi_ref[s]
        wi = wi_ref[s]

        def finish(rel):
            sc = jnp.sum(wi * jnp.maximum(rel, 0.0), axis=0, keepdims=True) * IDX_SCALE
            return jnp.where(sc > 0.5 * NEG_BIG, _sort_key(sc), INT_MIN)

        for pg in range(n_pages):
            page = page_refs[s * n_pages + pg][...].astype(BF16)
            o_ref[s, :, pg * PAGE_SIZE:(pg + 1) * PAGE_SIZE] = finish(_dot(qi, page))
        kin = kin_ref[s].astype(BF16).astype(F32)
        key_new = finish(jnp.sum(qi.astype(F32) * kin, axis=1, keepdims=True))
        o_ref[s, :, n_pages * PAGE_SIZE:] = jnp.where(lane == 0, key_new, INT_MIN)


def _page_specs(block, l, n_seq, n_pages):
    def spec(s, pg):
        return pl.BlockSpec(block, lambda bi, pt, *_: (l, pt[bi * n_seq + s, pg], 0, 0))

    return [spec(s, pg) for s in range(n_seq) for pg in range(n_pages)]


def _dsa_dec_scores(page_table, qi8, wi8, ki_new, cache_ki_t, l):
    n, n_pages = page_table.shape
    ns = min(DEC_SEQS_PER_STEP, n)
    width = (n_pages + 1) * PAGE_SIZE
    page_specs = _page_specs((None, None, IDX_DIM, PAGE_SIZE), l, ns, n_pages)
    return pl.pallas_call(
        functools.partial(_dsa_dec_score_kernel, n_pages=n_pages, n_seq=ns),
        grid_spec=pltpu.PrefetchScalarGridSpec(
            num_scalar_prefetch=1,
            grid=(n // ns,),
            in_specs=[pl.BlockSpec((ns, 8, IDX_DIM), lambda bi, pt: (bi, 0, 0)),
                      pl.BlockSpec((ns, 8, 1), lambda bi, pt: (bi, 0, 0)),
                      pl.BlockSpec((ns, 1, IDX_DIM), lambda bi, pt: (bi, 0, 0))] + page_specs,
            out_specs=pl.BlockSpec((ns, 1, width), lambda bi, pt: (bi, 0, 0))),
        out_shape=jax.ShapeDtypeStruct((n, 1, width), I32),
        compiler_params=_params(("arbitrary",)),
        name="dsa_dec_scores",
    )(page_table, qi8, wi8, ki_new, *([cache_ki_t] * (ns * n_pages)))


def _dsa_dec_thr_kernel(keys_ref, thr_ref, j_ref, *, n_sel):
    n, width = keys_ref.shape
    lane = lax.broadcasted_iota(I32, (n, width), 1)

    def count(pred):
        return jnp.sum(_lane_fold(pred(keys_ref[...], lane)), axis=1, keepdims=True)

    thr, j_lim = _topk_threshold(count, (n, 1), n_sel, int(width).bit_length(), width)
    thr_ref[...] = thr
    j_ref[...] = j_lim


def _dsa_dec_thr(keys2d, n_sel):
    n, width = keys2d.shape
    return pl.pallas_call(
        functools.partial(_dsa_dec_thr_kernel, n_sel=n_sel),
        grid=(1,),
        in_specs=[pl.BlockSpec((n, width), lambda i: (0, 0))],
        out_specs=[pl.BlockSpec((n, 1), lambda i: (0, 0)), pl.BlockSpec((n, 1), lambda i: (0, 0))],
        out_shape=[jax.ShapeDtypeStruct((n, 1), I32), jax.ShapeDtypeStruct((n, 1), I32)],
        compiler_params=_params(("arbitrary",)),
        name="dsa_dec_thr",
    )(keys2d)


def _dsa_dec_attn_kernel(pt_ref, thr_ref, j_ref, q_ref, keys_ref, kn_ref, vn_ref, *rest, n_pages, n_seq):
    k_pages = rest[:n_seq * n_pages]
    v_pages = rest[n_seq * n_pages:2 * n_seq * n_pages]
    o_ref, k_scr, v_scr = rest[2 * n_seq * n_pages:]
    n_past = n_pages * PAGE_SIZE
    idx = lax.broadcasted_iota(I32, (1, keys_ref.shape[-1]), 1)
    for s in range(n_seq):
        b = pl.program_id(0) * n_seq + s
        for pg in range(n_pages):
            sl = slice(pg * PAGE_SIZE, (pg + 1) * PAGE_SIZE)
            k_scr[s, :, sl] = k_pages[s * n_pages + pg][...].astype(BF16)
            v_scr[s, :, sl] = v_pages[s * n_pages + pg][...].astype(BF16)
        bias_all = _select_bias(keys_ref[s], thr_ref[b], idx, j_ref[b])
        bias = bias_all[:, :n_past]
        bias_new = bias_all[:, n_past:n_past + 1]
        kn = kn_ref[s].astype(BF16).astype(F32)
        vn = vn_ref[s].astype(BF16).astype(F32)
        for n in range(N_KV_HEADS):
            fs = slice(n * HEAD_DIM, (n + 1) * HEAD_DIM)
            q = q_ref[s, n]
            lg = _dot(q, k_scr[s, fs, :]) + bias
            lg_new = jnp.sum(q.astype(F32) * kn[:, fs], axis=1, keepdims=True) + bias_new
            m = jnp.maximum(jnp.max(lg, axis=1, keepdims=True), lg_new)
            p = jnp.exp2(lg - m)
            p_new = jnp.exp2(lg_new - m)
            den = jnp.sum(p, axis=1, keepdims=True) + p_new
            o_ref[s, n] = (_dot_nt(p.astype(BF16), v_scr[s, fs, :]) + p_new * vn[:, fs]) / den


def _dsa_dec_attn(page_table, thr, j_lim, q8, keys, k_new, v_new, cache_k_t, cache_v_t, l):
    n, n_pages = page_table.shape
    ns = min(DEC_SEQS_PER_STEP, n)
    width = (n_pages + 1) * PAGE_SIZE
    page_specs = _page_specs((None, None, KV_WIDTH, PAGE_SIZE), l, ns, n_pages)
    qspec = pl.BlockSpec((ns, N_KV_HEADS, 8, HEAD_DIM), lambda bi, pt, th, jl: (bi, 0, 0, 0))

    def per_seq(width_):
        return pl.BlockSpec((ns, 1, width_), lambda bi, pt, th, jl: (bi, 0, 0))

    return pl.pallas_call(
        functools.partial(_dsa_dec_attn_kernel, n_pages=n_pages, n_seq=ns),
        grid_spec=pltpu.PrefetchScalarGridSpec(
            num_scalar_prefetch=3,
            grid=(n // ns,),
            in_specs=[qspec, per_seq(width), per_seq(KV_WIDTH), per_seq(KV_WIDTH)] + page_specs + page_specs,
            out_specs=qspec,
            scratch_shapes=[pltpu.VMEM((ns, KV_WIDTH, n_pages * PAGE_SIZE), BF16),
                            pltpu.VMEM((ns, KV_WIDTH, n_pages * PAGE_SIZE), BF16)]),
        out_shape=jax.ShapeDtypeStruct((n, N_KV_HEADS, 8, HEAD_DIM), F32),
        compiler_params=_params(("arbitrary",)),
        name="dsa_dec_attn",
    )(page_table, thr, j_lim, q8, keys, k_new, v_new,
      *([cache_k_t] * (ns * n_pages)), *([cache_v_t] * (ns * n_pages)))


def _merge_kernel(x_ref, g0_ref, g1_ref, g2_ref, ya_ref, yb_ref, yc_ref, gt_ref,
                  wa_ref, wb_ref, wc_ref, wo_ref, o_ref):
    merged = (jax.nn.sigmoid(g0_ref[...]) * _dot(ya_ref[...].astype(BF16), wa_ref[...])
              + jax.nn.sigmoid(g1_ref[...]) * _dot(yb_ref[...].astype(BF16), wb_ref[...])
              + jax.nn.sigmoid(g2_ref[...]) * _dot(yc_ref[...].astype(BF16), wc_ref[...]))
    o_ref[...] = x_ref[...] + gt_ref[...] * _dot(merged.astype(BF16), wo_ref[...])


def _merge(x, p, ya, yb, yc, mods, w_ba, w_bb, w_bc, w_out, l, rows_per_seq):
    n = x.shape[0]
    tm = _row_tile(n, mods, rows_per_seq, 512)

    def wspec(rows):
        return pl.BlockSpec((None, rows, D_MODEL), lambda i: (l, 0, 0))

    def row(width):
        return pl.BlockSpec((tm, width), lambda i: (i, 0))

    def gate(bidx):
        return pl.BlockSpec((tm, D_MODEL), lambda i: (i, OFF_GL // D_MODEL + bidx))

    return pl.pallas_call(
        _merge_kernel,
        grid=(n // tm,),
        in_specs=[row(D_MODEL), gate(0), gate(1), gate(2), row(POOL_WIDTH), row(HGRN_HEADS * HGRN_V),
                  row(ATTN_WIDTH), _mod_spec(mods, l, 5, tm, rows_per_seq),
                  wspec(POOL_WIDTH), wspec(HGRN_HEADS * HGRN_V), wspec(ATTN_WIDTH), wspec(D_MODEL)],
        out_specs=row(D_MODEL),
        out_shape=jax.ShapeDtypeStruct((n, D_MODEL), F32),
        compiler_params=_params(("parallel",)),
        name="merge",
    )(x, p, p, p, ya, yb, yc, mods, w_ba, w_bb, w_bc, w_out)


def _permute_w_in(w_in):
    segs = [(2884, 5956), (256, 768), (768, 1280), (1792, 2304), (0, 256), (1280, 1536), (1536, 1792),
            (2560, 2816), (2304, 2432), (2432, 2560), (2816, 2884)]
    parts = [w_in[..., a:b] for a, b in segs]
    used = sum(b - a for a, b in segs)
    parts.append(jnp.zeros(w_in.shape[:-1] + (IN_PAD - used,), w_in.dtype))
    return jnp.concatenate(parts, axis=-1).astype(BF16)


def _lower_bounds(p):
    sm = jax.nn.softmax(p.astype(F32), axis=0)
    return jnp.cumsum(sm, axis=0) - sm[0:1]


def kernel(x_prompt, x_sample, c_prompt, c_sample, cache_k, cache_v, cache_ki, page_table, state_pool, state_hgrn, ada_w, ada_b, norm_g, ffn_wg, ffn_wu, ffn_wd, w_in, pool_w, pool_scale, hgrn_lb, hgrn_onorm, q_norm, k_norm, w_ba, w_bb, w_bc, w_out):
    bp, t_len, _ = x_prompt.shape
    bs = x_sample.shape[0]
    assert x_sample.shape[1] == 1
    n_pages = page_table.shape[1]
    past_len = n_pages * PAGE_SIZE
    n_phys = cache_k.shape[1]
    n_sel_p = min(TOPK_MAX, t_len // 4)
    n_sel_s = min(TOPK_MAX, (past_len + 1) // 4)

    wg, wu, wd = ffn_wg.astype(BF16), ffn_wu.astype(BF16), ffn_wd.astype(BF16)
    w_in_p = _permute_w_in(w_in)
    wa, wb, wc, wo = w_ba.astype(BF16), w_bb.astype(BF16), w_bc.astype(BF16), w_out.astype(BF16)
    eye = jnp.eye(len(POOL_WINDOWS), dtype=F32)
    pool_wbd = jnp.einsum('lgcd,gh->lgchd', pool_w, eye).reshape(DEPTH, POOL_WIDTH, POOL_WIDTH).astype(BF16)
    pool_sc = pool_scale.reshape(DEPTH, 1, POOL_WIDTH)
    norm_g4 = norm_g.reshape(DEPTH, N_SUB, 1, D_MODEL)
    lbs = _lower_bounds(hgrn_lb)
    qg = jnp.tile(q_norm, (1, LANES // HEAD_DIM)).reshape(DEPTH, 1, LANES)
    kg = jnp.tile(k_norm, (1, LANES // HEAD_DIM)).reshape(DEPTH, 1, LANES)
    gid = np.arange(LANES) // HEAD_DIM
    bd = jnp.asarray((gid[:, None] == gid[None, :]).astype(np.float32), BF16)
    mstack = _hgrn_consts(HGRN_CHUNK)
    cos_p, sin_p = _rope_tables(jnp.arange(t_len, dtype=I32))
    cos_s, sin_s = _rope_tables(jnp.full((bs,), past_len, I32))
    ck = jnp.transpose(cache_k, (0, 1, 3, 4, 2)).reshape(DEPTH, n_phys, KV_WIDTH, PAGE_SIZE)
    cv = jnp.transpose(cache_v, (0, 1, 3, 4, 2)).reshape(DEPTH, n_phys, KV_WIDTH, PAGE_SIZE)
    cki = jnp.transpose(cache_ki, (0, 1, 3, 2))
    state_hgrn_t = jnp.swapaxes(state_hgrn, -1, -2)

    rows_all = bp + bs
    rows_pad = -(-rows_all // 8) * 8
    c_all = jnp.concatenate([c_prompt, c_sample, jnp.zeros((rows_pad - rows_all, D_MODEL), F32)], axis=0)
    mods = _ada(c_all, ada_w, ada_b)
    mods_p = mods[:, :bp].reshape(DEPTH, bp, 1, 3 * N_SUB * D_MODEL)
    mods_s = mods[:, bp:rows_all]

    xp = x_prompt.reshape(bp * t_len, D_MODEL)
    xs = x_sample.reshape(bs, D_MODEL)
    pool_prev_p = jnp.zeros((bp, 16, POOL_WIDTH), F32)
    outs = [[] for _ in range(10)]
    tm_prep = min(512, t_len)

    for l in range(DEPTH):
        xp = _ffn(xp, mods_p, norm_g4, wg, wu, wd, l, 0, 0, t_len)
        p = _inproj(xp, mods_p, norm_g4, w_in_p, l, t_len)
        ya = _pool_prompt(p, pool_prev_p, pool_wbd, pool_sc, l, bp, t_len)
        yb, st_t = _hgrn_prompt(p, lbs[l].reshape(1, -1), hgrn_onorm[l].reshape(1, -1), mstack, bp, t_len)
        q_h, _, k_h, v_t, qi_h, _, ki_b, wit, k_tf, v_tf, ki_tf = _prep(
            p, cos_p, sin_p, qg[l], kg[l], bd, tm_prep, t_len // tm_prep)
        yc = _dsa_prompt(q_h, qi_h, wit, k_h, v_t, ki_b, bp, t_len, n_sel_p, tm_prep)
        xp = _merge(xp, p, ya, yb, yc, mods_p, wa, wb, wc, wo, l, t_len)
        xp = _ffn(xp, mods_p, norm_g4, wg, wu, wd, l, 1, 2, t_len)
        p3 = p.reshape(bp, t_len, IN_PAD)
        outs[0].append(jnp.transpose(k_tf.reshape(bp, N_KV_HEADS, HEAD_DIM, t_len), (0, 3, 1, 2)))
        outs[1].append(jnp.transpose(v_tf.reshape(bp, N_KV_HEADS, HEAD_DIM, t_len), (0, 3, 1, 2)))
        outs[2].append(jnp.swapaxes(ki_tf, 1, 2))
        outs[3].append(p3[:, t_len - POOL_STATE:, OFF_UP:OFF_UP + POOL_WIDTH])
        outs[4].append(jnp.swapaxes(st_t, -1, -2))

        xs = _ffn(xs, mods_s, norm_g4, wg, wu, wd, l, 0, 0, 1)
        ps = _inproj(xs, mods_s, norm_g4, w_in_p, l, 1)
        ya = _pool_dec(ps, jnp.swapaxes(state_pool[l], 0, 1), pool_wbd, pool_sc, l, past_len)
        kw = HGRN_HEADS * HGRN_K
        vw = HGRN_HEADS * HGRN_V
        yb4, st_new = _hgrn_dec(
            ps[:, OFF_HQ:OFF_HQ + kw].reshape(bs, HGRN_HEADS, 1, HGRN_K),
            ps[:, OFF_HF:OFF_HF + kw].reshape(bs, HGRN_HEADS, 1, HGRN_K),
            lbs[l].reshape(HGRN_HEADS, 1, HGRN_K),
            ps[:, OFF_HI:OFF_HI + vw].reshape(bs, HGRN_HEADS, HGRN_V, 1),
            ps[:, OFF_HG:OFF_HG + vw].reshape(bs, HGRN_HEADS, HGRN_V, 1),
            hgrn_onorm[l].reshape(-1, 1), state_hgrn_t, l)
        yb = yb4.reshape(bs, vw)
        q_h, k_r, _, _, qi_h, ki_r, _, _, _, _, _ = _prep(ps, cos_s, sin_s, qg[l], kg[l], bd, bs, 1)
        v_new = ps[:, OFF_AV:OFF_AV + KV_WIDTH]
        qi8 = jnp.pad(jnp.swapaxes(qi_h, 0, 1), ((0, 0), (0, 8 - IDX_HEADS), (0, 0)))
        wi8 = jnp.pad(ps[:, OFF_IKW + IDX_DIM:OFF_IKW + IDX_DIM + IDX_HEADS].reshape(bs, IDX_HEADS, 1),
                      ((0, 0), (0, 8 - IDX_HEADS), (0, 0)))
        keys = _dsa_dec_scores(page_table, qi8, wi8, ki_r.reshape(bs, 1, IDX_DIM), cki, l)
        thr, j_lim = _dsa_dec_thr(keys.reshape(bs, -1), n_sel_s)
        q8 = jnp.pad(jnp.swapaxes(q_h, 0, 1).reshape(bs, N_KV_HEADS, GROUP, HEAD_DIM),
                     ((0, 0), (0, 0), (0, 8 - GROUP), (0, 0)))
        yc8 = _dsa_dec_attn(page_table, thr.reshape(bs), j_lim.reshape(bs), q8, keys,
                            k_r.reshape(bs, 1, KV_WIDTH), v_new.reshape(bs, 1, KV_WIDTH), ck, cv, l)
        yc = yc8[:, :, :GROUP, :].reshape(bs, ATTN_WIDTH)
        xs = _merge(xs, ps, ya, yb, yc, mods_s, wa, wb, wc, wo, l, 1)
        xs = _ffn(xs, mods_s, norm_g4, wg, wu, wd, l, 1, 2, 1)
        outs[5].append(k_r.reshape(bs, 1, N_KV_HEADS, HEAD_DIM))
        outs[6].append(v_new.reshape(bs, 1, N_KV_HEADS, HEAD_DIM))
        outs[7].append(ki_r.reshape(bs, 1, IDX_DIM))
        outs[8].append(jnp.concatenate([state_pool[l][:, 1:], ps[:, None, OFF_UP:OFF_UP + POOL_WIDTH]], axis=1))
        outs[9].append(jnp.swapaxes(st_new, -1, -2))

    stacked = [jnp.stack(o) for o in outs]
    return (xp.reshape(bp, t_len, D_MODEL), xs.reshape(bs, 1, D_MODEL), *stacked)
```

```python
import functools

import numpy as np
import jax
import jax.numpy as jnp
from jax import lax
from jax.experimental import pallas as pl
from jax.experimental.pallas import tpu as pltpu

F32 = jnp.float32
BF16 = jnp.bfloat16
I32 = jnp.int32
I16 = jnp.int16

D_MODEL = 1024
DEPTH = 4
PAGE_SIZE = 128
N_SUB = 3
POOL_WINDOWS = (2, 4, 8, 16)
POOL_GROUP_DIM = 64
POOL_WIDTH = 256
POOL_STATE = 15
HGRN_HEADS = 4
HGRN_K = 128
HGRN_V = 64
F_FLOOR = 1e-30
N_HEADS = 8
N_KV_HEADS = 2
HEAD_DIM = 64
GROUP = N_HEADS // N_KV_HEADS
ATTN_WIDTH = 512
KV_WIDTH = 128
IDX_HEADS = 4
IDX_DIM = 64
TOPK_MAX = 256
ROPE_THETA = 10000.0
LOG2_E = 1.4426950408889634
ATTN_SCALE = HEAD_DIM ** -0.5
IDX_SCALE = (IDX_DIM * IDX_HEADS) ** -0.5
NEG_BIG = -1e30
D_FF = 2816
EPS = 1e-6
INT_MIN = np.int32(-2147483648)

LANES = 128
VMEM_LIMIT = 48 * 1024 * 1024

OFF_GL, OFF_HQ, OFF_HF, OFF_AQ = 0, 3072, 3584, 4096
OFF_UP, OFF_HI, OFF_HG, OFF_IQ = 4608, 4864, 5120, 5376
OFF_AK, OFF_AV, OFF_IKW = 5632, 5760, 5888
IN_PAD = 6144
HGRN_CHUNK = 128
HGRN_CHUNKS_PER_STEP = 4
HGRN_LEVELS = tuple(1 << i for i in range(HGRN_CHUNK.bit_length()))
HGRN_SMALL = tuple(h for h in HGRN_LEVELS if 1 < h < 8)


def _dot(a, b):
    return jnp.dot(a, b, preferred_element_type=F32)


def _dot_nt(a, b):
    return lax.dot_general(a, b, (((1,), (1,)), ((), ())), preferred_element_type=F32)


def _dot_tn(a, b):
    return lax.dot_general(a, b, (((0,), (0,)), ((), ())), preferred_element_type=F32)


def _split2(x):
    hi = x.astype(BF16)
    lo = (x - hi.astype(F32)).astype(BF16)
    return hi, lo


def _split3(x):
    hi = x.astype(BF16)
    r = x - hi.astype(F32)
    mid = r.astype(BF16)
    lo = (r - mid.astype(F32)).astype(BF16)
    return hi, mid, lo


def _silu(x):
    return x * jax.nn.sigmoid(x)


def _params(sem):
    return pltpu.CompilerParams(dimension_semantics=sem, vmem_limit_bytes=VMEM_LIMIT)


def _mod_spec(mods, l, m, tm, rows_per_seq):
    if mods.ndim == 4:
        return pl.BlockSpec((None, None, 1, D_MODEL), lambda i, *_: (l, (i * tm) // rows_per_seq, 0, m))
    return pl.BlockSpec((None, tm, D_MODEL), lambda i, *_: (l, i, m))


def _row_tile(n, mods, rows_per_seq, cap):
    return min(cap, rows_per_seq if mods.ndim == 4 else n)


def _prenorm(x, g, scale, shift):
    y = x * lax.rsqrt(jnp.mean(x * x, axis=-1, keepdims=True) + EPS) * g
    return y * (1.0 + scale) + shift


def _ada_kernel(c_ref, w_ref, b_ref, o_ref):
    c = c_ref[...]
    o_ref[...] = jnp.dot(_silu(c), w_ref[...], precision=lax.Precision.HIGHEST,
                         preferred_element_type=F32) + b_ref[...]


def _ada(c_all, ada_w, ada_b):
    rows = c_all.shape[0]
    width = ada_w.shape[-1]
    tn = 1024
    return pl.pallas_call(
        _ada_kernel,
        grid=(DEPTH, width // tn),
        in_specs=[pl.BlockSpec((rows, D_MODEL), lambda l, j: (0, 0)),
                  pl.BlockSpec((None, D_MODEL, tn), lambda l, j: (l, 0, j)),
                  pl.BlockSpec((None, 1, tn), lambda l, j: (l, 0, j))],
        out_specs=pl.BlockSpec((None, rows, tn), lambda l, j: (l, 0, j)),
        out_shape=jax.ShapeDtypeStruct((DEPTH, rows, width), F32),
        compiler_params=_params(("parallel", "parallel")),
        name="ada",
    )(c_all, ada_w, ada_b.reshape(DEPTH, 1, width))


def _ffn_kernel(x_ref, sh_ref, sc_ref, gt_ref, g_ref, wg_ref, wu_ref, wd_ref, o_ref, h_scr, acc_scr):
    j = pl.program_id(1)

    @pl.when(j == 0)
    def _():
        h_scr[...] = _prenorm(x_ref[...], g_ref[...], sc_ref[...], sh_ref[...]).astype(BF16)
        acc_scr[...] = jnp.zeros_like(acc_scr)

    h = h_scr[...]
    act = (_silu(_dot(h, wg_ref[...])) * _dot(h, wu_ref[...])).astype(BF16)
    acc_scr[...] += _dot(act, wd_ref[...])

    @pl.when(j == pl.num_programs(1) - 1)
    def _():
        o_ref[...] = x_ref[...] + 0.5 * gt_ref[...] * acc_scr[...]


def _ffn(x, mods, norm_g, wg, wu, wd, l, which, sub, rows_per_seq):
    n = x.shape[0]
    tm = _row_tile(n, mods, rows_per_seq, 1024)
    tf = 1408
    return pl.pallas_call(
        _ffn_kernel,
        grid=(n // tm, D_FF // tf),
        in_specs=[pl.BlockSpec((tm, D_MODEL), lambda i, j: (i, 0)),
                  _mod_spec(mods, l, 3 * sub, tm, rows_per_seq),
                  _mod_spec(mods, l, 3 * sub + 1, tm, rows_per_seq),
                  _mod_spec(mods, l, 3 * sub + 2, tm, rows_per_seq),
                  pl.BlockSpec((None, None, 1, D_MODEL), lambda i, j: (l, sub, 0, 0)),
                  pl.BlockSpec((None, None, D_MODEL, tf), lambda i, j: (l, which, 0, j)),
                  pl.BlockSpec((None, None, D_MODEL, tf), lambda i, j: (l, which, 0, j)),
                  pl.BlockSpec((None, None, tf, D_MODEL), lambda i, j: (l, which, j, 0))],
        out_specs=pl.BlockSpec((tm, D_MODEL), lambda i, j: (i, 0)),
        out_shape=jax.ShapeDtypeStruct((n, D_MODEL), F32),
        scratch_shapes=[pltpu.VMEM((tm, D_MODEL), BF16), pltpu.VMEM((tm, D_MODEL), F32)],
        compiler_params=_params(("parallel", "arbitrary")),
        name="ffn",
    )(x, mods, mods, mods, norm_g, wg, wu, wd)


def _inproj_kernel(x_ref, sh_ref, sc_ref, g_ref, w_ref, o_ref, h_scr):
    @pl.when(pl.program_id(1) == 0)
    def _():
        h_scr[...] = _prenorm(x_ref[...], g_ref[...], sc_ref[...], sh_ref[...]).astype(BF16)

    o_ref[...] = _dot(h_scr[...], w_ref[...])


def _inproj(x, mods, norm_g, w_in, l, rows_per_seq):
    n = x.shape[0]
    tm = _row_tile(n, mods, rows_per_seq, 1024)
    tn = 1536
    return pl.pallas_call(
        _inproj_kernel,
        grid=(n // tm, IN_PAD // tn),
        in_specs=[pl.BlockSpec((tm, D_MODEL), lambda i, j: (i, 0)),
                  _mod_spec(mods, l, 3, tm, rows_per_seq),
                  _mod_spec(mods, l, 4, tm, rows_per_seq),
                  pl.BlockSpec((None, None, 1, D_MODEL), lambda i, j: (l, 1, 0, 0)),
                  pl.BlockSpec((None, D_MODEL, tn), lambda i, j: (l, 0, j))],
        out_specs=pl.BlockSpec((tm, tn), lambda i, j: (i, j)),
        out_shape=jax.ShapeDtypeStruct((n, IN_PAD), F32),
        scratch_shapes=[pltpu.VMEM((tm, D_MODEL), BF16)],
        compiler_params=_params(("parallel", "arbitrary")),
        name="inproj",
    )(x, mods, mods, norm_g, w_in)


def _prep_kernel(aq_ref, ak_ref, av_ref, iq_ref, ikw_ref, cos_ref, sin_ref, qg_ref, kg_ref, bd_ref,
                 q_o, k_o, kb_o, vt_o, qi_o, ki_o, kib_o, wit_o, ktf_o, vtf_o, kitf_o):
    cos = cos_ref[...]
    sin = sin_ref[...]
    bd = bd_ref[...]
    lane = lax.broadcasted_iota(I32, cos.shape, 1)
    first_half = (lane & (HEAD_DIM // 2)) == 0

    def rope(x):
        partner = jnp.where(first_half, pltpu.roll(x, LANES - HEAD_DIM // 2, 1), pltpu.roll(x, HEAD_DIM // 2, 1))
        return x * cos + partner * sin

    def head_norm(x):
        hi, mid, lo = _split3(x * x)
        ms = (_dot(hi, bd) + _dot(mid, bd) + _dot(lo, bd)) * (1.0 / HEAD_DIM)
        return x * lax.rsqrt(ms + EPS)

    def put_heads(o_ref, first, y):
        for half in range(LANES // HEAD_DIM):
            o_ref[first + half] = y[:, half * HEAD_DIM:(half + 1) * HEAD_DIM].astype(BF16)

    qg = qg_ref[...]
    for c in range(ATTN_WIDTH // LANES):
        sl = slice(c * LANES, (c + 1) * LANES)
        put_heads(q_o, 2 * c, rope(head_norm(aq_ref[:, sl]) * qg) * (ATTN_SCALE * LOG2_E))
    k = rope(head_norm(ak_ref[...]) * kg_ref[...])
    k_o[...] = k
    ktf_o[...] = k.T
    put_heads(kb_o, 0, k)
    v_tf = av_ref[...].T
    vtf_o[...] = v_tf
    v_t = v_tf.astype(BF16)
    ones = jnp.ones((V_ROWS - HEAD_DIM, v_t.shape[1]), BF16)
    vt_o[...] = jnp.concatenate([v_t[:HEAD_DIM], ones, v_t[HEAD_DIM:], ones], axis=0)
    for c in range(IDX_HEADS * IDX_DIM // LANES):
        sl = slice(c * LANES, (c + 1) * LANES)
        put_heads(qi_o, 2 * c, rope(iq_ref[:, sl]))
    ikw = ikw_ref[...]
    ki_wide = rope(head_norm(ikw))
    ki = ki_wide[:, :IDX_DIM]
    ki_o[...] = ki
    kitf_o[...] = ki_wide.T[:IDX_DIM, :]
    kib_o[...] = ki.astype(BF16)
    wit_o[...] = ikw.T[IDX_DIM:IDX_DIM + 8, :]


def _prep(p, cos, sin, qg, kg, bd, tm, table_blocks):
    n = p.shape[0]

    def col(off, width):
        return pl.BlockSpec((tm, width), lambda i: (i, off // width))

    tab = pl.BlockSpec((tm, LANES), lambda i: (i % table_blocks, 0))
    vec = pl.BlockSpec((1, LANES), lambda i: (0, 0))

    def row(width):
        return pl.BlockSpec((tm, width), lambda i: (i, 0))

    def heads(h):
        return pl.BlockSpec((h, tm, HEAD_DIM), lambda i: (0, i, 0))

    n_seq = n // (tm * table_blocks)
    seq_len = tm * table_blocks

    def feature_major(width):
        return pl.BlockSpec((None, width, tm), lambda i: (i // table_blocks, 0, i % table_blocks))

    return pl.pallas_call(
        _prep_kernel,
        grid=(n // tm,),
        in_specs=[col(OFF_AQ, ATTN_WIDTH), col(OFF_AK, KV_WIDTH), col(OFF_AV, KV_WIDTH),
                  col(OFF_IQ, IDX_HEADS * IDX_DIM), col(OFF_IKW, LANES), tab, tab, vec, vec,
                  pl.BlockSpec((LANES, LANES), lambda i: (0, 0))],
        out_specs=[heads(N_HEADS), row(KV_WIDTH), heads(N_KV_HEADS),
                   pl.BlockSpec((None, N_KV_HEADS * V_ROWS, tm), lambda i: (i, 0, 0)),
                   heads(IDX_HEADS), row(IDX_DIM), row(IDX_DIM),
                   pl.BlockSpec((8, tm), lambda i: (0, i)),
                   feature_major(KV_WIDTH), feature_major(KV_WIDTH), feature_major(IDX_DIM)],
        out_shape=[jax.ShapeDtypeStruct((N_HEADS, n, HEAD_DIM), BF16),
                   jax.ShapeDtypeStruct((n, KV_WIDTH), F32),
                   jax.ShapeDtypeStruct((N_KV_HEADS, n, HEAD_DIM), BF16),
                   jax.ShapeDtypeStruct((n // tm, N_KV_HEADS * V_ROWS, tm), BF16),
                   jax.ShapeDtypeStruct((IDX_HEADS, n, IDX_DIM), BF16),
                   jax.ShapeDtypeStruct((n, IDX_DIM), F32),
                   jax.ShapeDtypeStruct((n, IDX_DIM), BF16),
                   jax.ShapeDtypeStruct((8, n), F32),
                   jax.ShapeDtypeStruct((n_seq, KV_WIDTH, seq_len), F32),
                   jax.ShapeDtypeStruct((n_seq, KV_WIDTH, seq_len), F32),
                   jax.ShapeDtypeStruct((n_seq, IDX_DIM, seq_len), F32)],
        compiler_params=_params(("parallel",)),
        name="prep",
    )(p, p, p, p, p, cos, sin, qg, kg, bd)


def _rope_tables(pos):
    half = HEAD_DIM // 2
    inv = ROPE_THETA ** (-jnp.arange(half, dtype=F32) / half)
    ang = pos.astype(F32)[:, None] * inv[None, :]
    c, s = jnp.cos(ang), jnp.sin(ang)
    return jnp.concatenate([c, c, c, c], axis=1), jnp.concatenate([-s, s, -s, s], axis=1)


def _pool_select(sums, lane):
    out = sums[-1]
    for gi in range(len(POOL_WINDOWS) - 2, -1, -1):
        out = jnp.where(lane < (gi + 1) * POOL_GROUP_DIM, sums[gi], out)
    return out


def _pool_kernel(u_ref, prev_ref, w_ref, s_ref, o_ref, carry_scr, *, tt, p0):
    t = pl.program_id(1)

    @pl.when(t == 0)
    def _():
        carry_scr[...] = prev_ref[...]

    u = u_ref[...]
    e = jnp.concatenate([carry_scr[...], u], axis=0)
    carry_scr[...] = u[tt - 16:, :]
    s2 = e[1:] + e[:-1]
    s4 = s2[2:] + s2[:-2]
    s8 = s4[4:] + s4[:-4]
    s16 = s8[8:] + s8[:-8]
    sums = (s2[15:15 + tt], s4[13:13 + tt], s8[9:9 + tt], s16[1:1 + tt])
    n_avail = (p0 + 1 + t * tt + lax.broadcasted_iota(I32, (tt, 1), 0)).astype(F32)
    lane = lax.broadcasted_iota(I32, (tt, POOL_WIDTH), 1)
    means = [sums[gi] / jnp.minimum(n_avail, float(w)) for gi, w in enumerate(POOL_WINDOWS)]
    pooled = _pool_select(means, lane) - u
    o_ref[...] = _dot(pooled.astype(BF16), w_ref[...]) * s_ref[...]


def _pool_prompt(p, prev16, wbd, scale, l, b, t_len):
    tt = min(2048, t_len)
    nt = t_len // tt
    return pl.pallas_call(
        functools.partial(_pool_kernel, tt=tt, p0=0),
        grid=(b, nt),
        in_specs=[pl.BlockSpec((tt, POOL_WIDTH), lambda bi, ti: (bi * nt + ti, OFF_UP // POOL_WIDTH)),
                  pl.BlockSpec((None, 16, POOL_WIDTH), lambda bi, ti: (bi, 0, 0)),
                  pl.BlockSpec((None, POOL_WIDTH, POOL_WIDTH), lambda bi, ti: (l, 0, 0)),
                  pl.BlockSpec((None, 1, POOL_WIDTH), lambda bi, ti: (l, 0, 0))],
        out_specs=pl.BlockSpec((tt, POOL_WIDTH), lambda bi, ti: (bi * nt + ti, 0)),
        out_shape=jax.ShapeDtypeStruct((b * t_len, POOL_WIDTH), F32),
        scratch_shapes=[pltpu.VMEM((16, POOL_WIDTH), F32)],
        compiler_params=_params(("parallel", "arbitrary")),
        name="pool",
    )(p, prev16, wbd, scale)


def _pool_dec_kernel(u_ref, st_ref, w_ref, s_ref, o_ref, *, p0):
    u = u_ref[...]
    lane = lax.broadcasted_iota(I32, u.shape, 1)
    means = []
    acc = u
    d = 1
    for w in POOL_WINDOWS:
        while d < w:
            acc = acc + st_ref[POOL_STATE - d]
            d += 1
        means.append(acc / float(min(p0 + 1, w)))
    pooled = _pool_select(means, lane) - u
    o_ref[...] = _dot(pooled.astype(BF16), w_ref[...]) * s_ref[...]


def _pool_dec(p, st_t, wbd, scale, l, p0):
    n = p.shape[0]
    return pl.pallas_call(
        functools.partial(_pool_dec_kernel, p0=p0),
        grid=(1,),
        in_specs=[pl.BlockSpec((n, POOL_WIDTH), lambda i: (0, OFF_UP // POOL_WIDTH)),
                  pl.BlockSpec((POOL_STATE, n, POOL_WIDTH), lambda i: (0, 0, 0)),
                  pl.BlockSpec((None, POOL_WIDTH, POOL_WIDTH), lambda i: (l, 0, 0)),
                  pl.BlockSpec((None, 1, POOL_WIDTH), lambda i: (l, 0, 0))],
        out_specs=pl.BlockSpec((n, POOL_WIDTH), lambda i: (0, 0)),
        out_shape=jax.ShapeDtypeStruct((n, POOL_WIDTH), F32),
        compiler_params=_params(("arbitrary",)),
        name="pool_dec",
    )(p, st_t, wbd, scale)


def _hgrn_consts(c):
    r = np.arange(c)[:, None]
    s = np.arange(c)[None, :]
    mats = [(s <= r) & (s // h == r // h) for h in HGRN_SMALL]
    mats += [(s > r) & (s // h == r // h) for h in HGRN_SMALL]
    mats.append(s <= r)
    return jnp.asarray(np.concatenate(mats, axis=0).astype(np.float32), BF16)


def _hgrn_kernel(hq_ref, hf_ref, hi_ref, hg_ref, lb_ref, on_ref, ms_ref, yb_ref, st_ref, s_scr, *, c, n_sub):
    step = pl.program_id(1)
    nl = len(HGRN_LEVELS)

    @pl.when(step == 0)
    def _():
        s_scr[...] = jnp.zeros_like(s_scr)

    ti = lax.broadcasted_iota(I32, (c, c), 0)
    si = lax.broadcasted_iota(I32, (c, c), 1)
    pairs = []
    for h in HGRN_LEVELS[:-1]:
        sh = h.bit_length() - 1
        tb = jnp.right_shift(ti, sh)
        pairs.append(((tb & 1) == 1) & (jnp.right_shift(si, sh) == tb - 1))

    kw = HGRN_HEADS * HGRN_K
    for sub in range(n_sub):
        _hgrn_chunk(hq_ref, hf_ref, hi_ref, hg_ref, lb_ref, on_ref, ms_ref, yb_ref, s_scr,
                    slice(sub * c, (sub + 1) * c), c, kw, nl, ti, si, pairs)

    @pl.when(step == pl.num_programs(1) - 1)
    def _():
        st_ref[...] = s_scr[...]


def _hgrn_chunk(hq_ref, hf_ref, hi_ref, hg_ref, lb_ref, on_ref, ms_ref, yb_ref, s_scr,
                rows, c, kw, nl, ti, si, pairs):
    q = hq_ref[rows, :]
    lb = lb_ref[...]
    f = lb + (1.0 - lb) * jax.nn.sigmoid(hf_ref[rows, :])
    g = jnp.log(jnp.maximum(f, F_FLOOR))
    kk = 1.0 - f
    g_hi, g_lo = _split2(g)
    r = _dot(ms_ref[...], jnp.concatenate([g_hi, g_lo], axis=1))
    r = r[:, :kw] + r[:, kw:]
    n_small = len(HGRN_SMALL)
    low = [g] + [r[k * c:(k + 1) * c] for k in range(n_small)]
    up = [None] + [r[(n_small + k) * c:(n_small + k + 1) * c] for k in range(n_small)]
    cum = r[2 * n_small * c:]
    for h in HGRN_LEVELS[1 + n_small:]:
        cum3 = cum.reshape(c // h, h, kw)
        ends = cum3[:, h - 1:h, :]
        if h == c:
            low.append(cum)
        else:
            starts = jnp.concatenate([jnp.zeros((1, 1, kw), F32), ends[:-1]], axis=0)
            low.append((cum3 - starts).reshape(c, kw))
        up.append((ends - cum3).reshape(c, kw))
    kkb = kk.astype(BF16)
    qes = [(q * jnp.exp(low[k])).astype(BF16) for k in range(nl)]
    kes = [kkb] + [(kk * jnp.exp(up[k])).astype(BF16) for k in range(1, nl)]
    qb = q.astype(BF16)
    decay = jnp.exp(cum[c - 1:c, :])

    for hd in range(HGRN_HEADS):
        ks = slice(hd * HGRN_K, (hd + 1) * HGRN_K)
        vs = slice(hd * HGRN_V, (hd + 1) * HGRN_V)
        ivb = hi_ref[rows, vs].astype(BF16)
        att = jnp.where(ti == si, _dot_nt(qb[:, ks], kkb[:, ks]), 0.0)
        for k in range(nl - 1):
            att = jnp.where(pairs[k], _dot_nt(qes[k][:, ks], kes[k][:, ks]), att)
        s_t = s_scr[hd]
        o = _dot(att.astype(BF16), ivb) + _dot_nt(qes[-1][:, ks], s_t.astype(BF16))
        s_scr[hd] = s_t * decay[:, ks] + _dot_tn(ivb, kes[-1][:, ks])

        on = o * lax.rsqrt(jnp.mean(o * o, axis=-1, keepdims=True) + EPS) * on_ref[...]
        yb_ref[rows, vs] = on * _silu(hg_ref[rows, vs])


def _hgrn_prompt(p, lb, onorm, mstack, b, t_len):
    c = HGRN_CHUNK
    n_sub = HGRN_CHUNKS_PER_STEP
    nc = t_len // (c * n_sub)
    kw = HGRN_HEADS * HGRN_K
    vw = HGRN_HEADS * HGRN_V

    def col(off, width):
        return pl.BlockSpec((c * n_sub, width), lambda bi, ci: (bi * nc + ci, off // width))

    return pl.pallas_call(
        functools.partial(_hgrn_kernel, c=c, n_sub=n_sub),
        grid=(b, nc),
        in_specs=[col(OFF_HQ, kw), col(OFF_HF, kw), col(OFF_HI, vw), col(OFF_HG, vw),
                  pl.BlockSpec((1, kw), lambda bi, ci: (0, 0)),
                  pl.BlockSpec((1, HGRN_V), lambda bi, ci: (0, 0)),
                  pl.BlockSpec(mstack.shape, lambda bi, ci: (0, 0))],
        out_specs=[pl.BlockSpec((c * n_sub, vw), lambda bi, ci: (bi * nc + ci, 0)),
                   pl.BlockSpec((None, HGRN_HEADS, HGRN_V, HGRN_K), lambda bi, ci: (bi, 0, 0, 0))],
        out_shape=[jax.ShapeDtypeStruct((b * t_len, vw), F32),
                   jax.ShapeDtypeStruct((b, HGRN_HEADS, HGRN_V, HGRN_K), F32)],
        scratch_shapes=[pltpu.VMEM((HGRN_HEADS, HGRN_V, HGRN_K), F32)],
        compiler_params=_params(("parallel", "arbitrary")),
        name="hgrn",
    )(p, p, p, p, lb, onorm, mstack)


def _hgrn_dec_kernel(q_ref, z_ref, lb_ref, i_ref, hg_ref, on_ref, s_ref, yb_ref, so_ref):
    lb = lb_ref[...]
    f = lb + (1.0 - lb) * jax.nn.sigmoid(z_ref[...])
    a = jnp.exp(jnp.log(jnp.maximum(f, F_FLOOR)))
    s_new = a * s_ref[...] + i_ref[...] * (1.0 - f)
    so_ref[...] = s_new
    o = jnp.sum(q_ref[...] * s_new, axis=-1, keepdims=True)
    on = o * lax.rsqrt(jnp.mean(o * o, axis=-2, keepdims=True) + EPS) * on_ref[...]
    yb_ref[...] = on * _silu(hg_ref[...])


def _hgrn_dec(q_r, z_r, lb_r, i_c, hg_c, onorm_c, state_t, l):
    n = state_t.shape[1]
    nb = 8
    row = pl.BlockSpec((nb, HGRN_HEADS, 1, HGRN_K), lambda i: (i, 0, 0, 0))
    col = pl.BlockSpec((nb, HGRN_HEADS, HGRN_V, 1), lambda i: (i, 0, 0, 0))
    return pl.pallas_call(
        _hgrn_dec_kernel,
        grid=(n // nb,),
        in_specs=[row, row, pl.BlockSpec((HGRN_HEADS, 1, HGRN_K), lambda i: (0, 0, 0)), col, col,
                  pl.BlockSpec((HGRN_V, 1), lambda i: (0, 0)),
                  pl.BlockSpec((None, nb, HGRN_HEADS, HGRN_V, HGRN_K), lambda i: (l, i, 0, 0, 0))],
        out_specs=[col, pl.BlockSpec((nb, HGRN_HEADS, HGRN_V, HGRN_K), lambda i: (i, 0, 0, 0))],
        out_shape=[jax.ShapeDtypeStruct((n, HGRN_HEADS, HGRN_V, 1), F32),
                   jax.ShapeDtypeStruct(state_t.shape[1:], F32)],
        compiler_params=_params(("parallel",)),
        name="hgrn_dec",
    )(q_r, z_r, lb_r, i_c, hg_c, onorm_c, state_t)


def _sort_key(s):
    bits = pltpu.bitcast(s, I32)
    return jnp.where(bits < 0, INT_MIN - bits, bits)


def _lane_fold(x):
    part = x[:, :LANES]
    for cidx in range(1, x.shape[1] // LANES):
        part = part + x[:, cidx * LANES:(cidx + 1) * LANES]
    return part


def _select_bias(key, thr, idx, j_lim):
    return jnp.where(key > thr, 0.0, jnp.where(key == thr, jnp.where(idx < j_lim, 0.0, NEG_BIG), NEG_BIG))


def _bisect_bits(count_ge, thr, first_bit, n_bits, k):
    def bit_body(it, thr):
        cand = thr + lax.shift_left(jnp.int32(1), first_bit - it)
        return jnp.where(count_ge(cand) >= k, cand, thr)

    return lax.fori_loop(0, n_bits, bit_body, thr)


def _topk_threshold(count, stat_shape, n_sel, idx_bits, idx_sentinel, kth_largest=None):
    k = float(n_sel)
    if kth_largest is None:
        thr = _bisect_bits(lambda cand: count(lambda key, idx: jnp.where(key >= cand, 1.0, 0.0)),
                           jnp.full(stat_shape, INT_MIN, I32), 31, 32, k)
    else:
        thr = kth_largest(k)
    n_ge = count(lambda key, idx: jnp.where(key >= thr, 1.0, 0.0))
    n_gt = count(lambda key, idx: jnp.where(key > thr, 1.0, 0.0))
    need = k - n_gt
    live = thr > INT_MIN
    tie = jnp.where(live, jnp.where(n_ge > k, 1.0, 0.0), 0.0)
    j_default = jnp.where(live, jnp.int32(idx_sentinel), jnp.int32(0))

    def tie_search():
        def idx_body(it, j_lim):
            cand = j_lim + lax.shift_left(jnp.int32(1), idx_bits - 1 - it)
            cnt = count(lambda key, idx: jnp.where(key == thr, jnp.where(idx < cand, 1.0, 0.0), 0.0))
            return jnp.where(cnt <= need, cand, j_lim)

        j_lim = lax.fori_loop(0, idx_bits, idx_body, jnp.zeros(stat_shape, I32))
        return jnp.where(tie > 0.0, j_lim, j_default)

    j_lim = lax.cond(jnp.max(tie) > 0.0, tie_search, lambda: j_default)
    return thr, j_lim


FOLD_ROWS = 64
V_ROWS = HEAD_DIM + 16


def _fold(x, op):
    parts = [x[r * FOLD_ROWS:(r + 1) * FOLD_ROWS] for r in range(x.shape[0] // FOLD_ROWS)]
    while len(parts) > 1:
        parts = [op(parts[i], parts[i + 1]) if i + 1 < len(parts) else parts[i]
                 for i in range(0, len(parts), 2)]
    return parts[0]


def _dsa_kernel(q_ref, qi_ref, wit_ref, k_ref, vt_ref, ki_ref, o_ref, keys_scr, k16_scr, lg_scr, p_scr, m_scr, acc_scr,
                *, tq, tk, n_sel, t_len):
    i = pl.program_id(1)
    nk = ((i + 1) * tq + tk - 1) // tk
    krow = lax.broadcasted_iota(I32, (tk, tq), 0)
    qpos = i * tq + lax.broadcasted_iota(I32, (tk, tq), 1)
    wit = wit_ref[...] * IDX_SCALE
    qi_all = qi_ref[...].reshape(IDX_HEADS * tq, IDX_DIM)

    def rows(j):
        return pl.ds(pl.multiple_of(j * tk, tk), tk)

    def score_tile(j, causal):
        rel = _dot_nt(ki_ref[rows(j), :], qi_all)
        s = wit[0:1, :] * jnp.maximum(rel[:, :tq], 0.0)
        for h in range(1, IDX_HEADS):
            s = s + wit[h:h + 1, :] * jnp.maximum(rel[:, h * tq:(h + 1) * tq], 0.0)
        key = jnp.where(s > 0.5 * NEG_BIG, _sort_key(s), INT_MIN)
        if causal:
            key = jnp.where(j * tk + krow <= qpos, key, INT_MIN)
        keys_scr[j] = key
        k16_scr[j] = jnp.right_shift(key, 16).astype(I16)

    def score_body(j, carry):
        score_tile(j, False)
        return carry

    lax.fori_loop(0, nk - 1, score_body, 0)
    score_tile(nk - 1, True)

    def count(pred):
        def body(j, acc):
            return acc + _fold(pred(keys_scr[j], j * tk + krow), jnp.add)

        acc = lax.fori_loop(0, nk, body, jnp.zeros((FOLD_ROWS, tq), F32))
        return jnp.sum(acc, axis=0, keepdims=True)

    one16, zero16 = jnp.int16(1), jnp.int16(0)

    def count16(pred):
        def body(j, acc):
            return acc + _fold(pred(k16_scr[j]), jnp.add)

        acc = lax.fori_loop(0, nk, body, jnp.zeros((FOLD_ROWS, tq), I16))
        return jnp.sum(acc.astype(F32), axis=0, keepdims=True)

    def high16(x):
        return jnp.right_shift(x, 16).astype(I16)

    def low16(x):
        return ((x & 0xFFFF) - 32768).astype(I16)

    def kth_largest(k):
        thr = jnp.full((1, tq), INT_MIN, I32)
        thr = _bisect_bits(lambda cand: count16(lambda t: jnp.where(t >= high16(cand), one16, zero16)),
                           thr, 31, 16, k)
        thr_hi = high16(thr)
        n_above = count16(lambda t: jnp.where(t > thr_hi, one16, zero16))

        def low_body(j, carry):
            key = keys_scr[j]
            k16_scr[j] = jnp.where(high16(key) == thr_hi, low16(key), jnp.int16(-32768))
            return carry

        lax.fori_loop(0, nk, low_body, 0)
        return _bisect_bits(
            lambda cand: n_above + count16(lambda t: jnp.where(t >= low16(cand), one16, zero16)),
            thr, 15, 16, k)

    thr, j_lim = _topk_threshold(count, (1, tq), n_sel, int(t_len).bit_length(), t_len, kth_largest)

    m_scr[...] = jnp.full(m_scr.shape, 0.5 * NEG_BIG, F32)
    acc_scr[...] = jnp.zeros_like(acc_scr)

    def attend_tile(j, carry):
        bias = _select_bias(keys_scr[j], thr, j * tk + krow, j_lim)
        bias = jnp.concatenate([bias] * GROUP, axis=1)
        v_t = vt_ref[j]
        heads = range(N_KV_HEADS)
        for n in heads:
            qg = q_ref[n * GROUP:(n + 1) * GROUP].reshape(GROUP * tq, HEAD_DIM)
            lg_scr[n] = _dot_nt(k_ref[n, rows(j), :], qg) + bias
        m_old = [m_scr[n] for n in heads]
        m_new = [jnp.maximum(m_old[n], jnp.max(_fold(lg_scr[n], jnp.maximum), axis=0, keepdims=True))
                 for n in heads]
        for n in heads:
            p_scr[n] = jnp.exp2(lg_scr[n] - m_new[n]).astype(BF16)
        for n in heads:
            acc_scr[n] = (jnp.exp2(m_old[n] - m_new[n]) * acc_scr[n]
                          + _dot(v_t[n * V_ROWS:(n + 1) * V_ROWS, :], p_scr[n]))
            m_scr[n] = m_new[n]
        return carry

    lax.fori_loop(0, nk, attend_tile, 0)
    outs = []
    for n in range(N_KV_HEADS):
        acc = acc_scr[n]
        o_n = acc[:HEAD_DIM] / acc[HEAD_DIM:HEAD_DIM + 1]
        outs += [o_n[:, g * tq:(g + 1) * tq] for g in range(GROUP)]
    o_ref[...] = jnp.concatenate(outs, axis=0).T


def _dsa_prompt(q_h, qi_h, wit, k_h, v_t, ki_b, b, t_len, n_sel, tk):
    tq = min(256, t_len)
    nq = t_len // tq
    nt = t_len // tk

    def heads(h, rows, imap):
        return pl.BlockSpec((h, rows, HEAD_DIM), imap)

    return pl.pallas_call(
        functools.partial(_dsa_kernel, tq=tq, tk=tk, n_sel=n_sel, t_len=t_len),
        grid=(b, nq),
        in_specs=[heads(N_HEADS, tq, lambda bi, qi: (0, bi * nq + qi, 0)),
                  heads(IDX_HEADS, tq, lambda bi, qi: (0, bi * nq + qi, 0)),
                  pl.BlockSpec((8, tq), lambda bi, qi: (0, bi * nq + qi)),
                  heads(N_KV_HEADS, t_len, lambda bi, qi: (0, bi, 0)),
                  pl.BlockSpec((nt, N_KV_HEADS * V_ROWS, tk), lambda bi, qi: (bi, 0, 0)),
                  pl.BlockSpec((t_len, IDX_DIM), lambda bi, qi: (bi, 0))],
        out_specs=pl.BlockSpec((tq, ATTN_WIDTH), lambda bi, qi: (bi * nq + qi, 0)),
        out_shape=jax.ShapeDtypeStruct((b * t_len, ATTN_WIDTH), F32),
        scratch_shapes=[pltpu.VMEM((nt, tk, tq), I32),
                        pltpu.VMEM((nt, tk, tq), I16),
                        pltpu.VMEM((N_KV_HEADS, tk, GROUP * tq), F32),
                        pltpu.VMEM((N_KV_HEADS, tk, GROUP * tq), BF16),
                        pltpu.VMEM((N_KV_HEADS, 1, GROUP * tq), F32),
                        pltpu.VMEM((N_KV_HEADS, V_ROWS, GROUP * tq), F32)],
        compiler_params=_params(("parallel", "arbitrary")),
        name="dsa",
    )(q_h, qi_h, wit, k_h, v_t, ki_b)


DEC_SEQS_PER_STEP = 4


def _dsa_dec_score_kernel(pt_ref, qi_ref, wi_ref, kin_ref, *rest, n_pages, n_seq):
    page_refs = rest[:n_seq * n_pages]
    o_ref = rest[n_seq * n_pages]
    lane = lax.broadcasted_iota(I32, (1, PAGE_SIZE), 1)
    for s in range(n_seq):
        qi = qi_ref[s]
        wi = wi_ref[s]

        def finish(rel):
            sc = jnp.sum(wi * jnp.maximum(rel, 0.0), axis=0, keepdims=True) * IDX_SCALE
            return jnp.where(sc > 0.5 * NEG_BIG, _sort_key(sc), INT_MIN)

        for pg in range(n_pages):
            page = page_refs[s * n_pages + pg][...].astype(BF16)
            o_ref[s, :, pg * PAGE_SIZE:(pg + 1) * PAGE_SIZE] = finish(_dot(qi, page))
        kin = kin_ref[s].astype(BF16).astype(F32)
        key_new = finish(jnp.sum(qi.astype(F32) * kin, axis=1, keepdims=True))
        o_ref[s, :, n_pages * PAGE_SIZE:] = jnp.where(lane == 0, key_new, INT_MIN)


def _page_specs(block, l, n_seq, n_pages):
    def spec(s, pg):
        return pl.BlockSpec(block, lambda bi, pt, *_: (l, pt[bi * n_seq + s, pg], 0, 0))

    return [spec(s, pg) for s in range(n_seq) for pg in range(n_pages)]


def _dsa_dec_scores(page_table, qi8, wi8, ki_new, cache_ki_t, l):
    n, n_pages = page_table.shape
    ns = min(DEC_SEQS_PER_STEP, n)
    width = (n_pages + 1) * PAGE_SIZE
    page_specs = _page_specs((None, None, IDX_DIM, PAGE_SIZE), l, ns, n_pages)
    return pl.pallas_call(
        functools.partial(_dsa_dec_score_kernel, n_pages=n_pages, n_seq=ns),
        grid_spec=pltpu.PrefetchScalarGridSpec(
            num_scalar_prefetch=1,
            grid=(n // ns,),
            in_specs=[pl.BlockSpec((ns, 8, IDX_DIM), lambda bi, pt: (bi, 0, 0)),
                      pl.BlockSpec((ns, 8, 1), lambda bi, pt: (bi, 0, 0)),
                      pl.BlockSpec((ns, 1, IDX_DIM), lambda bi, pt: (bi, 0, 0))] + page_specs,
            out_specs=pl.BlockSpec((ns, 1, width), lambda bi, pt: (bi, 0, 0))),
        out_shape=jax.ShapeDtypeStruct((n, 1, width), I32),
        compiler_params=_params(("arbitrary",)),
        name="dsa_dec_scores",
    )(page_table, qi8, wi8, ki_new, *([cache_ki_t] * (ns * n_pages)))


def _dsa_dec_thr_kernel(keys_ref, thr_ref, j_ref, *, n_sel):
    n, width = keys_ref.shape
    lane = lax.broadcasted_iota(I32, (n, width), 1)

    def count(pred):
        return jnp.sum(_lane_fold(pred(keys_ref[...], lane)), axis=1, keepdims=True)

    thr, j_lim = _topk_threshold(count, (n, 1), n_sel, int(width).bit_length(), width)
    thr_ref[...] = thr
    j_ref[...] = j_lim


def _dsa_dec_thr(keys2d, n_sel):
    n, width = keys2d.shape
    return pl.pallas_call(
        functools.partial(_dsa_dec_thr_kernel, n_sel=n_sel),
        grid=(1,),
        in_specs=[pl.BlockSpec((n, width), lambda i: (0, 0))],
        out_specs=[pl.BlockSpec((n, 1), lambda i: (0, 0)), pl.BlockSpec((n, 1), lambda i: (0, 0))],
        out_shape=[jax.ShapeDtypeStruct((n, 1), I32), jax.ShapeDtypeStruct((n, 1), I32)],
        compiler_params=_params(("arbitrary",)),
        name="dsa_dec_thr",
    )(keys2d)


def _dsa_dec_attn_kernel(pt_ref, thr_ref, j_ref, q_ref, keys_ref, kn_ref, vn_ref, *rest, n_pages, n_seq):
    k_pages = rest[:n_seq * n_pages]
    v_pages = rest[n_seq * n_pages:2 * n_seq * n_pages]
    o_ref, k_scr, v_scr = rest[2 * n_seq * n_pages:]
    n_past = n_pages * PAGE_SIZE
    idx = lax.broadcasted_iota(I32, (1, keys_ref.shape[-1]), 1)
    for s in range(n_seq):
        b = pl.program_id(0) * n_seq + s
        for pg in range(n_pages):
            sl = slice(pg * PAGE_SIZE, (pg + 1) * PAGE_SIZE)
            k_scr[s, :, sl] = k_pages[s * n_pages + pg][...].astype(BF16)
            v_scr[s, :, sl] = v_pages[s * n_pages + pg][...].astype(BF16)
        bias_all = _select_bias(keys_ref[s], thr_ref[b], idx, j_ref[b])
        bias = bias_all[:, :n_past]
        bias_new = bias_all[:, n_past:n_past + 1]
        kn = kn_ref[s].astype(BF16).astype(F32)
        vn = vn_ref[s].astype(BF16).astype(F32)
        for n in range(N_KV_HEADS):
            fs = slice(n * HEAD_DIM, (n + 1) * HEAD_DIM)
            q = q_ref[s, n]
            lg = _dot(q, k_scr[s, fs, :]) + bias
            lg_new = jnp.sum(q.astype(F32) * kn[:, fs], axis=1, keepdims=True) + bias_new
            m = jnp.maximum(jnp.max(lg, axis=1, keepdims=True), lg_new)
            p = jnp.exp2(lg - m)
            p_new = jnp.exp2(lg_new - m)
            den = jnp.sum(p, axis=1, keepdims=True) + p_new
            o_ref[s, n] = (_dot_nt(p.astype(BF16), v_scr[s, fs, :]) + p_new * vn[:, fs]) / den


def _dsa_dec_attn(page_table, thr, j_lim, q8, keys, k_new, v_new, cache_k_t, cache_v_t, l):
    n, n_pages = page_table.shape
    ns = min(DEC_SEQS_PER_STEP, n)
    width = (n_pages + 1) * PAGE_SIZE
    page_specs = _page_specs((None, None, KV_WIDTH, PAGE_SIZE), l, ns, n_pages)
    qspec = pl.BlockSpec((ns, N_KV_HEADS, 8, HEAD_DIM), lambda bi, pt, th, jl: (bi, 0, 0, 0))

    def per_seq(width_):
        return pl.BlockSpec((ns, 1, width_), lambda bi, pt, th, jl: (bi, 0, 0))

    return pl.pallas_call(
        functools.partial(_dsa_dec_attn_kernel, n_pages=n_pages, n_seq=ns),
        grid_spec=pltpu.PrefetchScalarGridSpec(
            num_scalar_prefetch=3,
            grid=(n // ns,),
            in_specs=[qspec, per_seq(width), per_seq(KV_WIDTH), per_seq(KV_WIDTH)] + page_specs + page_specs,
            out_specs=qspec,
            scratch_shapes=[pltpu.VMEM((ns, KV_WIDTH, n_pages * PAGE_SIZE), BF16),
                            pltpu.VMEM((ns, KV_WIDTH, n_pages * PAGE_SIZE), BF16)]),
        out_shape=jax.ShapeDtypeStruct((n, N_KV_HEADS, 8, HEAD_DIM), F32),
        compiler_params=_params(("arbitrary",)),
        name="dsa_dec_attn",
    )(page_table, thr, j_lim, q8, keys, k_new, v_new,
      *([cache_k_t] * (ns * n_pages)), *([cache_v_t] * (ns * n_pages)))


def _merge_kernel(x_ref, g0_ref, g1_ref, g2_ref, ya_ref, yb_ref, yc_ref, gt_ref,
                  wa_ref, wb_ref, wc_ref, wo_ref, o_ref):
    merged = (jax.nn.sigmoid(g0_ref[...]) * _dot(ya_ref[...].astype(BF16), wa_ref[...])
              + jax.nn.sigmoid(g1_ref[...]) * _dot(yb_ref[...].astype(BF16), wb_ref[...])
              + jax.nn.sigmoid(g2_ref[...]) * _dot(yc_ref[...].astype(BF16), wc_ref[...]))
    o_ref[...] = x_ref[...] + gt_ref[...] * _dot(merged.astype(BF16), wo_ref[...])


def _merge(x, p, ya, yb, yc, mods, w_ba, w_bb, w_bc, w_out, l, rows_per_seq):
    n = x.shape[0]
    tm = _row_tile(n, mods, rows_per_seq, 512)

    def wspec(rows):
        return pl.BlockSpec((None, rows, D_MODEL), lambda i: (l, 0, 0))

    def row(width):
        return pl.BlockSpec((tm, width), lambda i: (i, 0))

    def gate(bidx):
        return pl.BlockSpec((tm, D_MODEL), lambda i: (i, OFF_GL // D_MODEL + bidx))

    return pl.pallas_call(
        _merge_kernel,
        grid=(n // tm,),
        in_specs=[row(D_MODEL), gate(0), gate(1), gate(2), row(POOL_WIDTH), row(HGRN_HEADS * HGRN_V),
                  row(ATTN_WIDTH), _mod_spec(mods, l, 5, tm, rows_per_seq),
                  wspec(POOL_WIDTH), wspec(HGRN_HEADS * HGRN_V), wspec(ATTN_WIDTH), wspec(D_MODEL)],
        out_specs=row(D_MODEL),
        out_shape=jax.ShapeDtypeStruct((n, D_MODEL), F32),
        compiler_params=_params(("parallel",)),
        name="merge",
    )(x, p, p, p, ya, yb, yc, mods, w_ba, w_bb, w_bc, w_out)


def _permute_w_in(w_in):
    segs = [(2884, 5956), (256, 768), (768, 1280), (1792, 2304), (0, 256), (1280, 1536), (1536, 1792),
            (2560, 2816), (2304, 2432), (2432, 2560), (2816, 2884)]
    parts = [w_in[..., a:b] for a, b in segs]
    used = sum(b - a for a, b in segs)
    parts.append(jnp.zeros(w_in.shape[:-1] + (IN_PAD - used,), w_in.dtype))
    return jnp.concatenate(parts, axis=-1).astype(BF16)


def _lower_bounds(p):
    sm = jax.nn.softmax(p.astype(F32), axis=0)
    return jnp.cumsum(sm, axis=0) - sm[0:1]


def kernel(x_prompt, x_sample, c_prompt, c_sample, cache_k, cache_v, cache_ki, page_table, state_pool, state_hgrn, ada_w, ada_b, norm_g, ffn_wg, ffn_wu, ffn_wd, w_in, pool_w, pool_scale, hgrn_lb, hgrn_onorm, q_norm, k_norm, w_ba, w_bb, w_bc, w_out):
    bp, t_len, _ = x_prompt.shape
    bs = x_sample.shape[0]
    assert x_sample.shape[1] == 1
    n_pages = page_table.shape[1]
    past_len = n_pages * PAGE_SIZE
    n_phys = cache_k.shape[1]
    n_sel_p = min(TOPK_MAX, t_len // 4)
    n_sel_s = min(TOPK_MAX, (past_len + 1) // 4)

    wg, wu, wd = ffn_wg.astype(BF16), ffn_wu.astype(BF16), ffn_wd.astype(BF16)
    w_in_p = _permute_w_in(w_in)
    wa, wb, wc, wo = w_ba.astype(BF16), w_bb.astype(BF16), w_bc.astype(BF16), w_out.astype(BF16)
    eye = jnp.eye(len(POOL_WINDOWS), dtype=F32)
    pool_wbd = jnp.einsum('lgcd,gh->lgchd', pool_w, eye).reshape(DEPTH, POOL_WIDTH, POOL_WIDTH).astype(BF16)
    pool_sc = pool_scale.reshape(DEPTH, 1, POOL_WIDTH)
    norm_g4 = norm_g.reshape(DEPTH, N_SUB, 1, D_MODEL)
    lbs = _lower_bounds(hgrn_lb)
    qg = jnp.tile(q_norm, (1, LANES // HEAD_DIM)).reshape(DEPTH, 1, LANES)
    kg = jnp.tile(k_norm, (1, LANES // HEAD_DIM)).reshape(DEPTH, 1, LANES)
    gid = np.arange(LANES) // HEAD_DIM
    bd = jnp.asarray((gid[:, None] == gid[None, :]).astype(np.float32), BF16)
    mstack = _hgrn_consts(HGRN_CHUNK)
    cos_p, sin_p = _rope_tables(jnp.arange(t_len, dtype=I32))
    cos_s, sin_s = _rope_tables(jnp.full((bs,), past_len, I32))
    ck = jnp.transpose(cache_k, (0, 1, 3, 4, 2)).reshape(DEPTH, n_phys, KV_WIDTH, PAGE_SIZE)
    cv = jnp.transpose(cache_v, (0, 1, 3, 4, 2)).reshape(DEPTH, n_phys, KV_WIDTH, PAGE_SIZE)
    cki = jnp.transpose(cache_ki, (0, 1, 3, 2))
    state_hgrn_t = jnp.swapaxes(state_hgrn, -1, -2)

    rows_all = bp + bs
    rows_pad = -(-rows_all // 8) * 8
    c_all = jnp.concatenate([c_prompt, c_sample, jnp.zeros((rows_pad - rows_all, D_MODEL), F32)], axis=0)
    mods = _ada(c_all, ada_w, ada_b)
    mods_p = mods[:, :bp].reshape(DEPTH, bp, 1, 3 * N_SUB * D_MODEL)
    mods_s = mods[:, bp:rows_all]

    xp = x_prompt.reshape(bp * t_len, D_MODEL)
    xs = x_sample.reshape(bs, D_MODEL)
    pool_prev_p = jnp.zeros((bp, 16, POOL_WIDTH), F32)
    outs = [[] for _ in range(10)]
    tm_prep = min(512, t_len)

    for l in range(DEPTH):
        xp = _ffn(xp, mods_p, norm_g4, wg, wu, wd, l, 0, 0, t_len)
        p = _inproj(xp, mods_p, norm_g4, w_in_p, l, t_len)
        ya = _pool_prompt(p, pool_prev_p, pool_wbd, pool_sc, l, bp, t_len)
        yb, st_t = _hgrn_prompt(p, lbs[l].reshape(1, -1), hgrn_onorm[l].reshape(1, -1), mstack, bp, t_len)
        q_h, _, k_h, v_t, qi_h, _, ki_b, wit, k_tf, v_tf, ki_tf = _prep(
            p, cos_p, sin_p, qg[l], kg[l], bd, tm_prep, t_len // tm_prep)
        yc = _dsa_prompt(q_h, qi_h, wit, k_h, v_t, ki_b, bp, t_len, n_sel_p, tm_prep)
        xp = _merge(xp, p, ya, yb, yc, mods_p, wa, wb, wc, wo, l, t_len)
        xp = _ffn(xp, mods_p, norm_g4, wg, wu, wd, l, 1, 2, t_len)
        p3 = p.reshape(bp, t_len, IN_PAD)
        outs[0].append(jnp.transpose(k_tf.reshape(bp, N_KV_HEADS, HEAD_DIM, t_len), (0, 3, 1, 2)))
        outs[1].append(jnp.transpose(v_tf.reshape(bp, N_KV_HEADS, HEAD_DIM, t_len), (0, 3, 1, 2)))
        outs[2].append(jnp.swapaxes(ki_tf, 1, 2))
        outs[3].append(p3[:, t_len - POOL_STATE:, OFF_UP:OFF_UP + POOL_WIDTH])
        outs[4].append(jnp.swapaxes(st_t, -1, -2))

        xs = _ffn(xs, mods_s, norm_g4, wg, wu, wd, l, 0, 0, 1)
        ps = _inproj(xs, mods_s, norm_g4, w_in_p, l, 1)
        ya = _pool_dec(ps, jnp.swapaxes(state_pool[l], 0, 1), pool_wbd, pool_sc, l, past_len)
        kw = HGRN_HEADS * HGRN_K
        vw = HGRN_HEADS * HGRN_V
        yb4, st_new = _hgrn_dec(
            ps[:, OFF_HQ:OFF_HQ + kw].reshape(bs, HGRN_HEADS, 1, HGRN_K),
            ps[:, OFF_HF:OFF_HF + kw].reshape(bs, HGRN_HEADS, 1, HGRN_K),
            lbs[l].reshape(HGRN_HEADS, 1, HGRN_K),
            ps[:, OFF_HI:OFF_HI + vw].reshape(bs, HGRN_HEADS, HGRN_V, 1),
            ps[:, OFF_HG:OFF_HG + vw].reshape(bs, HGRN_HEADS, HGRN_V, 1),
            hgrn_onorm[l].reshape(-1, 1), state_hgrn_t, l)
        yb = yb4.reshape(bs, vw)
        q_h, k_r, _, _, qi_h, ki_r, _, _, _, _, _ = _prep(ps, cos_s, sin_s, qg[l], kg[l], bd, bs, 1)
        v_new = ps[:, OFF_AV:OFF_AV + KV_WIDTH]
        qi8 = jnp.pad(jnp.swapaxes(qi_h, 0, 1), ((0, 0), (0, 8 - IDX_HEADS), (0, 0)))
        wi8 = jnp.pad(ps[:, OFF_IKW + IDX_DIM:OFF_IKW + IDX_DIM + IDX_HEADS].reshape(bs, IDX_HEADS, 1),
                      ((0, 0), (0, 8 - IDX_HEADS), (0, 0)))
        keys = _dsa_dec_scores(page_table, qi8, wi8, ki_r.reshape(bs, 1, IDX_DIM), cki, l)
        thr, j_lim = _dsa_dec_thr(keys.reshape(bs, -1), n_sel_s)
        q8 = jnp.pad(jnp.swapaxes(q_h, 0, 1).reshape(bs, N_KV_HEADS, GROUP, HEAD_DIM),
                     ((0, 0), (0, 0), (0, 8 - GROUP), (0, 0)))
        yc8 = _dsa_dec_attn(page_table, thr.reshape(bs), j_lim.reshape(bs), q8, keys,
                            k_r.reshape(bs, 1, KV_WIDTH), v_new.reshape(bs, 1, KV_WIDTH), ck, cv, l)
        yc = yc8[:, :, :GROUP, :].reshape(bs, ATTN_WIDTH)
        xs = _merge(xs, ps, ya, yb, yc, mods_s, wa, wb, wc, wo, l, 1)
        xs = _ffn(xs, mods_s, norm_g4, wg, wu, wd, l, 1, 2, 1)
        outs[5].append(k_r.reshape(bs, 1, N_KV_HEADS, HEAD_DIM))
        outs[6].append(v_new.reshape(bs, 1, N_KV_HEADS, HEAD_DIM))
        outs[7].append(ki_r.reshape(bs, 1, IDX_DIM))
        outs[8].append(jnp.concatenate([state_pool[l][:, 1:], ps[:, None, OFF_UP:OFF_UP + POOL_WIDTH]], axis=1))
        outs[9].append(jnp.swapaxes(st_new, -1, -2))

    stacked = [jnp.stack(o) for o in outs]
    return (xp.reshape(bp, t_len, D_MODEL), xs.reshape(bs, 1, D_MODEL), *stacked)
```

```python
import functools

import numpy as np
import jax
import jax.numpy as jnp
from jax import lax
from jax.experimental import pallas as pl
from jax.experimental.pallas import tpu as pltpu

F32 = jnp.float32
BF16 = jnp.bfloat16
I32 = jnp.int32
I16 = jnp.int16

D_MODEL = 1024
DEPTH = 4
PAGE_SIZE = 128
N_SUB = 3
POOL_WINDOWS = (2, 4, 8, 16)
POOL_GROUP_DIM = 64
POOL_WIDTH = 256
POOL_STATE = 15
HGRN_HEADS = 4
HGRN_K = 128
HGRN_V = 64
F_FLOOR = 1e-30
N_HEADS = 8
N_KV_HEADS = 2
HEAD_DIM = 64
GROUP = N_HEADS // N_KV_HEADS
ATTN_WIDTH = 512
KV_WIDTH = 128
IDX_HEADS = 4
IDX_DIM = 64
TOPK_MAX = 256
ROPE_THETA = 10000.0
LOG2_E = 1.4426950408889634
ATTN_SCALE = HEAD_DIM ** -0.5
IDX_SCALE = (IDX_DIM * IDX_HEADS) ** -0.5
NEG_BIG = -1e30
D_FF = 2816
EPS = 1e-6
INT_MIN = np.int32(-2147483648)

LANES = 128
VMEM_LIMIT = 48 * 1024 * 1024

OFF_GL, OFF_HQ, OFF_HF, OFF_AQ = 0, 3072, 3584, 4096
OFF_UP, OFF_HI, OFF_HG, OFF_IQ = 4608, 4864, 5120, 5376
OFF_AK, OFF_AV, OFF_IKW = 5632, 5760, 5888
IN_PAD = 6144
HGRN_CHUNK = 128
HGRN_CHUNKS_PER_STEP = 4
HGRN_LEVELS = tuple(1 << i for i in range(HGRN_CHUNK.bit_length()))
HGRN_SMALL = tuple(h for h in HGRN_LEVELS if 1 < h < 8)


def _dot(a, b):
    return jnp.dot(a, b, preferred_element_type=F32)


def _dot_nt(a, b):
    return lax.dot_general(a, b, (((1,), (1,)), ((), ())), preferred_element_type=F32)


def _dot_tn(a, b):
    return lax.dot_general(a, b, (((0,), (0,)), ((), ())), preferred_element_type=F32)


def _split2(x):
    hi = x.astype(BF16)
    lo = (x - hi.astype(F32)).astype(BF16)
    return hi, lo


def _split3(x):
    hi = x.astype(BF16)
    r = x - hi.astype(F32)
    mid = r.astype(BF16)
    lo = (r - mid.astype(F32)).astype(BF16)
    return hi, mid, lo


def _silu(x):
    return x * jax.nn.sigmoid(x)


def _params(sem):
    return pltpu.CompilerParams(dimension_semantics=sem, vmem_limit_bytes=VMEM_LIMIT)


def _mod_spec(mods, l, m, tm, rows_per_seq):
    if mods.ndim == 4:
        return pl.BlockSpec((None, None, 1, D_MODEL), lambda i, *_: (l, (i * tm) // rows_per_seq, 0, m))
    return pl.BlockSpec((None, tm, D_MODEL), lambda i, *_: (l, i, m))


def _row_tile(n, mods, rows_per_seq, cap):
    return min(cap, rows_per_seq if mods.ndim == 4 else n)


def _prenorm(x, g, scale, shift):
    y = x * lax.rsqrt(jnp.mean(x * x, axis=-1, keepdims=True) + EPS) * g
    return y * (1.0 + scale) + shift


def _ada_kernel(c_ref, w_ref, b_ref, o_ref):
    c = c_ref[...]
    o_ref[...] = jnp.dot(_silu(c), w_ref[...], precision=lax.Precision.HIGHEST,
                         preferred_element_type=F32) + b_ref[...]


def _ada(c_all, ada_w, ada_b):
    rows = c_all.shape[0]
    width = ada_w.shape[-1]
    tn = 1024
    return pl.pallas_call(
        _ada_kernel,
        grid=(DEPTH, width // tn),
        in_specs=[pl.BlockSpec((rows, D_MODEL), lambda l, j: (0, 0)),
                  pl.BlockSpec((None, D_MODEL, tn), lambda l, j: (l, 0, j)),
                  pl.BlockSpec((None, 1, tn), lambda l, j: (l, 0, j))],
        out_specs=pl.BlockSpec((None, rows, tn), lambda l, j: (l, 0, j)),
        out_shape=jax.ShapeDtypeStruct((DEPTH, rows, width), F32),
        compiler_params=_params(("parallel", "parallel")),
        name="ada",
    )(c_all, ada_w, ada_b.reshape(DEPTH, 1, width))


def _ffn_kernel(x_ref, sh_ref, sc_ref, gt_ref, g_ref, wg_ref, wu_ref, wd_ref, o_ref, h_scr, acc_scr):
    j = pl.program_id(1)

    @pl.when(j == 0)
    def _():
        h_scr[...] = _prenorm(x_ref[...], g_ref[...], sc_ref[...], sh_ref[...]).astype(BF16)
        acc_scr[...] = jnp.zeros_like(acc_scr)

    h = h_scr[...]
    act = (_silu(_dot(h, wg_ref[...])) * _dot(h, wu_ref[...])).astype(BF16)
    acc_scr[...] += _dot(act, wd_ref[...])

    @pl.when(j == pl.num_programs(1) - 1)
    def _():
        o_ref[...] = x_ref[...] + 0.5 * gt_ref[...] * acc_scr[...]


def _ffn(x, mods, norm_g, wg, wu, wd, l, which, sub, rows_per_seq):
    n = x.shape[0]
    tm = _row_tile(n, mods, rows_per_seq, 1024)
    tf = 1408
    return pl.pallas_call(
        _ffn_kernel,
        grid=(n // tm, D_FF // tf),
        in_specs=[pl.BlockSpec((tm, D_MODEL), lambda i, j: (i, 0)),
                  _mod_spec(mods, l, 3 * sub, tm, rows_per_seq),
                  _mod_spec(mods, l, 3 * sub + 1, tm, rows_per_seq),
                  _mod_spec(mods, l, 3 * sub + 2, tm, rows_per_seq),
                  pl.BlockSpec((None, None, 1, D_MODEL), lambda i, j: (l, sub, 0, 0)),
                  pl.BlockSpec((None, None, D_MODEL, tf), lambda i, j: (l, which, 0, j)),
                  pl.BlockSpec((None, None, D_MODEL, tf), lambda i, j: (l, which, 0, j)),
                  pl.BlockSpec((None, None, tf, D_MODEL), lambda i, j: (l, which, j, 0))],
        out_specs=pl.BlockSpec((tm, D_MODEL), lambda i, j: (i, 0)),
        out_shape=jax.ShapeDtypeStruct((n, D_MODEL), F32),
        scratch_shapes=[pltpu.VMEM((tm, D_MODEL), BF16), pltpu.VMEM((tm, D_MODEL), F32)],
        compiler_params=_params(("parallel", "arbitrary")),
        name="ffn",
    )(x, mods, mods, mods, norm_g, wg, wu, wd)


def _inproj_kernel(x_ref, sh_ref, sc_ref, g_ref, w_ref, o_ref, h_scr):
    @pl.when(pl.program_id(1) == 0)
    def _():
        h_scr[...] = _prenorm(x_ref[...], g_ref[...], sc_ref[...], sh_ref[...]).astype(BF16)

    o_ref[...] = _dot_nt(h_scr[...], w_ref[...])


def _inproj(x, mods, norm_g, w_in_t, l, rows_per_seq):
    n = x.shape[0]
    tm = _row_tile(n, mods, rows_per_seq, 1024)
    tn = 1536
    return pl.pallas_call(
        _inproj_kernel,
        grid=(n // tm, IN_PAD // tn),
        in_specs=[pl.BlockSpec((tm, D_MODEL), lambda i, j: (i, 0)),
                  _mod_spec(mods, l, 3, tm, rows_per_seq),
                  _mod_spec(mods, l, 4, tm, rows_per_seq),
                  pl.BlockSpec((None, None, 1, D_MODEL), lambda i, j: (l, 1, 0, 0)),
                  pl.BlockSpec((tn, D_MODEL), lambda i, j: (j, l))],
        out_specs=pl.BlockSpec((tm, tn), lambda i, j: (i, j)),
        out_shape=jax.ShapeDtypeStruct((n, IN_PAD), F32),
        scratch_shapes=[pltpu.VMEM((tm, D_MODEL), BF16)],
        compiler_params=_params(("parallel", "arbitrary")),
        name="inproj",
    )(x, mods, mods, norm_g, w_in_t)


def _prep_kernel(aq_ref, ak_ref, av_ref, iq_ref, ikw_ref, cos_ref, sin_ref, qg_ref, kg_ref, bd_ref,
                 q_o, k_o, kb_o, vt_o, qi_o, ki_o, kib_o, wit_o, ktf_o, vtf_o, kitf_o):
    cos = cos_ref[...]
    sin = sin_ref[...]
    bd = bd_ref[...]
    lane = lax.broadcasted_iota(I32, cos.shape, 1)
    first_half = (lane & (HEAD_DIM // 2)) == 0

    def rope(x):
        partner = jnp.where(first_half, pltpu.roll(x, LANES - HEAD_DIM // 2, 1), pltpu.roll(x, HEAD_DIM // 2, 1))
        return x * cos + partner * sin

    def head_norm(x):
        hi, mid, lo = _split3(x * x)
        ms = (_dot(hi, bd) + _dot(mid, bd) + _dot(lo, bd)) * (1.0 / HEAD_DIM)
        return x * lax.rsqrt(ms + EPS)

    def put_heads(o_ref, first, y):
        for half in range(LANES // HEAD_DIM):
            o_ref[first + half] = y[:, half * HEAD_DIM:(half + 1) * HEAD_DIM].astype(BF16)

    qg = qg_ref[...]
    for c in range(ATTN_WIDTH // LANES):
        sl = slice(c * LANES, (c + 1) * LANES)
        put_heads(q_o, 2 * c, rope(head_norm(aq_ref[:, sl]) * qg) * (ATTN_SCALE * LOG2_E))
    k = rope(head_norm(ak_ref[...]) * kg_ref[...])
    k_o[...] = k
    ktf_o[...] = k.T
    put_heads(kb_o, 0, k)
    v_tf = av_ref[...].T
    vtf_o[...] = v_tf
    v_t = v_tf.astype(BF16)
    ones = jnp.ones((V_ROWS - HEAD_DIM, v_t.shape[1]), BF16)
    vt_o[...] = jnp.concatenate([v_t[:HEAD_DIM], ones, v_t[HEAD_DIM:], ones], axis=0)
    for c in range(IDX_HEADS * IDX_DIM // LANES):
        sl = slice(c * LANES, (c + 1) * LANES)
        put_heads(qi_o, 2 * c, rope(iq_ref[:, sl]))
    ikw = ikw_ref[...]
    ki_wide = rope(head_norm(ikw))
    ki = ki_wide[:, :IDX_DIM]
    ki_o[...] = ki
    kitf_o[...] = ki_wide.T[:IDX_DIM, :]
    kib_o[...] = ki.astype(BF16)
    wit_o[...] = ikw.T[IDX_DIM:IDX_DIM + 8, :]


def _prep(p, cos, sin, qg, kg, bd, tm, table_blocks):
    n = p.shape[0]

    def col(off, width):
        return pl.BlockSpec((tm, width), lambda i: (i, off // width))

    tab = pl.BlockSpec((tm, LANES), lambda i: (i % table_blocks, 0))
    vec = pl.BlockSpec((1, LANES), lambda i: (0, 0))

    def row(width):
        return pl.BlockSpec((tm, width), lambda i: (i, 0))

    def heads(h):
        return pl.BlockSpec((h, tm, HEAD_DIM), lambda i: (0, i, 0))

    n_seq = n // (tm * table_blocks)
    seq_len = tm * table_blocks

    def feature_major(width):
        return pl.BlockSpec((None, width, tm), lambda i: (i // table_blocks, 0, i % table_blocks))

    return pl.pallas_call(
        _prep_kernel,
        grid=(n // tm,),
        in_specs=[col(OFF_AQ, ATTN_WIDTH), col(OFF_AK, KV_WIDTH), col(OFF_AV, KV_WIDTH),
                  col(OFF_IQ, IDX_HEADS * IDX_DIM), col(OFF_IKW, LANES), tab, tab, vec, vec,
                  pl.BlockSpec((LANES, LANES), lambda i: (0, 0))],
        out_specs=[heads(N_HEADS), row(KV_WIDTH), heads(N_KV_HEADS),
                   pl.BlockSpec((None, N_KV_HEADS * V_ROWS, tm), lambda i: (i, 0, 0)),
                   heads(IDX_HEADS), row(IDX_DIM), row(IDX_DIM),
                   pl.BlockSpec((8, tm), lambda i: (0, i)),
                   feature_major(KV_WIDTH), feature_major(KV_WIDTH), feature_major(IDX_DIM)],
        out_shape=[jax.ShapeDtypeStruct((N_HEADS, n, HEAD_DIM), BF16),
                   jax.ShapeDtypeStruct((n, KV_WIDTH), F32),
                   jax.ShapeDtypeStruct((N_KV_HEADS, n, HEAD_DIM), BF16),
                   jax.ShapeDtypeStruct((n // tm, N_KV_HEADS * V_ROWS, tm), BF16),
                   jax.ShapeDtypeStruct((IDX_HEADS, n, IDX_DIM), BF16),
                   jax.ShapeDtypeStruct((n, IDX_DIM), F32),
                   jax.ShapeDtypeStruct((n, IDX_DIM), BF16),
                   jax.ShapeDtypeStruct((8, n), F32),
                   jax.ShapeDtypeStruct((n_seq, KV_WIDTH, seq_len), F32),
                   jax.ShapeDtypeStruct((n_seq, KV_WIDTH, seq_len), F32),
                   jax.ShapeDtypeStruct((n_seq, IDX_DIM, seq_len), F32)],
        compiler_params=_params(("parallel",)),
        name="prep",
    )(p, p, p, p, p, cos, sin, qg, kg, bd)


def _rope_tables(pos):
    half = HEAD_DIM // 2
    inv = ROPE_THETA ** (-jnp.arange(half, dtype=F32) / half)
    ang = pos.astype(F32)[:, None] * inv[None, :]
    c, s = jnp.cos(ang), jnp.sin(ang)
    return jnp.concatenate([c, c, c, c], axis=1), jnp.concatenate([-s, s, -s, s], axis=1)


def _pool_select(sums, lane):
    out = sums[-1]
    for gi in range(len(POOL_WINDOWS) - 2, -1, -1):
        out = jnp.where(lane < (gi + 1) * POOL_GROUP_DIM, sums[gi], out)
    return out


def _pool_kernel(u_ref, prev_ref, w_ref, s_ref, o_ref, carry_scr, *, tt, p0):
    t = pl.program_id(1)

    @pl.when(t == 0)
    def _():
        carry_scr[...] = prev_ref[...]

    u = u_ref[...]
    e = jnp.concatenate([carry_scr[...], u], axis=0)
    carry_scr[...] = u[tt - 16:, :]
    s2 = e[1:] + e[:-1]
    s4 = s2[2:] + s2[:-2]
    s8 = s4[4:] + s4[:-4]
    s16 = s8[8:] + s8[:-8]
    sums = (s2[15:15 + tt], s4[13:13 + tt], s8[9:9 + tt], s16[1:1 + tt])
    n_avail = (p0 + 1 + t * tt + lax.broadcasted_iota(I32, (tt, 1), 0)).astype(F32)
    lane = lax.broadcasted_iota(I32, (tt, POOL_WIDTH), 1)
    means = [sums[gi] / jnp.minimum(n_avail, float(w)) for gi, w in enumerate(POOL_WINDOWS)]
    pooled = _pool_select(means, lane) - u
    o_ref[...] = _dot(pooled.astype(BF16), w_ref[...]) * s_ref[...]


def _pool_prompt(p, prev16, wbd, scale, l, b, t_len):
    tt = min(2048, t_len)
    nt = t_len // tt
    return pl.pallas_call(
        functools.partial(_pool_kernel, tt=tt, p0=0),
        grid=(b, nt),
        in_specs=[pl.BlockSpec((tt, POOL_WIDTH), lambda bi, ti: (bi * nt + ti, OFF_UP // POOL_WIDTH)),
                  pl.BlockSpec((None, 16, POOL_WIDTH), lambda bi, ti: (bi, 0, 0)),
                  pl.BlockSpec((None, POOL_WIDTH, POOL_WIDTH), lambda bi, ti: (l, 0, 0)),
                  pl.BlockSpec((None, 1, POOL_WIDTH), lambda bi, ti: (l, 0, 0))],
        out_specs=pl.BlockSpec((tt, POOL_WIDTH), lambda bi, ti: (bi * nt + ti, 0)),
        out_shape=jax.ShapeDtypeStruct((b * t_len, POOL_WIDTH), F32),
        scratch_shapes=[pltpu.VMEM((16, POOL_WIDTH), F32)],
        compiler_params=_params(("parallel", "arbitrary")),
        name="pool",
    )(p, prev16, wbd, scale)


def _pool_dec_kernel(u_ref, st_ref, w_ref, s_ref, o_ref, *, p0):
    u = u_ref[...]
    lane = lax.broadcasted_iota(I32, u.shape, 1)
    means = []
    acc = u
    d = 1
    for w in POOL_WINDOWS:
        while d < w:
            acc = acc + st_ref[POOL_STATE - d]
            d += 1
        means.append(acc / float(min(p0 + 1, w)))
    pooled = _pool_select(means, lane) - u
    o_ref[...] = _dot(pooled.astype(BF16), w_ref[...]) * s_ref[...]


def _pool_dec(p, st_t, wbd, scale, l, p0):
    n = p.shape[0]
    return pl.pallas_call(
        functools.partial(_pool_dec_kernel, p0=p0),
        grid=(1,),
        in_specs=[pl.BlockSpec((n, POOL_WIDTH), lambda i: (0, OFF_UP // POOL_WIDTH)),
                  pl.BlockSpec((POOL_STATE, n, POOL_WIDTH), lambda i: (0, 0, 0)),
                  pl.BlockSpec((None, POOL_WIDTH, POOL_WIDTH), lambda i: (l, 0, 0)),
                  pl.BlockSpec((None, 1, POOL_WIDTH), lambda i: (l, 0, 0))],
        out_specs=pl.BlockSpec((n, POOL_WIDTH), lambda i: (0, 0)),
        out_shape=jax.ShapeDtypeStruct((n, POOL_WIDTH), F32),
        compiler_params=_params(("arbitrary",)),
        name="pool_dec",
    )(p, st_t, wbd, scale)


def _hgrn_consts(c):
    r = np.arange(c)[:, None]
    s = np.arange(c)[None, :]
    mats = [(s <= r) & (s // h == r // h) for h in HGRN_SMALL]
    mats += [(s > r) & (s // h == r // h) for h in HGRN_SMALL]
    mats.append(s <= r)
    return jnp.asarray(np.concatenate(mats, axis=0).astype(np.float32), BF16)


def _hgrn_kernel(hq_ref, hf_ref, hi_ref, hg_ref, lb_ref, on_ref, ms_ref, yb_ref, st_ref, s_scr, *, c, n_sub):
    step = pl.program_id(1)
    nl = len(HGRN_LEVELS)

    @pl.when(step == 0)
    def _():
        s_scr[...] = jnp.zeros_like(s_scr)

    ti = lax.broadcasted_iota(I32, (c, c), 0)
    si = lax.broadcasted_iota(I32, (c, c), 1)
    pairs = []
    for h in HGRN_LEVELS[:-1]:
        sh = h.bit_length() - 1
        tb = jnp.right_shift(ti, sh)
        pairs.append(((tb & 1) == 1) & (jnp.right_shift(si, sh) == tb - 1))

    kw = HGRN_HEADS * HGRN_K
    for sub in range(n_sub):
        _hgrn_chunk(hq_ref, hf_ref, hi_ref, hg_ref, lb_ref, on_ref, ms_ref, yb_ref, s_scr,
                    slice(sub * c, (sub + 1) * c), c, kw, nl, ti, si, pairs)

    @pl.when(step == pl.num_programs(1) - 1)
    def _():
        st_ref[...] = s_scr[...]


def _hgrn_chunk(hq_ref, hf_ref, hi_ref, hg_ref, lb_ref, on_ref, ms_ref, yb_ref, s_scr,
                rows, c, kw, nl, ti, si, pairs):
    q = hq_ref[rows, :]
    lb = lb_ref[...]
    f = lb + (1.0 - lb) * jax.nn.sigmoid(hf_ref[rows, :])
    g = jnp.log(jnp.maximum(f, F_FLOOR))
    kk = 1.0 - f
    g_hi, g_lo = _split2(g)
    r = _dot(ms_ref[...], jnp.concatenate([g_hi, g_lo], axis=1))
    r = r[:, :kw] + r[:, kw:]
    n_small = len(HGRN_SMALL)
    low = [g] + [r[k * c:(k + 1) * c] for k in range(n_small)]
    up = [None] + [r[(n_small + k) * c:(n_small + k + 1) * c] for k in range(n_small)]
    cum = r[2 * n_small * c:]
    for h in HGRN_LEVELS[1 + n_small:]:
        cum3 = cum.reshape(c // h, h, kw)
        ends = cum3[:, h - 1:h, :]
        if h == c:
            low.append(cum)
        else:
            starts = jnp.concatenate([jnp.zeros((1, 1, kw), F32), ends[:-1]], axis=0)
            low.append((cum3 - starts).reshape(c, kw))
        up.append((ends - cum3).reshape(c, kw))
    kkb = kk.astype(BF16)
    qes = [(q * jnp.exp(low[k])).astype(BF16) for k in range(nl)]
    kes = [kkb] + [(kk * jnp.exp(up[k])).astype(BF16) for k in range(1, nl)]
    qb = q.astype(BF16)
    decay = jnp.exp(cum[c - 1:c, :])

    for hd in range(HGRN_HEADS):
        ks = slice(hd * HGRN_K, (hd + 1) * HGRN_K)
        vs = slice(hd * HGRN_V, (hd + 1) * HGRN_V)
        ivb = hi_ref[rows, vs].astype(BF16)
        att = jnp.where(ti == si, _dot_nt(qb[:, ks], kkb[:, ks]), 0.0)
        for k in range(nl - 1):
            att = jnp.where(pairs[k], _dot_nt(qes[k][:, ks], kes[k][:, ks]), att)
        s_t = s_scr[hd]
        o = _dot(att.astype(BF16), ivb) + _dot_nt(qes[-1][:, ks], s_t.astype(BF16))
        s_scr[hd] = s_t * decay[:, ks] + _dot_tn(ivb, kes[-1][:, ks])

        on = o * lax.rsqrt(jnp.mean(o * o, axis=-1, keepdims=True) + EPS) * on_ref[...]
        yb_ref[rows, vs] = on * _silu(hg_ref[rows, vs])


def _hgrn_prompt(p, lb, onorm, mstack, b, t_len):
    c = HGRN_CHUNK
    n_sub = HGRN_CHUNKS_PER_STEP
    nc = t_len // (c * n_sub)
    kw = HGRN_HEADS * HGRN_K
    vw = HGRN_HEADS * HGRN_V

    def col(off, width):
        return pl.BlockSpec((c * n_sub, width), lambda bi, ci: (bi * nc + ci, off // width))

    return pl.pallas_call(
        functools.partial(_hgrn_kernel, c=c, n_sub=n_sub),
        grid=(b, nc),
        in_specs=[col(OFF_HQ, kw), col(OFF_HF, kw), col(OFF_HI, vw), col(OFF_HG, vw),
                  pl.BlockSpec((1, kw), lambda bi, ci: (0, 0)),
                  pl.BlockSpec((1, HGRN_V), lambda bi, ci: (0, 0)),
                  pl.BlockSpec(mstack.shape, lambda bi, ci: (0, 0))],
        out_specs=[pl.BlockSpec((c * n_sub, vw), lambda bi, ci: (bi * nc + ci, 0)),
                   pl.BlockSpec((None, HGRN_HEADS, HGRN_V, HGRN_K), lambda bi, ci: (bi, 0, 0, 0))],
        out_shape=[jax.ShapeDtypeStruct((b * t_len, vw), F32),
                   jax.ShapeDtypeStruct((b, HGRN_HEADS, HGRN_V, HGRN_K), F32)],
        scratch_shapes=[pltpu.VMEM((HGRN_HEADS, HGRN_V, HGRN_K), F32)],
        compiler_params=_params(("parallel", "arbitrary")),
        name="hgrn",
    )(p, p, p, p, lb, onorm, mstack)


def _hgrn_dec_kernel(q_ref, z_ref, lb_ref, i_ref, hg_ref, on_ref, s_ref, yb_ref, so_ref):
    lb = lb_ref[...]
    f = lb + (1.0 - lb) * jax.nn.sigmoid(z_ref[...])
    a = jnp.exp(jnp.log(jnp.maximum(f, F_FLOOR)))
    s_new = a * s_ref[...] + i_ref[...] * (1.0 - f)
    so_ref[...] = s_new
    o = jnp.sum(q_ref[...] * s_new, axis=-1, keepdims=True)
    on = o * lax.rsqrt(jnp.mean(o * o, axis=-2, keepdims=True) + EPS) * on_ref[...]
    yb_ref[...] = on * _silu(hg_ref[...])


def _hgrn_dec(q_r, z_r, lb_r, i_c, hg_c, onorm_c, state_t, l):
    n = state_t.shape[1]
    nb = 8
    row = pl.BlockSpec((nb, HGRN_HEADS, 1, HGRN_K), lambda i: (i, 0, 0, 0))
    col = pl.BlockSpec((nb, HGRN_HEADS, HGRN_V, 1), lambda i: (i, 0, 0, 0))
    return pl.pallas_call(
        _hgrn_dec_kernel,
        grid=(n // nb,),
        in_specs=[row, row, pl.BlockSpec((HGRN_HEADS, 1, HGRN_K), lambda i: (0, 0, 0)), col, col,
                  pl.BlockSpec((HGRN_V, 1), lambda i: (0, 0)),
                  pl.BlockSpec((None, nb, HGRN_HEADS, HGRN_V, HGRN_K), lambda i: (l, i, 0, 0, 0))],
        out_specs=[col, pl.BlockSpec((nb, HGRN_HEADS, HGRN_V, HGRN_K), lambda i: (i, 0, 0, 0))],
        out_shape=[jax.ShapeDtypeStruct((n, HGRN_HEADS, HGRN_V, 1), F32),
                   jax.ShapeDtypeStruct(state_t.shape[1:], F32)],
        compiler_params=_params(("parallel",)),
        name="hgrn_dec",
    )(q_r, z_r, lb_r, i_c, hg_c, onorm_c, state_t)


def _sort_key(s):
    bits = pltpu.bitcast(s, I32)
    return jnp.where(bits < 0, INT_MIN - bits, bits)


def _lane_fold(x):
    part = x[:, :LANES]
    for cidx in range(1, x.shape[1] // LANES):
        part = part + x[:, cidx * LANES:(cidx + 1) * LANES]
    return part


def _select_bias(key, thr, idx, j_lim):
    return jnp.where(key > thr, 0.0, jnp.where(key == thr, jnp.where(idx < j_lim, 0.0, NEG_BIG), NEG_BIG))


def _bisect_bits(count_ge, thr, first_bit, n_bits, k):
    def bit_body(it, thr):
        cand = thr + lax.shift_left(jnp.int32(1), first_bit - it)
        return jnp.where(count_ge(cand) >= k, cand, thr)

    return lax.fori_loop(0, n_bits, bit_body, thr)


def _topk_threshold(count, stat_shape, n_sel, idx_bits, idx_sentinel, kth_largest=None):
    k = float(n_sel)
    if kth_largest is None:
        thr = _bisect_bits(lambda cand: count(lambda key, idx: jnp.where(key >= cand, 1.0, 0.0)),
                           jnp.full(stat_shape, INT_MIN, I32), 31, 32, k)
    else:
        thr = kth_largest(k)
    n_ge = count(lambda key, idx: jnp.where(key >= thr, 1.0, 0.0))
    n_gt = count(lambda key, idx: jnp.where(key > thr, 1.0, 0.0))
    need = k - n_gt
    live = thr > INT_MIN
    tie = jnp.where(live, jnp.where(n_ge > k, 1.0, 0.0), 0.0)
    j_default = jnp.where(live, jnp.int32(idx_sentinel), jnp.int32(0))

    def tie_search():
        def idx_body(it, j_lim):
            cand = j_lim + lax.shift_left(jnp.int32(1), idx_bits - 1 - it)
            cnt = count(lambda key, idx: jnp.where(key == thr, jnp.where(idx < cand, 1.0, 0.0), 0.0))
            return jnp.where(cnt <= need, cand, j_lim)

        j_lim = lax.fori_loop(0, idx_bits, idx_body, jnp.zeros(stat_shape, I32))
        return jnp.where(tie > 0.0, j_lim, j_default)

    j_lim = lax.cond(jnp.max(tie) > 0.0, tie_search, lambda: j_default)
    return thr, j_lim


FOLD_ROWS = 64
V_ROWS = HEAD_DIM + 16


def _fold(x, op):
    parts = [x[r * FOLD_ROWS:(r + 1) * FOLD_ROWS] for r in range(x.shape[0] // FOLD_ROWS)]
    while len(parts) > 1:
        parts = [op(parts[i], parts[i + 1]) if i + 1 < len(parts) else parts[i]
                 for i in range(0, len(parts), 2)]
    return parts[0]


def _dsa_kernel(q_ref, qi_ref, wit_ref, k_ref, vt_ref, ki_ref, o_ref, keys_scr, k16_scr, lg_scr, p_scr, m_scr, acc_scr,
                *, tq, tk, n_sel, t_len):
    i = pl.program_id(1)
    nk = ((i + 1) * tq + tk - 1) // tk
    krow = lax.broadcasted_iota(I32, (tk, tq), 0)
    qpos = i * tq + lax.broadcasted_iota(I32, (tk, tq), 1)
    wit = wit_ref[...] * IDX_SCALE
    qi_all = qi_ref[...].reshape(IDX_HEADS * tq, IDX_DIM)

    def rows(j):
        return pl.ds(pl.multiple_of(j * tk, tk), tk)

    def score_tile(j, causal):
        rel = _dot_nt(ki_ref[rows(j), :], qi_all)
        s = wit[0:1, :] * jnp.maximum(rel[:, :tq], 0.0)
        for h in range(1, IDX_HEADS):
            s = s + wit[h:h + 1, :] * jnp.maximum(rel[:, h * tq:(h + 1) * tq], 0.0)
        key = jnp.where(s > 0.5 * NEG_BIG, _sort_key(s), INT_MIN)
        if causal:
            key = jnp.where(j * tk + krow <= qpos, key, INT_MIN)
        keys_scr[j] = key
        k16_scr[j] = jnp.right_shift(key, 16).astype(I16)

    def score_body(j, carry):
        score_tile(j, False)
        return carry

    lax.fori_loop(0, nk - 1, score_body, 0)
    score_tile(nk - 1, True)

    def count(pred):
        def body(j, acc):
            return acc + _fold(pred(keys_scr[j], j * tk + krow), jnp.add)

        acc = lax.fori_loop(0, nk, body, jnp.zeros((FOLD_ROWS, tq), F32))
        return jnp.sum(acc, axis=0, keepdims=True)

    one16, zero16 = jnp.int16(1), jnp.int16(0)

    def count16(pred):
        def body(j, acc):
            return acc + _fold(pred(k16_scr[j]), jnp.add)

        acc = lax.fori_loop(0, nk, body, jnp.zeros((FOLD_ROWS, tq), I16))
        return jnp.sum(acc.astype(F32), axis=0, keepdims=True)

    def high16(x):
        return jnp.right_shift(x, 16).astype(I16)

    def low16(x):
        return ((x & 0xFFFF) - 32768).astype(I16)

    def kth_largest(k):
        thr = jnp.full((1, tq), INT_MIN, I32)
        thr = _bisect_bits(lambda cand: count16(lambda t: jnp.where(t >= high16(cand), one16, zero16)),
                           thr, 31, 16, k)
        thr_hi = high16(thr)
        n_above = count16(lambda t: jnp.where(t > thr_hi, one16, zero16))

        def low_body(j, carry):
            key = keys_scr[j]
            k16_scr[j] = jnp.where(high16(key) == thr_hi, low16(key), jnp.int16(-32768))
            return carry

        lax.fori_loop(0, nk, low_body, 0)
        return _bisect_bits(
            lambda cand: n_above + count16(lambda t: jnp.where(t >= low16(cand), one16, zero16)),
            thr, 15, 16, k)

    thr, j_lim = _topk_threshold(count, (1, tq), n_sel, int(t_len).bit_length(), t_len, kth_largest)

    m_scr[...] = jnp.full(m_scr.shape, 0.5 * NEG_BIG, F32)
    acc_scr[...] = jnp.zeros_like(acc_scr)

    def attend_tile(j, carry):
        bias = _select_bias(keys_scr[j], thr, j * tk + krow, j_lim)
        bias = jnp.concatenate([bias] * GROUP, axis=1)
        v_t = vt_ref[j]
        heads = range(N_KV_HEADS)
        for n in heads:
            qg = q_ref[n * GROUP:(n + 1) * GROUP].reshape(GROUP * tq, HEAD_DIM)
            lg_scr[n] = _dot_nt(k_ref[n, rows(j), :], qg) + bias
        m_old = [m_scr[n] for n in heads]
        m_new = [jnp.maximum(m_old[n], jnp.max(_fold(lg_scr[n], jnp.maximum), axis=0, keepdims=True))
                 for n in heads]
        for n in heads:
            p_scr[n] = jnp.exp2(lg_scr[n] - m_new[n]).astype(BF16)
        for n in heads:
            acc_scr[n] = (jnp.exp2(m_old[n] - m_new[n]) * acc_scr[n]
                          + _dot(v_t[n * V_ROWS:(n + 1) * V_ROWS, :], p_scr[n]))
            m_scr[n] = m_new[n]
        return carry

    lax.fori_loop(0, nk, attend_tile, 0)
    outs = []
    for n in range(N_KV_HEADS):
        acc = acc_scr[n]
        o_n = acc[:HEAD_DIM] / acc[HEAD_DIM:HEAD_DIM + 1]
        outs += [o_n[:, g * tq:(g + 1) * tq] for g in range(GROUP)]
    o_ref[...] = jnp.concatenate(outs, axis=0).T


def _dsa_prompt(q_h, qi_h, wit, k_h, v_t, ki_b, b, t_len, n_sel, tk):
    tq = min(256, t_len)
    nq = t_len // tq
    nt = t_len // tk

    def heads(h, rows, imap):
        return pl.BlockSpec((h, rows, HEAD_DIM), imap)

    return pl.pallas_call(
        functools.partial(_dsa_kernel, tq=tq, tk=tk, n_sel=n_sel, t_len=t_len),
        grid=(b, nq),
        in_specs=[heads(N_HEADS, tq, lambda bi, qi: (0, bi * nq + qi, 0)),
                  heads(IDX_HEADS, tq, lambda bi, qi: (0, bi * nq + qi, 0)),
                  pl.BlockSpec((8, tq), lambda bi, qi: (0, bi * nq + qi)),
                  heads(N_KV_HEADS, t_len, lambda bi, qi: (0, bi, 0)),
                  pl.BlockSpec((nt, N_KV_HEADS * V_ROWS, tk), lambda bi, qi: (bi, 0, 0)),
                  pl.BlockSpec((t_len, IDX_DIM), lambda bi, qi: (bi, 0))],
        out_specs=pl.BlockSpec((tq, ATTN_WIDTH), lambda bi, qi: (bi * nq + qi, 0)),
        out_shape=jax.ShapeDtypeStruct((b * t_len, ATTN_WIDTH), F32),
        scratch_shapes=[pltpu.VMEM((nt, tk, tq), I32),
                        pltpu.VMEM((nt, tk, tq), I16),
                        pltpu.VMEM((N_KV_HEADS, tk, GROUP * tq), F32),
                        pltpu.VMEM((N_KV_HEADS, tk, GROUP * tq), BF16),
                        pltpu.VMEM((N_KV_HEADS, 1, GROUP * tq), F32),
                        pltpu.VMEM((N_KV_HEADS, V_ROWS, GROUP * tq), F32)],
        compiler_params=_params(("parallel", "arbitrary")),
        name="dsa",
    )(q_h, qi_h, wit, k_h, v_t, ki_b)


DEC_SEQS_PER_STEP = 4


def _dsa_dec_score_kernel(pt_ref, qi_ref, wi_ref, kin_ref, *rest, n_pages, n_seq):
    page_refs = rest[:n_seq * n_pages]
    o_ref = rest[n_seq * n_pages]
    lane = lax.broadcasted_iota(I32, (1, PAGE_SIZE), 1)
    for s in range(n_seq):
        qi = qi_ref[s]
        wi = wi_ref[s]

        def finish(rel):
            sc = jnp.sum(wi * jnp.maximum(rel, 0.0), axis=0, keepdims=True) * IDX_SCALE
            return jnp.where(sc > 0.5 * NEG_BIG, _sort_key(sc), INT_MIN)

        for pg in range(n_pages):
            page = page_refs[s * n_pages + pg][...].astype(BF16)
            o_ref[s, :, pg * PAGE_SIZE:(pg + 1) * PAGE_SIZE] = finish(_dot(qi, page))
        kin = kin_ref[s].astype(BF16).astype(F32)
        key_new = finish(jnp.sum(qi.astype(F32) * kin, axis=1, keepdims=True))
        o_ref[s, :, n_pages * PAGE_SIZE:] = jnp.where(lane == 0, key_new, INT_MIN)


def _page_specs(block, l, n_seq, n_pages):
    def spec(s, pg):
        return pl.BlockSpec(block, lambda bi, pt, *_: (l, pt[bi * n_seq + s, pg], 0, 0))

    return [spec(s, pg) for s in range(n_seq) for pg in range(n_pages)]


def _dsa_dec_scores(page_table, qi8, wi8, ki_new, cache_ki_t, l):
    n, n_pages = page_table.shape
    ns = min(DEC_SEQS_PER_STEP, n)
    width = (n_pages + 1) * PAGE_SIZE
    page_specs = _page_specs((None, None, IDX_DIM, PAGE_SIZE), l, ns, n_pages)
    return pl.pallas_call(
        functools.partial(_dsa_dec_score_kernel, n_pages=n_pages, n_seq=ns),
        grid_spec=pltpu.PrefetchScalarGridSpec(
            num_scalar_prefetch=1,
            grid=(n // ns,),
            in_specs=[pl.BlockSpec((ns, 8, IDX_DIM), lambda bi, pt: (bi, 0, 0)),
                      pl.BlockSpec((ns, 8, 1), lambda bi, pt: (bi, 0, 0)),
                      pl.BlockSpec((ns, 1, IDX_DIM), lambda bi, pt: (bi, 0, 0))] + page_specs,
            out_specs=pl.BlockSpec((ns, 1, width), lambda bi, pt: (bi, 0, 0))),
        out_shape=jax.ShapeDtypeStruct((n, 1, width), I32),
        compiler_params=_params(("arbitrary",)),
        name="dsa_dec_scores",
    )(page_table, qi8, wi8, ki_new, *([cache_ki_t] * (ns * n_pages)))


def _dsa_dec_thr_kernel(keys_ref, thr_ref, j_ref, *, n_sel):
    n, width = keys_ref.shape
    lane = lax.broadcasted_iota(I32, (n, width), 1)

    def count(pred):
        return jnp.sum(_lane_fold(pred(keys_ref[...], lane)), axis=1, keepdims=True)

    thr, j_lim = _topk_threshold(count, (n, 1), n_sel, int(width).bit_length(), width)
    thr_ref[...] = thr
    j_ref[...] = j_lim


def _dsa_dec_thr(keys2d, n_sel):
    n, width = keys2d.shape
    return pl.pallas_call(
        functools.partial(_dsa_dec_thr_kernel, n_sel=n_sel),
        grid=(1,),
        in_specs=[pl.BlockSpec((n, width), lambda i: (0, 0))],
        out_specs=[pl.BlockSpec((n, 1), lambda i: (0, 0)), pl.BlockSpec((n, 1), lambda i: (0, 0))],
        out_shape=[jax.ShapeDtypeStruct((n, 1), I32), jax.ShapeDtypeStruct((n, 1), I32)],
        compiler_params=_params(("arbitrary",)),
        name="dsa_dec_thr",
    )(keys2d)


def _dsa_dec_attn_kernel(pt_ref, thr_ref, j_ref, q_ref, keys_ref, kn_ref, vn_ref, *rest, n_pages, n_seq):
    k_pages = rest[:n_seq * n_pages]
    v_pages = rest[n_seq * n_pages:2 * n_seq * n_pages]
    o_ref, k_scr, v_scr = rest[2 * n_seq * n_pages:]
    n_past = n_pages * PAGE_SIZE
    idx = lax.broadcasted_iota(I32, (1, keys_ref.shape[-1]), 1)
    for s in range(n_seq):
        b = pl.program_id(0) * n_seq + s
        for pg in range(n_pages):
            sl = slice(pg * PAGE_SIZE, (pg + 1) * PAGE_SIZE)
            k_scr[s, :, sl] = k_pages[s * n_pages + pg][...].astype(BF16)
            v_scr[s, :, sl] = v_pages[s * n_pages + pg][...].astype(BF16)
        bias_all = _select_bias(keys_ref[s], thr_ref[b], idx, j_ref[b])
        bias = bias_all[:, :n_past]
        bias_new = bias_all[:, n_past:n_past + 1]
        kn = kn_ref[s].astype(BF16).astype(F32)
        vn = vn_ref[s].astype(BF16).astype(F32)
        for n in range(N_KV_HEADS):
            fs = slice(n * HEAD_DIM, (n + 1) * HEAD_DIM)
            q = q_ref[s, n]
            lg = _dot(q, k_scr[s, fs, :]) + bias
            lg_new = jnp.sum(q.astype(F32) * kn[:, fs], axis=1, keepdims=True) + bias_new
            m = jnp.maximum(jnp.max(lg, axis=1, keepdims=True), lg_new)
            p = jnp.exp2(lg - m)
            p_new = jnp.exp2(lg_new - m)
            den = jnp.sum(p, axis=1, keepdims=True) + p_new
            o_ref[s, n] = (_dot_nt(p.astype(BF16), v_scr[s, fs, :]) + p_new * vn[:, fs]) / den


def _dsa_dec_attn(page_table, thr, j_lim, q8, keys, k_new, v_new, cache_k_t, cache_v_t, l):
    n, n_pages = page_table.shape
    ns = min(DEC_SEQS_PER_STEP, n)
    width = (n_pages + 1) * PAGE_SIZE
    page_specs = _page_specs((None, None, KV_WIDTH, PAGE_SIZE), l, ns, n_pages)
    qspec = pl.BlockSpec((ns, N_KV_HEADS, 8, HEAD_DIM), lambda bi, pt, th, jl: (bi, 0, 0, 0))

    def per_seq(width_):
        return pl.BlockSpec((ns, 1, width_), lambda bi, pt, th, jl: (bi, 0, 0))

    return pl.pallas_call(
        functools.partial(_dsa_dec_attn_kernel, n_pages=n_pages, n_seq=ns),
        grid_spec=pltpu.PrefetchScalarGridSpec(
            num_scalar_prefetch=3,
            grid=(n // ns,),
            in_specs=[qspec, per_seq(width), per_seq(KV_WIDTH), per_seq(KV_WIDTH)] + page_specs + page_specs,
            out_specs=qspec,
            scratch_shapes=[pltpu.VMEM((ns, KV_WIDTH, n_pages * PAGE_SIZE), BF16),
                            pltpu.VMEM((ns, KV_WIDTH, n_pages * PAGE_SIZE), BF16)]),
        out_shape=jax.ShapeDtypeStruct((n, N_KV_HEADS, 8, HEAD_DIM), F32),
        compiler_params=_params(("arbitrary",)),
        name="dsa_dec_attn",
    )(page_table, thr, j_lim, q8, keys, k_new, v_new,
      *([cache_k_t] * (ns * n_pages)), *([cache_v_t] * (ns * n_pages)))


def _merge_kernel(x_ref, g0_ref, g1_ref, g2_ref, ya_ref, yb_ref, yc_ref, gt_ref,
                  wa_ref, wb_ref, wc_ref, wo_ref, o_ref):
    merged = (jax.nn.sigmoid(g0_ref[...]) * _dot(ya_ref[...].astype(BF16), wa_ref[...])
              + jax.nn.sigmoid(g1_ref[...]) * _dot(yb_ref[...].astype(BF16), wb_ref[...])
              + jax.nn.sigmoid(g2_ref[...]) * _dot(yc_ref[...].astype(BF16), wc_ref[...]))
    o_ref[...] = x_ref[...] + gt_ref[...] * _dot(merged.astype(BF16), wo_ref[...])


def _merge(x, p, ya, yb, yc, mods, w_ba, w_bb, w_bc, w_out, l, rows_per_seq):
    n = x.shape[0]
    tm = _row_tile(n, mods, rows_per_seq, 512)

    def wspec(rows):
        return pl.BlockSpec((None, rows, D_MODEL), lambda i: (l, 0, 0))

    def row(width):
        return pl.BlockSpec((tm, width), lambda i: (i, 0))

    def gate(bidx):
        return pl.BlockSpec((tm, D_MODEL), lambda i: (i, OFF_GL // D_MODEL + bidx))

    return pl.pallas_call(
        _merge_kernel,
        grid=(n // tm,),
        in_specs=[row(D_MODEL), gate(0), gate(1), gate(2), row(POOL_WIDTH), row(HGRN_HEADS * HGRN_V),
                  row(ATTN_WIDTH), _mod_spec(mods, l, 5, tm, rows_per_seq),
                  wspec(POOL_WIDTH), wspec(HGRN_HEADS * HGRN_V), wspec(ATTN_WIDTH), wspec(D_MODEL)],
        out_specs=row(D_MODEL),
        out_shape=jax.ShapeDtypeStruct((n, D_MODEL), F32),
        compiler_params=_params(("parallel",)),
        name="merge",
    )(x, p, p, p, ya, yb, yc, mods, w_ba, w_bb, w_bc, w_out)


def _permute_w_in(w_in):
    segs = [(2884, 5956), (256, 768), (768, 1280), (1792, 2304), (0, 256), (1280, 1536), (1536, 1792),
            (2560, 2816), (2304, 2432), (2432, 2560), (2816, 2884)]
    w_t = jnp.transpose(w_in, (2, 0, 1)).reshape(w_in.shape[-1], DEPTH * D_MODEL)
    parts = [w_t[a:b] for a, b in segs]
    used = sum(b - a for a, b in segs)
    parts.append(jnp.zeros((IN_PAD - used, DEPTH * D_MODEL), w_in.dtype))
    return jnp.concatenate(parts, axis=0).astype(BF16)


def _lower_bounds(p):
    sm = jax.nn.softmax(p.astype(F32), axis=0)
    return jnp.cumsum(sm, axis=0) - sm[0:1]


def kernel(x_prompt, x_sample, c_prompt, c_sample, cache_k, cache_v, cache_ki, page_table, state_pool, state_hgrn, ada_w, ada_b, norm_g, ffn_wg, ffn_wu, ffn_wd, w_in, pool_w, pool_scale, hgrn_lb, hgrn_onorm, q_norm, k_norm, w_ba, w_bb, w_bc, w_out):
    bp, t_len, _ = x_prompt.shape
    bs = x_sample.shape[0]
    assert x_sample.shape[1] == 1
    n_pages = page_table.shape[1]
    past_len = n_pages * PAGE_SIZE
    n_phys = cache_k.shape[1]
    n_sel_p = min(TOPK_MAX, t_len // 4)
    n_sel_s = min(TOPK_MAX, (past_len + 1) // 4)

    wg, wu, wd = ffn_wg.astype(BF16), ffn_wu.astype(BF16), ffn_wd.astype(BF16)
    w_in_p = _permute_w_in(w_in)
    wa, wb, wc, wo = w_ba.astype(BF16), w_bb.astype(BF16), w_bc.astype(BF16), w_out.astype(BF16)
    eye = jnp.eye(len(POOL_WINDOWS), dtype=F32)
    pool_wbd = jnp.einsum('lgcd,gh->lgchd', pool_w, eye).reshape(DEPTH, POOL_WIDTH, POOL_WIDTH).astype(BF16)
    pool_sc = pool_scale.reshape(DEPTH, 1, POOL_WIDTH)
    norm_g4 = norm_g.reshape(DEPTH, N_SUB, 1, D_MODEL)
    lbs = _lower_bounds(hgrn_lb)
    qg = jnp.tile(q_norm, (1, LANES // HEAD_DIM)).reshape(DEPTH, 1, LANES)
    kg = jnp.tile(k_norm, (1, LANES // HEAD_DIM)).reshape(DEPTH, 1, LANES)
    gid = np.arange(LANES) // HEAD_DIM
    bd = jnp.asarray((gid[:, None] == gid[None, :]).astype(np.float32), BF16)
    mstack = _hgrn_consts(HGRN_CHUNK)
    cos_p, sin_p = _rope_tables(jnp.arange(t_len, dtype=I32))
    cos_s, sin_s = _rope_tables(jnp.full((bs,), past_len, I32))
    ck = jnp.transpose(cache_k, (0, 1, 3, 4, 2)).reshape(DEPTH, n_phys, KV_WIDTH, PAGE_SIZE)
    cv = jnp.transpose(cache_v, (0, 1, 3, 4, 2)).reshape(DEPTH, n_phys, KV_WIDTH, PAGE_SIZE)
    cki = jnp.transpose(cache_ki, (0, 1, 3, 2))
    state_hgrn_t = jnp.swapaxes(state_hgrn, -1, -2)

    rows_all = bp + bs
    rows_pad = -(-rows_all // 8) * 8
    c_all = jnp.concatenate([c_prompt, c_sample, jnp.zeros((rows_pad - rows_all, D_MODEL), F32)], axis=0)
    mods = _ada(c_all, ada_w, ada_b)
    mods_p = mods[:, :bp].reshape(DEPTH, bp, 1, 3 * N_SUB * D_MODEL)
    mods_s = mods[:, bp:rows_all]

    xp = x_prompt.reshape(bp * t_len, D_MODEL)
    xs = x_sample.reshape(bs, D_MODEL)
    pool_prev_p = jnp.zeros((bp, 16, POOL_WIDTH), F32)
    outs = [[] for _ in range(10)]
    tm_prep = min(512, t_len)

    for l in range(DEPTH):
        xp = _ffn(xp, mods_p, norm_g4, wg, wu, wd, l, 0, 0, t_len)
        p = _inproj(xp, mods_p, norm_g4, w_in_p, l, t_len)
        ya = _pool_prompt(p, pool_prev_p, pool_wbd, pool_sc, l, bp, t_len)
        yb, st_t = _hgrn_prompt(p, lbs[l].reshape(1, -1), hgrn_onorm[l].reshape(1, -1), mstack, bp, t_len)
        q_h, _, k_h, v_t, qi_h, _, ki_b, wit, k_tf, v_tf, ki_tf = _prep(
            p, cos_p, sin_p, qg[l], kg[l], bd, tm_prep, t_len // tm_prep)
        yc = _dsa_prompt(q_h, qi_h, wit, k_h, v_t, ki_b, bp, t_len, n_sel_p, tm_prep)
        xp = _merge(xp, p, ya, yb, yc, mods_p, wa, wb, wc, wo, l, t_len)
        xp = _ffn(xp, mods_p, norm_g4, wg, wu, wd, l, 1, 2, t_len)
        p3 = p.reshape(bp, t_len, IN_PAD)
        outs[0].append(jnp.transpose(k_tf.reshape(bp, N_KV_HEADS, HEAD_DIM, t_len), (0, 3, 1, 2)))
        outs[1].append(jnp.transpose(v_tf.reshape(bp, N_KV_HEADS, HEAD_DIM, t_len), (0, 3, 1, 2)))
        outs[2].append(jnp.swapaxes(ki_tf, 1, 2))
        outs[3].append(p3[:, t_len - POOL_STATE:, OFF_UP:OFF_UP + POOL_WIDTH])
        outs[4].append(jnp.swapaxes(st_t, -1, -2))

        xs = _ffn(xs, mods_s, norm_g4, wg, wu, wd, l, 0, 0, 1)
        ps = _inproj(xs, mods_s, norm_g4, w_in_p, l, 1)
        ya = _pool_dec(ps, jnp.swapaxes(state_pool[l], 0, 1), pool_wbd, pool_sc, l, past_len)
        kw = HGRN_HEADS * HGRN_K
        vw = HGRN_HEADS * HGRN_V
        yb4, st_new = _hgrn_dec(
            ps[:, OFF_HQ:OFF_HQ + kw].reshape(bs, HGRN_HEADS, 1, HGRN_K),
            ps[:, OFF_HF:OFF_HF + kw].reshape(bs, HGRN_HEADS, 1, HGRN_K),
            lbs[l].reshape(HGRN_HEADS, 1, HGRN_K),
            ps[:, OFF_HI:OFF_HI + vw].reshape(bs, HGRN_HEADS, HGRN_V, 1),
            ps[:, OFF_HG:OFF_HG + vw].reshape(bs, HGRN_HEADS, HGRN_V, 1),
            hgrn_onorm[l].reshape(-1, 1), state_hgrn_t, l)
        yb = yb4.reshape(bs, vw)
        q_h, k_r, _, _, qi_h, ki_r, _, _, _, _, _ = _prep(ps, cos_s, sin_s, qg[l], kg[l], bd, bs, 1)
        v_new = ps[:, OFF_AV:OFF_AV + KV_WIDTH]
        qi8 = jnp.pad(jnp.swapaxes(qi_h, 0, 1), ((0, 0), (0, 8 - IDX_HEADS), (0, 0)))
        wi8 = jnp.pad(ps[:, OFF_IKW + IDX_DIM:OFF_IKW + IDX_DIM + IDX_HEADS].reshape(bs, IDX_HEADS, 1),
                      ((0, 0), (0, 8 - IDX_HEADS), (0, 0)))
        keys = _dsa_dec_scores(page_table, qi8, wi8, ki_r.reshape(bs, 1, IDX_DIM), cki, l)
        thr, j_lim = _dsa_dec_thr(keys.reshape(bs, -1), n_sel_s)
        q8 = jnp.pad(jnp.swapaxes(q_h, 0, 1).reshape(bs, N_KV_HEADS, GROUP, HEAD_DIM),
                     ((0, 0), (0, 0), (0, 8 - GROUP), (0, 0)))
        yc8 = _dsa_dec_attn(page_table, thr.reshape(bs), j_lim.reshape(bs), q8, keys,
                            k_r.reshape(bs, 1, KV_WIDTH), v_new.reshape(bs, 1, KV_WIDTH), ck, cv, l)
        yc = yc8[:, :, :GROUP, :].reshape(bs, ATTN_WIDTH)
        xs = _merge(xs, ps, ya, yb, yc, mods_s, wa, wb, wc, wo, l, 1)
        xs = _ffn(xs, mods_s, norm_g4, wg, wu, wd, l, 1, 2, 1)
        outs[5].append(k_r.reshape(bs, 1, N_KV_HEADS, HEAD_DIM))
        outs[6].append(v_new.reshape(bs, 1, N_KV_HEADS, HEAD_DIM))
        outs[7].append(ki_r.reshape(bs, 1, IDX_DIM))
        outs[8].append(jnp.concatenate([state_pool[l][:, 1:], ps[:, None, OFF_UP:OFF_UP + POOL_WIDTH]], axis=1))
        outs[9].append(jnp.swapaxes(st_new, -1, -2))

    stacked = [jnp.stack(o) for o in outs]
    return (xp.reshape(bp, t_len, D_MODEL), xs.reshape(bs, 1, D_MODEL), *stacked)
```

```python
import functools

import numpy as np
import jax
import jax.numpy as jnp
from jax import lax
from jax.experimental import pallas as pl
from jax.experimental.pallas import tpu as pltpu

F32 = jnp.float32
BF16 = jnp.bfloat16
I32 = jnp.int32
I16 = jnp.int16

D_MODEL = 1024
DEPTH = 4
PAGE_SIZE = 128
N_SUB = 3
POOL_WINDOWS = (2, 4, 8, 16)
POOL_GROUP_DIM = 64
POOL_WIDTH = 256
POOL_STATE = 15
HGRN_HEADS = 4
HGRN_K = 128
HGRN_V = 64
F_FLOOR = 1e-30
N_HEADS = 8
N_KV_HEADS = 2
HEAD_DIM = 64
GROUP = N_HEADS // N_KV_HEADS
ATTN_WIDTH = 512
KV_WIDTH = 128
IDX_HEADS = 4
IDX_DIM = 64
TOPK_MAX = 256
ROPE_THETA = 10000.0
LOG2_E = 1.4426950408889634
ATTN_SCALE = HEAD_DIM ** -0.5
IDX_SCALE = (IDX_DIM * IDX_HEADS) ** -0.5
NEG_BIG = -1e30
D_FF = 2816
EPS = 1e-6
INT_MIN = np.int32(-2147483648)

LANES = 128
VMEM_LIMIT = 48 * 1024 * 1024

ADA_COL_TILE = 1024
FFN_ROW_TILE = 1024
FFN_HIDDEN_TILE = 1408
INPROJ_ROW_TILE = 1024
INPROJ_COL_TILE = 1536
MERGE_ROW_TILE = 512
PREP_ROW_TILE = 512
POOL_ROW_TILE = 2048
DSA_QUERY_TILE = 256
HGRN_DEC_SEQS = 8

OFF_GL, OFF_HQ, OFF_HF, OFF_AQ = 0, 3072, 3584, 4096
OFF_UP, OFF_HI, OFF_HG, OFF_IQ = 4608, 4864, 5120, 5376
OFF_AK, OFF_AV, OFF_IKW = 5632, 5760, 5888
IN_PAD = 6144
HGRN_CHUNK = 128
HGRN_CHUNKS_PER_STEP = 8
HGRN_LEVELS = tuple(1 << i for i in range(HGRN_CHUNK.bit_length()))
HGRN_SMALL = tuple(h for h in HGRN_LEVELS if 1 < h < 8)


def _dot(a, b):
    return jnp.dot(a, b, preferred_element_type=F32)


def _dot_nt(a, b):
    return lax.dot_general(a, b, (((1,), (1,)), ((), ())), preferred_element_type=F32)


def _dot_tn(a, b):
    return lax.dot_general(a, b, (((0,), (0,)), ((), ())), preferred_element_type=F32)


def _split2(x):
    hi = x.astype(BF16)
    lo = (x - hi.astype(F32)).astype(BF16)
    return hi, lo


def _split3(x):
    hi = x.astype(BF16)
    r = x - hi.astype(F32)
    mid = r.astype(BF16)
    lo = (r - mid.astype(F32)).astype(BF16)
    return hi, mid, lo


def _silu(x):
    return x * jax.nn.sigmoid(x)


def _params(sem):
    return pltpu.CompilerParams(dimension_semantics=sem, vmem_limit_bytes=VMEM_LIMIT)


def _mod_spec(mods, l, m, tm, rows_per_seq):
    if mods.ndim == 4:
        return pl.BlockSpec((None, None, 1, D_MODEL), lambda i, *_: (l, (i * tm) // rows_per_seq, 0, m))
    return pl.BlockSpec((None, tm, D_MODEL), lambda i, *_: (l, i, m))


def _row_tile(n, mods, rows_per_seq, cap):
    return min(cap, rows_per_seq if mods.ndim == 4 else n)


def _prenorm(x, g, scale, shift):
    y = x * lax.rsqrt(jnp.mean(x * x, axis=-1, keepdims=True) + EPS) * g
    return y * (1.0 + scale) + shift


def _ada_kernel(c_ref, w_ref, b_ref, o_ref):
    c = c_ref[...]
    o_ref[...] = jnp.dot(_silu(c), w_ref[...], precision=lax.Precision.HIGHEST,
                         preferred_element_type=F32) + b_ref[...]


def _ada(c_all, ada_w, ada_b):
    rows = c_all.shape[0]
    width = ada_w.shape[-1]
    tn = ADA_COL_TILE
    return pl.pallas_call(
        _ada_kernel,
        grid=(DEPTH, width // tn),
        in_specs=[pl.BlockSpec((rows, D_MODEL), lambda l, j: (0, 0)),
                  pl.BlockSpec((None, D_MODEL, tn), lambda l, j: (l, 0, j)),
                  pl.BlockSpec((None, 1, tn), lambda l, j: (l, 0, j))],
        out_specs=pl.BlockSpec((None, rows, tn), lambda l, j: (l, 0, j)),
        out_shape=jax.ShapeDtypeStruct((DEPTH, rows, width), F32),
        compiler_params=_params(("parallel", "parallel")),
        name="ada",
    )(c_all, ada_w, ada_b.reshape(DEPTH, 1, width))


def _ffn_kernel(x_ref, sh_ref, sc_ref, gt_ref, g_ref, wg_ref, wu_ref, wd_ref, o_ref, h_scr, acc_scr):
    j = pl.program_id(1)

    @pl.when(j == 0)
    def _():
        h_scr[...] = _prenorm(x_ref[...], g_ref[...], sc_ref[...], sh_ref[...]).astype(BF16)
        acc_scr[...] = jnp.zeros_like(acc_scr)

    h = h_scr[...]
    act = (_silu(_dot(h, wg_ref[...])) * _dot(h, wu_ref[...])).astype(BF16)
    acc_scr[...] += _dot(act, wd_ref[...])

    @pl.when(j == pl.num_programs(1) - 1)
    def _():
        o_ref[...] = x_ref[...] + 0.5 * gt_ref[...] * acc_scr[...]


def _ffn(x, mods, norm_g, wg, wu, wd, l, which, sub, rows_per_seq):
    n = x.shape[0]
    tm = _row_tile(n, mods, rows_per_seq, FFN_ROW_TILE)
    tf = FFN_HIDDEN_TILE
    return pl.pallas_call(
        _ffn_kernel,
        grid=(n // tm, D_FF // tf),
        in_specs=[pl.BlockSpec((tm, D_MODEL), lambda i, j: (i, 0)),
                  _mod_spec(mods, l, 3 * sub, tm, rows_per_seq),
                  _mod_spec(mods, l, 3 * sub + 1, tm, rows_per_seq),
                  _mod_spec(mods, l, 3 * sub + 2, tm, rows_per_seq),
                  pl.BlockSpec((None, None, 1, D_MODEL), lambda i, j: (l, sub, 0, 0)),
                  pl.BlockSpec((None, None, D_MODEL, tf), lambda i, j: (l, which, 0, j)),
                  pl.BlockSpec((None, None, D_MODEL, tf), lambda i, j: (l, which, 0, j)),
                  pl.BlockSpec((None, None, tf, D_MODEL), lambda i, j: (l, which, j, 0))],
        out_specs=pl.BlockSpec((tm, D_MODEL), lambda i, j: (i, 0)),
        out_shape=jax.ShapeDtypeStruct((n, D_MODEL), F32),
        scratch_shapes=[pltpu.VMEM((tm, D_MODEL), BF16), pltpu.VMEM((tm, D_MODEL), F32)],
        compiler_params=_params(("parallel", "arbitrary")),
        name="ffn",
    )(x, mods, mods, mods, norm_g, wg, wu, wd)


def _inproj_kernel(x_ref, sh_ref, sc_ref, g_ref, w_ref, o_ref, h_scr):
    @pl.when(pl.program_id(1) == 0)
    def _():
        h_scr[...] = _prenorm(x_ref[...], g_ref[...], sc_ref[...], sh_ref[...]).astype(BF16)

    o_ref[...] = _dot(h_scr[...], w_ref[...])


def _inproj(x, mods, norm_g, w_in, l, rows_per_seq):
    n = x.shape[0]
    tm = _row_tile(n, mods, rows_per_seq, INPROJ_ROW_TILE)
    tn = INPROJ_COL_TILE
    return pl.pallas_call(
        _inproj_kernel,
        grid=(n // tm, IN_PAD // tn),
        in_specs=[pl.BlockSpec((tm, D_MODEL), lambda i, j: (i, 0)),
                  _mod_spec(mods, l, 3, tm, rows_per_seq),
                  _mod_spec(mods, l, 4, tm, rows_per_seq),
                  pl.BlockSpec((None, None, 1, D_MODEL), lambda i, j: (l, 1, 0, 0)),
                  pl.BlockSpec((None, D_MODEL, tn), lambda i, j: (l, 0, j))],
        out_specs=pl.BlockSpec((tm, tn), lambda i, j: (i, j)),
        out_shape=jax.ShapeDtypeStruct((n, IN_PAD), F32),
        scratch_shapes=[pltpu.VMEM((tm, D_MODEL), BF16)],
        compiler_params=_params(("parallel", "arbitrary")),
        name="inproj",
    )(x, mods, mods, norm_g, w_in)


def _prep_kernel(aq_ref, ak_ref, av_ref, iq_ref, ikw_ref, cos_ref, sin_ref, qg_ref, kg_ref, bd_ref,
                 q_o, k_o, kb_o, vt_o, qi_o, ki_o, kib_o, wit_o, ktf_o, vtf_o, kitf_o):
    cos = cos_ref[...]
    sin = sin_ref[...]
    bd = bd_ref[...]
    lane = lax.broadcasted_iota(I32, cos.shape, 1)
    first_half = (lane & (HEAD_DIM // 2)) == 0

    def rope(x):
        partner = jnp.where(first_half, pltpu.roll(x, LANES - HEAD_DIM // 2, 1), pltpu.roll(x, HEAD_DIM // 2, 1))
        return x * cos + partner * sin

    def head_norm(x):
        hi, mid, lo = _split3(x * x)
        ms = (_dot(hi, bd) + _dot(mid, bd) + _dot(lo, bd)) * (1.0 / HEAD_DIM)
        return x * lax.rsqrt(ms + EPS)

    def put_heads(o_ref, first, y):
        for half in range(LANES // HEAD_DIM):
            o_ref[first + half] = y[:, half * HEAD_DIM:(half + 1) * HEAD_DIM].astype(BF16)

    qg = qg_ref[...]
    for c in range(ATTN_WIDTH // LANES):
        sl = slice(c * LANES, (c + 1) * LANES)
        put_heads(q_o, 2 * c, rope(head_norm(aq_ref[:, sl]) * qg) * (ATTN_SCALE * LOG2_E))
    k = rope(head_norm(ak_ref[...]) * kg_ref[...])
    k_o[...] = k
    ktf_o[...] = k.T
    put_heads(kb_o, 0, k)
    v_tf = av_ref[...].T
    vtf_o[...] = v_tf
    v_t = v_tf.astype(BF16)
    ones = jnp.ones((V_ROWS - HEAD_DIM, v_t.shape[1]), BF16)
    vt_o[...] = jnp.concatenate([v_t[:HEAD_DIM], ones, v_t[HEAD_DIM:], ones], axis=0)
    for c in range(IDX_HEADS * IDX_DIM // LANES):
        sl = slice(c * LANES, (c + 1) * LANES)
        put_heads(qi_o, 2 * c, rope(iq_ref[:, sl]))
    ikw = ikw_ref[...]
    ki_wide = rope(head_norm(ikw))
    ki = ki_wide[:, :IDX_DIM]
    ki_o[...] = ki
    kitf_o[...] = ki_wide.T[:IDX_DIM, :]
    kib_o[...] = ki.astype(BF16)
    wit_o[...] = ikw.T[IDX_DIM:IDX_DIM + 8, :]


def _prep(p, cos, sin, qg, kg, bd, tm, table_blocks):
    n = p.shape[0]

    def col(off, width):
        return pl.BlockSpec((tm, width), lambda i: (i, off // width))

    tab = pl.BlockSpec((tm, LANES), lambda i: (i % table_blocks, 0))
    vec = pl.BlockSpec((1, LANES), lambda i: (0, 0))

    def row(width):
        return pl.BlockSpec((tm, width), lambda i: (i, 0))

    def heads(h):
        return pl.BlockSpec((h, tm, HEAD_DIM), lambda i: (0, i, 0))

    n_seq = n // (tm * table_blocks)
    seq_len = tm * table_blocks

    def feature_major(width):
        return pl.BlockSpec((None, width, tm), lambda i: (i // table_blocks, 0, i % table_blocks))

    return pl.pallas_call(
        _prep_kernel,
        grid=(n // tm,),
        in_specs=[col(OFF_AQ, ATTN_WIDTH), col(OFF_AK, KV_WIDTH), col(OFF_AV, KV_WIDTH),
                  col(OFF_IQ, IDX_HEADS * IDX_DIM), col(OFF_IKW, LANES), tab, tab, vec, vec,
                  pl.BlockSpec((LANES, LANES), lambda i: (0, 0))],
        out_specs=[heads(N_HEADS), row(KV_WIDTH), heads(N_KV_HEADS),
                   pl.BlockSpec((None, N_KV_HEADS * V_ROWS, tm), lambda i: (i, 0, 0)),
                   heads(IDX_HEADS), row(IDX_DIM), row(IDX_DIM),
                   pl.BlockSpec((8, tm), lambda i: (0, i)),
                   feature_major(KV_WIDTH), feature_major(KV_WIDTH), feature_major(IDX_DIM)],
        out_shape=[jax.ShapeDtypeStruct((N_HEADS, n, HEAD_DIM), BF16),
                   jax.ShapeDtypeStruct((n, KV_WIDTH), F32),
                   jax.ShapeDtypeStruct((N_KV_HEADS, n, HEAD_DIM), BF16),
                   jax.ShapeDtypeStruct((n // tm, N_KV_HEADS * V_ROWS, tm), BF16),
                   jax.ShapeDtypeStruct((IDX_HEADS, n, IDX_DIM), BF16),
                   jax.ShapeDtypeStruct((n, IDX_DIM), F32),
                   jax.ShapeDtypeStruct((n, IDX_DIM), BF16),
                   jax.ShapeDtypeStruct((8, n), F32),
                   jax.ShapeDtypeStruct((n_seq, KV_WIDTH, seq_len), F32),
                   jax.ShapeDtypeStruct((n_seq, KV_WIDTH, seq_len), F32),
                   jax.ShapeDtypeStruct((n_seq, IDX_DIM, seq_len), F32)],
        compiler_params=_params(("parallel",)),
        name="prep",
    )(p, p, p, p, p, cos, sin, qg, kg, bd)


def _rope_tables(pos):
    half = HEAD_DIM // 2
    inv = ROPE_THETA ** (-jnp.arange(half, dtype=F32) / half)
    ang = pos.astype(F32)[:, None] * inv[None, :]
    c, s = jnp.cos(ang), jnp.sin(ang)
    return jnp.concatenate([c, c, c, c], axis=1), jnp.concatenate([-s, s, -s, s], axis=1)


def _pool_select(sums, lane):
    out = sums[-1]
    for gi in range(len(POOL_WINDOWS) - 2, -1, -1):
        out = jnp.where(lane < (gi + 1) * POOL_GROUP_DIM, sums[gi], out)
    return out


def _pool_kernel(u_ref, prev_ref, w_ref, s_ref, o_ref, carry_scr, *, tt, p0):
    t = pl.program_id(1)

    @pl.when(t == 0)
    def _():
        carry_scr[...] = prev_ref[...]

    u = u_ref[...]
    e = jnp.concatenate([carry_scr[...], u], axis=0)
    carry_scr[...] = u[tt - 16:, :]
    s2 = e[1:] + e[:-1]
    s4 = s2[2:] + s2[:-2]
    s8 = s4[4:] + s4[:-4]
    s16 = s8[8:] + s8[:-8]
    sums = (s2[15:15 + tt], s4[13:13 + tt], s8[9:9 + tt], s16[1:1 + tt])
    n_avail = (p0 + 1 + t * tt + lax.broadcasted_iota(I32, (tt, 1), 0)).astype(F32)
    lane = lax.broadcasted_iota(I32, (tt, POOL_WIDTH), 1)
    means = [sums[gi] / jnp.minimum(n_avail, float(w)) for gi, w in enumerate(POOL_WINDOWS)]
    pooled = _pool_select(means, lane) - u
    o_ref[...] = _dot(pooled.astype(BF16), w_ref[...]) * s_ref[...]


def _pool_prompt(p, prev16, wbd, scale, l, b, t_len):
    tt = min(POOL_ROW_TILE, t_len)
    nt = t_len // tt
    return pl.pallas_call(
        functools.partial(_pool_kernel, tt=tt, p0=0),
        grid=(b, nt),
        in_specs=[pl.BlockSpec((tt, POOL_WIDTH), lambda bi, ti: (bi * nt + ti, OFF_UP // POOL_WIDTH)),
                  pl.BlockSpec((None, 16, POOL_WIDTH), lambda bi, ti: (bi, 0, 0)),
                  pl.BlockSpec((None, POOL_WIDTH, POOL_WIDTH), lambda bi, ti: (l, 0, 0)),
                  pl.BlockSpec((None, 1, POOL_WIDTH), lambda bi, ti: (l, 0, 0))],
        out_specs=pl.BlockSpec((tt, POOL_WIDTH), lambda bi, ti: (bi * nt + ti, 0)),
        out_shape=jax.ShapeDtypeStruct((b * t_len, POOL_WIDTH), F32),
        scratch_shapes=[pltpu.VMEM((16, POOL_WIDTH), F32)],
        compiler_params=_params(("parallel", "arbitrary")),
        name="pool",
    )(p, prev16, wbd, scale)


def _pool_dec_kernel(u_ref, st_ref, w_ref, s_ref, o_ref, *, p0):
    u = u_ref[...]
    lane = lax.broadcasted_iota(I32, u.shape, 1)
    means = []
    acc = u
    d = 1
    for w in POOL_WINDOWS:
        while d < w:
            acc = acc + st_ref[POOL_STATE - d]
            d += 1
        means.append(acc / float(min(p0 + 1, w)))
    pooled = _pool_select(means, lane) - u
    o_ref[...] = _dot(pooled.astype(BF16), w_ref[...]) * s_ref[...]


def _pool_dec(p, st_t, wbd, scale, l, p0):
    n = p.shape[0]
    return pl.pallas_call(
        functools.partial(_pool_dec_kernel, p0=p0),
        grid=(1,),
        in_specs=[pl.BlockSpec((n, POOL_WIDTH), lambda i: (0, OFF_UP // POOL_WIDTH)),
                  pl.BlockSpec((POOL_STATE, n, POOL_WIDTH), lambda i: (0, 0, 0)),
                  pl.BlockSpec((None, POOL_WIDTH, POOL_WIDTH), lambda i: (l, 0, 0)),
                  pl.BlockSpec((None, 1, POOL_WIDTH), lambda i: (l, 0, 0))],
        out_specs=pl.BlockSpec((n, POOL_WIDTH), lambda i: (0, 0)),
        out_shape=jax.ShapeDtypeStruct((n, POOL_WIDTH), F32),
        compiler_params=_params(("arbitrary",)),
        name="pool_dec",
    )(p, st_t, wbd, scale)


def _hgrn_consts(c):
    r = np.arange(c)[:, None]
    s = np.arange(c)[None, :]
    mats = [(s <= r) & (s // h == r // h) for h in HGRN_SMALL]
    mats += [(s > r) & (s // h == r // h) for h in HGRN_SMALL]
    mats.append(s <= r)
    return jnp.asarray(np.concatenate(mats, axis=0).astype(np.float32), BF16)


def _hgrn_kernel(hq_ref, hf_ref, hi_ref, hg_ref, lb_ref, on_ref, ms_ref, yb_ref, st_ref, s_scr, *, c, n_sub):
    step = pl.program_id(1)
    nl = len(HGRN_LEVELS)

    @pl.when(step == 0)
    def _():
        s_scr[...] = jnp.zeros_like(s_scr)

    ti = lax.broadcasted_iota(I32, (c, c), 0)
    si = lax.broadcasted_iota(I32, (c, c), 1)
    pairs = []
    for h in HGRN_LEVELS[:-1]:
        sh = h.bit_length() - 1
        tb = jnp.right_shift(ti, sh)
        pairs.append(((tb & 1) == 1) & (jnp.right_shift(si, sh) == tb - 1))

    kw = HGRN_HEADS * HGRN_K
    for sub in range(n_sub):
        _hgrn_chunk(hq_ref, hf_ref, hi_ref, hg_ref, lb_ref, on_ref, ms_ref, yb_ref, s_scr,
                    slice(sub * c, (sub + 1) * c), c, kw, nl, ti, si, pairs)

    @pl.when(step == pl.num_programs(1) - 1)
    def _():
        st_ref[...] = s_scr[...]


def _hgrn_chunk(hq_ref, hf_ref, hi_ref, hg_ref, lb_ref, on_ref, ms_ref, yb_ref, s_scr,
                rows, c, kw, nl, ti, si, pairs):
    q = hq_ref[rows, :]
    lb = lb_ref[...]
    f = lb + (1.0 - lb) * jax.nn.sigmoid(hf_ref[rows, :])
    g = jnp.log(jnp.maximum(f, F_FLOOR))
    kk = 1.0 - f
    g_hi, g_lo = _split2(g)
    r = _dot(ms_ref[...], jnp.concatenate([g_hi, g_lo], axis=1))
    r = r[:, :kw] + r[:, kw:]
    n_small = len(HGRN_SMALL)
    low = [g] + [r[k * c:(k + 1) * c] for k in range(n_small)]
    up = [None] + [r[(n_small + k) * c:(n_small + k + 1) * c] for k in range(n_small)]
    cum = r[2 * n_small * c:]
    for h in HGRN_LEVELS[1 + n_small:]:
        cum3 = cum.reshape(c // h, h, kw)
        ends = cum3[:, h - 1:h, :]
        if h == c:
            low.append(cum)
        else:
            starts = jnp.concatenate([jnp.zeros((1, 1, kw), F32), ends[:-1]], axis=0)
            low.append((cum3 - starts).reshape(c, kw))
        up.append((ends - cum3).reshape(c, kw))
    kkb = kk.astype(BF16)
    qes = [(q * jnp.exp(low[k])).astype(BF16) for k in range(nl)]
    kes = [kkb] + [(kk * jnp.exp(up[k])).astype(BF16) for k in range(1, nl)]
    qb = q.astype(BF16)
    decay = jnp.exp(cum[c - 1:c, :])

    for hd in range(HGRN_HEADS):
        ks = slice(hd * HGRN_K, (hd + 1) * HGRN_K)
        vs = slice(hd * HGRN_V, (hd + 1) * HGRN_V)
        ivb = hi_ref[rows, vs].astype(BF16)
        att = jnp.where(ti == si, _dot_nt(qb[:, ks], kkb[:, ks]), 0.0)
        for k in range(nl - 1):
            att = jnp.where(pairs[k], _dot_nt(qes[k][:, ks], kes[k][:, ks]), att)
        s_t = s_scr[hd]
        o = _dot(att.astype(BF16), ivb) + _dot_nt(qes[-1][:, ks], s_t.astype(BF16))
        s_scr[hd] = s_t * decay[:, ks] + _dot_tn(ivb, kes[-1][:, ks])

        on = o * lax.rsqrt(jnp.mean(o * o, axis=-1, keepdims=True) + EPS) * on_ref[...]
        yb_ref[rows, vs] = on * _silu(hg_ref[rows, vs])


def _hgrn_prompt(p, lb, onorm, mstack, b, t_len):
    c = HGRN_CHUNK
    n_sub = min(HGRN_CHUNKS_PER_STEP, t_len // c)
    nc = t_len // (c * n_sub)
    kw = HGRN_HEADS * HGRN_K
    vw = HGRN_HEADS * HGRN_V

    def col(off, width):
        return pl.BlockSpec((c * n_sub, width), lambda bi, ci: (bi * nc + ci, off // width))

    return pl.pallas_call(
        functools.partial(_hgrn_kernel, c=c, n_sub=n_sub),
        grid=(b, nc),
        in_specs=[col(OFF_HQ, kw), col(OFF_HF, kw), col(OFF_HI, vw), col(OFF_HG, vw),
                  pl.BlockSpec((1, kw), lambda bi, ci: (0, 0)),
                  pl.BlockSpec((1, HGRN_V), lambda bi, ci: (0, 0)),
                  pl.BlockSpec(mstack.shape, lambda bi, ci: (0, 0))],
        out_specs=[pl.BlockSpec((c * n_sub, vw), lambda bi, ci: (bi * nc + ci, 0)),
                   pl.BlockSpec((None, HGRN_HEADS, HGRN_V, HGRN_K), lambda bi, ci: (bi, 0, 0, 0))],
        out_shape=[jax.ShapeDtypeStruct((b * t_len, vw), F32),
                   jax.ShapeDtypeStruct((b, HGRN_HEADS, HGRN_V, HGRN_K), F32)],
        scratch_shapes=[pltpu.VMEM((HGRN_HEADS, HGRN_V, HGRN_K), F32)],
        compiler_params=_params(("parallel", "arbitrary")),
        name="hgrn",
    )(p, p, p, p, lb, onorm, mstack)


def _hgrn_dec_kernel(q_ref, z_ref, lb_ref, i_ref, hg_ref, on_ref, s_ref, yb_ref, so_ref):
    lb = lb_ref[...]
    f = lb + (1.0 - lb) * jax.nn.sigmoid(z_ref[...])
    a = jnp.exp(jnp.log(jnp.maximum(f, F_FLOOR)))
    s_new = a * s_ref[...] + i_ref[...] * (1.0 - f)
    so_ref[...] = s_new
    o = jnp.sum(q_ref[...] * s_new, axis=-1, keepdims=True)
    on = o * lax.rsqrt(jnp.mean(o * o, axis=-2, keepdims=True) + EPS) * on_ref[...]
    yb_ref[...] = on * _silu(hg_ref[...])


def _hgrn_dec(q_r, z_r, lb_r, i_c, hg_c, onorm_c, state_t, l):
    n = state_t.shape[1]
    nb = min(HGRN_DEC_SEQS, n)
    row = pl.BlockSpec((nb, HGRN_HEADS, 1, HGRN_K), lambda i: (i, 0, 0, 0))
    col = pl.BlockSpec((nb, HGRN_HEADS, HGRN_V, 1), lambda i: (i, 0, 0, 0))
    return pl.pallas_call(
        _hgrn_dec_kernel,
        grid=(n // nb,),
        in_specs=[row, row, pl.BlockSpec((HGRN_HEADS, 1, HGRN_K), lambda i: (0, 0, 0)), col, col,
                  pl.BlockSpec((HGRN_V, 1), lambda i: (0, 0)),
                  pl.BlockSpec((None, nb, HGRN_HEADS, HGRN_V, HGRN_K), lambda i: (l, i, 0, 0, 0))],
        out_specs=[col, pl.BlockSpec((nb, HGRN_HEADS, HGRN_V, HGRN_K), lambda i: (i, 0, 0, 0))],
        out_shape=[jax.ShapeDtypeStruct((n, HGRN_HEADS, HGRN_V, 1), F32),
                   jax.ShapeDtypeStruct(state_t.shape[1:], F32)],
        compiler_params=_params(("parallel",)),
        name="hgrn_dec",
    )(q_r, z_r, lb_r, i_c, hg_c, onorm_c, state_t)


def _sort_key(s):
    bits = pltpu.bitcast(s, I32)
    return jnp.where(bits < 0, INT_MIN - bits, bits)


def _lane_fold(x):
    part = x[:, :LANES]
    for cidx in range(1, x.shape[1] // LANES):
        part = part + x[:, cidx * LANES:(cidx + 1) * LANES]
    return part


def _select_bias(key, thr, idx, j_lim):
    return jnp.where(key > thr, 0.0, jnp.where(key == thr, jnp.where(idx < j_lim, 0.0, NEG_BIG), NEG_BIG))


def _bisect_bits(count_ge, thr, first_bit, n_bits, k):
    def bit_body(it, thr):
        cand = thr + lax.shift_left(jnp.int32(1), first_bit - it)
        return jnp.where(count_ge(cand) >= k, cand, thr)

    return lax.fori_loop(0, n_bits, bit_body, thr)


def _topk_threshold(count, stat_shape, n_sel, idx_bits, idx_sentinel, kth_largest=None):
    k = float(n_sel)
    if kth_largest is None:
        thr = _bisect_bits(lambda cand: count(lambda key, idx: jnp.where(key >= cand, 1.0, 0.0)),
                           jnp.full(stat_shape, INT_MIN, I32), 31, 32, k)
    else:
        thr = kth_largest(k)
    n_ge = count(lambda key, idx: jnp.where(key >= thr, 1.0, 0.0))
    n_gt = count(lambda key, idx: jnp.where(key > thr, 1.0, 0.0))
    need = k - n_gt
    live = thr > INT_MIN
    tie = jnp.where(live, jnp.where(n_ge > k, 1.0, 0.0), 0.0)
    j_default = jnp.where(live, jnp.int32(idx_sentinel), jnp.int32(0))

    def tie_search():
        def idx_body(it, j_lim):
            cand = j_lim + lax.shift_left(jnp.int32(1), idx_bits - 1 - it)
            cnt = count(lambda key, idx: jnp.where(key == thr, jnp.where(idx < cand, 1.0, 0.0), 0.0))
            return jnp.where(cnt <= need, cand, j_lim)

        j_lim = lax.fori_loop(0, idx_bits, idx_body, jnp.zeros(stat_shape, I32))
        return jnp.where(tie > 0.0, j_lim, j_default)

    j_lim = lax.cond(jnp.max(tie) > 0.0, tie_search, lambda: j_default)
    return thr, j_lim


FOLD_ROWS = 64
V_ROWS = HEAD_DIM + 16


def _fold(x, op):
    parts = [x[r * FOLD_ROWS:(r + 1) * FOLD_ROWS] for r in range(x.shape[0] // FOLD_ROWS)]
    while len(parts) > 1:
        parts = [op(parts[i], parts[i + 1]) if i + 1 < len(parts) else parts[i]
                 for i in range(0, len(parts), 2)]
    return parts[0]


def _dsa_kernel(q_ref, qi_ref, wit_ref, k_ref, vt_ref, ki_ref, o_ref, keys_scr, k16_scr, lg_scr, p_scr, m_scr, acc_scr,
                *, tq, tk, n_sel, t_len):
    i = pl.program_id(1)
    nk = ((i + 1) * tq + tk - 1) // tk
    krow = lax.broadcasted_iota(I32, (tk, tq), 0)
    qpos = i * tq + lax.broadcasted_iota(I32, (tk, tq), 1)
    wit = wit_ref[...] * IDX_SCALE
    qi_all = qi_ref[...].reshape(IDX_HEADS * tq, IDX_DIM)

    def rows(j):
        return pl.ds(pl.multiple_of(j * tk, tk), tk)

    def score_tile(j, causal):
        rel = _dot_nt(ki_ref[rows(j), :], qi_all)
        s = wit[0:1, :] * jnp.maximum(rel[:, :tq], 0.0)
        for h in range(1, IDX_HEADS):
            s = s + wit[h:h + 1, :] * jnp.maximum(rel[:, h * tq:(h + 1) * tq], 0.0)
        key = jnp.where(s > 0.5 * NEG_BIG, _sort_key(s), INT_MIN)
        if causal:
            key = jnp.where(j * tk + krow <= qpos, key, INT_MIN)
        keys_scr[j] = key
        k16_scr[j] = jnp.right_shift(key, 16).astype(I16)

    def score_body(j, carry):
        score_tile(j, False)
        return carry

    lax.fori_loop(0, nk - 1, score_body, 0)
    score_tile(nk - 1, True)

    def count(pred):
        def body(j, acc):
            return acc + _fold(pred(keys_scr[j], j * tk + krow), jnp.add)

        acc = lax.fori_loop(0, nk, body, jnp.zeros((FOLD_ROWS, tq), F32))
        return jnp.sum(acc, axis=0, keepdims=True)

    one16, zero16 = jnp.int16(1), jnp.int16(0)

    def count16(pred):
        def body(j, acc):
            return acc + _fold(pred(k16_scr[j]), jnp.add)

        acc = lax.fori_loop(0, nk, body, jnp.zeros((FOLD_ROWS, tq), I16))
        return jnp.sum(acc.astype(F32), axis=0, keepdims=True)

    def high16(x):
        return jnp.right_shift(x, 16).astype(I16)

    def low16(x):
        return ((x & 0xFFFF) - 32768).astype(I16)

    def kth_largest(k):
        thr = jnp.full((1, tq), INT_MIN, I32)
        thr = _bisect_bits(lambda cand: count16(lambda t: jnp.where(t >= high16(cand), one16, zero16)),
                           thr, 31, 16, k)
        thr_hi = high16(thr)
        n_above = count16(lambda t: jnp.where(t > thr_hi, one16, zero16))

        def low_body(j, carry):
            key = keys_scr[j]
            k16_scr[j] = jnp.where(high16(key) == thr_hi, low16(key), jnp.int16(-32768))
            return carry

        lax.fori_loop(0, nk, low_body, 0)
        return _bisect_bits(
            lambda cand: n_above + count16(lambda t: jnp.where(t >= low16(cand), one16, zero16)),
            thr, 15, 16, k)

    thr, j_lim = _topk_threshold(count, (1, tq), n_sel, int(t_len).bit_length(), t_len, kth_largest)

    m_scr[...] = jnp.full(m_scr.shape, 0.5 * NEG_BIG, F32)
    acc_scr[...] = jnp.zeros_like(acc_scr)

    def attend_tile(j, carry):
        bias = _select_bias(keys_scr[j], thr, j * tk + krow, j_lim)
        bias = jnp.concatenate([bias] * GROUP, axis=1)
        v_t = vt_ref[j]
        heads = range(N_KV_HEADS)
        for n in heads:
            qg = q_ref[n * GROUP:(n + 1) * GROUP].reshape(GROUP * tq, HEAD_DIM)
            lg_scr[n] = _dot_nt(k_ref[n, rows(j), :], qg) + bias
        m_old = [m_scr[n] for n in heads]
        m_new = [jnp.maximum(m_old[n], jnp.max(_fold(lg_scr[n], jnp.maximum), axis=0, keepdims=True))
                 for n in heads]
        for n in heads:
            p_scr[n] = jnp.exp2(lg_scr[n] - m_new[n]).astype(BF16)
        for n in heads:
            acc_scr[n] = (jnp.exp2(m_old[n] - m_new[n]) * acc_scr[n]
                          + _dot(v_t[n * V_ROWS:(n + 1) * V_ROWS, :], p_scr[n]))
            m_scr[n] = m_new[n]
        return carry

    lax.fori_loop(0, nk, attend_tile, 0)
    outs = []
    for n in range(N_KV_HEADS):
        acc = acc_scr[n]
        o_n = acc[:HEAD_DIM] / acc[HEAD_DIM:HEAD_DIM + 1]
        outs += [o_n[:, g * tq:(g + 1) * tq] for g in range(GROUP)]
    o_ref[...] = jnp.concatenate(outs, axis=0).T


def _dsa_prompt(q_h, qi_h, wit, k_h, v_t, ki_b, b, t_len, n_sel, tk):
    tq = min(DSA_QUERY_TILE, t_len)
    nq = t_len // tq
    nt = t_len // tk

    def heads(h, rows, imap):
        return pl.BlockSpec((h, rows, HEAD_DIM), imap)

    return pl.pallas_call(
        functools.partial(_dsa_kernel, tq=tq, tk=tk, n_sel=n_sel, t_len=t_len),
        grid=(b, nq),
        in_specs=[heads(N_HEADS, tq, lambda bi, qi: (0, bi * nq + qi, 0)),
                  heads(IDX_HEADS, tq, lambda bi, qi: (0, bi * nq + qi, 0)),
                  pl.BlockSpec((8, tq), lambda bi, qi: (0, bi * nq + qi)),
                  heads(N_KV_HEADS, t_len, lambda bi, qi: (0, bi, 0)),
                  pl.BlockSpec((nt, N_KV_HEADS * V_ROWS, tk), lambda bi, qi: (bi, 0, 0)),
                  pl.BlockSpec((t_len, IDX_DIM), lambda bi, qi: (bi, 0))],
        out_specs=pl.BlockSpec((tq, ATTN_WIDTH), lambda bi, qi: (bi * nq + qi, 0)),
        out_shape=jax.ShapeDtypeStruct((b * t_len, ATTN_WIDTH), F32),
        scratch_shapes=[pltpu.VMEM((nt, tk, tq), I32),
                        pltpu.VMEM((nt, tk, tq), I16),
                        pltpu.VMEM((N_KV_HEADS, tk, GROUP * tq), F32),
                        pltpu.VMEM((N_KV_HEADS, tk, GROUP * tq), BF16),
                        pltpu.VMEM((N_KV_HEADS, 1, GROUP * tq), F32),
                        pltpu.VMEM((N_KV_HEADS, V_ROWS, GROUP * tq), F32)],
        compiler_params=_params(("parallel", "arbitrary")),
        name="dsa",
    )(q_h, qi_h, wit, k_h, v_t, ki_b)


DEC_SEQS_PER_STEP = 4


def _dsa_dec_score_kernel(pt_ref, qi_ref, wi_ref, kin_ref, *rest, n_pages, n_seq):
    page_refs = rest[:n_seq * n_pages]
    o_ref = rest[n_seq * n_pages]
    lane = lax.broadcasted_iota(I32, (1, PAGE_SIZE), 1)
    for s in range(n_seq):
        qi = qi_ref[s]
        wi = wi_ref[s]

        def finish(rel):
            sc = jnp.sum(wi * jnp.maximum(rel, 0.0), axis=0, keepdims=True) * IDX_SCALE
            return jnp.where(sc > 0.5 * NEG_BIG, _sort_key(sc), INT_MIN)

        for pg in range(n_pages):
            page = page_refs[s * n_pages + pg][...].astype(BF16)
            o_ref[s, :, pg * PAGE_SIZE:(pg + 1) * PAGE_SIZE] = finish(_dot(qi, page))
        kin = kin_ref[s].astype(BF16).astype(F32)
        key_new = finish(jnp.sum(qi.astype(F32) * kin, axis=1, keepdims=True))
        o_ref[s, :, n_pages * PAGE_SIZE:] = jnp.where(lane == 0, key_new, INT_MIN)


def _page_specs(block, l, n_seq, n_pages):
    def spec(s, pg):
        return pl.BlockSpec(block, lambda bi, pt, *_: (l, pt[bi * n_seq + s, pg], 0, 0))

    return [spec(s, pg) for s in range(n_seq) for pg in range(n_pages)]


def _dsa_dec_scores(page_table, qi8, wi8, ki_new, cache_ki_t, l):
    n, n_pages = page_table.shape
    ns = min(DEC_SEQS_PER_STEP, n)
    width = (n_pages + 1) * PAGE_SIZE
    page_specs = _page_specs((None, None, IDX_DIM, PAGE_SIZE), l, ns, n_pages)
    return pl.pallas_call(
        functools.partial(_dsa_dec_score_kernel, n_pages=n_pages, n_seq=ns),
        grid_spec=pltpu.PrefetchScalarGridSpec(
            num_scalar_prefetch=1,
            grid=(n // ns,),
            in_specs=[pl.BlockSpec((ns, 8, IDX_DIM), lambda bi, pt: (bi, 0, 0)),
                      pl.BlockSpec((ns, 8, 1), lambda bi, pt: (bi, 0, 0)),
                      pl.BlockSpec((ns, 1, IDX_DIM), lambda bi, pt: (bi, 0, 0))] + page_specs,
            out_specs=pl.BlockSpec((ns, 1, width), lambda bi, pt: (bi, 0, 0))),
        out_shape=jax.ShapeDtypeStruct((n, 1, width), I32),
        compiler_params=_params(("arbitrary",)),
        name="dsa_dec_scores",
    )(page_table, qi8, wi8, ki_new, *([cache_ki_t] * (ns * n_pages)))


def _dsa_dec_thr_kernel(keys_ref, thr_ref, j_ref, *, n_sel):
    n, width = keys_ref.shape
    lane = lax.broadcasted_iota(I32, (n, width), 1)

    def count(pred):
        return jnp.sum(_lane_fold(pred(keys_ref[...], lane)), axis=1, keepdims=True)

    thr, j_lim = _topk_threshold(count, (n, 1), n_sel, int(width).bit_length(), width)
    thr_ref[...] = thr
    j_ref[...] = j_lim


def _dsa_dec_thr(keys2d, n_sel):
    n, width = keys2d.shape
    return pl.pallas_call(
        functools.partial(_dsa_dec_thr_kernel, n_sel=n_sel),
        grid=(1,),
        in_specs=[pl.BlockSpec((n, width), lambda i: (0, 0))],
        out_specs=[pl.BlockSpec((n, 1), lambda i: (0, 0)), pl.BlockSpec((n, 1), lambda i: (0, 0))],
        out_shape=[jax.ShapeDtypeStruct((n, 1), I32), jax.ShapeDtypeStruct((n, 1), I32)],
        compiler_params=_params(("arbitrary",)),
        name="dsa_dec_thr",
    )(keys2d)


def _dsa_dec_attn_kernel(pt_ref, thr_ref, j_ref, q_ref, keys_ref, kn_ref, vn_ref, *rest, n_pages, n_seq):
    k_pages = rest[:n_seq * n_pages]
    v_pages = rest[n_seq * n_pages:2 * n_seq * n_pages]
    o_ref, k_scr, v_scr = rest[2 * n_seq * n_pages:]
    n_past = n_pages * PAGE_SIZE
    idx = lax.broadcasted_iota(I32, (1, keys_ref.shape[-1]), 1)
    for s in range(n_seq):
        b = pl.program_id(0) * n_seq + s
        for pg in range(n_pages):
            sl = slice(pg * PAGE_SIZE, (pg + 1) * PAGE_SIZE)
            k_scr[s, :, sl] = k_pages[s * n_pages + pg][...].astype(BF16)
            v_scr[s, :, sl] = v_pages[s * n_pages + pg][...].astype(BF16)
        bias_all = _select_bias(keys_ref[s], thr_ref[b], idx, j_ref[b])
        bias = bias_all[:, :n_past]
        bias_new = bias_all[:, n_past:n_past + 1]
        kn = kn_ref[s].astype(BF16).astype(F32)
        vn = vn_ref[s].astype(BF16).astype(F32)
        for n in range(N_KV_HEADS):
            fs = slice(n * HEAD_DIM, (n + 1) * HEAD_DIM)
            q = q_ref[s, n]
            lg = _dot(q, k_scr[s, fs, :]) + bias
            lg_new = jnp.sum(q.astype(F32) * kn[:, fs], axis=1, keepdims=True) + bias_new
            m = jnp.maximum(jnp.max(lg, axis=1, keepdims=True), lg_new)
            p = jnp.exp2(lg - m)
            p_new = jnp.exp2(lg_new - m)
            den = jnp.sum(p, axis=1, keepdims=True) + p_new
            o_ref[s, n] = (_dot_nt(p.astype(BF16), v_scr[s, fs, :]) + p_new * vn[:, fs]) / den


def _dsa_dec_attn(page_table, thr, j_lim, q8, keys, k_new, v_new, cache_k_t, cache_v_t, l):
    n, n_pages = page_table.shape
    ns = min(DEC_SEQS_PER_STEP, n)
    width = (n_pages + 1) * PAGE_SIZE
    page_specs = _page_specs((None, None, KV_WIDTH, PAGE_SIZE), l, ns, n_pages)
    qspec = pl.BlockSpec((ns, N_KV_HEADS, 8, HEAD_DIM), lambda bi, pt, th, jl: (bi, 0, 0, 0))

    def per_seq(width_):
        return pl.BlockSpec((ns, 1, width_), lambda bi, pt, th, jl: (bi, 0, 0))

    return pl.pallas_call(
        functools.partial(_dsa_dec_attn_kernel, n_pages=n_pages, n_seq=ns),
        grid_spec=pltpu.PrefetchScalarGridSpec(
            num_scalar_prefetch=3,
            grid=(n // ns,),
            in_specs=[qspec, per_seq(width), per_seq(KV_WIDTH), per_seq(KV_WIDTH)] + page_specs + page_specs,
            out_specs=qspec,
            scratch_shapes=[pltpu.VMEM((ns, KV_WIDTH, n_pages * PAGE_SIZE), BF16),
                            pltpu.VMEM((ns, KV_WIDTH, n_pages * PAGE_SIZE), BF16)]),
        out_shape=jax.ShapeDtypeStruct((n, N_KV_HEADS, 8, HEAD_DIM), F32),
        compiler_params=_params(("arbitrary",)),
        name="dsa_dec_attn",
    )(page_table, thr, j_lim, q8, keys, k_new, v_new,
      *([cache_k_t] * (ns * n_pages)), *([cache_v_t] * (ns * n_pages)))


def _merge_kernel(x_ref, g0_ref, g1_ref, g2_ref, ya_ref, yb_ref, yc_ref, gt_ref,
                  wa_ref, wb_ref, wc_ref, wo_ref, o_ref):
    merged = (jax.nn.sigmoid(g0_ref[...]) * _dot(ya_ref[...].astype(BF16), wa_ref[...])
              + jax.nn.sigmoid(g1_ref[...]) * _dot(yb_ref[...].astype(BF16), wb_ref[...])
              + jax.nn.sigmoid(g2_ref[...]) * _dot(yc_ref[...].astype(BF16), wc_ref[...]))
    o_ref[...] = x_ref[...] + gt_ref[...] * _dot(merged.astype(BF16), wo_ref[...])


def _merge(x, p, ya, yb, yc, mods, w_ba, w_bb, w_bc, w_out, l, rows_per_seq):
    n = x.shape[0]
    tm = _row_tile(n, mods, rows_per_seq, MERGE_ROW_TILE)

    def wspec(rows):
        return pl.BlockSpec((None, rows, D_MODEL), lambda i: (l, 0, 0))

    def row(width):
        return pl.BlockSpec((tm, width), lambda i: (i, 0))

    def gate(bidx):
        return pl.BlockSpec((tm, D_MODEL), lambda i: (i, OFF_GL // D_MODEL + bidx))

    return pl.pallas_call(
        _merge_kernel,
        grid=(n // tm,),
        in_specs=[row(D_MODEL), gate(0), gate(1), gate(2), row(POOL_WIDTH), row(HGRN_HEADS * HGRN_V),
                  row(ATTN_WIDTH), _mod_spec(mods, l, 5, tm, rows_per_seq),
                  wspec(POOL_WIDTH), wspec(HGRN_HEADS * HGRN_V), wspec(ATTN_WIDTH), wspec(D_MODEL)],
        out_specs=row(D_MODEL),
        out_shape=jax.ShapeDtypeStruct((n, D_MODEL), F32),
        compiler_params=_params(("parallel",)),
        name="merge",
    )(x, p, p, p, ya, yb, yc, mods, w_ba, w_bb, w_bc, w_out)


def _permute_w_in(w_in):
    segs = [(2884, 5956), (256, 768), (768, 1280), (1792, 2304), (0, 256), (1280, 1536), (1536, 1792),
            (2560, 2816), (2304, 2432), (2432, 2560), (2816, 2884)]
    parts = [w_in[..., a:b] for a, b in segs]
    used = sum(b - a for a, b in segs)
    parts.append(jnp.zeros(w_in.shape[:-1] + (IN_PAD - used,), w_in.dtype))
    return jnp.concatenate(parts, axis=-1).astype(BF16)


def _lower_bounds(p):
    sm = jax.nn.softmax(p.astype(F32), axis=0)
    return jnp.cumsum(sm, axis=0) - sm[0:1]


def kernel(x_prompt, x_sample, c_prompt, c_sample, cache_k, cache_v, cache_ki, page_table, state_pool, state_hgrn, ada_w, ada_b, norm_g, ffn_wg, ffn_wu, ffn_wd, w_in, pool_w, pool_scale, hgrn_lb, hgrn_onorm, q_norm, k_norm, w_ba, w_bb, w_bc, w_out):
    bp, t_len, _ = x_prompt.shape
    bs = x_sample.shape[0]
    assert x_sample.shape[1] == 1
    n_pages = page_table.shape[1]
    past_len = n_pages * PAGE_SIZE
    n_phys = cache_k.shape[1]
    n_sel_p = min(TOPK_MAX, t_len // 4)
    n_sel_s = min(TOPK_MAX, (past_len + 1) // 4)

    wg, wu, wd = ffn_wg.astype(BF16), ffn_wu.astype(BF16), ffn_wd.astype(BF16)
    w_in_p = _permute_w_in(w_in)
    wa, wb, wc, wo = w_ba.astype(BF16), w_bb.astype(BF16), w_bc.astype(BF16), w_out.astype(BF16)
    eye = jnp.eye(len(POOL_WINDOWS), dtype=F32)
    pool_wbd = jnp.einsum('lgcd,gh->lgchd', pool_w, eye).reshape(DEPTH, POOL_WIDTH, POOL_WIDTH).astype(BF16)
    pool_sc = pool_scale.reshape(DEPTH, 1, POOL_WIDTH)
    norm_g4 = norm_g.reshape(DEPTH, N_SUB, 1, D_MODEL)
    lbs = _lower_bounds(hgrn_lb)
    qg = jnp.tile(q_norm, (1, LANES // HEAD_DIM)).reshape(DEPTH, 1, LANES)
    kg = jnp.tile(k_norm, (1, LANES // HEAD_DIM)).reshape(DEPTH, 1, LANES)
    gid = np.arange(LANES) // HEAD_DIM
    bd = jnp.asarray((gid[:, None] == gid[None, :]).astype(np.float32), BF16)
    mstack = _hgrn_consts(HGRN_CHUNK)
    cos_p, sin_p = _rope_tables(jnp.arange(t_len, dtype=I32))
    cos_s, sin_s = _rope_tables(jnp.full((bs,), past_len, I32))
    ck = jnp.transpose(cache_k, (0, 1, 3, 4, 2)).reshape(DEPTH, n_phys, KV_WIDTH, PAGE_SIZE)
    cv = jnp.transpose(cache_v, (0, 1, 3, 4, 2)).reshape(DEPTH, n_phys, KV_WIDTH, PAGE_SIZE)
    cki = jnp.transpose(cache_ki, (0, 1, 3, 2))
    state_hgrn_t = jnp.swapaxes(state_hgrn, -1, -2)

    rows_all = bp + bs
    rows_pad = -(-rows_all // 8) * 8
    c_all = jnp.concatenate([c_prompt, c_sample, jnp.zeros((rows_pad - rows_all, D_MODEL), F32)], axis=0)
    mods = _ada(c_all, ada_w, ada_b)
    mods_p = mods[:, :bp].reshape(DEPTH, bp, 1, 3 * N_SUB * D_MODEL)
    mods_s = mods[:, bp:rows_all]

    xp = x_prompt.reshape(bp * t_len, D_MODEL)
    xs = x_sample.reshape(bs, D_MODEL)
    pool_prev_p = jnp.zeros((bp, 16, POOL_WIDTH), F32)
    outs = [[] for _ in range(10)]
    tm_prep = min(PREP_ROW_TILE, t_len)

    for l in range(DEPTH):
        xp = _ffn(xp, mods_p, norm_g4, wg, wu, wd, l, 0, 0, t_len)
        p = _inproj(xp, mods_p, norm_g4, w_in_p, l, t_len)
        ya = _pool_prompt(p, pool_prev_p, pool_wbd, pool_sc, l, bp, t_len)
        yb, st_t = _hgrn_prompt(p, lbs[l].reshape(1, -1), hgrn_onorm[l].reshape(1, -1), mstack, bp, t_len)
        q_h, _, k_h, v_t, qi_h, _, ki_b, wit, k_tf, v_tf, ki_tf = _prep(
            p, cos_p, sin_p, qg[l], kg[l], bd, tm_prep, t_len // tm_prep)
        yc = _dsa_prompt(q_h, qi_h, wit, k_h, v_t, ki_b, bp, t_len, n_sel_p, tm_prep)
        xp = _merge(xp, p, ya, yb, yc, mods_p, wa, wb, wc, wo, l, t_len)
        xp = _ffn(xp, mods_p, norm_g4, wg, wu, wd, l, 1, 2, t_len)
        p3 = p.reshape(bp, t_len, IN_PAD)
        outs[0].append(jnp.transpose(k_tf.reshape(bp, N_KV_HEADS, HEAD_DIM, t_len), (0, 3, 1, 2)))
        outs[1].append(jnp.transpose(v_tf.reshape(bp, N_KV_HEADS, HEAD_DIM, t_len), (0, 3, 1, 2)))
        outs[2].append(jnp.swapaxes(ki_tf, 1, 2))
        outs[3].append(p3[:, t_len - POOL_STATE:, OFF_UP:OFF_UP + POOL_WIDTH])
        outs[4].append(jnp.swapaxes(st_t, -1, -2))

        xs = _ffn(xs, mods_s, norm_g4, wg, wu, wd, l, 0, 0, 1)
        ps = _inproj(xs, mods_s, norm_g4, w_in_p, l, 1)
        ya = _pool_dec(ps, jnp.swapaxes(state_pool[l], 0, 1), pool_wbd, pool_sc, l, past_len)
        kw = HGRN_HEADS * HGRN_K
        vw = HGRN_HEADS * HGRN_V
        yb4, st_new = _hgrn_dec(
            ps[:, OFF_HQ:OFF_HQ + kw].reshape(bs, HGRN_HEADS, 1, HGRN_K),
            ps[:, OFF_HF:OFF_HF + kw].reshape(bs, HGRN_HEADS, 1, HGRN_K),
            lbs[l].reshape(HGRN_HEADS, 1, HGRN_K),
            ps[:, OFF_HI:OFF_HI + vw].reshape(bs, HGRN_HEADS, HGRN_V, 1),
            ps[:, OFF_HG:OFF_HG + vw].reshape(bs, HGRN_HEADS, HGRN_V, 1),
            hgrn_onorm[l].reshape(-1, 1), state_hgrn_t, l)
        yb = yb4.reshape(bs, vw)
        q_h, k_r, _, _, qi_h, ki_r, _, _, _, _, _ = _prep(ps, cos_s, sin_s, qg[l], kg[l], bd, bs, 1)
        v_new = ps[:, OFF_AV:OFF_AV + KV_WIDTH]
        qi8 = jnp.pad(jnp.swapaxes(qi_h, 0, 1), ((0, 0), (0, 8 - IDX_HEADS), (0, 0)))
        wi8 = jnp.pad(ps[:, OFF_IKW + IDX_DIM:OFF_IKW + IDX_DIM + IDX_HEADS].reshape(bs, IDX_HEADS, 1),
                      ((0, 0), (0, 8 - IDX_HEADS), (0, 0)))
        keys = _dsa_dec_scores(page_table, qi8, wi8, ki_r.reshape(bs, 1, IDX_DIM), cki, l)
        thr, j_lim = _dsa_dec_thr(keys.reshape(bs, -1), n_sel_s)
        q8 = jnp.pad(jnp.swapaxes(q_h, 0, 1).reshape(bs, N_KV_HEADS, GROUP, HEAD_DIM),
                     ((0, 0), (0, 0), (0, 8 - GROUP), (0, 0)))
        yc8 = _dsa_dec_attn(page_table, thr.reshape(bs), j_lim.reshape(bs), q8, keys,
                            k_r.reshape(bs, 1, KV_WIDTH), v_new.reshape(bs, 1, KV_WIDTH), ck, cv, l)
        yc = yc8[:, :, :GROUP, :].reshape(bs, ATTN_WIDTH)
        xs = _merge(xs, ps, ya, yb, yc, mods_s, wa, wb, wc, wo, l, 1)
        xs = _ffn(xs, mods_s, norm_g4, wg, wu, wd, l, 1, 2, 1)
        outs[5].append(k_r.reshape(bs, 1, N_KV_HEADS, HEAD_DIM))
        outs[6].append(v_new.reshape(bs, 1, N_KV_HEADS, HEAD_DIM))
        outs[7].append(ki_r.reshape(bs, 1, IDX_DIM))
        outs[8].append(jnp.concatenate([state_pool[l][:, 1:], ps[:, None, OFF_UP:OFF_UP + POOL_WIDTH]], axis=1))
        outs[9].append(jnp.swapaxes(st_new, -1, -2))

    stacked = [jnp.stack(o) for o in outs]
    return (xp.reshape(bp, t_len, D_MODEL), xs.reshape(bs, 1, D_MODEL), *stacked)
```

```python
import functools

import numpy as np
import jax
import jax.numpy as jnp
from jax import lax
from jax.experimental import pallas as pl
from jax.experimental.pallas import tpu as pltpu

F32 = jnp.float32
BF16 = jnp.bfloat16
I32 = jnp.int32
I16 = jnp.int16

D_MODEL = 1024
DEPTH = 4
PAGE_SIZE = 128
N_SUB = 3
POOL_WINDOWS = (2, 4, 8, 16)
POOL_GROUP_DIM = 64
POOL_WIDTH = 256
POOL_STATE = 15
HGRN_HEADS = 4
HGRN_K = 128
HGRN_V = 64
F_FLOOR = 1e-30
N_HEADS = 8
N_KV_HEADS = 2
HEAD_DIM = 64
GROUP = N_HEADS // N_KV_HEADS
ATTN_WIDTH = 512
KV_WIDTH = 128
IDX_HEADS = 4
IDX_DIM = 64
TOPK_MAX = 256
ROPE_THETA = 10000.0
LOG2_E = 1.4426950408889634
ATTN_SCALE = HEAD_DIM ** -0.5
IDX_SCALE = (IDX_DIM * IDX_HEADS) ** -0.5
NEG_BIG = -1e30
D_FF = 2816
EPS = 1e-6
INT_MIN = np.int32(-2147483648)

LANES = 128
VMEM_LIMIT = 48 * 1024 * 1024

ADA_COL_TILE = 1024
FFN_ROW_TILE = 1024
FFN_HIDDEN_TILE = 1408
INPROJ_ROW_TILE = 1024
INPROJ_COL_TILE = 1536
MERGE_ROW_TILE = 512
PREP_ROW_TILE = 512
POOL_ROW_TILE = 2048
DSA_QUERY_TILE = 256
HGRN_DEC_SEQS = 8

OFF_GL, OFF_HQ, OFF_HF, OFF_AQ = 0, 3072, 3584, 4096
OFF_UP, OFF_HI, OFF_HG, OFF_IQ = 4608, 4864, 5120, 5376
OFF_AK, OFF_AV, OFF_IKW = 5632, 5760, 5888
IN_PAD = 6144
HGRN_CHUNK = 128
HGRN_CHUNKS_PER_STEP = 8
HGRN_LEVELS = tuple(1 << i for i in range(HGRN_CHUNK.bit_length()))
HGRN_SMALL = tuple(h for h in HGRN_LEVELS if 1 < h < 8)


def _dot(a, b):
    return jnp.dot(a, b, preferred_element_type=F32)


def _dot_nt(a, b):
    return lax.dot_general(a, b, (((1,), (1,)), ((), ())), preferred_element_type=F32)


def _dot_tn(a, b):
    return lax.dot_general(a, b, (((0,), (0,)), ((), ())), preferred_element_type=F32)


def _split2(x):
    hi = x.astype(BF16)
    lo = (x - hi.astype(F32)).astype(BF16)
    return hi, lo


def _split3(x):
    hi = x.astype(BF16)
    r = x - hi.astype(F32)
    mid = r.astype(BF16)
    lo = (r - mid.astype(F32)).astype(BF16)
    return hi, mid, lo


def _silu(x):
    return x * jax.nn.sigmoid(x)


def _params(sem):
    return pltpu.CompilerParams(dimension_semantics=sem, vmem_limit_bytes=VMEM_LIMIT)


def _mod_spec(mods, l, m, tm, rows_per_seq):
    if mods.ndim == 4:
        return pl.BlockSpec((None, None, 1, D_MODEL), lambda i, *_: (l, (i * tm) // rows_per_seq, 0, m))
    return pl.BlockSpec((None, tm, D_MODEL), lambda i, *_: (l, i, m))


def _row_tile(n, mods, rows_per_seq, cap):
    return min(cap, rows_per_seq if mods.ndim == 4 else n)


def _prenorm(x, g, scale, shift):
    y = x * lax.rsqrt(jnp.mean(x * x, axis=-1, keepdims=True) + EPS) * g
    return y * (1.0 + scale) + shift


def _ada_kernel(c_ref, w_ref, b_ref, o_ref):
    c = c_ref[...]
    o_ref[...] = jnp.dot(_silu(c), w_ref[...], precision=lax.Precision.HIGHEST,
                         preferred_element_type=F32) + b_ref[...]


def _ada(c_all, ada_w, ada_b):
    rows = c_all.shape[0]
    width = ada_w.shape[-1]
    tn = ADA_COL_TILE
    return pl.pallas_call(
        _ada_kernel,
        grid=(DEPTH, width // tn),
        in_specs=[pl.BlockSpec((rows, D_MODEL), lambda l, j: (0, 0)),
                  pl.BlockSpec((None, D_MODEL, tn), lambda l, j: (l, 0, j)),
                  pl.BlockSpec((None, 1, tn), lambda l, j: (l, 0, j))],
        out_specs=pl.BlockSpec((None, rows, tn), lambda l, j: (l, 0, j)),
        out_shape=jax.ShapeDtypeStruct((DEPTH, rows, width), F32),
        compiler_params=_params(("parallel", "parallel")),
        name="ada",
    )(c_all, ada_w, ada_b.reshape(DEPTH, 1, width))


def _ffn_kernel(x_ref, sh_ref, sc_ref, gt_ref, g_ref, wg_ref, wu_ref, wd_ref, o_ref, h_scr, acc_scr):
    j = pl.program_id(1)

    @pl.when(j == 0)
    def _():
        h_scr[...] = _prenorm(x_ref[...], g_ref[...], sc_ref[...], sh_ref[...]).astype(BF16)
        acc_scr[...] = jnp.zeros_like(acc_scr)

    h = h_scr[...]
    act = (_silu(_dot(h, wg_ref[...])) * _dot(h, wu_ref[...])).astype(BF16)
    acc_scr[...] += _dot(act, wd_ref[...])

    @pl.when(j == pl.num_programs(1) - 1)
    def _():
        o_ref[...] = x_ref[...] + 0.5 * gt_ref[...] * acc_scr[...]


def _ffn(x, mods, norm_g, wg, wu, wd, l, which, sub, rows_per_seq):
    n = x.shape[0]
    tm = _row_tile(n, mods, rows_per_seq, FFN_ROW_TILE)
    tf = FFN_HIDDEN_TILE
    return pl.pallas_call(
        _ffn_kernel,
        grid=(n // tm, D_FF // tf),
        in_specs=[pl.BlockSpec((tm, D_MODEL), lambda i, j: (i, 0)),
                  _mod_spec(mods, l, 3 * sub, tm, rows_per_seq),
                  _mod_spec(mods, l, 3 * sub + 1, tm, rows_per_seq),
                  _mod_spec(mods, l, 3 * sub + 2, tm, rows_per_seq),
                  pl.BlockSpec((None, None, 1, D_MODEL), lambda i, j: (l, sub, 0, 0)),
                  pl.BlockSpec((None, None, D_MODEL, tf), lambda i, j: (l, which, 0, j)),
                  pl.BlockSpec((None, None, D_MODEL, tf), lambda i, j: (l, which, 0, j)),
                  pl.BlockSpec((None, None, tf, D_MODEL), lambda i, j: (l, which, j, 0))],
        out_specs=pl.BlockSpec((tm, D_MODEL), lambda i, j: (i, 0)),
        out_shape=jax.ShapeDtypeStruct((n, D_MODEL), F32),
        scratch_shapes=[pltpu.VMEM((tm, D_MODEL), BF16), pltpu.VMEM((tm, D_MODEL), F32)],
        compiler_params=_params(("parallel", "arbitrary")),
        name="ffn",
    )(x, mods, mods, mods, norm_g, wg, wu, wd)


def _inproj_kernel(x_ref, sh_ref, sc_ref, g_ref, w_ref, o_ref, h_scr):
    @pl.when(pl.program_id(1) == 0)
    def _():
        h_scr[...] = _prenorm(x_ref[...], g_ref[...], sc_ref[...], sh_ref[...]).astype(BF16)

    o_ref[...] = _dot(h_scr[...], w_ref[...])


def _inproj(x, mods, norm_g, w_in, l, rows_per_seq):
    n = x.shape[0]
    tm = _row_tile(n, mods, rows_per_seq, INPROJ_ROW_TILE)
    tn = INPROJ_COL_TILE
    return pl.pallas_call(
        _inproj_kernel,
        grid=(n // tm, IN_PAD // tn),
        in_specs=[pl.BlockSpec((tm, D_MODEL), lambda i, j: (i, 0)),
                  _mod_spec(mods, l, 3, tm, rows_per_seq),
                  _mod_spec(mods, l, 4, tm, rows_per_seq),
                  pl.BlockSpec((None, None, 1, D_MODEL), lambda i, j: (l, 1, 0, 0)),
                  pl.BlockSpec((None, D_MODEL, tn), lambda i, j: (l, 0, j))],
        out_specs=pl.BlockSpec((tm, tn), lambda i, j: (i, j)),
        out_shape=jax.ShapeDtypeStruct((n, IN_PAD), F32),
        scratch_shapes=[pltpu.VMEM((tm, D_MODEL), BF16)],
        compiler_params=_params(("parallel", "arbitrary")),
        name="inproj",
    )(x, mods, mods, norm_g, w_in)


def _prep_kernel(aq_ref, ak_ref, av_ref, iq_ref, ikw_ref, cos_ref, sin_ref, qg_ref, kg_ref, bd_ref,
                 q_o, k_o, kb_o, vt_o, qi_o, ki_o, kib_o, wit_o, ktf_o, vtf_o, kitf_o):
    cos = cos_ref[...]
    sin = sin_ref[...]
    bd = bd_ref[...]
    lane = lax.broadcasted_iota(I32, cos.shape, 1)
    first_half = (lane & (HEAD_DIM // 2)) == 0

    def rope(x):
        partner = jnp.where(first_half, pltpu.roll(x, LANES - HEAD_DIM // 2, 1), pltpu.roll(x, HEAD_DIM // 2, 1))
        return x * cos + partner * sin

    def head_norm(x):
        hi, mid, lo = _split3(x * x)
        ms = (_dot(hi, bd) + _dot(mid, bd) + _dot(lo, bd)) * (1.0 / HEAD_DIM)
        return x * lax.rsqrt(ms + EPS)

    def put_heads(o_ref, first, y):
        for half in range(LANES // HEAD_DIM):
            o_ref[first + half] = y[:, half * HEAD_DIM:(half + 1) * HEAD_DIM].astype(BF16)

    qg = qg_ref[...]
    for c in range(ATTN_WIDTH // LANES):
        sl = slice(c * LANES, (c + 1) * LANES)
        put_heads(q_o, 2 * c, rope(head_norm(aq_ref[:, sl]) * qg) * (ATTN_SCALE * LOG2_E))
    k = rope(head_norm(ak_ref[...]) * kg_ref[...])
    k_o[...] = k
    ktf_o[...] = k.T
    put_heads(kb_o, 0, k)
    v_tf = av_ref[...].T
    vtf_o[...] = v_tf
    v_t = v_tf.astype(BF16)
    ones = jnp.ones((V_ROWS - HEAD_DIM, v_t.shape[1]), BF16)
    vt_o[...] = jnp.concatenate([v_t[:HEAD_DIM], ones, v_t[HEAD_DIM:], ones], axis=0)
    for c in range(IDX_HEADS * IDX_DIM // LANES):
        sl = slice(c * LANES, (c + 1) * LANES)
        put_heads(qi_o, 2 * c, rope(iq_ref[:, sl]))
    ikw = ikw_ref[...]
    ki_wide = rope(head_norm(ikw))
    ki = ki_wide[:, :IDX_DIM]
    ki_o[...] = ki
    kitf_o[...] = ki_wide.T[:IDX_DIM, :]
    kib_o[...] = ki.astype(BF16)
    wit_o[...] = ikw.T[IDX_DIM:IDX_DIM + 8, :]


def _prep(p, cos, sin, qg, kg, bd, tm, table_blocks):
    n = p.shape[0]

    def col(off, width):
        return pl.BlockSpec((tm, width), lambda i: (i, off // width))

    tab = pl.BlockSpec((tm, LANES), lambda i: (i % table_blocks, 0))
    vec = pl.BlockSpec((1, LANES), lambda i: (0, 0))

    def row(width):
        return pl.BlockSpec((tm, width), lambda i: (i, 0))

    def heads(h):
        return pl.BlockSpec((h, tm, HEAD_DIM), lambda i: (0, i, 0))

    n_seq = n // (tm * table_blocks)
    seq_len = tm * table_blocks

    def feature_major(width):
        return pl.BlockSpec((None, width, tm), lambda i: (i // table_blocks, 0, i % table_blocks))

    return pl.pallas_call(
        _prep_kernel,
        grid=(n // tm,),
        in_specs=[col(OFF_AQ, ATTN_WIDTH), col(OFF_AK, KV_WIDTH), col(OFF_AV, KV_WIDTH),
                  col(OFF_IQ, IDX_HEADS * IDX_DIM), col(OFF_IKW, LANES), tab, tab, vec, vec,
                  pl.BlockSpec((LANES, LANES), lambda i: (0, 0))],
        out_specs=[heads(N_HEADS), row(KV_WIDTH), heads(N_KV_HEADS),
                   pl.BlockSpec((None, N_KV_HEADS * V_ROWS, tm), lambda i: (i, 0, 0)),
                   heads(IDX_HEADS), row(IDX_DIM), row(IDX_DIM),
                   pl.BlockSpec((8, tm), lambda i: (0, i)),
                   feature_major(KV_WIDTH), feature_major(KV_WIDTH), feature_major(IDX_DIM)],
        out_shape=[jax.ShapeDtypeStruct((N_HEADS, n, HEAD_DIM), BF16),
                   jax.ShapeDtypeStruct((n, KV_WIDTH), F32),
                   jax.ShapeDtypeStruct((N_KV_HEADS, n, HEAD_DIM), BF16),
                   jax.ShapeDtypeStruct((n // tm, N_KV_HEADS * V_ROWS, tm), BF16),
                   jax.ShapeDtypeStruct((IDX_HEADS, n, IDX_DIM), BF16),
                   jax.ShapeDtypeStruct((n, IDX_DIM), F32),
                   jax.ShapeDtypeStruct((n, IDX_DIM), BF16),
                   jax.ShapeDtypeStruct((8, n), F32),
                   jax.ShapeDtypeStruct((n_seq, KV_WIDTH, seq_len), F32),
                   jax.ShapeDtypeStruct((n_seq, KV_WIDTH, seq_len), F32),
                   jax.ShapeDtypeStruct((n_seq, IDX_DIM, seq_len), F32)],
        compiler_params=_params(("parallel",)),
        name="prep",
    )(p, p, p, p, p, cos, sin, qg, kg, bd)


def _rope_tables(pos):
    half = HEAD_DIM // 2
    inv = ROPE_THETA ** (-jnp.arange(half, dtype=F32) / half)
    ang = pos.astype(F32)[:, None] * inv[None, :]
    c, s = jnp.cos(ang), jnp.sin(ang)
    return jnp.concatenate([c, c, c, c], axis=1), jnp.concatenate([-s, s, -s, s], axis=1)


def _pool_select(sums, lane):
    out = sums[-1]
    for gi in range(len(POOL_WINDOWS) - 2, -1, -1):
        out = jnp.where(lane < (gi + 1) * POOL_GROUP_DIM, sums[gi], out)
    return out


def _pool_kernel(u_ref, prev_ref, w_ref, s_ref, o_ref, carry_scr, *, tt, p0):
    t = pl.program_id(1)

    @pl.when(t == 0)
    def _():
        carry_scr[...] = prev_ref[...]

    u = u_ref[...]
    e = jnp.concatenate([carry_scr[...], u], axis=0)
    carry_scr[...] = u[tt - 16:, :]
    s2 = e[1:] + e[:-1]
    s4 = s2[2:] + s2[:-2]
    s8 = s4[4:] + s4[:-4]
    s16 = s8[8:] + s8[:-8]
    sums = (s2[15:15 + tt], s4[13:13 + tt], s8[9:9 + tt], s16[1:1 + tt])
    n_avail = (p0 + 1 + t * tt + lax.broadcasted_iota(I32, (tt, 1), 0)).astype(F32)
    lane = lax.broadcasted_iota(I32, (tt, POOL_WIDTH), 1)
    means = [sums[gi] / jnp.minimum(n_avail, float(w)) for gi, w in enumerate(POOL_WINDOWS)]
    pooled = _pool_select(means, lane) - u
    o_ref[...] = _dot(pooled.astype(BF16), w_ref[...]) * s_ref[...]


def _pool_prompt(p, prev16, wbd, scale, l, b, t_len):
    tt = min(POOL_ROW_TILE, t_len)
    nt = t_len // tt
    return pl.pallas_call(
        functools.partial(_pool_kernel, tt=tt, p0=0),
        grid=(b, nt),
        in_specs=[pl.BlockSpec((tt, POOL_WIDTH), lambda bi, ti: (bi * nt + ti, OFF_UP // POOL_WIDTH)),
                  pl.BlockSpec((None, 16, POOL_WIDTH), lambda bi, ti: (bi, 0, 0)),
                  pl.BlockSpec((None, POOL_WIDTH, POOL_WIDTH), lambda bi, ti: (l, 0, 0)),
                  pl.BlockSpec((None, 1, POOL_WIDTH), lambda bi, ti: (l, 0, 0))],
        out_specs=pl.BlockSpec((tt, POOL_WIDTH), lambda bi, ti: (bi * nt + ti, 0)),
        out_shape=jax.ShapeDtypeStruct((b * t_len, POOL_WIDTH), F32),
        scratch_shapes=[pltpu.VMEM((16, POOL_WIDTH), F32)],
        compiler_params=_params(("parallel", "arbitrary")),
        name="pool",
    )(p, prev16, wbd, scale)


def _pool_dec_kernel(u_ref, st_ref, w_ref, s_ref, o_ref, *, p0):
    u = u_ref[...]
    lane = lax.broadcasted_iota(I32, u.shape, 1)
    means = []
    acc = u
    d = 1
    for w in POOL_WINDOWS:
        while d < w:
            acc = acc + st_ref[POOL_STATE - d]
            d += 1
        means.append(acc / float(min(p0 + 1, w)))
    pooled = _pool_select(means, lane) - u
    o_ref[...] = _dot(pooled.astype(BF16), w_ref[...]) * s_ref[...]


def _pool_dec(p, st_t, wbd, scale, l, p0):
    n = p.shape[0]
    return pl.pallas_call(
        functools.partial(_pool_dec_kernel, p0=p0),
        grid=(1,),
        in_specs=[pl.BlockSpec((n, POOL_WIDTH), lambda i: (0, OFF_UP // POOL_WIDTH)),
                  pl.BlockSpec((POOL_STATE, n, POOL_WIDTH), lambda i: (0, 0, 0)),
                  pl.BlockSpec((None, POOL_WIDTH, POOL_WIDTH), lambda i: (l, 0, 0)),
                  pl.BlockSpec((None, 1, POOL_WIDTH), lambda i: (l, 0, 0))],
        out_specs=pl.BlockSpec((n, POOL_WIDTH), lambda i: (0, 0)),
        out_shape=jax.ShapeDtypeStruct((n, POOL_WIDTH), F32),
        compiler_params=_params(("arbitrary",)),
        name="pool_dec",
    )(p, st_t, wbd, scale)


def _hgrn_consts(c):
    r = np.arange(c)[:, None]
    s = np.arange(c)[None, :]
    mats = [(s <= r) & (s // h == r // h) for h in HGRN_SMALL]
    mats += [(s > r) & (s // h == r // h) for h in HGRN_SMALL]
    mats.append(s <= r)
    return jnp.asarray(np.concatenate(mats, axis=0).astype(np.float32), BF16)


def _hgrn_kernel(hq_ref, hf_ref, hi_ref, hg_ref, lb_ref, on_ref, ms_ref, yb_ref, st_ref, s_scr, *, c, n_sub):
    step = pl.program_id(1)
    nl = len(HGRN_LEVELS)

    @pl.when(step == 0)
    def _():
        s_scr[...] = jnp.zeros_like(s_scr)

    ti = lax.broadcasted_iota(I32, (c, c), 0)
    si = lax.broadcasted_iota(I32, (c, c), 1)
    pairs = []
    for h in HGRN_LEVELS[:-1]:
        sh = h.bit_length() - 1
        tb = jnp.right_shift(ti, sh)
        pairs.append(((tb & 1) == 1) & (jnp.right_shift(si, sh) == tb - 1))

    kw = HGRN_HEADS * HGRN_K
    for sub in range(n_sub):
        _hgrn_chunk(hq_ref, hf_ref, hi_ref, hg_ref, lb_ref, on_ref, ms_ref, yb_ref, s_scr,
                    slice(sub * c, (sub + 1) * c), c, kw, nl, ti, si, pairs)

    @pl.when(step == pl.num_programs(1) - 1)
    def _():
        st_ref[...] = s_scr[...]


def _hgrn_chunk(hq_ref, hf_ref, hi_ref, hg_ref, lb_ref, on_ref, ms_ref, yb_ref, s_scr,
                rows, c, kw, nl, ti, si, pairs):
    q = hq_ref[rows, :]
    lb = lb_ref[...]
    f = lb + (1.0 - lb) * jax.nn.sigmoid(hf_ref[rows, :])
    g = jnp.log(jnp.maximum(f, F_FLOOR))
    kk = 1.0 - f
    g_hi, g_lo = _split2(g)
    r = _dot(ms_ref[...], jnp.concatenate([g_hi, g_lo], axis=1))
    r = r[:, :kw] + r[:, kw:]
    n_small = len(HGRN_SMALL)
    low = [g] + [r[k * c:(k + 1) * c] for k in range(n_small)]
    up = [None] + [r[(n_small + k) * c:(n_small + k + 1) * c] for k in range(n_small)]
    cum = r[2 * n_small * c:]
    for h in HGRN_LEVELS[1 + n_small:]:
        cum3 = cum.reshape(c // h, h, kw)
        ends = cum3[:, h - 1:h, :]
        if h == c:
            low.append(cum)
        else:
            starts = jnp.concatenate([jnp.zeros((1, 1, kw), F32), ends[:-1]], axis=0)
            low.append((cum3 - starts).reshape(c, kw))
        up.append((ends - cum3).reshape(c, kw))
    kkb = kk.astype(BF16)
    qes = [(q * jnp.exp(low[k])).astype(BF16) for k in range(nl)]
    kes = [kkb] + [(kk * jnp.exp(up[k])).astype(BF16) for k in range(1, nl)]
    qb = q.astype(BF16)
    decay = jnp.exp(cum[c - 1:c, :])

    for hd in range(HGRN_HEADS):
        ks = slice(hd * HGRN_K, (hd + 1) * HGRN_K)
        vs = slice(hd * HGRN_V, (hd + 1) * HGRN_V)
        ivb = hi_ref[rows, vs].astype(BF16)
        att = jnp.where(ti == si, _dot_nt(qb[:, ks], kkb[:, ks]), 0.0)
        for k in range(nl - 1):
            att = jnp.where(pairs[k], _dot_nt(qes[k][:, ks], kes[k][:, ks]), att)
        s_t = s_scr[hd]
        o = _dot(att.astype(BF16), ivb) + _dot_nt(qes[-1][:, ks], s_t.astype(BF16))
        s_scr[hd] = s_t * decay[:, ks] + _dot_tn(ivb, kes[-1][:, ks])

        on = o * lax.rsqrt(jnp.mean(o * o, axis=-1, keepdims=True) + EPS) * on_ref[...]
        yb_ref[rows, vs] = on * _silu(hg_ref[rows, vs])


def _hgrn_prompt(p, lb, onorm, mstack, b, t_len):
    c = HGRN_CHUNK
    n_sub = min(HGRN_CHUNKS_PER_STEP, t_len // c)
    nc = t_len // (c * n_sub)
    kw = HGRN_HEADS * HGRN_K
    vw = HGRN_HEADS * HGRN_V

    def col(off, width):
        return pl.BlockSpec((c * n_sub, width), lambda bi, ci: (bi * nc + ci, off // width))

    return pl.pallas_call(
        functools.partial(_hgrn_kernel, c=c, n_sub=n_sub),
        grid=(b, nc),
        in_specs=[col(OFF_HQ, kw), col(OFF_HF, kw), col(OFF_HI, vw), col(OFF_HG, vw),
                  pl.BlockSpec((1, kw), lambda bi, ci: (0, 0)),
                  pl.BlockSpec((1, HGRN_V), lambda bi, ci: (0, 0)),
                  pl.BlockSpec(mstack.shape, lambda bi, ci: (0, 0))],
        out_specs=[pl.BlockSpec((c * n_sub, vw), lambda bi, ci: (bi * nc + ci, 0)),
                   pl.BlockSpec((None, HGRN_HEADS, HGRN_V, HGRN_K), lambda bi, ci: (bi, 0, 0, 0))],
        out_shape=[jax.ShapeDtypeStruct((b * t_len, vw), F32),
                   jax.ShapeDtypeStruct((b, HGRN_HEADS, HGRN_V, HGRN_K), F32)],
        scratch_shapes=[pltpu.VMEM((HGRN_HEADS, HGRN_V, HGRN_K), F32)],
        compiler_params=_params(("parallel", "arbitrary")),
        name="hgrn",
    )(p, p, p, p, lb, onorm, mstack)


def _hgrn_dec_kernel(q_ref, z_ref, lb_ref, i_ref, hg_ref, on_ref, s_ref, yb_ref, so_ref):
    lb = lb_ref[...]
    f = lb + (1.0 - lb) * jax.nn.sigmoid(z_ref[...])
    a = jnp.exp(jnp.log(jnp.maximum(f, F_FLOOR)))
    s_new = a * s_ref[...] + i_ref[...] * (1.0 - f)
    so_ref[...] = s_new
    o = jnp.sum(q_ref[...] * s_new, axis=-1, keepdims=True)
    on = o * lax.rsqrt(jnp.mean(o * o, axis=-2, keepdims=True) + EPS) * on_ref[...]
    yb_ref[...] = on * _silu(hg_ref[...])


def _hgrn_dec(q_r, z_r, lb_r, i_c, hg_c, onorm_c, state_t, l):
    n = state_t.shape[1]
    nb = min(HGRN_DEC_SEQS, n)
    row = pl.BlockSpec((nb, HGRN_HEADS, 1, HGRN_K), lambda i: (i, 0, 0, 0))
    col = pl.BlockSpec((nb, HGRN_HEADS, HGRN_V, 1), lambda i: (i, 0, 0, 0))
    return pl.pallas_call(
        _hgrn_dec_kernel,
        grid=(n // nb,),
        in_specs=[row, row, pl.BlockSpec((HGRN_HEADS, 1, HGRN_K), lambda i: (0, 0, 0)), col, col,
                  pl.BlockSpec((HGRN_V, 1), lambda i: (0, 0)),
                  pl.BlockSpec((None, nb, HGRN_HEADS, HGRN_V, HGRN_K), lambda i: (l, i, 0, 0, 0))],
        out_specs=[col, pl.BlockSpec((nb, HGRN_HEADS, HGRN_V, HGRN_K), lambda i: (i, 0, 0, 0))],
        out_shape=[jax.ShapeDtypeStruct((n, HGRN_HEADS, HGRN_V, 1), F32),
                   jax.ShapeDtypeStruct(state_t.shape[1:], F32)],
        compiler_params=_params(("parallel",)),
        name="hgrn_dec",
    )(q_r, z_r, lb_r, i_c, hg_c, onorm_c, state_t)


def _sort_key(s):
    bits = pltpu.bitcast(s, I32)
    return jnp.where(bits < 0, INT_MIN - bits, bits)


def _lane_fold(x):
    part = x[:, :LANES]
    for cidx in range(1, x.shape[1] // LANES):
        part = part + x[:, cidx * LANES:(cidx + 1) * LANES]
    return part


def _select_bias(key, thr, idx, j_lim):
    return jnp.where(key > thr, 0.0, jnp.where(key == thr, jnp.where(idx < j_lim, 0.0, NEG_BIG), NEG_BIG))


def _bisect_bits(count_ge, thr, n_ge, first_bit, n_bits, k):
    def bit_body(it, carry):
        thr, n_ge = carry
        cand = thr + lax.shift_left(jnp.int32(1), first_bit - it)
        cnt = count_ge(cand)
        take = cnt >= k
        return jnp.where(take, cand, thr), jnp.where(take, cnt, n_ge)

    return lax.fori_loop(0, n_bits, bit_body, (thr, n_ge))


def _topk_threshold(count, stat_shape, n_sel, idx_bits, idx_sentinel, kth_largest=None):
    k = float(n_sel)
    if kth_largest is None:
        thr, n_ge = _bisect_bits(lambda cand: count(lambda key, idx: jnp.where(key >= cand, 1.0, 0.0)),
                                 jnp.full(stat_shape, INT_MIN, I32), jnp.zeros(stat_shape, F32), 31, 32, k)
    else:
        thr, n_ge = kth_largest(k)
    live = thr > INT_MIN
    tie = jnp.where(live, jnp.where(n_ge > k, 1.0, 0.0), 0.0)
    j_default = jnp.where(live, jnp.int32(idx_sentinel), jnp.int32(0))

    def tie_search():
        need = k - count(lambda key, idx: jnp.where(key > thr, 1.0, 0.0))

        def idx_body(it, j_lim):
            cand = j_lim + lax.shift_left(jnp.int32(1), idx_bits - 1 - it)
            cnt = count(lambda key, idx: jnp.where(key == thr, jnp.where(idx < cand, 1.0, 0.0), 0.0))
            return jnp.where(cnt <= need, cand, j_lim)

        j_lim = lax.fori_loop(0, idx_bits, idx_body, jnp.zeros(stat_shape, I32))
        return jnp.where(tie > 0.0, j_lim, j_default)

    j_lim = lax.cond(jnp.max(tie) > 0.0, tie_search, lambda: j_default)
    return thr, j_lim


FOLD_ROWS = 64
V_ROWS = HEAD_DIM + 16


def _fold(x, op):
    parts = [x[r * FOLD_ROWS:(r + 1) * FOLD_ROWS] for r in range(x.shape[0] // FOLD_ROWS)]
    while len(parts) > 1:
        parts = [op(parts[i], parts[i + 1]) if i + 1 < len(parts) else parts[i]
                 for i in range(0, len(parts), 2)]
    return parts[0]


def _dsa_kernel(q_ref, qi_ref, wit_ref, k_ref, vt_ref, ki_ref, o_ref, keys_scr, k16_scr, lg_scr, p_scr, m_scr, acc_scr,
                *, tq, tk, n_sel, t_len):
    i = pl.program_id(1)
    nk = ((i + 1) * tq + tk - 1) // tk
    krow = lax.broadcasted_iota(I32, (tk, tq), 0)
    qpos = i * tq + lax.broadcasted_iota(I32, (tk, tq), 1)
    wit = wit_ref[...] * IDX_SCALE
    qi_all = qi_ref[...].reshape(IDX_HEADS * tq, IDX_DIM)

    def rows(j):
        return pl.ds(pl.multiple_of(j * tk, tk), tk)

    def score_tile(j, causal):
        rel = _dot_nt(ki_ref[rows(j), :], qi_all)
        s = wit[0:1, :] * jnp.maximum(rel[:, :tq], 0.0)
        for h in range(1, IDX_HEADS):
            s = s + wit[h:h + 1, :] * jnp.maximum(rel[:, h * tq:(h + 1) * tq], 0.0)
        key = jnp.where(s > 0.5 * NEG_BIG, _sort_key(s), INT_MIN)
        if causal:
            key = jnp.where(j * tk + krow <= qpos, key, INT_MIN)
        keys_scr[j] = key
        k16_scr[j] = jnp.right_shift(key, 16).astype(I16)

    def score_body(j, carry):
        score_tile(j, False)
        return carry

    lax.fori_loop(0, nk - 1, score_body, 0)
    score_tile(nk - 1, True)

    def count(pred):
        def body(j, acc):
            return acc + _fold(pred(keys_scr[j], j * tk + krow), jnp.add)

        acc = lax.fori_loop(0, nk, body, jnp.zeros((FOLD_ROWS, tq), F32))
        return jnp.sum(acc, axis=0, keepdims=True)

    one16, zero16 = jnp.int16(1), jnp.int16(0)

    def count16(pred):
        def body(j, acc):
            return acc + _fold(pred(k16_scr[j]), jnp.add)

        acc = lax.fori_loop(0, nk, body, jnp.zeros((FOLD_ROWS, tq), I16))
        return jnp.sum(acc.astype(F32), axis=0, keepdims=True)

    def high16(x):
        return jnp.right_shift(x, 16).astype(I16)

    def low16(x):
        return ((x & 0xFFFF) - 32768).astype(I16)

    def kth_largest(k):
        thr = jnp.full((1, tq), INT_MIN, I32)
        thr, n_ge = _bisect_bits(lambda cand: count16(lambda t: jnp.where(t >= high16(cand), one16, zero16)),
                                 thr, jnp.zeros((1, tq), F32), 31, 16, k)
        thr_hi = high16(thr)
        n_above = count16(lambda t: jnp.where(t > thr_hi, one16, zero16))

        def low_body(j, carry):
            key = keys_scr[j]
            k16_scr[j] = jnp.where(high16(key) == thr_hi, low16(key), jnp.int16(-32768))
            return carry

        lax.fori_loop(0, nk, low_body, 0)
        return _bisect_bits(
            lambda cand: n_above + count16(lambda t: jnp.where(t >= low16(cand), one16, zero16)),
            thr, n_ge, 15, 16, k)

    thr, j_lim = _topk_threshold(count, (1, tq), n_sel, int(t_len).bit_length(), t_len, kth_largest)

    m_scr[...] = jnp.full(m_scr.shape, 0.5 * NEG_BIG, F32)
    acc_scr[...] = jnp.zeros_like(acc_scr)

    def attend_tile(j, carry):
        bias = _select_bias(keys_scr[j], thr, j * tk + krow, j_lim)
        bias = jnp.concatenate([bias] * GROUP, axis=1)
        v_t = vt_ref[j]
        heads = range(N_KV_HEADS)
        for n in heads:
            qg = q_ref[n * GROUP:(n + 1) * GROUP].reshape(GROUP * tq, HEAD_DIM)
            lg_scr[n] = _dot_nt(k_ref[n, rows(j), :], qg) + bias
        m_old = [m_scr[n] for n in heads]
        m_new = [jnp.maximum(m_old[n], jnp.max(_fold(lg_scr[n], jnp.maximum), axis=0, keepdims=True))
                 for n in heads]
        for n in heads:
            p_scr[n] = jnp.exp2(lg_scr[n] - m_new[n]).astype(BF16)
        for n in heads:
            acc_scr[n] = (jnp.exp2(m_old[n] - m_new[n]) * acc_scr[n]
                          + _dot(v_t[n * V_ROWS:(n + 1) * V_ROWS, :], p_scr[n]))
            m_scr[n] = m_new[n]
        return carry

    lax.fori_loop(0, nk, attend_tile, 0)
    outs = []
    for n in range(N_KV_HEADS):
        acc = acc_scr[n]
        o_n = acc[:HEAD_DIM] / acc[HEAD_DIM:HEAD_DIM + 1]
        outs += [o_n[:, g * tq:(g + 1) * tq] for g in range(GROUP)]
    o_ref[...] = jnp.concatenate(outs, axis=0).T


def _dsa_prompt(q_h, qi_h, wit, k_h, v_t, ki_b, b, t_len, n_sel, tk):
    tq = min(DSA_QUERY_TILE, t_len)
    nq = t_len // tq
    nt = t_len // tk

    def heads(h, rows, imap):
        return pl.BlockSpec((h, rows, HEAD_DIM), imap)

    return pl.pallas_call(
        functools.partial(_dsa_kernel, tq=tq, tk=tk, n_sel=n_sel, t_len=t_len),
        grid=(b, nq),
        in_specs=[heads(N_HEADS, tq, lambda bi, qi: (0, bi * nq + qi, 0)),
                  heads(IDX_HEADS, tq, lambda bi, qi: (0, bi * nq + qi, 0)),
                  pl.BlockSpec((8, tq), lambda bi, qi: (0, bi * nq + qi)),
                  heads(N_KV_HEADS, t_len, lambda bi, qi: (0, bi, 0)),
                  pl.BlockSpec((nt, N_KV_HEADS * V_ROWS, tk), lambda bi, qi: (bi, 0, 0)),
                  pl.BlockSpec((t_len, IDX_DIM), lambda bi, qi: (bi, 0))],
        out_specs=pl.BlockSpec((tq, ATTN_WIDTH), lambda bi, qi: (bi * nq + qi, 0)),
        out_shape=jax.ShapeDtypeStruct((b * t_len, ATTN_WIDTH), F32),
        scratch_shapes=[pltpu.VMEM((nt, tk, tq), I32),
                        pltpu.VMEM((nt, tk, tq), I16),
                        pltpu.VMEM((N_KV_HEADS, tk, GROUP * tq), F32),
                        pltpu.VMEM((N_KV_HEADS, tk, GROUP * tq), BF16),
                        pltpu.VMEM((N_KV_HEADS, 1, GROUP * tq), F32),
                        pltpu.VMEM((N_KV_HEADS, V_ROWS, GROUP * tq), F32)],
        compiler_params=_params(("parallel", "arbitrary")),
        name="dsa",
    )(q_h, qi_h, wit, k_h, v_t, ki_b)


DEC_SEQS_PER_STEP = 4


def _dsa_dec_score_kernel(pt_ref, qi_ref, wi_ref, kin_ref, *rest, n_pages, n_seq):
    page_refs = rest[:n_seq * n_pages]
    o_ref = rest[n_seq * n_pages]
    lane = lax.broadcasted_iota(I32, (1, PAGE_SIZE), 1)
    for s in range(n_seq):
        qi = qi_ref[s]
        wi = wi_ref[s]

        def finish(rel):
            sc = jnp.sum(wi * jnp.maximum(rel, 0.0), axis=0, keepdims=True) * IDX_SCALE
            return jnp.where(sc > 0.5 * NEG_BIG, _sort_key(sc), INT_MIN)

        for pg in range(n_pages):
            page = page_refs[s * n_pages + pg][...].astype(BF16)
            o_ref[s, :, pg * PAGE_SIZE:(pg + 1) * PAGE_SIZE] = finish(_dot(qi, page))
        kin = kin_ref[s].astype(BF16).astype(F32)
        key_new = finish(jnp.sum(qi.astype(F32) * kin, axis=1, keepdims=True))
        o_ref[s, :, n_pages * PAGE_SIZE:] = jnp.where(lane == 0, key_new, INT_MIN)


def _page_specs(block, l, n_seq, n_pages):
    def spec(s, pg):
        return pl.BlockSpec(block, lambda bi, pt, *_: (l, pt[bi * n_seq + s, pg], 0, 0))

    return [spec(s, pg) for s in range(n_seq) for pg in range(n_pages)]


def _dsa_dec_scores(page_table, qi8, wi8, ki_new, cache_ki_t, l):
    n, n_pages = page_table.shape
    ns = min(DEC_SEQS_PER_STEP, n)
    width = (n_pages + 1) * PAGE_SIZE
    page_specs = _page_specs((None, None, IDX_DIM, PAGE_SIZE), l, ns, n_pages)
    return pl.pallas_call(
        functools.partial(_dsa_dec_score_kernel, n_pages=n_pages, n_seq=ns),
        grid_spec=pltpu.PrefetchScalarGridSpec(
            num_scalar_prefetch=1,
            grid=(n // ns,),
            in_specs=[pl.BlockSpec((ns, 8, IDX_DIM), lambda bi, pt: (bi, 0, 0)),
                      pl.BlockSpec((ns, 8, 1), lambda bi, pt: (bi, 0, 0)),
                      pl.BlockSpec((ns, 1, IDX_DIM), lambda bi, pt: (bi, 0, 0))] + page_specs,
            out_specs=pl.BlockSpec((ns, 1, width), lambda bi, pt: (bi, 0, 0))),
        out_shape=jax.ShapeDtypeStruct((n, 1, width), I32),
        compiler_params=_params(("arbitrary",)),
        name="dsa_dec_scores",
    )(page_table, qi8, wi8, ki_new, *([cache_ki_t] * (ns * n_pages)))


def _dsa_dec_thr_kernel(keys_ref, thr_ref, j_ref, *, n_sel):
    n, width = keys_ref.shape
    lane = lax.broadcasted_iota(I32, (n, width), 1)

    def count(pred):
        return jnp.sum(_lane_fold(pred(keys_ref[...], lane)), axis=1, keepdims=True)

    thr, j_lim = _topk_threshold(count, (n, 1), n_sel, int(width).bit_length(), width)
    thr_ref[...] = thr
    j_ref[...] = j_lim


def _dsa_dec_thr(keys2d, n_sel):
    n, width = keys2d.shape
    return pl.pallas_call(
        functools.partial(_dsa_dec_thr_kernel, n_sel=n_sel),
        grid=(1,),
        in_specs=[pl.BlockSpec((n, width), lambda i: (0, 0))],
        out_specs=[pl.BlockSpec((n, 1), lambda i: (0, 0)), pl.BlockSpec((n, 1), lambda i: (0, 0))],
        out_shape=[jax.ShapeDtypeStruct((n, 1), I32), jax.ShapeDtypeStruct((n, 1), I32)],
        compiler_params=_params(("arbitrary",)),
        name="dsa_dec_thr",
    )(keys2d)


def _dsa_dec_attn_kernel(pt_ref, thr_ref, j_ref, q_ref, keys_ref, kn_ref, vn_ref, *rest, n_pages, n_seq):
    k_pages = rest[:n_seq * n_pages]
    v_pages = rest[n_seq * n_pages:2 * n_seq * n_pages]
    o_ref, k_scr, v_scr = rest[2 * n_seq * n_pages:]
    n_past = n_pages * PAGE_SIZE
    idx = lax.broadcasted_iota(I32, (1, keys_ref.shape[-1]), 1)
    for s in range(n_seq):
        b = pl.program_id(0) * n_seq + s
        for pg in range(n_pages):
            sl = slice(pg * PAGE_SIZE, (pg + 1) * PAGE_SIZE)
            k_scr[s, :, sl] = k_pages[s * n_pages + pg][...].astype(BF16)
            v_scr[s, :, sl] = v_pages[s * n_pages + pg][...].astype(BF16)
        bias_all = _select_bias(keys_ref[s], thr_ref[b], idx, j_ref[b])
        bias = bias_all[:, :n_past]
        bias_new = bias_all[:, n_past:n_past + 1]
        kn = kn_ref[s].astype(BF16).astype(F32)
        vn = vn_ref[s].astype(BF16).astype(F32)
        for n in range(N_KV_HEADS):
            fs = slice(n * HEAD_DIM, (n + 1) * HEAD_DIM)
            q = q_ref[s, n]
            lg = _dot(q, k_scr[s, fs, :]) + bias
            lg_new = jnp.sum(q.astype(F32) * kn[:, fs], axis=1, keepdims=True) + bias_new
            m = jnp.maximum(jnp.max(lg, axis=1, keepdims=True), lg_new)
            p = jnp.exp2(lg - m)
            p_new = jnp.exp2(lg_new - m)
            den = jnp.sum(p, axis=1, keepdims=True) + p_new
            o_ref[s, n] = (_dot_nt(p.astype(BF16), v_scr[s, fs, :]) + p_new * vn[:, fs]) / den


def _dsa_dec_attn(page_table, thr, j_lim, q8, keys, k_new, v_new, cache_k_t, cache_v_t, l):
    n, n_pages = page_table.shape
    ns = min(DEC_SEQS_PER_STEP, n)
    width = (n_pages + 1) * PAGE_SIZE
    page_specs = _page_specs((None, None, KV_WIDTH, PAGE_SIZE), l, ns, n_pages)
    qspec = pl.BlockSpec((ns, N_KV_HEADS, 8, HEAD_DIM), lambda bi, pt, th, jl: (bi, 0, 0, 0))

    def per_seq(width_):
        return pl.BlockSpec((ns, 1, width_), lambda bi, pt, th, jl: (bi, 0, 0))

    return pl.pallas_call(
        functools.partial(_dsa_dec_attn_kernel, n_pages=n_pages, n_seq=ns),
        grid_spec=pltpu.PrefetchScalarGridSpec(
            num_scalar_prefetch=3,
            grid=(n // ns,),
            in_specs=[qspec, per_seq(width), per_seq(KV_WIDTH), per_seq(KV_WIDTH)] + page_specs + page_specs,
            out_specs=qspec,
            scratch_shapes=[pltpu.VMEM((ns, KV_WIDTH, n_pages * PAGE_SIZE), BF16),
                            pltpu.VMEM((ns, KV_WIDTH, n_pages * PAGE_SIZE), BF16)]),
        out_shape=jax.ShapeDtypeStruct((n, N_KV_HEADS, 8, HEAD_DIM), F32),
        compiler_params=_params(("arbitrary",)),
        name="dsa_dec_attn",
    )(page_table, thr, j_lim, q8, keys, k_new, v_new,
      *([cache_k_t] * (ns * n_pages)), *([cache_v_t] * (ns * n_pages)))


def _merge_kernel(x_ref, g0_ref, g1_ref, g2_ref, ya_ref, yb_ref, yc_ref, gt_ref,
                  wa_ref, wb_ref, wc_ref, wo_ref, o_ref):
    merged = (jax.nn.sigmoid(g0_ref[...]) * _dot(ya_ref[...].astype(BF16), wa_ref[...])
              + jax.nn.sigmoid(g1_ref[...]) * _dot(yb_ref[...].astype(BF16), wb_ref[...])
              + jax.nn.sigmoid(g2_ref[...]) * _dot(yc_ref[...].astype(BF16), wc_ref[...]))
    o_ref[...] = x_ref[...] + gt_ref[...] * _dot(merged.astype(BF16), wo_ref[...])


def _merge(x, p, ya, yb, yc, mods, w_ba, w_bb, w_bc, w_out, l, rows_per_seq):
    n = x.shape[0]
    tm = _row_tile(n, mods, rows_per_seq, MERGE_ROW_TILE)

    def wspec(rows):
        return pl.BlockSpec((None, rows, D_MODEL), lambda i: (l, 0, 0))

    def row(width):
        return pl.BlockSpec((tm, width), lambda i: (i, 0))

    def gate(bidx):
        return pl.BlockSpec((tm, D_MODEL), lambda i: (i, OFF_GL // D_MODEL + bidx))

    return pl.pallas_call(
        _merge_kernel,
        grid=(n // tm,),
        in_specs=[row(D_MODEL), gate(0), gate(1), gate(2), row(POOL_WIDTH), row(HGRN_HEADS * HGRN_V),
                  row(ATTN_WIDTH), _mod_spec(mods, l, 5, tm, rows_per_seq),
                  wspec(POOL_WIDTH), wspec(HGRN_HEADS * HGRN_V), wspec(ATTN_WIDTH), wspec(D_MODEL)],
        out_specs=row(D_MODEL),
        out_shape=jax.ShapeDtypeStruct((n, D_MODEL), F32),
        compiler_params=_params(("parallel",)),
        name="merge",
    )(x, p, p, p, ya, yb, yc, mods, w_ba, w_bb, w_bc, w_out)


def _permute_w_in(w_in):
    segs = [(2884, 5956), (256, 768), (768, 1280), (1792, 2304), (0, 256), (1280, 1536), (1536, 1792),
            (2560, 2816), (2304, 2432), (2432, 2560), (2816, 2884)]
    parts = [w_in[..., a:b] for a, b in segs]
    used = sum(b - a for a, b in segs)
    parts.append(jnp.zeros(w_in.shape[:-1] + (IN_PAD - used,), w_in.dtype))
    return jnp.concatenate(parts, axis=-1).astype(BF16)


def _lower_bounds(p):
    sm = jax.nn.softmax(p.astype(F32), axis=0)
    return jnp.cumsum(sm, axis=0) - sm[0:1]


def kernel(x_prompt, x_sample, c_prompt, c_sample, cache_k, cache_v, cache_ki, page_table, state_pool, state_hgrn, ada_w, ada_b, norm_g, ffn_wg, ffn_wu, ffn_wd, w_in, pool_w, pool_scale, hgrn_lb, hgrn_onorm, q_norm, k_norm, w_ba, w_bb, w_bc, w_out):
    bp, t_len, _ = x_prompt.shape
    bs = x_sample.shape[0]
    assert x_sample.shape[1] == 1
    n_pages = page_table.shape[1]
    past_len = n_pages * PAGE_SIZE
    n_phys = cache_k.shape[1]
    n_sel_p = min(TOPK_MAX, t_len // 4)
    n_sel_s = min(TOPK_MAX, (past_len + 1) // 4)

    wg, wu, wd = ffn_wg.astype(BF16), ffn_wu.astype(BF16), ffn_wd.astype(BF16)
    w_in_p = _permute_w_in(w_in)
    wa, wb, wc, wo = w_ba.astype(BF16), w_bb.astype(BF16), w_bc.astype(BF16), w_out.astype(BF16)
    eye = jnp.eye(len(POOL_WINDOWS), dtype=F32)
    pool_wbd = jnp.einsum('lgcd,gh->lgchd', pool_w, eye).reshape(DEPTH, POOL_WIDTH, POOL_WIDTH).astype(BF16)
    pool_sc = pool_scale.reshape(DEPTH, 1, POOL_WIDTH)
    norm_g4 = norm_g.reshape(DEPTH, N_SUB, 1, D_MODEL)
    lbs = _lower_bounds(hgrn_lb)
    qg = jnp.tile(q_norm, (1, LANES // HEAD_DIM)).reshape(DEPTH, 1, LANES)
    kg = jnp.tile(k_norm, (1, LANES // HEAD_DIM)).reshape(DEPTH, 1, LANES)
    gid = np.arange(LANES) // HEAD_DIM
    bd = jnp.asarray((gid[:, None] == gid[None, :]).astype(np.float32), BF16)
    mstack = _hgrn_consts(HGRN_CHUNK)
    cos_p, sin_p = _rope_tables(jnp.arange(t_len, dtype=I32))
    cos_s, sin_s = _rope_tables(jnp.full((bs,), past_len, I32))
    ck = jnp.transpose(cache_k, (0, 1, 3, 4, 2)).reshape(DEPTH, n_phys, KV_WIDTH, PAGE_SIZE)
    cv = jnp.transpose(cache_v, (0, 1, 3, 4, 2)).reshape(DEPTH, n_phys, KV_WIDTH, PAGE_SIZE)
    cki = jnp.transpose(cache_ki, (0, 1, 3, 2))
    state_hgrn_t = jnp.swapaxes(state_hgrn, -1, -2)

    rows_all = bp + bs
    rows_pad = -(-rows_all // 8) * 8
    c_all = jnp.concatenate([c_prompt, c_sample, jnp.zeros((rows_pad - rows_all, D_MODEL), F32)], axis=0)
    mods = _ada(c_all, ada_w, ada_b)
    mods_p = mods[:, :bp].reshape(DEPTH, bp, 1, 3 * N_SUB * D_MODEL)
    mods_s = mods[:, bp:rows_all]

    xp = x_prompt.reshape(bp * t_len, D_MODEL)
    xs = x_sample.reshape(bs, D_MODEL)
    pool_prev_p = jnp.zeros((bp, 16, POOL_WIDTH), F32)
    outs = [[] for _ in range(10)]
    tm_prep = min(PREP_ROW_TILE, t_len)

    for l in range(DEPTH):
        xp = _ffn(xp, mods_p, norm_g4, wg, wu, wd, l, 0, 0, t_len)
        p = _inproj(xp, mods_p, norm_g4, w_in_p, l, t_len)
        ya = _pool_prompt(p, pool_prev_p, pool_wbd, pool_sc, l, bp, t_len)
        yb, st_t = _hgrn_prompt(p, lbs[l].reshape(1, -1), hgrn_onorm[l].reshape(1, -1), mstack, bp, t_len)
        q_h, _, k_h, v_t, qi_h, _, ki_b, wit, k_tf, v_tf, ki_tf = _prep(
            p, cos_p, sin_p, qg[l], kg[l], bd, tm_prep, t_len // tm_prep)
        yc = _dsa_prompt(q_h, qi_h, wit, k_h, v_t, ki_b, bp, t_len, n_sel_p, tm_prep)
        xp = _merge(xp, p, ya, yb, yc, mods_p, wa, wb, wc, wo, l, t_len)
        xp = _ffn(xp, mods_p, norm_g4, wg, wu, wd, l, 1, 2, t_len)
        p3 = p.reshape(bp, t_len, IN_PAD)
        outs[0].append(jnp.transpose(k_tf.reshape(bp, N_KV_HEADS, HEAD_DIM, t_len), (0, 3, 1, 2)))
        outs[1].append(jnp.transpose(v_tf.reshape(bp, N_KV_HEADS, HEAD_DIM, t_len), (0, 3, 1, 2)))
        outs[2].append(jnp.swapaxes(ki_tf, 1, 2))
        outs[3].append(p3[:, t_len - POOL_STATE:, OFF_UP:OFF_UP + POOL_WIDTH])
        outs[4].append(jnp.swapaxes(st_t, -1, -2))

        xs = _ffn(xs, mods_s, norm_g4, wg, wu, wd, l, 0, 0, 1)
        ps = _inproj(xs, mods_s, norm_g4, w_in_p, l, 1)
        ya = _pool_dec(ps, jnp.swapaxes(state_pool[l], 0, 1), pool_wbd, pool_sc, l, past_len)
        kw = HGRN_HEADS * HGRN_K
        vw = HGRN_HEADS * HGRN_V
        yb4, st_new = _hgrn_dec(
            ps[:, OFF_HQ:OFF_HQ + kw].reshape(bs, HGRN_HEADS, 1, HGRN_K),
            ps[:, OFF_HF:OFF_HF + kw].reshape(bs, HGRN_HEADS, 1, HGRN_K),
            lbs[l].reshape(HGRN_HEADS, 1, HGRN_K),
            ps[:, OFF_HI:OFF_HI + vw].reshape(bs, HGRN_HEADS, HGRN_V, 1),
            ps[:, OFF_HG:OFF_HG + vw].reshape(bs, HGRN_HEADS, HGRN_V, 1),
            hgrn_onorm[l].reshape(-1, 1), state_hgrn_t, l)
        yb = yb4.reshape(bs, vw)
        q_h, k_r, _, _, qi_h, ki_r, _, _, _, _, _ = _prep(ps, cos_s, sin_s, qg[l], kg[l], bd, bs, 1)
        v_new = ps[:, OFF_AV:OFF_AV + KV_WIDTH]
        qi8 = jnp.pad(jnp.swapaxes(qi_h, 0, 1), ((0, 0), (0, 8 - IDX_HEADS), (0, 0)))
        wi8 = jnp.pad(ps[:, OFF_IKW + IDX_DIM:OFF_IKW + IDX_DIM + IDX_HEADS].reshape(bs, IDX_HEADS, 1),
                      ((0, 0), (0, 8 - IDX_HEADS), (0, 0)))
        keys = _dsa_dec_scores(page_table, qi8, wi8, ki_r.reshape(bs, 1, IDX_DIM), cki, l)
        thr, j_lim = _dsa_dec_thr(keys.reshape(bs, -1), n_sel_s)
        q8 = jnp.pad(jnp.swapaxes(q_h, 0, 1).reshape(bs, N_KV_HEADS, GROUP, HEAD_DIM),
                     ((0, 0), (0, 0), (0, 8 - GROUP), (0, 0)))
        yc8 = _dsa_dec_attn(page_table, thr.reshape(bs), j_lim.reshape(bs), q8, keys,
                            k_r.reshape(bs, 1, KV_WIDTH), v_new.reshape(bs, 1, KV_WIDTH), ck, cv, l)
        yc = yc8[:, :, :GROUP, :].reshape(bs, ATTN_WIDTH)
        xs = _merge(xs, ps, ya, yb, yc, mods_s, wa, wb, wc, wo, l, 1)
        xs = _ffn(xs, mods_s, norm_g4, wg, wu, wd, l, 1, 2, 1)
        outs[5].append(k_r.reshape(bs, 1, N_KV_HEADS, HEAD_DIM))
        outs[6].append(v_new.reshape(bs, 1, N_KV_HEADS, HEAD_DIM))
        outs[7].append(ki_r.reshape(bs, 1, IDX_DIM))
        outs[8].append(jnp.concatenate([state_pool[l][:, 1:], ps[:, None, OFF_UP:OFF_UP + POOL_WIDTH]], axis=1))
        outs[9].append(jnp.swapaxes(st_new, -1, -2))

    stacked = [jnp.stack(o) for o in outs]
    return (xp.reshape(bp, t_len, D_MODEL), xs.reshape(bs, 1, D_MODEL), *stacked)
```

```python
import functools

import numpy as np
import jax
import jax.numpy as jnp
from jax import lax
from jax.experimental import pallas as pl
from jax.experimental.pallas import tpu as pltpu

F32 = jnp.float32
BF16 = jnp.bfloat16
I32 = jnp.int32
I16 = jnp.int16

D_MODEL = 1024
DEPTH = 4
PAGE_SIZE = 128
N_SUB = 3
POOL_WINDOWS = (2, 4, 8, 16)
POOL_GROUP_DIM = 64
POOL_WIDTH = 256
POOL_STATE = 15
HGRN_HEADS = 4
HGRN_K = 128
HGRN_V = 64
F_FLOOR = 1e-30
N_HEADS = 8
N_KV_HEADS = 2
HEAD_DIM = 64
GROUP = N_HEADS // N_KV_HEADS
ATTN_WIDTH = 512
KV_WIDTH = 128
IDX_HEADS = 4
IDX_DIM = 64
TOPK_MAX = 256
ROPE_THETA = 10000.0
LOG2_E = 1.4426950408889634
ATTN_SCALE = HEAD_DIM ** -0.5
IDX_SCALE = (IDX_DIM * IDX_HEADS) ** -0.5
NEG_BIG = -1e30
D_FF = 2816
EPS = 1e-6
INT_MIN = np.int32(-2147483648)

LANES = 128
VMEM_LIMIT = 48 * 1024 * 1024

ADA_COL_TILE = 1024
FFN_ROW_TILE = 1024
FFN_HIDDEN_TILE = 1408
INPROJ_ROW_TILE = 1024
INPROJ_COL_TILE = 1536
MERGE_ROW_TILE = 512
PREP_ROW_TILE = 512
POOL_ROW_TILE = 2048
DSA_QUERY_TILE = 256
HGRN_DEC_SEQS = 8

OFF_GL, OFF_HQ, OFF_HF, OFF_AQ = 0, 3072, 3584, 4096
OFF_UP, OFF_HI, OFF_HG, OFF_IQ = 4608, 4864, 5120, 5376
OFF_AK, OFF_AV, OFF_IKW = 5632, 5760, 5888
IN_PAD = 6144
HGRN_CHUNK = 128
HGRN_CHUNKS_PER_STEP = 8
HGRN_LEVELS = tuple(1 << i for i in range(HGRN_CHUNK.bit_length()))
HGRN_SMALL = tuple(h for h in HGRN_LEVELS if 1 < h < 8)


def _dot(a, b):
    return jnp.dot(a, b, preferred_element_type=F32)


def _dot_nt(a, b):
    return lax.dot_general(a, b, (((1,), (1,)), ((), ())), preferred_element_type=F32)


def _dot_tn(a, b):
    return lax.dot_general(a, b, (((0,), (0,)), ((), ())), preferred_element_type=F32)


def _split2(x):
    hi = x.astype(BF16)
    lo = (x - hi.astype(F32)).astype(BF16)
    return hi, lo


def _split3(x):
    hi = x.astype(BF16)
    r = x - hi.astype(F32)
    mid = r.astype(BF16)
    lo = (r - mid.astype(F32)).astype(BF16)
    return hi, mid, lo


def _silu(x):
    return x * jax.nn.sigmoid(x)


def _params(sem):
    return pltpu.CompilerParams(dimension_semantics=sem, vmem_limit_bytes=VMEM_LIMIT)


def _mod_spec(mods, l, m, tm, rows_per_seq):
    if mods.ndim == 4:
        return pl.BlockSpec((None, None, 1, D_MODEL), lambda i, *_: (l, (i * tm) // rows_per_seq, 0, m))
    return pl.BlockSpec((None, tm, D_MODEL), lambda i, *_: (l, i, m))


def _row_tile(n, mods, rows_per_seq, cap):
    return min(cap, rows_per_seq if mods.ndim == 4 else n)


def _prenorm(x, g, scale, shift):
    y = x * lax.rsqrt(jnp.mean(x * x, axis=-1, keepdims=True) + EPS) * g
    return y * (1.0 + scale) + shift


def _ada_kernel(c_ref, w_ref, b_ref, o_ref):
    c = c_ref[...]
    o_ref[...] = jnp.dot(_silu(c), w_ref[...], precision=lax.Precision.HIGHEST,
                         preferred_element_type=F32) + b_ref[...]


def _ada(c_all, ada_w, ada_b):
    rows = c_all.shape[0]
    width = ada_w.shape[-1]
    tn = ADA_COL_TILE
    return pl.pallas_call(
        _ada_kernel,
        grid=(DEPTH, width // tn),
        in_specs=[pl.BlockSpec((rows, D_MODEL), lambda l, j: (0, 0)),
                  pl.BlockSpec((None, D_MODEL, tn), lambda l, j: (l, 0, j)),
                  pl.BlockSpec((None, 1, tn), lambda l, j: (l, 0, j))],
        out_specs=pl.BlockSpec((None, rows, tn), lambda l, j: (l, 0, j)),
        out_shape=jax.ShapeDtypeStruct((DEPTH, rows, width), F32),
        compiler_params=_params(("parallel", "parallel")),
        name="ada",
    )(c_all, ada_w, ada_b.reshape(DEPTH, 1, width))


def _ffn_kernel(x_ref, sh_ref, sc_ref, gt_ref, g_ref, wg_ref, wu_ref, wd_ref, o_ref, h_scr, acc_scr):
    j = pl.program_id(1)

    @pl.when(j == 0)
    def _():
        h_scr[...] = _prenorm(x_ref[...], g_ref[...], sc_ref[...], sh_ref[...]).astype(BF16)
        acc_scr[...] = jnp.zeros_like(acc_scr)

    h = h_scr[...]
    act = (_silu(_dot(h, wg_ref[...])) * _dot(h, wu_ref[...])).astype(BF16)
    acc_scr[...] += _dot(act, wd_ref[...])

    @pl.when(j == pl.num_programs(1) - 1)
    def _():
        o_ref[...] = x_ref[...] + 0.5 * gt_ref[...] * acc_scr[...]


def _ffn(x, mods, norm_g, wg, wu, wd, l, which, sub, rows_per_seq):
    n = x.shape[0]
    tm = _row_tile(n, mods, rows_per_seq, FFN_ROW_TILE)
    tf = FFN_HIDDEN_TILE
    return pl.pallas_call(
        _ffn_kernel,
        grid=(n // tm, D_FF // tf),
        in_specs=[pl.BlockSpec((tm, D_MODEL), lambda i, j: (i, 0)),
                  _mod_spec(mods, l, 3 * sub, tm, rows_per_seq),
                  _mod_spec(mods, l, 3 * sub + 1, tm, rows_per_seq),
                  _mod_spec(mods, l, 3 * sub + 2, tm, rows_per_seq),
                  pl.BlockSpec((None, None, 1, D_MODEL), lambda i, j: (l, sub, 0, 0)),
                  pl.BlockSpec((None, None, D_MODEL, tf), lambda i, j: (l, which, 0, j)),
                  pl.BlockSpec((None, None, D_MODEL, tf), lambda i, j: (l, which, 0, j)),
                  pl.BlockSpec((None, None, tf, D_MODEL), lambda i, j: (l, which, j, 0))],
        out_specs=pl.BlockSpec((tm, D_MODEL), lambda i, j: (i, 0)),
        out_shape=jax.ShapeDtypeStruct((n, D_MODEL), F32),
        scratch_shapes=[pltpu.VMEM((tm, D_MODEL), BF16), pltpu.VMEM((tm, D_MODEL), F32)],
        compiler_params=_params(("parallel", "arbitrary")),
        name="ffn",
    )(x, mods, mods, mods, norm_g, wg, wu, wd)


def _inproj_kernel(x_ref, sh_ref, sc_ref, g_ref, w_ref, o_ref, h_scr):
    @pl.when(pl.program_id(1) == 0)
    def _():
        h_scr[...] = _prenorm(x_ref[...], g_ref[...], sc_ref[...], sh_ref[...]).astype(BF16)

    o_ref[...] = _dot(h_scr[...], w_ref[...])


def _inproj(x, mods, norm_g, w_in, l, rows_per_seq):
    n = x.shape[0]
    tm = _row_tile(n, mods, rows_per_seq, INPROJ_ROW_TILE)
    tn = INPROJ_COL_TILE
    return pl.pallas_call(
        _inproj_kernel,
        grid=(n // tm, IN_PAD // tn),
        in_specs=[pl.BlockSpec((tm, D_MODEL), lambda i, j: (i, 0)),
                  _mod_spec(mods, l, 3, tm, rows_per_seq),
                  _mod_spec(mods, l, 4, tm, rows_per_seq),
                  pl.BlockSpec((None, None, 1, D_MODEL), lambda i, j: (l, 1, 0, 0)),
                  pl.BlockSpec((None, D_MODEL, tn), lambda i, j: (l, 0, j))],
        out_specs=pl.BlockSpec((tm, tn), lambda i, j: (i, j)),
        out_shape=jax.ShapeDtypeStruct((n, IN_PAD), F32),
        scratch_shapes=[pltpu.VMEM((tm, D_MODEL), BF16)],
        compiler_params=_params(("parallel", "arbitrary")),
        name="inproj",
    )(x, mods, mods, norm_g, w_in)


def _prep_kernel(aq_ref, ak_ref, av_ref, iq_ref, ikw_ref, cos_ref, sin_ref, qg_ref, kg_ref, bd_ref,
                 q_o, k_o, kb_o, vt_o, qi_o, ki_o, kib_o, wit_o, ktf_o, vtf_o, kitf_o):
    cos = cos_ref[...]
    sin = sin_ref[...]
    bd = bd_ref[...]
    lane = lax.broadcasted_iota(I32, cos.shape, 1)
    first_half = (lane & (HEAD_DIM // 2)) == 0

    def rope(x):
        partner = jnp.where(first_half, pltpu.roll(x, LANES - HEAD_DIM // 2, 1), pltpu.roll(x, HEAD_DIM // 2, 1))
        return x * cos + partner * sin

    def head_norm(x):
        hi, mid, lo = _split3(x * x)
        ms = (_dot(hi, bd) + _dot(mid, bd) + _dot(lo, bd)) * (1.0 / HEAD_DIM)
        return x * lax.rsqrt(ms + EPS)

    def put_heads(o_ref, first, y):
        for half in range(LANES // HEAD_DIM):
            o_ref[first + half] = y[:, half * HEAD_DIM:(half + 1) * HEAD_DIM].astype(BF16)

    qg = qg_ref[...]
    for c in range(ATTN_WIDTH // LANES):
        sl = slice(c * LANES, (c + 1) * LANES)
        put_heads(q_o, 2 * c, rope(head_norm(aq_ref[:, sl]) * qg) * (ATTN_SCALE * LOG2_E))
    k = rope(head_norm(ak_ref[...]) * kg_ref[...])
    k_o[...] = k
    ktf_o[...] = k.T
    put_heads(kb_o, 0, k)
    v_tf = av_ref[...].T
    vtf_o[...] = v_tf
    v_t = v_tf.astype(BF16)
    ones = jnp.ones((V_ROWS - HEAD_DIM, v_t.shape[1]), BF16)
    vt_o[...] = jnp.concatenate([v_t[:HEAD_DIM], ones, v_t[HEAD_DIM:], ones], axis=0)
    for c in range(IDX_HEADS * IDX_DIM // LANES):
        sl = slice(c * LANES, (c + 1) * LANES)
        put_heads(qi_o, 2 * c, rope(iq_ref[:, sl]))
    ikw = ikw_ref[...]
    ki_wide = rope(head_norm(ikw))
    ki = ki_wide[:, :IDX_DIM]
    ki_o[...] = ki
    kitf_o[...] = ki_wide.T[:IDX_DIM, :]
    kib_o[...] = ki.astype(BF16)
    wit_o[...] = ikw.T[IDX_DIM:IDX_DIM + 8, :]


def _prep(p, cos, sin, qg, kg, bd, tm, table_blocks):
    n = p.shape[0]

    def col(off, width):
        return pl.BlockSpec((tm, width), lambda i: (i, off // width))

    tab = pl.BlockSpec((tm, LANES), lambda i: (i % table_blocks, 0))
    vec = pl.BlockSpec((1, LANES), lambda i: (0, 0))

    def row(width):
        return pl.BlockSpec((tm, width), lambda i: (i, 0))

    def heads(h):
        return pl.BlockSpec((h, tm, HEAD_DIM), lambda i: (0, i, 0))

    n_seq = n // (tm * table_blocks)
    seq_len = tm * table_blocks

    def feature_major(width):
        return pl.BlockSpec((None, width, tm), lambda i: (i // table_blocks, 0, i % table_blocks))

    return pl.pallas_call(
        _prep_kernel,
        grid=(n // tm,),
        in_specs=[col(OFF_AQ, ATTN_WIDTH), col(OFF_AK, KV_WIDTH), col(OFF_AV, KV_WIDTH),
                  col(OFF_IQ, IDX_HEADS * IDX_DIM), col(OFF_IKW, LANES), tab, tab, vec, vec,
                  pl.BlockSpec((LANES, LANES), lambda i: (0, 0))],
        out_specs=[heads(N_HEADS), row(KV_WIDTH), heads(N_KV_HEADS),
                   pl.BlockSpec((None, N_KV_HEADS * V_ROWS, tm), lambda i: (i, 0, 0)),
                   heads(IDX_HEADS), row(IDX_DIM), row(IDX_DIM),
                   pl.BlockSpec((8, tm), lambda i: (0, i)),
                   feature_major(KV_WIDTH), feature_major(KV_WIDTH), feature_major(IDX_DIM)],
        out_shape=[jax.ShapeDtypeStruct((N_HEADS, n, HEAD_DIM), BF16),
                   jax.ShapeDtypeStruct((n, KV_WIDTH), F32),
                   jax.ShapeDtypeStruct((N_KV_HEADS, n, HEAD_DIM), BF16),
                   jax.ShapeDtypeStruct((n // tm, N_KV_HEADS * V_ROWS, tm), BF16),
                   jax.ShapeDtypeStruct((IDX_HEADS, n, IDX_DIM), BF16),
                   jax.ShapeDtypeStruct((n, IDX_DIM), F32),
                   jax.ShapeDtypeStruct((n, IDX_DIM), BF16),
                   jax.ShapeDtypeStruct((8, n), F32),
                   jax.ShapeDtypeStruct((n_seq, KV_WIDTH, seq_len), F32),
                   jax.ShapeDtypeStruct((n_seq, KV_WIDTH, seq_len), F32),
                   jax.ShapeDtypeStruct((n_seq, IDX_DIM, seq_len), F32)],
        compiler_params=_params(("parallel",)),
        name="prep",
    )(p, p, p, p, p, cos, sin, qg, kg, bd)


def _rope_tables(pos):
    half = HEAD_DIM // 2
    inv = ROPE_THETA ** (-jnp.arange(half, dtype=F32) / half)
    ang = pos.astype(F32)[:, None] * inv[None, :]
    c, s = jnp.cos(ang), jnp.sin(ang)
    return jnp.concatenate([c, c, c, c], axis=1), jnp.concatenate([-s, s, -s, s], axis=1)


def _pool_select(sums, lane):
    out = sums[-1]
    for gi in range(len(POOL_WINDOWS) - 2, -1, -1):
        out = jnp.where(lane < (gi + 1) * POOL_GROUP_DIM, sums[gi], out)
    return out


def _pool_kernel(u_ref, prev_ref, w_ref, s_ref, o_ref, carry_scr, *, tt, p0):
    t = pl.program_id(1)

    @pl.when(t == 0)
    def _():
        carry_scr[...] = prev_ref[...]

    u = u_ref[...]
    e = jnp.concatenate([carry_scr[...], u], axis=0)
    carry_scr[...] = u[tt - 16:, :]
    s2 = e[1:] + e[:-1]
    s4 = s2[2:] + s2[:-2]
    s8 = s4[4:] + s4[:-4]
    s16 = s8[8:] + s8[:-8]
    sums = (s2[15:15 + tt], s4[13:13 + tt], s8[9:9 + tt], s16[1:1 + tt])
    n_avail = (p0 + 1 + t * tt + lax.broadcasted_iota(I32, (tt, 1), 0)).astype(F32)
    lane = lax.broadcasted_iota(I32, (tt, POOL_WIDTH), 1)
    means = [sums[gi] / jnp.minimum(n_avail, float(w)) for gi, w in enumerate(POOL_WINDOWS)]
    pooled = _pool_select(means, lane) - u
    o_ref[...] = _dot(pooled.astype(BF16), w_ref[...]) * s_ref[...]


def _pool_prompt(p, prev16, wbd, scale, l, b, t_len):
    tt = min(POOL_ROW_TILE, t_len)
    nt = t_len // tt
    return pl.pallas_call(
        functools.partial(_pool_kernel, tt=tt, p0=0),
        grid=(b, nt),
        in_specs=[pl.BlockSpec((tt, POOL_WIDTH), lambda bi, ti: (bi * nt + ti, OFF_UP // POOL_WIDTH)),
                  pl.BlockSpec((None, 16, POOL_WIDTH), lambda bi, ti: (bi, 0, 0)),
                  pl.BlockSpec((None, POOL_WIDTH, POOL_WIDTH), lambda bi, ti: (l, 0, 0)),
                  pl.BlockSpec((None, 1, POOL_WIDTH), lambda bi, ti: (l, 0, 0))],
        out_specs=pl.BlockSpec((tt, POOL_WIDTH), lambda bi, ti: (bi * nt + ti, 0)),
        out_shape=jax.ShapeDtypeStruct((b * t_len, POOL_WIDTH), F32),
        scratch_shapes=[pltpu.VMEM((16, POOL_WIDTH), F32)],
        compiler_params=_params(("parallel", "arbitrary")),
        name="pool",
    )(p, prev16, wbd, scale)


def _pool_dec_kernel(u_ref, st_ref, w_ref, s_ref, o_ref, *, p0):
    u = u_ref[...]
    lane = lax.broadcasted_iota(I32, u.shape, 1)
    means = []
    acc = u
    d = 1
    for w in POOL_WINDOWS:
        while d < w:
            acc = acc + st_ref[POOL_STATE - d]
            d += 1
        means.append(acc / float(min(p0 + 1, w)))
    pooled = _pool_select(means, lane) - u
    o_ref[...] = _dot(pooled.astype(BF16), w_ref[...]) * s_ref[...]


def _pool_dec(p, st_t, wbd, scale, l, p0):
    n = p.shape[0]
    return pl.pallas_call(
        functools.partial(_pool_dec_kernel, p0=p0),
        grid=(1,),
        in_specs=[pl.BlockSpec((n, POOL_WIDTH), lambda i: (0, OFF_UP // POOL_WIDTH)),
                  pl.BlockSpec((POOL_STATE, n, POOL_WIDTH), lambda i: (0, 0, 0)),
                  pl.BlockSpec((None, POOL_WIDTH, POOL_WIDTH), lambda i: (l, 0, 0)),
                  pl.BlockSpec((None, 1, POOL_WIDTH), lambda i: (l, 0, 0))],
        out_specs=pl.BlockSpec((n, POOL_WIDTH), lambda i: (0, 0)),
        out_shape=jax.ShapeDtypeStruct((n, POOL_WIDTH), F32),
        compiler_params=_params(("arbitrary",)),
        name="pool_dec",
    )(p, st_t, wbd, scale)


def _hgrn_consts(c):
    r = np.arange(c)[:, None]
    s = np.arange(c)[None, :]
    mats = [(s <= r) & (s // h == r // h) for h in HGRN_SMALL]
    mats += [(s > r) & (s // h == r // h) for h in HGRN_SMALL]
    mats.append(s <= r)
    return jnp.asarray(np.concatenate(mats, axis=0).astype(np.float32), BF16)


def _hgrn_kernel(hq_ref, hf_ref, hi_ref, hg_ref, lb_ref, on_ref, ms_ref, yb_ref, st_ref, s_scr, *, c, n_sub):
    step = pl.program_id(1)
    nl = len(HGRN_LEVELS)

    @pl.when(step == 0)
    def _():
        s_scr[...] = jnp.zeros_like(s_scr)

    ti = lax.broadcasted_iota(I32, (c, c), 0)
    si = lax.broadcasted_iota(I32, (c, c), 1)
    pairs = []
    for h in HGRN_LEVELS[:-1]:
        sh = h.bit_length() - 1
        tb = jnp.right_shift(ti, sh)
        pairs.append(((tb & 1) == 1) & (jnp.right_shift(si, sh) == tb - 1))

    kw = HGRN_HEADS * HGRN_K
    for sub in range(n_sub):
        _hgrn_chunk(hq_ref, hf_ref, hi_ref, hg_ref, lb_ref, on_ref, ms_ref, yb_ref, s_scr,
                    slice(sub * c, (sub + 1) * c), c, kw, nl, ti, si, pairs)

    @pl.when(step == pl.num_programs(1) - 1)
    def _():
        st_ref[...] = s_scr[...]


def _hgrn_chunk(hq_ref, hf_ref, hi_ref, hg_ref, lb_ref, on_ref, ms_ref, yb_ref, s_scr,
                rows, c, kw, nl, ti, si, pairs):
    q = hq_ref[rows, :]
    lb = lb_ref[...]
    f = lb + (1.0 - lb) * jax.nn.sigmoid(hf_ref[rows, :])
    g = jnp.log(jnp.maximum(f, F_FLOOR))
    kk = 1.0 - f
    g_hi, g_lo = _split2(g)
    r = _dot(ms_ref[...], jnp.concatenate([g_hi, g_lo], axis=1))
    r = r[:, :kw] + r[:, kw:]
    n_small = len(HGRN_SMALL)
    low = [g] + [r[k * c:(k + 1) * c] for k in range(n_small)]
    up = [None] + [r[(n_small + k) * c:(n_small + k + 1) * c] for k in range(n_small)]
    cum = r[2 * n_small * c:]
    for h in HGRN_LEVELS[1 + n_small:]:
        cum3 = cum.reshape(c // h, h, kw)
        ends = cum3[:, h - 1:h, :]
        if h == c:
            low.append(cum)
        else:
            starts = jnp.concatenate([jnp.zeros((1, 1, kw), F32), ends[:-1]], axis=0)
            low.append((cum3 - starts).reshape(c, kw))
        up.append((ends - cum3).reshape(c, kw))
    kkb = kk.astype(BF16)
    qes = [(q * jnp.exp(low[k])).astype(BF16) for k in range(nl)]
    kes = [kkb] + [(kk * jnp.exp(up[k])).astype(BF16) for k in range(1, nl)]
    qb = q.astype(BF16)
    decay = jnp.exp(cum[c - 1:c, :])

    for hd in range(HGRN_HEADS):
        ks = slice(hd * HGRN_K, (hd + 1) * HGRN_K)
        vs = slice(hd * HGRN_V, (hd + 1) * HGRN_V)
        ivb = hi_ref[rows, vs].astype(BF16)
        att = jnp.where(ti == si, _dot_nt(qb[:, ks], kkb[:, ks]), 0.0)
        for k in range(nl - 1):
            att = jnp.where(pairs[k], _dot_nt(qes[k][:, ks], kes[k][:, ks]), att)
        s_t = s_scr[hd]
        o = _dot(att.astype(BF16), ivb) + _dot_nt(qes[-1][:, ks], s_t.astype(BF16))
        s_scr[hd] = s_t * decay[:, ks] + _dot_tn(ivb, kes[-1][:, ks])

        on = o * lax.rsqrt(jnp.mean(o * o, axis=-1, keepdims=True) + EPS) * on_ref[...]
        yb_ref[rows, vs] = on * _silu(hg_ref[rows, vs])


def _hgrn_prompt(p, lb, onorm, mstack, b, t_len):
    c = HGRN_CHUNK
    n_sub = min(HGRN_CHUNKS_PER_STEP, t_len // c)
    nc = t_len // (c * n_sub)
    kw = HGRN_HEADS * HGRN_K
    vw = HGRN_HEADS * HGRN_V

    def col(off, width):
        return pl.BlockSpec((c * n_sub, width), lambda bi, ci: (bi * nc + ci, off // width))

    return pl.pallas_call(
        functools.partial(_hgrn_kernel, c=c, n_sub=n_sub),
        grid=(b, nc),
        in_specs=[col(OFF_HQ, kw), col(OFF_HF, kw), col(OFF_HI, vw), col(OFF_HG, vw),
                  pl.BlockSpec((1, kw), lambda bi, ci: (0, 0)),
                  pl.BlockSpec((1, HGRN_V), lambda bi, ci: (0, 0)),
                  pl.BlockSpec(mstack.shape, lambda bi, ci: (0, 0))],
        out_specs=[pl.BlockSpec((c * n_sub, vw), lambda bi, ci: (bi * nc + ci, 0)),
                   pl.BlockSpec((None, HGRN_HEADS, HGRN_V, HGRN_K), lambda bi, ci: (bi, 0, 0, 0))],
        out_shape=[jax.ShapeDtypeStruct((b * t_len, vw), F32),
                   jax.ShapeDtypeStruct((b, HGRN_HEADS, HGRN_V, HGRN_K), F32)],
        scratch_shapes=[pltpu.VMEM((HGRN_HEADS, HGRN_V, HGRN_K), F32)],
        compiler_params=_params(("parallel", "arbitrary")),
        name="hgrn",
    )(p, p, p, p, lb, onorm, mstack)


def _hgrn_dec_kernel(q_ref, z_ref, lb_ref, i_ref, hg_ref, on_ref, s_ref, yb_ref, so_ref):
    lb = lb_ref[...]
    f = lb + (1.0 - lb) * jax.nn.sigmoid(z_ref[...])
    a = jnp.exp(jnp.log(jnp.maximum(f, F_FLOOR)))
    s_new = a * s_ref[...] + i_ref[...] * (1.0 - f)
    so_ref[...] = s_new
    o = jnp.sum(q_ref[...] * s_new, axis=-1, keepdims=True)
    on = o * lax.rsqrt(jnp.mean(o * o, axis=-2, keepdims=True) + EPS) * on_ref[...]
    yb_ref[...] = on * _silu(hg_ref[...])


def _hgrn_dec(q_r, z_r, lb_r, i_c, hg_c, onorm_c, state_t, l):
    n = state_t.shape[1]
    nb = min(HGRN_DEC_SEQS, n)
    row = pl.BlockSpec((nb, HGRN_HEADS, 1, HGRN_K), lambda i: (i, 0, 0, 0))
    col = pl.BlockSpec((nb, HGRN_HEADS, HGRN_V, 1), lambda i: (i, 0, 0, 0))
    return pl.pallas_call(
        _hgrn_dec_kernel,
        grid=(n // nb,),
        in_specs=[row, row, pl.BlockSpec((HGRN_HEADS, 1, HGRN_K), lambda i: (0, 0, 0)), col, col,
                  pl.BlockSpec((HGRN_V, 1), lambda i: (0, 0)),
                  pl.BlockSpec((None, nb, HGRN_HEADS, HGRN_V, HGRN_K), lambda i: (l, i, 0, 0, 0))],
        out_specs=[col, pl.BlockSpec((nb, HGRN_HEADS, HGRN_V, HGRN_K), lambda i: (i, 0, 0, 0))],
        out_shape=[jax.ShapeDtypeStruct((n, HGRN_HEADS, HGRN_V, 1), F32),
                   jax.ShapeDtypeStruct(state_t.shape[1:], F32)],
        compiler_params=_params(("parallel",)),
        name="hgrn_dec",
    )(q_r, z_r, lb_r, i_c, hg_c, onorm_c, state_t)


def _sort_key(s):
    bits = pltpu.bitcast(s, I32)
    return jnp.where(bits < 0, INT_MIN - bits, bits)


def _lane_fold(x):
    part = x[:, :LANES]
    for cidx in range(1, x.shape[1] // LANES):
        part = part + x[:, cidx * LANES:(cidx + 1) * LANES]
    return part


def _select_bias(key, thr, idx, j_lim):
    return jnp.where(key > thr, 0.0, jnp.where(key == thr, jnp.where(idx < j_lim, 0.0, NEG_BIG), NEG_BIG))


def _bisect_bits(count_ge, thr, n_ge, first_bit, n_bits, k):
    def bit_body(it, carry):
        thr, n_ge = carry
        cand = thr + lax.shift_left(jnp.int32(1), first_bit - it)
        cnt = count_ge(cand)
        take = cnt >= k
        return jnp.where(take, cand, thr), jnp.where(take, cnt, n_ge)

    return lax.fori_loop(0, n_bits, bit_body, (thr, n_ge))


def _topk_threshold(count, stat_shape, n_sel, idx_bits, idx_sentinel, kth_largest=None):
    k = float(n_sel)
    if kth_largest is None:
        thr, n_ge = _bisect_bits(lambda cand: count(lambda key, idx: jnp.where(key >= cand, 1.0, 0.0)),
                                 jnp.full(stat_shape, INT_MIN, I32), jnp.zeros(stat_shape, F32), 31, 32, k)
    else:
        thr, n_ge = kth_largest(k)
    live = thr > INT_MIN
    tie = jnp.where(live, jnp.where(n_ge > k, 1.0, 0.0), 0.0)
    j_default = jnp.where(live, jnp.int32(idx_sentinel), jnp.int32(0))

    def tie_search():
        need = k - count(lambda key, idx: jnp.where(key > thr, 1.0, 0.0))

        def idx_body(it, j_lim):
            cand = j_lim + lax.shift_left(jnp.int32(1), idx_bits - 1 - it)
            cnt = count(lambda key, idx: jnp.where(key == thr, jnp.where(idx < cand, 1.0, 0.0), 0.0))
            return jnp.where(cnt <= need, cand, j_lim)

        j_lim = lax.fori_loop(0, idx_bits, idx_body, jnp.zeros(stat_shape, I32))
        return jnp.where(tie > 0.0, j_lim, j_default)

    j_lim = lax.cond(jnp.max(tie) > 0.0, tie_search, lambda: j_default)
    return thr, j_lim


FOLD_ROWS = 64
V_ROWS = HEAD_DIM + 16


def _fold(x, op):
    parts = [x[r * FOLD_ROWS:(r + 1) * FOLD_ROWS] for r in range(x.shape[0] // FOLD_ROWS)]
    while len(parts) > 1:
        parts = [op(parts[i], parts[i + 1]) if i + 1 < len(parts) else parts[i]
                 for i in range(0, len(parts), 2)]
    return parts[0]


def _dsa_kernel(q_ref, qi_ref, wit_ref, k_ref, vt_ref, ki_ref, o_ref, keys_scr, k16_scr, lg_scr, p_scr, m_scr, acc_scr,
                *, tq, tk, n_sel, t_len):
    i = pl.program_id(1)
    nk = ((i + 1) * tq + tk - 1) // tk
    krow = lax.broadcasted_iota(I32, (tk, tq), 0)
    qpos = i * tq + lax.broadcasted_iota(I32, (tk, tq), 1)
    wit = wit_ref[...] * IDX_SCALE
    qi_all = qi_ref[...].reshape(IDX_HEADS * tq, IDX_DIM)

    def rows(j):
        return pl.ds(pl.multiple_of(j * tk, tk), tk)

    def score_tile(j, causal):
        rel = _dot_nt(ki_ref[rows(j), :], qi_all)
        s = wit[0:1, :] * jnp.maximum(rel[:, :tq], 0.0)
        for h in range(1, IDX_HEADS):
            s = s + wit[h:h + 1, :] * jnp.maximum(rel[:, h * tq:(h + 1) * tq], 0.0)
        key = jnp.where(s > 0.5 * NEG_BIG, _sort_key(s), INT_MIN)
        if causal:
            key = jnp.where(j * tk + krow <= qpos, key, INT_MIN)
        keys_scr[j] = key
        k16_scr[j] = jnp.right_shift(key, 16).astype(I16)

    def score_body(j, carry):
        score_tile(j, False)
        return carry

    lax.fori_loop(0, nk - 1, score_body, 0)
    score_tile(nk - 1, True)

    def count(pred):
        def body(j, acc):
            return acc + _fold(pred(keys_scr[j], j * tk + krow), jnp.add)

        acc = lax.fori_loop(0, nk, body, jnp.zeros((FOLD_ROWS, tq), F32))
        return jnp.sum(acc, axis=0, keepdims=True)

    one16, zero16 = jnp.int16(1), jnp.int16(0)

    def count16(pred):
        def body(j, acc):
            return acc + _fold(pred(k16_scr[j]), jnp.add)

        acc = lax.fori_loop(0, nk, body, jnp.zeros((FOLD_ROWS, tq), I16))
        return jnp.sum(acc.astype(F32), axis=0, keepdims=True)

    def high16(x):
        return jnp.right_shift(x, 16).astype(I16)

    def low16(x):
        return ((x & 0xFFFF) - 32768).astype(I16)

    def kth_largest(k):
        thr = jnp.full((1, tq), INT_MIN, I32)
        thr, n_ge = _bisect_bits(lambda cand: count16(lambda t: jnp.where(t >= high16(cand), one16, zero16)),
                                 thr, jnp.zeros((1, tq), F32), 31, 16, k)
        thr_hi = high16(thr)
        n_above = count16(lambda t: jnp.where(t > thr_hi, one16, zero16))

        def low_body(j, carry):
            key = keys_scr[j]
            k16_scr[j] = jnp.where(high16(key) == thr_hi, low16(key), jnp.int16(-32768))
            return carry

        lax.fori_loop(0, nk, low_body, 0)
        return _bisect_bits(
            lambda cand: n_above + count16(lambda t: jnp.where(t >= low16(cand), one16, zero16)),
            thr, n_ge, 15, 16, k)

    thr, j_lim = _topk_threshold(count, (1, tq), n_sel, int(t_len).bit_length(), t_len, kth_largest)

    m_scr[...] = jnp.full(m_scr.shape, 0.5 * NEG_BIG, F32)
    acc_scr[...] = jnp.zeros_like(acc_scr)

    def attend_tile(j, carry):
        bias = _select_bias(keys_scr[j], thr, j * tk + krow, j_lim)
        half = GROUP // 2
        bias = jnp.concatenate([bias] * half, axis=1)
        v_t = vt_ref[j]
        units = [(n, hf) for n in range(N_KV_HEADS) for hf in range(2)]

        def lanes(hf):
            return slice(hf * half * tq, (hf + 1) * half * tq)

        for n, hf in units:
            h0 = n * GROUP + hf * half
            qg = q_ref[h0:h0 + half].reshape(half * tq, HEAD_DIM)
            lg_scr[n, :, lanes(hf)] = _dot_nt(k_ref[n, rows(j), :], qg) + bias
        m_old = [m_scr[n, :, lanes(hf)] for n, hf in units]
        m_new = [jnp.maximum(m_old[u], jnp.max(_fold(lg_scr[n, :, lanes(hf)], jnp.maximum), axis=0, keepdims=True))
                 for u, (n, hf) in enumerate(units)]
        for u, (n, hf) in enumerate(units):
            p_scr[n, :, lanes(hf)] = jnp.exp2(lg_scr[n, :, lanes(hf)] - m_new[u]).astype(BF16)
        for u, (n, hf) in enumerate(units):
            acc_scr[n, :, lanes(hf)] = (jnp.exp2(m_old[u] - m_new[u]) * acc_scr[n, :, lanes(hf)]
                                        + _dot(v_t[n * V_ROWS:(n + 1) * V_ROWS, :], p_scr[n, :, lanes(hf)]))
            m_scr[n, :, lanes(hf)] = m_new[u]
        return carry

    lax.fori_loop(0, nk, attend_tile, 0)
    outs = []
    for n in range(N_KV_HEADS):
        acc = acc_scr[n]
        o_n = acc[:HEAD_DIM] / acc[HEAD_DIM:HEAD_DIM + 1]
        outs += [o_n[:, g * tq:(g + 1) * tq] for g in range(GROUP)]
    o_ref[...] = jnp.concatenate(outs, axis=0).T


def _dsa_prompt(q_h, qi_h, wit, k_h, v_t, ki_b, b, t_len, n_sel, tk):
    tq = min(DSA_QUERY_TILE, t_len)
    nq = t_len // tq
    nt = t_len // tk

    def heads(h, rows, imap):
        return pl.BlockSpec((h, rows, HEAD_DIM), imap)

    return pl.pallas_call(
        functools.partial(_dsa_kernel, tq=tq, tk=tk, n_sel=n_sel, t_len=t_len),
        grid=(b, nq),
        in_specs=[heads(N_HEADS, tq, lambda bi, qi: (0, bi * nq + qi, 0)),
                  heads(IDX_HEADS, tq, lambda bi, qi: (0, bi * nq + qi, 0)),
                  pl.BlockSpec((8, tq), lambda bi, qi: (0, bi * nq + qi)),
                  heads(N_KV_HEADS, t_len, lambda bi, qi: (0, bi, 0)),
                  pl.BlockSpec((nt, N_KV_HEADS * V_ROWS, tk), lambda bi, qi: (bi, 0, 0)),
                  pl.BlockSpec((t_len, IDX_DIM), lambda bi, qi: (bi, 0))],
        out_specs=pl.BlockSpec((tq, ATTN_WIDTH), lambda bi, qi: (bi * nq + qi, 0)),
        out_shape=jax.ShapeDtypeStruct((b * t_len, ATTN_WIDTH), F32),
        scratch_shapes=[pltpu.VMEM((nt, tk, tq), I32),
                        pltpu.VMEM((nt, tk, tq), I16),
                        pltpu.VMEM((N_KV_HEADS, tk, GROUP * tq), F32),
                        pltpu.VMEM((N_KV_HEADS, tk, GROUP * tq), BF16),
                        pltpu.VMEM((N_KV_HEADS, 1, GROUP * tq), F32),
                        pltpu.VMEM((N_KV_HEADS, V_ROWS, GROUP * tq), F32)],
        compiler_params=_params(("parallel", "arbitrary")),
        name="dsa",
    )(q_h, qi_h, wit, k_h, v_t, ki_b)


DEC_SEQS_PER_STEP = 4


def _dsa_dec_score_kernel(pt_ref, qi_ref, wi_ref, kin_ref, *rest, n_pages, n_seq):
    page_refs = rest[:n_seq * n_pages]
    o_ref = rest[n_seq * n_pages]
    lane = lax.broadcasted_iota(I32, (1, PAGE_SIZE), 1)
    for s in range(n_seq):
        qi = qi_ref[s]
        wi = wi_ref[s]

        def finish(rel):
            sc = jnp.sum(wi * jnp.maximum(rel, 0.0), axis=0, keepdims=True) * IDX_SCALE
            return jnp.where(sc > 0.5 * NEG_BIG, _sort_key(sc), INT_MIN)

        for pg in range(n_pages):
            page = page_refs[s * n_pages + pg][...].astype(BF16)
            o_ref[s, :, pg * PAGE_SIZE:(pg + 1) * PAGE_SIZE] = finish(_dot(qi, page))
        kin = kin_ref[s].astype(BF16).astype(F32)
        key_new = finish(jnp.sum(qi.astype(F32) * kin, axis=1, keepdims=True))
        o_ref[s, :, n_pages * PAGE_SIZE:] = jnp.where(lane == 0, key_new, INT_MIN)


def _page_specs(block, l, n_seq, n_pages):
    def spec(s, pg):
        return pl.BlockSpec(block, lambda bi, pt, *_: (l, pt[bi * n_seq + s, pg], 0, 0))

    return [spec(s, pg) for s in range(n_seq) for pg in range(n_pages)]


def _dsa_dec_scores(page_table, qi8, wi8, ki_new, cache_ki_t, l):
    n, n_pages = page_table.shape
    ns = min(DEC_SEQS_PER_STEP, n)
    width = (n_pages + 1) * PAGE_SIZE
    page_specs = _page_specs((None, None, IDX_DIM, PAGE_SIZE), l, ns, n_pages)
    return pl.pallas_call(
        functools.partial(_dsa_dec_score_kernel, n_pages=n_pages, n_seq=ns),
        grid_spec=pltpu.PrefetchScalarGridSpec(
            num_scalar_prefetch=1,
            grid=(n // ns,),
            in_specs=[pl.BlockSpec((ns, 8, IDX_DIM), lambda bi, pt: (bi, 0, 0)),
                      pl.BlockSpec((ns, 8, 1), lambda bi, pt: (bi, 0, 0)),
                      pl.BlockSpec((ns, 1, IDX_DIM), lambda bi, pt: (bi, 0, 0))] + page_specs,
            out_specs=pl.BlockSpec((ns, 1, width), lambda bi, pt: (bi, 0, 0))),
        out_shape=jax.ShapeDtypeStruct((n, 1, width), I32),
        compiler_params=_params(("arbitrary",)),
        name="dsa_dec_scores",
    )(page_table, qi8, wi8, ki_new, *([cache_ki_t] * (ns * n_pages)))


def _dsa_dec_thr_kernel(keys_ref, thr_ref, j_ref, *, n_sel):
    n, width = keys_ref.shape
    lane = lax.broadcasted_iota(I32, (n, width), 1)

    def count(pred):
        return jnp.sum(_lane_fold(pred(keys_ref[...], lane)), axis=1, keepdims=True)

    thr, j_lim = _topk_threshold(count, (n, 1), n_sel, int(width).bit_length(), width)
    thr_ref[...] = thr
    j_ref[...] = j_lim


def _dsa_dec_thr(keys2d, n_sel):
    n, width = keys2d.shape
    return pl.pallas_call(
        functools.partial(_dsa_dec_thr_kernel, n_sel=n_sel),
        grid=(1,),
        in_specs=[pl.BlockSpec((n, width), lambda i: (0, 0))],
        out_specs=[pl.BlockSpec((n, 1), lambda i: (0, 0)), pl.BlockSpec((n, 1), lambda i: (0, 0))],
        out_shape=[jax.ShapeDtypeStruct((n, 1), I32), jax.ShapeDtypeStruct((n, 1), I32)],
        compiler_params=_params(("arbitrary",)),
        name="dsa_dec_thr",
    )(keys2d)


def _dsa_dec_attn_kernel(pt_ref, thr_ref, j_ref, q_ref, keys_ref, kn_ref, vn_ref, *rest, n_pages, n_seq):
    k_pages = rest[:n_seq * n_pages]
    v_pages = rest[n_seq * n_pages:2 * n_seq * n_pages]
    o_ref, k_scr, v_scr = rest[2 * n_seq * n_pages:]
    n_past = n_pages * PAGE_SIZE
    idx = lax.broadcasted_iota(I32, (1, keys_ref.shape[-1]), 1)
    for s in range(n_seq):
        b = pl.program_id(0) * n_seq + s
        for pg in range(n_pages):
            sl = slice(pg * PAGE_SIZE, (pg + 1) * PAGE_SIZE)
            k_scr[s, :, sl] = k_pages[s * n_pages + pg][...].astype(BF16)
            v_scr[s, :, sl] = v_pages[s * n_pages + pg][...].astype(BF16)
        bias_all = _select_bias(keys_ref[s], thr_ref[b], idx, j_ref[b])
        bias = bias_all[:, :n_past]
        bias_new = bias_all[:, n_past:n_past + 1]
        kn = kn_ref[s].astype(BF16).astype(F32)
        vn = vn_ref[s].astype(BF16).astype(F32)
        for n in range(N_KV_HEADS):
            fs = slice(n * HEAD_DIM, (n + 1) * HEAD_DIM)
            q = q_ref[s, n]
            lg = _dot(q, k_scr[s, fs, :]) + bias
            lg_new = jnp.sum(q.astype(F32) * kn[:, fs], axis=1, keepdims=True) + bias_new
            m = jnp.maximum(jnp.max(lg, axis=1, keepdims=True), lg_new)
            p = jnp.exp2(lg - m)
            p_new = jnp.exp2(lg_new - m)
            den = jnp.sum(p, axis=1, keepdims=True) + p_new
            o_ref[s, n] = (_dot_nt(p.astype(BF16), v_scr[s, fs, :]) + p_new * vn[:, fs]) / den


def _dsa_dec_attn(page_table, thr, j_lim, q8, keys, k_new, v_new, cache_k_t, cache_v_t, l):
    n, n_pages = page_table.shape
    ns = min(DEC_SEQS_PER_STEP, n)
    width = (n_pages + 1) * PAGE_SIZE
    page_specs = _page_specs((None, None, KV_WIDTH, PAGE_SIZE), l, ns, n_pages)
    qspec = pl.BlockSpec((ns, N_KV_HEADS, 8, HEAD_DIM), lambda bi, pt, th, jl: (bi, 0, 0, 0))

    def per_seq(width_):
        return pl.BlockSpec((ns, 1, width_), lambda bi, pt, th, jl: (bi, 0, 0))

    return pl.pallas_call(
        functools.partial(_dsa_dec_attn_kernel, n_pages=n_pages, n_seq=ns),
        grid_spec=pltpu.PrefetchScalarGridSpec(
            num_scalar_prefetch=3,
            grid=(n // ns,),
            in_specs=[qspec, per_seq(width), per_seq(KV_WIDTH), per_seq(KV_WIDTH)] + page_specs + page_specs,
            out_specs=qspec,
            scratch_shapes=[pltpu.VMEM((ns, KV_WIDTH, n_pages * PAGE_SIZE), BF16),
                            pltpu.VMEM((ns, KV_WIDTH, n_pages * PAGE_SIZE), BF16)]),
        out_shape=jax.ShapeDtypeStruct((n, N_KV_HEADS, 8, HEAD_DIM), F32),
        compiler_params=_params(("arbitrary",)),
        name="dsa_dec_attn",
    )(page_table, thr, j_lim, q8, keys, k_new, v_new,
      *([cache_k_t] * (ns * n_pages)), *([cache_v_t] * (ns * n_pages)))


def _merge_kernel(x_ref, g0_ref, g1_ref, g2_ref, ya_ref, yb_ref, yc_ref, gt_ref,
                  wa_ref, wb_ref, wc_ref, wo_ref, o_ref):
    merged = (jax.nn.sigmoid(g0_ref[...]) * _dot(ya_ref[...].astype(BF16), wa_ref[...])
              + jax.nn.sigmoid(g1_ref[...]) * _dot(yb_ref[...].astype(BF16), wb_ref[...])
              + jax.nn.sigmoid(g2_ref[...]) * _dot(yc_ref[...].astype(BF16), wc_ref[...]))
    o_ref[...] = x_ref[...] + gt_ref[...] * _dot(merged.astype(BF16), wo_ref[...])


def _merge(x, p, ya, yb, yc, mods, w_ba, w_bb, w_bc, w_out, l, rows_per_seq):
    n = x.shape[0]
    tm = _row_tile(n, mods, rows_per_seq, MERGE_ROW_TILE)

    def wspec(rows):
        return pl.BlockSpec((None, rows, D_MODEL), lambda i: (l, 0, 0))

    def row(width):
        return pl.BlockSpec((tm, width), lambda i: (i, 0))

    def gate(bidx):
        return pl.BlockSpec((tm, D_MODEL), lambda i: (i, OFF_GL // D_MODEL + bidx))

    return pl.pallas_call(
        _merge_kernel,
        grid=(n // tm,),
        in_specs=[row(D_MODEL), gate(0), gate(1), gate(2), row(POOL_WIDTH), row(HGRN_HEADS * HGRN_V),
                  row(ATTN_WIDTH), _mod_spec(mods, l, 5, tm, rows_per_seq),
                  wspec(POOL_WIDTH), wspec(HGRN_HEADS * HGRN_V), wspec(ATTN_WIDTH), wspec(D_MODEL)],
        out_specs=row(D_MODEL),
        out_shape=jax.ShapeDtypeStruct((n, D_MODEL), F32),
        compiler_params=_params(("parallel",)),
        name="merge",
    )(x, p, p, p, ya, yb, yc, mods, w_ba, w_bb, w_bc, w_out)


def _permute_w_in(w_in):
    segs = [(2884, 5956), (256, 768), (768, 1280), (1792, 2304), (0, 256), (1280, 1536), (1536, 1792),
            (2560, 2816), (2304, 2432), (2432, 2560), (2816, 2884)]
    parts = [w_in[..., a:b] for a, b in segs]
    used = sum(b - a for a, b in segs)
    parts.append(jnp.zeros(w_in.shape[:-1] + (IN_PAD - used,), w_in.dtype))
    return jnp.concatenate(parts, axis=-1).astype(BF16)


def _lower_bounds(p):
    sm = jax.nn.softmax(p.astype(F32), axis=0)
    return jnp.cumsum(sm, axis=0) - sm[0:1]


def kernel(x_prompt, x_sample, c_prompt, c_sample, cache_k, cache_v, cache_ki, page_table, state_pool, state_hgrn, ada_w, ada_b, norm_g, ffn_wg, ffn_wu, ffn_wd, w_in, pool_w, pool_scale, hgrn_lb, hgrn_onorm, q_norm, k_norm, w_ba, w_bb, w_bc, w_out):
    bp, t_len, _ = x_prompt.shape
    bs = x_sample.shape[0]
    assert x_sample.shape[1] == 1
    n_pages = page_table.shape[1]
    past_len = n_pages * PAGE_SIZE
    n_phys = cache_k.shape[1]
    n_sel_p = min(TOPK_MAX, t_len // 4)
    n_sel_s = min(TOPK_MAX, (past_len + 1) // 4)

    wg, wu, wd = ffn_wg.astype(BF16), ffn_wu.astype(BF16), ffn_wd.astype(BF16)
    w_in_p = _permute_w_in(w_in)
    wa, wb, wc, wo = w_ba.astype(BF16), w_bb.astype(BF16), w_bc.astype(BF16), w_out.astype(BF16)
    eye = jnp.eye(len(POOL_WINDOWS), dtype=F32)
    pool_wbd = jnp.einsum('lgcd,gh->lgchd', pool_w, eye).reshape(DEPTH, POOL_WIDTH, POOL_WIDTH).astype(BF16)
    pool_sc = pool_scale.reshape(DEPTH, 1, POOL_WIDTH)
    norm_g4 = norm_g.reshape(DEPTH, N_SUB, 1, D_MODEL)
    lbs = _lower_bounds(hgrn_lb)
    qg = jnp.tile(q_norm, (1, LANES // HEAD_DIM)).reshape(DEPTH, 1, LANES)
    kg = jnp.tile(k_norm, (1, LANES // HEAD_DIM)).reshape(DEPTH, 1, LANES)
    gid = np.arange(LANES) // HEAD_DIM
    bd = jnp.asarray((gid[:, None] == gid[None, :]).astype(np.float32), BF16)
    mstack = _hgrn_consts(HGRN_CHUNK)
    cos_p, sin_p = _rope_tables(jnp.arange(t_len, dtype=I32))
    cos_s, sin_s = _rope_tables(jnp.full((bs,), past_len, I32))
    ck = jnp.transpose(cache_k, (0, 1, 3, 4, 2)).reshape(DEPTH, n_phys, KV_WIDTH, PAGE_SIZE)
    cv = jnp.transpose(cache_v, (0, 1, 3, 4, 2)).reshape(DEPTH, n_phys, KV_WIDTH, PAGE_SIZE)
    cki = jnp.transpose(cache_ki, (0, 1, 3, 2))
    state_hgrn_t = jnp.swapaxes(state_hgrn, -1, -2)

    rows_all = bp + bs
    rows_pad = -(-rows_all // 8) * 8
    c_all = jnp.concatenate([c_prompt, c_sample, jnp.zeros((rows_pad - rows_all, D_MODEL), F32)], axis=0)
    mods = _ada(c_all, ada_w, ada_b)
    mods_p = mods[:, :bp].reshape(DEPTH, bp, 1, 3 * N_SUB * D_MODEL)
    mods_s = mods[:, bp:rows_all]

    xp = x_prompt.reshape(bp * t_len, D_MODEL)
    xs = x_sample.reshape(bs, D_MODEL)
    pool_prev_p = jnp.zeros((bp, 16, POOL_WIDTH), F32)
    outs = [[] for _ in range(10)]
    tm_prep = min(PREP_ROW_TILE, t_len)

    for l in range(DEPTH):
        xp = _ffn(xp, mods_p, norm_g4, wg, wu, wd, l, 0, 0, t_len)
        p = _inproj(xp, mods_p, norm_g4, w_in_p, l, t_len)
        ya = _pool_prompt(p, pool_prev_p, pool_wbd, pool_sc, l, bp, t_len)
        yb, st_t = _hgrn_prompt(p, lbs[l].reshape(1, -1), hgrn_onorm[l].reshape(1, -1), mstack, bp, t_len)
        q_h, _, k_h, v_t, qi_h, _, ki_b, wit, k_tf, v_tf, ki_tf = _prep(
            p, cos_p, sin_p, qg[l], kg[l], bd, tm_prep, t_len // tm_prep)
        yc = _dsa_prompt(q_h, qi_h, wit, k_h, v_t, ki_b, bp, t_len, n_sel_p, tm_prep)
        xp = _merge(xp, p, ya, yb, yc, mods_p, wa, wb, wc, wo, l, t_len)
        xp = _ffn(xp, mods_p, norm_g4, wg, wu, wd, l, 1, 2, t_len)
        p3 = p.reshape(bp, t_len, IN_PAD)
        outs[0].append(jnp.transpose(k_tf.reshape(bp, N_KV_HEADS, HEAD_DIM, t_len), (0, 3, 1, 2)))
        outs[1].append(jnp.transpose(v_tf.reshape(bp, N_KV_HEADS, HEAD_DIM, t_len), (0, 3, 1, 2)))
        outs[2].append(jnp.swapaxes(ki_tf, 1, 2))
        outs[3].append(p3[:, t_len - POOL_STATE:, OFF_UP:OFF_UP + POOL_WIDTH])
        outs[4].append(jnp.swapaxes(st_t, -1, -2))

        xs = _ffn(xs, mods_s, norm_g4, wg, wu, wd, l, 0, 0, 1)
        ps = _inproj(xs, mods_s, norm_g4, w_in_p, l, 1)
        ya = _pool_dec(ps, jnp.swapaxes(state_pool[l], 0, 1), pool_wbd, pool_sc, l, past_len)
        kw = HGRN_HEADS * HGRN_K
        vw = HGRN_HEADS * HGRN_V
        yb4, st_new = _hgrn_dec(
            ps[:, OFF_HQ:OFF_HQ + kw].reshape(bs, HGRN_HEADS, 1, HGRN_K),
            ps[:, OFF_HF:OFF_HF + kw].reshape(bs, HGRN_HEADS, 1, HGRN_K),
            lbs[l].reshape(HGRN_HEADS, 1, HGRN_K),
            ps[:, OFF_HI:OFF_HI + vw].reshape(bs, HGRN_HEADS, HGRN_V, 1),
            ps[:, OFF_HG:OFF_HG + vw].reshape(bs, HGRN_HEADS, HGRN_V, 1),
            hgrn_onorm[l].reshape(-1, 1), state_hgrn_t, l)
        yb = yb4.reshape(bs, vw)
        q_h, k_r, _, _, qi_h, ki_r, _, _, _, _, _ = _prep(ps, cos_s, sin_s, qg[l], kg[l], bd, bs, 1)
        v_new = ps[:, OFF_AV:OFF_AV + KV_WIDTH]
        qi8 = jnp.pad(jnp.swapaxes(qi_h, 0, 1), ((0, 0), (0, 8 - IDX_HEADS), (0, 0)))
        wi8 = jnp.pad(ps[:, OFF_IKW + IDX_DIM:OFF_IKW + IDX_DIM + IDX_HEADS].reshape(bs, IDX_HEADS, 1),
                      ((0, 0), (0, 8 - IDX_HEADS), (0, 0)))
        keys = _dsa_dec_scores(page_table, qi8, wi8, ki_r.reshape(bs, 1, IDX_DIM), cki, l)
        thr, j_lim = _dsa_dec_thr(keys.reshape(bs, -1), n_sel_s)
        q8 = jnp.pad(jnp.swapaxes(q_h, 0, 1).reshape(bs, N_KV_HEADS, GROUP, HEAD_DIM),
                     ((0, 0), (0, 0), (0, 8 - GROUP), (0, 0)))
        yc8 = _dsa_dec_attn(page_table, thr.reshape(bs), j_lim.reshape(bs), q8, keys,
                            k_r.reshape(bs, 1, KV_WIDTH), v_new.reshape(bs, 1, KV_WIDTH), ck, cv, l)
        yc = yc8[:, :, :GROUP, :].reshape(bs, ATTN_WIDTH)
        xs = _merge(xs, ps, ya, yb, yc, mods_s, wa, wb, wc, wo, l, 1)
        xs = _ffn(xs, mods_s, norm_g4, wg, wu, wd, l, 1, 2, 1)
        outs[5].append(k_r.reshape(bs, 1, N_KV_HEADS, HEAD_DIM))
        outs[6].append(v_new.reshape(bs, 1, N_KV_HEADS, HEAD_DIM))
        outs[7].append(ki_r.reshape(bs, 1, IDX_DIM))
        outs[8].append(jnp.concatenate([state_pool[l][:, 1:], ps[:, None, OFF_UP:OFF_UP + POOL_WIDTH]], axis=1))
        outs[9].append(jnp.swapaxes(st_new, -1, -2))

    stacked = [jnp.stack(o) for o in outs]
    return (xp.reshape(bp, t_len, D_MODEL), xs.reshape(bs, 1, D_MODEL), *stacked)
```
